```python
import math
import jax
import jax.numpy as jnp
from jax import lax
import numpy as np

D_MODEL = 1024
BATCH = 8
SEQ = 2048
DEPTH = 2

GRID_W = 64
CTX_LEN = 256
N_EVEN = (DEPTH + 1) // 2
N_ODD = DEPTH // 2
EPS = 1e-6
NEG_INF = -1e30
BLOCK = 128
ROPE_BASE = 10000.0
HY_CH = D_MODEL // 2
HY_ORDER = 2
HY_EMB = 33
HY_FILT_HID = 64
HY_MAX_DECAY = math.log(1e-2) / 0.3
HY_MIN_DECAY = math.log(1e-2) / 1.5
DA_HEADS = 4
DA_HD = D_MODEL // 16
RET_HEADS = 4
RET_DK = D_MODEL // 8
RET_DV = D_MODEL // 8
RET_CHUNK = 128
GQ_KV = 2
GQ_GROUP = 4
GQ_HD = D_MODEL // 16
WINDOW = 128
N_GROUPS = 4
EXP_PER_GROUP = 8
TOP_K = 2
D_EXPERT = D_MODEL // 4

E_SPLIT = (3 * HY_CH, DA_HEADS * 2 * DA_HD, DA_HEADS * 2 * DA_HD, DA_HEADS * 2 * DA_HD)
E_IN = sum(E_SPLIT)
E_MIX = HY_CH + DA_HEADS * 2 * DA_HD
O_SPLIT = (RET_HEADS * RET_DK, RET_HEADS * RET_DK, RET_HEADS * RET_DV, RET_HEADS * RET_DV,
           GQ_KV * GQ_GROUP * GQ_HD, GQ_KV * GQ_HD, GQ_KV * GQ_HD)
O_IN = sum(O_SPLIT)
O_MIX = RET_HEADS * RET_DV + GQ_KV * GQ_GROUP * GQ_HD

kernel_name = "hybrid_hyena_diffattn_retention_swa_hmoe_dit"

f32 = jnp.float32


def split_cols(t, sizes):
    return jnp.split(t, [int(i) for i in np.cumsum(sizes)[:-1]], axis=-1)


def rms_norm(x):
    xf = x.astype(f32)
    return (xf * lax.rsqrt(jnp.mean(xf * xf, axis=-1, keepdims=True) + EPS)).astype(x.dtype)


def modulate(x, shift, scale):
    return x * (1 + scale) + shift


def axial_rope(rows, head_dim):
    row = jnp.repeat(jnp.arange(rows), GRID_W).astype(f32)
    col = jnp.tile(jnp.arange(GRID_W), rows).astype(f32)
    nf = head_dim // 4
    inv = ROPE_BASE ** (-jnp.arange(nf, dtype=f32) / nf)
    ang = jnp.stack([row[:, None] * inv, col[:, None] * inv], axis=1)
    return jnp.cos(ang), jnp.sin(ang)


def apply_axial_rope(x, cos, sin):
    shp = x.shape
    d = shp[-1]
    xr = x.astype(f32).reshape(shp[:-1] + (2, 2, d // 4))
    x1, x2 = xr[..., 0, :], xr[..., 1, :]
    bshape = (cos.shape[0],) + (1,) * (x.ndim - 3) + cos.shape[1:]
    c, s = cos.reshape(bshape), sin.reshape(bshape)
    out = jnp.stack([x1 * c - x2 * s, x2 * c + x1 * s], axis=-2)
    return out.reshape(shp).astype(x.dtype)


def seq_rope(n_tok, head_dim):
    inv = 1.0 / (ROPE_BASE ** jnp.linspace(0.0, 1.0, head_dim // 2, dtype=f32))
    ang = jnp.arange(n_tok, dtype=f32)[:, None] * inv
    return jnp.cos(ang), jnp.sin(ang)


def apply_rope_1d(x, cos, sin):
    x1, x2 = jnp.split(x.astype(f32), 2, axis=-1)
    c, s = cos[:, None, :], sin[:, None, :]
    return jnp.concatenate([x1 * c - x2 * s, x2 * c + x1 * s], axis=-1).astype(x.dtype)


def depthwise_conv3(u, w, b):
    y = lax.conv_general_dilated(u, w[:, None, :].astype(u.dtype), window_strides=(1,),
                                 padding=((1, 1),), dimension_numbers=('NWC', 'WIO', 'NWC'),
                                 feature_group_count=u.shape[-1])
    return y + b


def hyena_filters(L, w1, b1, w2, b2, w3, freq):
    bands = (HY_EMB - 1) // 2
    t = jnp.linspace(0.0, 1.0, L, dtype=f32)[:, None]
    w = (2.0 * math.pi / L) * jnp.arange(L, dtype=f32)[:, None]
    fb = jnp.linspace(1e-4, bands - 1, bands, dtype=f32)[None, :]
    z = jnp.concatenate([t, jnp.cos(fb * w), -jnp.sin(fb * w)], axis=-1)
    h = jnp.sin(freq[0] * (z @ w1 + b1))
    h = jnp.sin(freq[1] * (h @ w2 + b2))
    h = (h @ w3).astype(f32).reshape(L, HY_ORDER, 2, HY_CH)
    deltas = jnp.abs(jnp.linspace(HY_MIN_DECAY, HY_MAX_DECAY, HY_CH, dtype=f32))
    h = h * jnp.exp(-t * deltas)[:, None, None, :]
    fwd, bwd = h[:, :, 0], h[:, :, 1]
    full = jnp.concatenate([fwd, jnp.zeros_like(fwd[:1]), bwd[:0:-1]], axis=0)
    return jnp.fft.rfft(full, axis=0)


def hyena_mixer(u, conv_w, conv_b, filt_f, bias):
    L = u.shape[1]
    u = depthwise_conv3(u, conv_w, conv_b)
    v, x1, x2 = jnp.split(u, 3, axis=-1)
    z = v.astype(f32)
    for n, gate in enumerate((x1, x2)):
        zf = jnp.fft.rfft(z, n=2 * L, axis=1)
        z = jnp.fft.irfft(zf * filt_f[None, :, n, :], n=2 * L, axis=1)[:, :L] + z * bias[n].astype(f32)
        z = gate.astype(f32) * z
    return z.astype(u.dtype)


def diff_attention(q_l, k_l, v_l, q_c, k_c, v_c, lam, lam_init, subln):
    B, S = q_l.shape[:2]
    nb = S // BLOCK
    keys = jnp.concatenate([k_c, k_l], axis=1)
    vals = jnp.concatenate([v_c, v_l], axis=1)

    def attend(qb, kk, vv):
        s = jnp.einsum('bqhmd,bkhmd->bhmqk', qb, kk).astype(f32)
        p = jax.nn.softmax(s, axis=-1)
        a = p[:, :, 0] - lam * p[:, :, 1]
        return jnp.einsum('bhqk,bkhe->bqhe', a.astype(vv.dtype), vv)

    qb = jnp.moveaxis(q_l.reshape((B, nb, BLOCK) + q_l.shape[2:]), 1, 0)
    o = lax.map(lambda qq: attend(qq, keys, vals), qb)
    o_l = jnp.moveaxis(o, 0, 1).reshape((B, S) + o.shape[3:])

    def post(o):
        return (rms_norm(o) * subln * (1.0 - lam_init)).reshape(o.shape[:2] + (-1,))

    o_c = post(attend(q_c, k_c, v_c)) if q_c is not None else None
    return post(o_l), o_c


def sink_softmax(sink, *scores):
    sk = jnp.broadcast_to(sink, scores[0].shape[:-1] + (1,))
    p = jax.nn.softmax(jnp.concatenate((sk,) + scores, axis=-1), axis=-1)
    sizes = [t.shape[-1] for t in scores]
    return jnp.split(p[..., 1:], [int(i) for i in np.cumsum(sizes)[:-1]], axis=-1)


def window_gqa(q_l, k_l, v_l, q_c, k_c, v_c, sink):
    B, S = q_l.shape[:2]
    nb = S // BLOCK

    def bands(t):
        tp = jnp.pad(t, ((0, 0), (BLOCK, BLOCK), (0, 0), (0, 0))).reshape(B, nb + 2, BLOCK, GQ_KV, GQ_HD)
        w = jnp.concatenate([tp[:, :-2], tp[:, 1:-1], tp[:, 2:]], axis=2)
        return jnp.moveaxis(w, 1, 0)

    kw, vw = bands(k_l), bands(v_l)
    qb = jnp.moveaxis(q_l.reshape(B, nb, BLOCK, GQ_KV, GQ_GROUP, GQ_HD), 1, 0)
    i = jnp.arange(BLOCK)[:, None]
    j = jnp.arange(3 * BLOCK)[None, :]
    kpos = (jnp.arange(nb) * BLOCK)[:, None, None] - BLOCK + j[None]
    mask = (jnp.abs(j - BLOCK - i) <= WINDOW)[None] & (kpos >= 0) & (kpos < S)
    sink_col = sink.astype(f32).reshape(GQ_KV, GQ_GROUP, 1, 1)

    def block(args):
        qq, kk, vv, mm = args
        s_ctx = jnp.einsum('bqkgd,bckd->bkgqc', qq, k_c).astype(f32)
        s_loc = jnp.where(mm, jnp.einsum('bqkgd,bjkd->bkgqj', qq, kk).astype(f32), NEG_INF)
        p_ctx, p_loc = sink_softmax(sink_col, s_ctx, s_loc)
        return (jnp.einsum('bkgqc,bckd->bqkgd', p_ctx.astype(v_c.dtype), v_c)
                + jnp.einsum('bkgqj,bjkd->bqkgd', p_loc.astype(vv.dtype), vv))

    o = lax.map(block, (qb, kw, vw, mask))
    o_l = jnp.moveaxis(o, 0, 1).reshape(B, S, -1)
    o_c = None
    if q_c is not None:
        s = jnp.einsum('bqkgd,bckd->bkgqc', q_c, k_c).astype(f32)
        (p,) = sink_softmax(sink_col, s)
        o_c = jnp.einsum('bkgqc,bckd->bqkgd', p.astype(v_c.dtype), v_c).reshape(q_c.shape[0], q_c.shape[1], -1)
    return o_l, o_c


def retention_scan(q, k, v, log_g, s0):
    B, L, H, dk = k.shape
    dv = v.shape[-1]
    n = L // RET_CHUNK
    kc = k.reshape(B, n, RET_CHUNK, H, dk)
    vc = v.reshape(B, n, RET_CHUNK, H, dv)
    idx = jnp.arange(RET_CHUNK, dtype=f32)
    zeta = jnp.exp((RET_CHUNK - 1 - idx)[:, None] * log_g[None, :])
    u = jnp.einsum('bnjhd,jh,bnjhe->bnhde', kc, zeta, vc)
    g_chunk = jnp.exp(RET_CHUNK * log_g)[None, :, None, None]

    def step(s, u_i):
        return g_chunk * s + u_i, s

    s_fin, s_prev = lax.scan(step, s0, jnp.moveaxis(u, 1, 0))
    if q is None:
        return None, s_fin
    qc = q.reshape(B, n, RET_CHUNK, H, dk)
    rel = idx[:, None] - idx[None, :]
    dmat = jnp.where(rel >= 0, jnp.exp(jnp.maximum(rel, 0.0)[None] * log_g[:, None, None]), 0.0)
    att = jnp.einsum('bnihd,bnjhd->bnhij', qc, kc) * dmat
    o = jnp.einsum('bnhij,bnjhe->bnihe', att, vc)
    xi = jnp.exp((idx + 1)[:, None] * log_g[None, :])
    o = o + jnp.einsum('bnihd,nbhde->bnihe', qc, s_prev) * xi[None, None, :, :, None]
    return o.reshape(B, L, H, dv), s_fin


def bidir_retention(q_l, k_l, v_l, q_c, k_c, v_c, decay):
    lg = jax.nn.log_sigmoid(decay.astype(f32))
    cast = lambda t: None if t is None else t.astype(f32)
    flip = lambda t: None if t is None else t[:, ::-1]
    q_l, k_l, v_l, q_c, k_c, v_c = map(cast, (q_l, k_l, v_l, q_c, k_c, v_c))
    B, _, H, dk = k_l.shape
    s0 = jnp.zeros((B, H, dk, v_l.shape[-1]), f32)
    o_cf, s_cf = retention_scan(q_c, k_c, v_c, lg[0], s0)
    o_cb, s_cb = retention_scan(flip(q_c), flip(k_c), flip(v_c), lg[1], s0)
    o_lf, _ = retention_scan(q_l, k_l, v_l, lg[0], s_cf)
    o_lb, _ = retention_scan(flip(q_l), flip(k_l), flip(v_l), lg[1], s_cb)
    o_c = o_cf + flip(o_cb) if q_c is not None else None
    return o_lf + flip(o_lb), o_c


def retention_out(o, g, gn_w):
    B, L, H, dv = o.shape
    mu = jnp.mean(o, axis=-1, keepdims=True)
    var = jnp.mean(jnp.square(o - mu), axis=-1, keepdims=True)
    o = ((o - mu) * lax.rsqrt(var + EPS)).reshape(B, L, H * dv)
    return (o * gn_w.astype(f32) * jax.nn.silu(g.astype(f32))).astype(g.dtype)


def even_mixer(ul, uc, need_ctx, lam_init, w_in, w_out, conv_w, conv_b, f_w1, f_b1, f_w2, f_b2, f_w3,
               f_freq, hy_b, q_norm, k_norm, lam_p, subln, cos_ax, sin_ax):
    B, S, _ = ul.shape
    C = uc.shape[1]
    hy_l, q_l, k_l, v_l = split_cols(ul @ w_in, E_SPLIT)
    hy_c, q_c, k_c, v_c = split_cols(uc @ w_in, E_SPLIT)
    y_hy_l = hyena_mixer(hy_l, conv_w, conv_b, hyena_filters(S, f_w1, f_b1, f_w2, f_b2, f_w3, f_freq), hy_b)
    qk = lambda t, g, L: rms_norm(t.reshape(B, L, DA_HEADS, 2, DA_HD)) * g
    scale = DA_HD ** -0.5
    lam_p = lam_p.astype(f32)
    lam = jnp.exp(jnp.sum(lam_p[0] * lam_p[1])) - jnp.exp(jnp.sum(lam_p[2] * lam_p[3])) + lam_init
    ql = apply_axial_rope(qk(q_l, q_norm, S), cos_ax, sin_ax) * scale
    kl = apply_axial_rope(qk(k_l, k_norm, S), cos_ax, sin_ax)
    kc = qk(k_c, k_norm, C)
    vl = v_l.reshape(B, S, DA_HEADS, 2 * DA_HD)
    vc = v_c.reshape(B, C, DA_HEADS, 2 * DA_HD)
    qc = qk(q_c, q_norm, C) * scale if need_ctx else None
    o_l, o_c = diff_attention(ql, kl, vl, qc, kc, vc, lam, lam_init, subln)
    out_l = jnp.concatenate([y_hy_l, o_l], axis=-1) @ w_out
    out_c = None
    if need_ctx:
        y_hy_c = hyena_mixer(hy_c, conv_w, conv_b, hyena_filters(C, f_w1, f_b1, f_w2, f_b2, f_w3, f_freq), hy_b)
        out_c = jnp.concatenate([y_hy_c, o_c], axis=-1) @ w_out
    return out_l, out_c


def odd_mixer(ul, uc, need_ctx, w_in, w_out, decay, gn_w, q_norm, k_norm, sink, cos_ax, sin_ax, cos_rt, sin_rt):
    B, S, _ = ul.shape
    C = uc.shape[1]
    rq_l, rk_l, rv_l, rg_l, gq_l, gk_l, gv_l = split_cols(ul @ w_in, O_SPLIT)
    rq_c, rk_c, rv_c, rg_c, gq_c, gk_c, gv_c = split_cols(uc @ w_in, O_SPLIT)
    hd = lambda t, L, d: t.reshape(B, L, RET_HEADS, d)
    rscale = RET_DK ** -0.5
    rql = apply_rope_1d(hd(rq_l, S, RET_DK), cos_rt, sin_rt)
    rkl = apply_rope_1d(hd(rk_l, S, RET_DK), cos_rt, sin_rt) * rscale
    rqc = hd(rq_c, C, RET_DK) if need_ctx else None
    o_rl, o_rc = bidir_retention(rql, rkl, hd(rv_l, S, RET_DV), rqc, hd(rk_c, C, RET_DK) * rscale,
                                 hd(rv_c, C, RET_DV), decay)
    y_rl = retention_out(o_rl, rg_l, gn_w)
    gscale = GQ_HD ** -0.5
    gql = apply_axial_rope(rms_norm(gq_l.reshape(B, S, GQ_KV, GQ_GROUP, GQ_HD)) * q_norm, cos_ax, sin_ax) * gscale
    gkl = apply_axial_rope(rms_norm(gk_l.reshape(B, S, GQ_KV, GQ_HD)) * k_norm, cos_ax, sin_ax)
    gkc = rms_norm(gk_c.reshape(B, C, GQ_KV, GQ_HD)) * k_norm
    gqc = rms_norm(gq_c.reshape(B, C, GQ_KV, GQ_GROUP, GQ_HD)) * q_norm * gscale if need_ctx else None
    y_gl, y_gc = window_gqa(gql, gkl, gv_l.reshape(B, S, GQ_KV, GQ_HD), gqc, gkc,
                            gv_c.reshape(B, C, GQ_KV, GQ_HD), sink)
    out_l = jnp.concatenate([y_rl, y_gl], axis=-1) @ w_out
    out_c = None
    if need_ctx:
        out_c = jnp.concatenate([retention_out(o_rc, rg_c, gn_w), y_gc], axis=-1) @ w_out
    return out_l, out_c


def hier_moe(t, w_grp, b_grp, w_rt, b_rt, w_gate, w_up, w_down):
    n_tok = t.shape[0]
    lg = (t @ w_grp + b_grp).astype(f32)
    pg = jax.nn.softmax(lg, axis=-1)
    oh_g = jax.nn.one_hot(jnp.argmax(lg, axis=-1), N_GROUPS, dtype=f32)
    le = (t @ w_rt + b_rt).astype(f32).reshape(n_tok, N_GROUPS, EXP_PER_GROUP)
    le_sel = jnp.einsum('tge,tg->te', le, oh_g)
    top_v, top_i = lax.top_k(le_sel, TOP_K)
    w = jax.nn.softmax(top_v, axis=-1) * jnp.max(pg, axis=-1, keepdims=True)
    comb_e = jnp.einsum('tke,tk->te', jax.nn.one_hot(top_i, EXP_PER_GROUP, dtype=f32), w)
    comb = (oh_g[:, :, None] * comb_e[:, None, :]).astype(t.dtype)
    out = jnp.zeros_like(t)
    for g in range(N_GROUPS):
        a = jax.nn.silu(jnp.einsum('td,edf->tef', t, w_gate[g])) * jnp.einsum('td,edf->tef', t, w_up[g])
        out = out + jnp.einsum('tef,efd->td', a * comb[:, g, :, None], w_down[g])
    return out


def setup_inputs(seed: int = 0) -> dict:
    key = jax.random.key(seed)
    ks = iter(jax.random.split(key, 64))

    def nrm(shape, std):
        return jax.random.normal(next(ks), shape, jnp.float32) * std

    D = D_MODEL
    G, E, F = N_GROUPS, EXP_PER_GROUP, D_EXPERT
    ret_base = jnp.log(jnp.exp2(5.0 + jnp.arange(RET_HEADS, dtype=jnp.float32)) - 1.0)
    return {
        "x": nrm((BATCH, SEQ, D), 1.0),
        "c": nrm((BATCH, D), 1.0),
        "ctx": nrm((BATCH, CTX_LEN, D), 1.0),
        "c_ctx": nrm((D,), 1.0),
        "ada_w": nrm((DEPTH, D, 6 * D), 0.5 * D ** -0.5),
        "ada_b": nrm((DEPTH, 6 * D), 0.02),
        "e_w_in": nrm((N_EVEN, D, E_IN), D ** -0.5),
        "e_w_out": nrm((N_EVEN, E_MIX, D), E_MIX ** -0.5),
        "hy_conv_w": nrm((N_EVEN, 3, 3 * HY_CH), 3 ** -0.5),
        "hy_conv_b": nrm((N_EVEN, 3 * HY_CH), 0.02),
        "hy_f_w1": nrm((N_EVEN, HY_EMB, HY_FILT_HID), HY_EMB ** -0.5),
        "hy_f_b1": nrm((N_EVEN, HY_FILT_HID), 0.1),
        "hy_f_w2": nrm((N_EVEN, HY_FILT_HID, HY_FILT_HID), HY_FILT_HID ** -0.5),
        "hy_f_b2": nrm((N_EVEN, HY_FILT_HID), 0.1),
        "hy_f_w3": nrm((N_EVEN, HY_FILT_HID, HY_ORDER * 2 * HY_CH), 0.05 * HY_FILT_HID ** -0.5),
        "hy_f_freq": 1.0 + nrm((N_EVEN, 2, HY_FILT_HID), 0.1),
        "hy_bias": nrm((N_EVEN, HY_ORDER, HY_CH), 0.1),
        "da_q_norm": 1.0 + nrm((N_EVEN, DA_HD), 0.02),
        "da_k_norm": 1.0 + nrm((N_EVEN, DA_HD), 0.02),
        "da_lam": nrm((N_EVEN, 4, DA_HD), 0.1),
        "da_subln": 1.0 + nrm((N_EVEN, 2 * DA_HD), 0.02),
        "o_w_in": nrm((N_ODD, D, O_IN), D ** -0.5),
        "o_w_out": nrm((N_ODD, O_MIX, D), O_MIX ** -0.5),
        "ret_decay": ret_base[None, None, :] + nrm((N_ODD, 2, RET_HEADS), 0.1),
        "ret_gn": 1.0 + nrm((N_ODD, RET_HEADS * RET_DV), 0.02),
        "gq_q_norm": 1.0 + nrm((N_ODD, GQ_HD), 0.02),
        "gq_k_norm": 1.0 + nrm((N_ODD, GQ_HD), 0.02),
        "gq_sink": nrm((N_ODD, GQ_KV * GQ_GROUP), 0.5),
        "moe_w_grp": nrm((DEPTH, D, G), D ** -0.5),
        "moe_b_grp": nrm((DEPTH, G), 0.01),
        "moe_w_rt": nrm((DEPTH, D, G * E), D ** -0.5),
        "moe_b_rt": nrm((DEPTH, G * E), 0.01),
        "moe_w_gate": nrm((DEPTH, G, E, D, F), D ** -0.5),
        "moe_w_up": nrm((DEPTH, G, E, D, F), D ** -0.5),
        "moe_w_down": nrm((DEPTH, G, E, F, D), F ** -0.5),
    }


def reference(x, c, ctx, c_ctx, ada_w, ada_b, e_w_in, e_w_out, hy_conv_w, hy_conv_b, hy_f_w1, hy_f_b1,
              hy_f_w2, hy_f_b2, hy_f_w3, hy_f_freq, hy_bias, da_q_norm, da_k_norm, da_lam, da_subln,
              o_w_in, o_w_out, ret_decay, ret_gn, gq_q_norm, gq_k_norm, gq_sink, moe_w_grp, moe_b_grp,
              moe_w_rt, moe_b_rt, moe_w_gate, moe_w_up, moe_w_down):
    B, S, D = x.shape
    C = ctx.shape[1]
    ROWS = S // GRID_W
    cos_da, sin_da = axial_rope(ROWS, DA_HD)
    cos_gq, sin_gq = axial_rope(ROWS, GQ_HD)
    cos_rt, sin_rt = seq_rope(S, RET_DK)
    silu_c = jax.nn.silu(c)
    silu_cc = jax.nn.silu(c_ctx)
    hl, hc = x, ctx
    for l in range(DEPTH):
        last = l == DEPTH - 1
        i = l // 2
        mod_l = (silu_c @ ada_w[l] + ada_b[l])[:, None, :]
        mod_c = (silu_cc @ ada_w[l] + ada_b[l])[None, None, :]
        sh1, sc1, g1, sh2, sc2, g2 = jnp.split(mod_l, 6, axis=-1)
        csh1, csc1, cg1, csh2, csc2, cg2 = jnp.split(mod_c, 6, axis=-1)
        ul = modulate(rms_norm(hl), sh1, sc1)
        uc = modulate(rms_norm(hc), csh1, csc1)
        if l % 2 == 0:
            lam_init = 0.8 - 0.6 * math.exp(-0.3 * l)
            ml, mc = even_mixer(ul, uc, not last, lam_init, e_w_in[i], e_w_out[i], hy_conv_w[i], hy_conv_b[i],
                                hy_f_w1[i], hy_f_b1[i], hy_f_w2[i], hy_f_b2[i], hy_f_w3[i], hy_f_freq[i],
                                hy_bias[i], da_q_norm[i], da_k_norm[i], da_lam[i], da_subln[i], cos_da, sin_da)
        else:
            ml, mc = odd_mixer(ul, uc, not last, o_w_in[i], o_w_out[i], ret_decay[i], ret_gn[i],
                               gq_q_norm[i], gq_k_norm[i], gq_sink[i], cos_gq, sin_gq, cos_rt, sin_rt)
        hl = hl + g1 * ml
        vl = modulate(rms_norm(hl), sh2, sc2).reshape(B * S, D)
        moe_args = (moe_w_grp[l], moe_b_grp[l], moe_w_rt[l], moe_b_rt[l], moe_w_gate[l], moe_w_up[l], moe_w_down[l])
        if last:
            hl = hl + g2 * hier_moe(vl, *moe_args).reshape(B, S, D)
        else:
            hc = hc + cg1 * mc
            vc = modulate(rms_norm(hc), csh2, csc2).reshape(B * C, D)
            y = hier_moe(jnp.concatenate([vl, vc], axis=0), *moe_args)
            hl = hl + g2 * y[:B * S].reshape(B, S, D)
            hc = hc + cg2 * y[B * S:].reshape(B, C, D)
    return hl
```

```python
import functools
import math

import numpy as np
import jax
import jax.numpy as jnp
from jax import lax
from jax.experimental import pallas as pl
from jax.experimental.pallas import tpu as pltpu

f32 = jnp.float32
bf16 = jnp.bfloat16

D_MODEL = 1024
BATCH = 8
SEQ = 2048
DEPTH = 2
GRID_W = 64
CTX_LEN = 256
EPS = 1e-6
NEG_INF = -1e30
ROPE_BASE = 10000.0
HY_CH = D_MODEL // 2
HY_EMB = 33
HY_FILT_HID = 64
HY_MAX_DECAY = math.log(1e-2) / 0.3
HY_MIN_DECAY = math.log(1e-2) / 1.5
DA_HEADS = 4
DA_HD = D_MODEL // 16
RET_HEADS = 4
RET_DK = D_MODEL // 8
RET_CHUNK = 128
GQ_KV = 2
GQ_GROUP = 4
GQ_HD = D_MODEL // 16
WINDOW = 128
N_GROUPS = 4
EXP_PER_GROUP = 8
N_EXPERTS = N_GROUPS * EXP_PER_GROUP
D_EXPERT = D_MODEL // 4

T_LAT = BATCH * SEQ
T_CTX = BATCH * CTX_LEN
T_ALL = T_LAT + T_CTX
PROJ_W = 3072
SEG = 512
CTX_MOD_ROW = BATCH
MOD_ROWS = 16

LANES = 128
VMEM_LIMIT_BYTES = 56 * 1024 * 1024


def _cparams(*sem):
    return pltpu.CompilerParams(dimension_semantics=sem, vmem_limit_bytes=VMEM_LIMIT_BYTES)


def _dot(a, b):
    return jnp.dot(a, b, preferred_element_type=f32)


def _dot_nt(a, b):
    return lax.dot_general(a, b, (((1,), (1,)), ((), ())), preferred_element_type=f32)


def _split(x):
    hi = x.astype(bf16)
    lo = (x - hi.astype(f32)).astype(bf16)
    return hi, lo


def _dot3(a, b):
    ah, al = _split(a)
    bh, bl = _split(b)
    return _dot(ah, bh) + _dot(al, bh) + _dot(ah, bl)


def _dot2(a, b_bf16):
    ah, al = _split(a)
    return _dot(ah, b_bf16) + _dot(al, b_bf16)


def _silu(x):
    return x * jax.nn.sigmoid(x)


def _rms(x):
    return x * lax.rsqrt(jnp.mean(x * x, axis=-1, keepdims=True) + EPS)


def _const_spec(shape):
    nd = len(shape)
    return pl.BlockSpec(shape, lambda *_: (0,) * nd)


def _const_spec1(shape):
    nd = len(shape)
    return pl.BlockSpec(shape, lambda *_: (0,) * nd, pipeline_mode=pl.Buffered(1))


ADA_TN = 1536


def _ada_kernel(c_ref, w_ref, b_ref, o_ref):
    x = _silu(c_ref[...])
    o_ref[0] = _dot3(x, w_ref[0]) + b_ref[0]


def ada_modulation(c_rows, ada_w, ada_b):
    n = 6 * D_MODEL
    return pl.pallas_call(
        _ada_kernel,
        grid=(DEPTH, n // ADA_TN),
        in_specs=[
            pl.BlockSpec((MOD_ROWS, D_MODEL), lambda l, j: (0, 0)),
            pl.BlockSpec((1, D_MODEL, ADA_TN), lambda l, j: (l, 0, j)),
            pl.BlockSpec((1, 1, ADA_TN), lambda l, j: (l, 0, j)),
        ],
        out_specs=pl.BlockSpec((1, MOD_ROWS, ADA_TN), lambda l, j: (l, 0, j)),
        out_shape=jax.ShapeDtypeStruct((DEPTH, MOD_ROWS, n), f32),
        compiler_params=_cparams("arbitrary", "arbitrary"),
        name="ada_modulation",
    )(c_rows, ada_w, ada_b.reshape(DEPTH, 1, n))


PROJ_TM = 512


def _mod_row(i, tm):
    return jnp.minimum((i * tm) // SEQ, CTX_MOD_ROW)


def _tile4(t):
    return jnp.concatenate([t, t, t, t], axis=1)


def _group_norm64(y, gmat):
    ms = _dot2(y * y, gmat)
    return y * lax.rsqrt(ms + EPS)


def _rope(y, tab, shift):
    w = y.shape[1]
    return y * tab[0] + pltpu.roll(y, shift, 1) * tab[1] + pltpu.roll(y, w - shift, 1) * tab[2]


def _inproj_kernel(layer_kind, h_ref, mod_ref, w_ref, gain_ref, gmat_ref, *rest):
    if layer_kind == "even":
        ax_ref, o_ref = rest
    else:
        ax_ref, r1_ref, o_ref = rest
    i = pl.program_id(0)
    r = _mod_row(i, PROJ_TM)
    sh = mod_ref[pl.ds(r, 1), pl.ds(0, D_MODEL)]
    sc = mod_ref[pl.ds(r, 1), pl.ds(D_MODEL, D_MODEL)]
    u = (_rms(h_ref[...]) * (1.0 + sc) + sh).astype(bf16)

    def seg(j):
        return _dot(u, w_ref[:, j * SEG:(j + 1) * SEG])

    def put(j, y):
        o_ref[:, j * SEG:(j + 1) * SEG] = y.astype(bf16)

    def gain(j):
        return gain_ref[:, j * SEG:(j + 1) * SEG]

    gmat = gmat_ref[...]
    ax = ax_ref[...]
    ax4 = (_tile4(ax[0]), _tile4(ax[1]), _tile4(ax[2]))
    if layer_kind == "even":
        for j in (0, 1, 2, 5):
            put(j, seg(j))
        for j in (3, 4):
            put(j, _rope(_group_norm64(seg(j), gmat) * gain(j), ax4, DA_HD // 4))
    else:
        r1 = r1_ref[...]
        r14 = (_tile4(r1[0]), _tile4(r1[1]), _tile4(r1[2]))
        for j in (0, 1):
            put(j, _rope(seg(j) * gain(j), r14, RET_DK // 2))
        for j in (2, 3):
            put(j, seg(j))
        put(4, _rope(_group_norm64(seg(4), gmat) * gain(4), ax4, GQ_HD // 4))
        y = seg(5)
        kw = GQ_KV * GQ_HD
        yk = _rope(_group_norm64(y[:, :kw], gmat[:kw, :kw]) * gain(5)[:, :kw], ax, GQ_HD // 4)
        o_ref[:, 5 * SEG:5 * SEG + kw] = yk.astype(bf16)
        o_ref[:, 5 * SEG + kw:6 * SEG] = y[:, kw:].astype(bf16)


def in_projection(layer_kind, h, mod, w, gain, gmat, tables):
    n_lat_tiles = T_LAT // PROJ_TM
    n_pos_tiles = SEQ // PROJ_TM

    def tab_map(i):
        return (0, jnp.where(i < n_lat_tiles, i % n_pos_tiles, n_pos_tiles), 0)

    tab_specs = [pl.BlockSpec((3, PROJ_TM, LANES), tab_map) for _ in tables]
    return pl.pallas_call(
        functools.partial(_inproj_kernel, layer_kind),
        grid=(T_ALL // PROJ_TM,),
        in_specs=[
            pl.BlockSpec((PROJ_TM, D_MODEL), lambda i: (i, 0)),
            _const_spec((MOD_ROWS, 6 * D_MODEL)),
            _const_spec((D_MODEL, PROJ_W)),
            _const_spec((1, PROJ_W)),
            _const_spec((SEG, SEG)),
        ] + tab_specs,
        out_specs=pl.BlockSpec((PROJ_TM, PROJ_W), lambda i: (i, 0)),
        out_shape=jax.ShapeDtypeStruct((T_ALL, PROJ_W), bf16),
        compiler_params=_cparams("parallel"),
        name="in_projection_" + layer_kind,
    )(h, mod, w, gain, gmat, *tables)


HY_TC = 256
HY_FREQ_CHUNK = 512
FEAT_PAD = 64


def _alt_sign(shape, axis):
    idx = lax.broadcasted_iota(jnp.int32, shape, axis)
    return jnp.where((idx & 1) == 0, 1.0, -1.0).astype(f32)


def _filter_kernel(L, z_ref, w1_ref, b1_ref, w2_ref, b2_ref, wf_ref, wb_ref, freq_ref, dec_ref, c_ref, s_ref,
                   spec_ref, nyq_ref):
    hid = jnp.sin(freq_ref[0:1, :] * (_dot3(z_ref[...], w1_ref[...]) + b1_ref[...]))
    hid = jnp.sin(freq_ref[1:2, :] * (_dot3(hid, w2_ref[...]) + b2_ref[...]))
    dec = dec_ref[...]
    fwd = _dot3(hid, wf_ref[0]) * dec
    bwd = _dot3(hid, wb_ref[0]) * dec
    row = lax.broadcasted_iota(jnp.int32, fwd.shape, 0)
    bwd = jnp.where(row == 0, 0.0, bwd)
    even = fwd + bwd
    odd = bwd - fwd
    wk = jnp.where(row == 0, 0.5 / L, 1.0 / L).astype(f32)
    spec_ref[0, 0] = _dot(c_ref[...], even.astype(bf16)) * wk
    spec_ref[0, 1] = _dot(s_ref[...], odd.astype(bf16)) * wk
    nyq = jnp.sum(even * _alt_sign(even.shape, 0), axis=0, keepdims=True) * (0.5 / L)
    nyq_ref[0] = jnp.broadcast_to(nyq, (8, nyq.shape[1]))


def hyena_filter_spectra(L, zfeat, w1, b1, w2, b2, w3r, freq, decay, cmat, smat):
    nct = HY_CH // HY_TC
    return pl.pallas_call(
        functools.partial(_filter_kernel, L),
        grid=(2, nct),
        in_specs=[
            _const_spec((L, FEAT_PAD)),
            _const_spec((FEAT_PAD, HY_FILT_HID)),
            _const_spec((1, HY_FILT_HID)),
            _const_spec((HY_FILT_HID, HY_FILT_HID)),
            _const_spec((1, HY_FILT_HID)),
            pl.BlockSpec((1, HY_FILT_HID, HY_TC), lambda n, c: (2 * n, 0, c)),
            pl.BlockSpec((1, HY_FILT_HID, HY_TC), lambda n, c: (2 * n + 1, 0, c)),
            _const_spec((2, HY_FILT_HID)),
            pl.BlockSpec((L, HY_TC), lambda n, c: (0, c)),
            _const_spec1((L, L)),
            _const_spec1((L, L)),
        ],
        out_specs=[
            pl.BlockSpec((1, 2, L, HY_TC), lambda n, c: (n, 0, 0, c)),
            pl.BlockSpec((1, 8, HY_TC), lambda n, c: (n, 0, c)),
        ],
        out_shape=[
            jax.ShapeDtypeStruct((2, 2, L, HY_CH), f32),
            jax.ShapeDtypeStruct((2, 8, HY_CH), f32),
        ],
        compiler_params=_cparams("arbitrary", "arbitrary"),
        name="hyena_filter_L%d" % L,
    )(zfeat, w1, b1, w2, b2, w3r, w3r, freq, decay, cmat, smat)


def _conv3(u, w, b):
    L = u.shape[0]
    row = lax.broadcasted_iota(jnp.int32, u.shape, 0)
    prev = jnp.where(row == 0, 0.0, pltpu.roll(u, 1, 0))
    nxt = jnp.where(row == L - 1, 0.0, pltpu.roll(u, L - 1, 0))
    return prev * w[0:1, :] + u * w[1:2, :] + nxt * w[2:3, :] + b


def _hyena_kernel(v_ref, x1_ref, x2_ref, wv_ref, w1_ref, w2_ref, bv_ref, b1_ref, b2_ref, spec_ref, nyq_ref,
                  bias_ref, c_ref, s_ref, *rest):
    o_ref, yr_ref, yi_ref = rest[-3:]
    L = v_ref.shape[0]
    fch = min(L, HY_FREQ_CHUNK)
    z = _conv3(v_ref[...].astype(f32), wv_ref[...], bv_ref[...])
    gate_refs = ((x1_ref, w1_ref, b1_ref), (x2_ref, w2_ref, b2_ref))
    alt = _alt_sign(z.shape, 0)
    for n in range(2):
        zb = z.astype(bf16)
        for k in range(L // fch):
            rows = slice(k * fch, (k + 1) * fch)
            a = _dot(c_ref[rows, :], zb)
            b = _dot(s_ref[rows, :], zb)
            hr = spec_ref[n, 0, rows, :]
            hi = spec_ref[n, 1, rows, :]
            yr_ref[rows, :] = (a * hr + b * hi).astype(bf16)
            yi_ref[rows, :] = (a * hi - b * hr).astype(bf16)
        x_nyq = jnp.sum(z * alt, axis=0, keepdims=True)
        y = (_dot(c_ref[...], yr_ref[...]) - _dot(s_ref[...], yi_ref[...])
             + alt * (x_nyq * nyq_ref[n, 0:1, :]))
        x_ref, w_ref, b_ref = gate_refs[n]
        gate = _conv3(x_ref[...].astype(f32), w_ref[...], b_ref[...])
        z = gate * (y + z * bias_ref[n:n + 1, :])
    o_ref[...] = z.astype(bf16)


def hyena_mix(L, row_block0, proj, conv_w, conv_b, spec, nyq, bias, cmat, smat, prev_out=None):
    nct = HY_CH // HY_TC
    nseg = HY_CH // HY_TC

    def col(k):
        return lambda c, b: (row_block0 + b, k * nseg + c)

    def par(k):
        return lambda c, b: (0, k * nseg + c)

    in_specs = (
        [pl.BlockSpec((L, HY_TC), col(k)) for k in range(3)]
        + [pl.BlockSpec((3, HY_TC), par(k)) for k in range(3)]
        + [pl.BlockSpec((1, HY_TC), par(k)) for k in range(3)]
        + [
            pl.BlockSpec((2, 2, L, HY_TC), lambda c, b: (0, 0, 0, c), pipeline_mode=pl.Buffered(1)),
            pl.BlockSpec((2, 8, HY_TC), lambda c, b: (0, 0, c)),
            pl.BlockSpec((2, HY_TC), lambda c, b: (0, c)),
            _const_spec1((L, L)),
            _const_spec1((L, L)),
        ]
    )
    args = [proj, proj, proj, conv_w, conv_w, conv_w, conv_b, conv_b, conv_b, spec, nyq, bias, cmat, smat]
    aliases = {}
    if prev_out is not None:
        in_specs.append(pl.BlockSpec(memory_space=pl.ANY))
        args.append(prev_out)
        aliases = {len(args) - 1: 0}
    return pl.pallas_call(
        _hyena_kernel,
        grid=(nct, BATCH),
        in_specs=in_specs,
        out_specs=pl.BlockSpec((L, HY_TC), lambda c, b: (row_block0 + b, c)),
        out_shape=jax.ShapeDtypeStruct((T_ALL, HY_CH), bf16),
        input_output_aliases=aliases,
        scratch_shapes=[pltpu.VMEM((L, HY_TC), bf16), pltpu.VMEM((L, HY_TC), bf16)],
        compiler_params=_cparams("arbitrary", "arbitrary"),
        name="hyena_mix_L%d" % L,
    )(*args)


DA_TQ = 256


def _diff_attn_kernel(lam_init, n_src, q_ref, *rest):
    kv_refs = rest[:2 * n_src]
    lam_ref, subln_ref = rest[2 * n_src:2 * n_src + 2]
    o_ref = rest[-1]
    lp = lam_ref[...]
    lam = (jnp.exp(jnp.sum(lp[0:1] * lp[1:2], axis=-1, keepdims=True))
           - jnp.exp(jnp.sum(lp[2:3] * lp[3:4], axis=-1, keepdims=True)) + lam_init)
    q = q_ref[...]
    lane = lax.broadcasted_iota(jnp.int32, (q.shape[0], 2 * DA_HD), 1)
    hw = 2 * DA_HD
    for h in range(DA_HEADS):
        qh = q[:, h * hw:(h + 1) * hw]
        ks = [kv_refs[2 * s][:, h * hw:(h + 1) * hw] for s in range(n_src)]
        vs = [kv_refs[2 * s + 1][:, h * hw:(h + 1) * hw] for s in range(n_src)]
        es = []
        invs = []
        for m in range(2):
            qm = jnp.where((lane < DA_HD) == (m == 0), qh, jnp.zeros_like(qh))
            ss = [_dot_nt(qm, k) for k in ks]
            mx = functools.reduce(jnp.maximum, [jnp.max(s, axis=-1, keepdims=True) for s in ss])
            e = [jnp.exp(s - mx) for s in ss]
            den = functools.reduce(jnp.add, [jnp.sum(x, axis=-1, keepdims=True) for x in e])
            es.append(e)
            invs.append(1.0 / den)
        c0 = invs[0]
        c1 = lam * invs[1]
        oh = functools.reduce(jnp.add, [
            _dot((es[0][s] * c0 - es[1][s] * c1).astype(bf16), vs[s]) for s in range(n_src)])
        oh = _rms(oh) * subln_ref[...] * (1.0 - lam_init)
        o_ref[:, h * hw:(h + 1) * hw] = oh.astype(bf16)


def diff_attention(proj, lam_p, subln, lam_init, queries, prev_out=None):
    width = DA_HEADS * 2 * DA_HD
    qcol, kcol, vcol = 3, 4, 5
    ctx_blk0 = T_LAT // CTX_LEN
    ctx_k = pl.BlockSpec((CTX_LEN, width), lambda b, i: (ctx_blk0 + b, kcol))
    ctx_v = pl.BlockSpec((CTX_LEN, width), lambda b, i: (ctx_blk0 + b, vcol))
    if queries == "latent":
        nq = SEQ // DA_TQ
        q_map = lambda b, i: (b * nq + i, qcol)
        o_map = lambda b, i: (b * nq + i, 0)
        kv_specs = [ctx_k, ctx_v,
                    pl.BlockSpec((SEQ, width), lambda b, i: (b, kcol)),
                    pl.BlockSpec((SEQ, width), lambda b, i: (b, vcol))]
        n_src = 2
    else:
        nq = CTX_LEN // DA_TQ
        q_map = lambda b, i: (ctx_blk0 + b * nq + i, qcol)
        o_map = lambda b, i: (ctx_blk0 + b * nq + i, 0)
        kv_specs = [ctx_k, ctx_v]
        n_src = 1
    in_specs = [pl.BlockSpec((DA_TQ, width), q_map)] + kv_specs + [
        _const_spec((4, DA_HD)), _const_spec((1, 2 * DA_HD))]
    args = [proj] * (1 + 2 * n_src) + [lam_p, subln]
    aliases = {}
    if prev_out is not None:
        in_specs.append(pl.BlockSpec(memory_space=pl.ANY))
        args.append(prev_out)
        aliases = {len(args) - 1: 0}
    return pl.pallas_call(
        functools.partial(_diff_attn_kernel, lam_init, n_src),
        grid=(BATCH, nq),
        in_specs=in_specs,
        out_specs=pl.BlockSpec((DA_TQ, width), o_map),
        out_shape=jax.ShapeDtypeStruct((T_ALL, width), bf16),
        input_output_aliases=aliases,
        compiler_params=_cparams("parallel", "arbitrary"),
        name="diff_attention_" + queries,
    )(*args)


def _log_sigmoid(x):
    return jnp.minimum(x, 0.0) - jnp.log(1.0 + jnp.exp(-jnp.abs(x)))


def _retention_kernel(q_ref, k_ref, v_ref, g_ref, kc_ref, vc_ref, decay_ref, gn_ref, o_ref, sf_ref):
    h = pl.program_id(1)
    ch = RET_CHUNK
    nchunk = SEQ // ch
    lgs = _log_sigmoid(decay_ref[...])
    sel = lax.broadcasted_iota(jnp.int32, lgs.shape, 1) == h
    lg = jnp.sum(jnp.where(sel, lgs, 0.0), axis=-1, keepdims=True)
    lgf = lg[0:1, :]
    lgb = lg[1:2, :]
    ri = lax.broadcasted_iota(jnp.int32, (ch, ch), 0).astype(f32)
    ci = lax.broadcasted_iota(jnp.int32, (ch, ch), 1).astype(f32)
    rel = ri - ci
    dsum = (jnp.where(rel >= 0, jnp.exp(jnp.maximum(rel, 0.0) * lgf), 0.0)
            + jnp.where(rel <= 0, jnp.exp(jnp.maximum(-rel, 0.0) * lgb), 0.0))
    zeta_f = jnp.exp((ch - 1 - ri) * lgf)
    zeta_b = jnp.exp(ri * lgb)
    xi_f = jnp.exp((ri + 1.0) * lgf)
    xi_b = jnp.exp((ch - ri) * lgb)
    gch_f = jnp.exp(ch * lgf)
    gch_b = jnp.exp(ch * lgb)

    def kv_state(kw, v):
        return _dot(kw.T.astype(bf16), v)

    kc = kc_ref[...].astype(f32)
    vc = vc_ref[...]
    cr = lax.broadcasted_iota(jnp.int32, kc.shape, 0).astype(f32)
    s_f0 = kv_state(kc * jnp.exp((CTX_LEN - 1 - cr) * lgf), vc)
    s_b0 = kv_state(kc * jnp.exp(cr * lgb), vc)

    def chunk(n):
        return pl.ds(pl.multiple_of(n * ch, ch), ch)

    def fwd(n, s):
        sf_ref[n] = s
        kn = k_ref[chunk(n), :].astype(f32)
        return gch_f * s + kv_state(kn * zeta_f, v_ref[chunk(n), :])

    lax.fori_loop(0, nchunk, fwd, s_f0)

    gn = gn_ref[...]

    def bwd(t, s):
        n = nchunk - 1 - t
        qn = q_ref[chunk(n), :]
        kn = k_ref[chunk(n), :]
        vn = v_ref[chunk(n), :]
        att = _dot_nt(qn, kn) * dsum
        o = (_dot(att.astype(bf16), vn)
             + xi_f * _dot(qn, sf_ref[n].astype(bf16))
             + xi_b * _dot(qn, s.astype(bf16)))
        mu = jnp.mean(o, axis=-1, keepdims=True)
        oc = o - mu
        var = jnp.mean(oc * oc, axis=-1, keepdims=True)
        y = oc * lax.rsqrt(var + EPS) * gn * _silu(g_ref[chunk(n), :].astype(f32))
        o_ref[chunk(n), :] = y.astype(bf16)
        return gch_b * s + kv_state(kn.astype(f32) * zeta_b, vn)

    lax.fori_loop(0, nchunk, bwd, s_b0)


def retention(proj, decay, gn_w):
    dk = RET_DK
    ctx_blk0 = T_LAT // CTX_LEN
    return pl.pallas_call(
        _retention_kernel,
        grid=(BATCH, RET_HEADS),
        in_specs=[
            pl.BlockSpec((SEQ, dk), lambda b, h: (b, h)),
            pl.BlockSpec((SEQ, dk), lambda b, h: (b, RET_HEADS + h)),
            pl.BlockSpec((SEQ, dk), lambda b, h: (b, 2 * RET_HEADS + h)),
            pl.BlockSpec((SEQ, dk), lambda b, h: (b, 3 * RET_HEADS + h)),
            pl.BlockSpec((CTX_LEN, dk), lambda b, h: (ctx_blk0 + b, RET_HEADS + h)),
            pl.BlockSpec((CTX_LEN, dk), lambda b, h: (ctx_blk0 + b, 2 * RET_HEADS + h)),
            _const_spec((2, RET_HEADS)),
            pl.BlockSpec((1, dk), lambda b, h: (0, h)),
        ],
        out_specs=pl.BlockSpec((SEQ, dk), lambda b, h: (b, h)),
        out_shape=jax.ShapeDtypeStruct((T_LAT, RET_HEADS * dk), bf16),
        scratch_shapes=[pltpu.VMEM((SEQ // RET_CHUNK, dk, dk), f32)],
        compiler_params=_cparams("parallel", "arbitrary"),
        name="retention",
    )(proj, proj, proj, proj, proj, proj, decay, gn_w)


GQ_TQ = 128
GQ_SPAN = 3 * GQ_TQ


def _gqa_kernel(q_ref, kv_ref, kvc_ref, sink_ref, o_ref):
    n = pl.program_id(1)
    start = pl.multiple_of(jnp.clip((n - 1) * GQ_TQ, 0, SEQ - GQ_SPAN), GQ_TQ)
    kw = GQ_KV * GQ_HD
    kl_all = kv_ref[pl.ds(start, GQ_SPAN), pl.ds(0, kw)]
    vl_all = kv_ref[pl.ds(start, GQ_SPAN), pl.ds(kw, kw)]
    kc_all = kvc_ref[:, 0:kw]
    vc_all = kvc_ref[:, kw:2 * kw]
    qpos = n * GQ_TQ + lax.broadcasted_iota(jnp.int32, (GQ_TQ, GQ_SPAN), 0)
    kpos = start + lax.broadcasted_iota(jnp.int32, (GQ_TQ, GQ_SPAN), 1)
    mask = jnp.abs(kpos - qpos) <= WINDOW
    q = q_ref[...]
    hd = GQ_HD
    for kv in range(GQ_KV):
        kl = kl_all[:, kv * hd:(kv + 1) * hd]
        vl = vl_all[:, kv * hd:(kv + 1) * hd]
        kc = kc_all[:, kv * hd:(kv + 1) * hd]
        vc = vc_all[:, kv * hd:(kv + 1) * hd]
        for g in range(GQ_GROUP):
            hidx = kv * GQ_GROUP + g
            qh = q[:, hidx * hd:(hidx + 1) * hd]
            sink = sink_ref[hidx]
            s_c = _dot_nt(qh, kc)
            s_l = jnp.where(mask, _dot_nt(qh, kl), NEG_INF)
            mx = jnp.maximum(jnp.maximum(jnp.max(s_c, axis=-1, keepdims=True),
                                         jnp.max(s_l, axis=-1, keepdims=True)), sink)
            e_c = jnp.exp(s_c - mx)
            e_l = jnp.exp(s_l - mx)
            den = (jnp.sum(e_c, axis=-1, keepdims=True) + jnp.sum(e_l, axis=-1, keepdims=True)
                   + jnp.exp(sink - mx))
            inv = 1.0 / den
            o = _dot((e_c * inv).astype(bf16), vc) + _dot((e_l * inv).astype(bf16), vl)
            o_ref[:, hidx * hd:(hidx + 1) * hd] = o.astype(bf16)


def window_gqa(proj, sink):
    width = GQ_KV * GQ_GROUP * GQ_HD
    nq = SEQ // GQ_TQ
    kvw = 2 * GQ_KV * GQ_HD
    kv_col = (5 * SEG) // kvw
    ctx_blk0 = T_LAT // CTX_LEN
    return pl.pallas_call(
        _gqa_kernel,
        grid=(BATCH, nq),
        in_specs=[
            pl.BlockSpec((GQ_TQ, width), lambda b, n: (b * nq + n, 4)),
            pl.BlockSpec((SEQ, kvw), lambda b, n: (b, kv_col)),
            pl.BlockSpec((CTX_LEN, kvw), lambda b, n: (ctx_blk0 + b, kv_col)),
            pl.BlockSpec(memory_space=pltpu.SMEM),
        ],
        out_specs=pl.BlockSpec((GQ_TQ, width), lambda b, n: (b * nq + n, 0)),
        out_shape=jax.ShapeDtypeStruct((T_LAT, width), bf16),
        compiler_params=_cparams("parallel", "arbitrary"),
        name="window_gqa",
    )(proj, proj, proj, sink)


OUT_TM = 512
ROUTE_W = LANES


def _route(logits):
    lane_i = lax.broadcasted_iota(jnp.int32, logits.shape, 1)
    lane = lane_i.astype(f32)
    big = float(1 << 20)
    valid = lane_i < N_EXPERTS
    le = logits
    lgx = pltpu.roll(logits, ROUTE_W - N_EXPERTS, 1)
    lgx = jnp.where(valid, lgx, NEG_INF)
    gmax = jnp.max(lgx, axis=-1, keepdims=True)
    grp = (lane_i // EXP_PER_GROUP).astype(f32)
    g_sel = jnp.min(jnp.where(lgx == gmax, grp, big), axis=-1, keepdims=True)
    p_grp = float(EXP_PER_GROUP) / jnp.sum(jnp.exp(lgx - gmax), axis=-1, keepdims=True)
    lm = jnp.where(valid, jnp.where(grp == g_sel, le, NEG_INF), NEG_INF)
    v1 = jnp.max(lm, axis=-1, keepdims=True)
    i1 = jnp.min(jnp.where(lm == v1, lane, big), axis=-1, keepdims=True)
    lm2 = jnp.where(lane == i1, NEG_INF, lm)
    v2 = jnp.max(lm2, axis=-1, keepdims=True)
    i2 = jnp.min(jnp.where(lm2 == v2, lane, big), axis=-1, keepdims=True)
    e2 = jnp.exp(v2 - v1)
    w1 = p_grp / (1.0 + e2)
    w2 = w1 * e2
    return jnp.where(lane == i1, w1, 0.0) + jnp.where(lane == i2, w2, 0.0)


def _outproj_kernel(ya_ref, yb_ref, w_ref, h_ref, mod_ref, wr_ref, br_ref, hn_ref, v_ref, comb_ref):
    i = pl.program_id(0)
    r = _mod_row(i, OUT_TM)
    g1 = mod_ref[pl.ds(r, 1), pl.ds(2 * D_MODEL, D_MODEL)]
    sh2 = mod_ref[pl.ds(r, 1), pl.ds(3 * D_MODEL, D_MODEL)]
    sc2 = mod_ref[pl.ds(r, 1), pl.ds(4 * D_MODEL, D_MODEL)]
    half = ya_ref.shape[1]
    m = _dot(ya_ref[...], w_ref[0:half, :]) + _dot(yb_ref[...], w_ref[half:2 * half, :])
    hn = h_ref[...] + g1 * m
    hn_ref[...] = hn
    v = _rms(hn) * (1.0 + sc2) + sh2
    v_ref[...] = v.astype(bf16)
    comb_ref[...] = _route(_dot3(v, wr_ref[...]) + br_ref[...])


def out_projection(n_rows, ya, yb, w, h, mod, wr, br):
    half = ya.shape[1]
    return pl.pallas_call(
        _outproj_kernel,
        grid=(n_rows // OUT_TM,),
        in_specs=[
            pl.BlockSpec((OUT_TM, half), lambda i: (i, 0)),
            pl.BlockSpec((OUT_TM, half), lambda i: (i, 0)),
            _const_spec((2 * half, D_MODEL)),
            pl.BlockSpec((OUT_TM, D_MODEL), lambda i: (i, 0)),
            _const_spec((MOD_ROWS, 6 * D_MODEL)),
            _const_spec((D_MODEL, ROUTE_W)),
            _const_spec((1, ROUTE_W)),
        ],
        out_specs=[
            pl.BlockSpec((OUT_TM, D_MODEL), lambda i: (i, 0)),
            pl.BlockSpec((OUT_TM, D_MODEL), lambda i: (i, 0)),
            pl.BlockSpec((OUT_TM, ROUTE_W), lambda i: (i, 0)),
        ],
        out_shape=[
            jax.ShapeDtypeStruct((n_rows, D_MODEL), f32),
            jax.ShapeDtypeStruct((n_rows, D_MODEL), bf16),
            jax.ShapeDtypeStruct((n_rows, ROUTE_W), f32),
        ],
        compiler_params=_cparams("parallel"),
        name="out_projection",
    )(ya, yb, w, h, mod, wr, br)


MOE_TM = 1024


def _moe_kernel(x_ref, wgu_ref, wd_ref, comb_ref, h_ref, mod_ref, o_ref, acc_ref):
    i = pl.program_id(0)
    e = pl.program_id(1)

    @pl.when(e == 0)
    def _():
        acc_ref[...] = jnp.zeros_like(acc_ref)

    gu = _dot(x_ref[...], wgu_ref[0])
    comb = comb_ref[...]
    lane = lax.broadcasted_iota(jnp.int32, comb.shape, 1)
    ce = jnp.sum(jnp.where(lane == e, comb, 0.0), axis=-1, keepdims=True)
    a = _silu(gu[:, :D_EXPERT]) * gu[:, D_EXPERT:] * ce
    acc_ref[...] += _dot(a.astype(bf16), wd_ref[0])

    @pl.when(e == N_EXPERTS - 1)
    def _():
        r = _mod_row(i, MOE_TM)
        g2 = mod_ref[pl.ds(r, 1), pl.ds(5 * D_MODEL, D_MODEL)]
        o_ref[...] = h_ref[...] + g2 * acc_ref[...]


def moe(n_rows, x, wgu, wd, comb, h, mod):
    return pl.pallas_call(
        _moe_kernel,
        grid=(n_rows // MOE_TM, N_EXPERTS),
        in_specs=[
            pl.BlockSpec((MOE_TM, D_MODEL), lambda i, e: (i, 0)),
            pl.BlockSpec((1, D_MODEL, 2 * D_EXPERT), lambda i, e: (e, 0, 0)),
            pl.BlockSpec((1, D_EXPERT, D_MODEL), lambda i, e: (e, 0, 0)),
            pl.BlockSpec((MOE_TM, ROUTE_W), lambda i, e: (i, 0)),
            pl.BlockSpec((MOE_TM, D_MODEL), lambda i, e: (i, 0)),
            _const_spec((MOD_ROWS, 6 * D_MODEL)),
        ],
        out_specs=pl.BlockSpec((MOE_TM, D_MODEL), lambda i, e: (i, 0)),
        out_shape=jax.ShapeDtypeStruct((n_rows, D_MODEL), f32),
        scratch_shapes=[pltpu.VMEM((MOE_TM, D_MODEL), f32)],
        compiler_params=_cparams("parallel", "arbitrary"),
        name="moe",
    )(x, wgu, wd, comb, h, mod)


def _dft_tables(L):
    k = np.arange(L, dtype=np.int64)
    ang = (2.0 * np.pi / (2 * L)) * ((k[:, None] * k[None, :]) % (2 * L)).astype(np.float64)
    return np.cos(ang).astype(np.float32), np.sin(ang).astype(np.float32)


def _filter_features(L):
    bands = (HY_EMB - 1) // 2
    t = np.linspace(0.0, 1.0, L, dtype=np.float32).astype(np.float64)[:, None]
    w = (2.0 * np.pi / L) * np.arange(L, dtype=np.float64)[:, None]
    fb = np.linspace(1e-4, bands - 1, bands, dtype=np.float32).astype(np.float64)[None, :]
    z = np.concatenate([t, np.cos(fb * w), -np.sin(fb * w)], axis=-1)
    zp = np.zeros((L, FEAT_PAD), np.float32)
    zp[:, :HY_EMB] = z
    deltas = np.abs(np.linspace(HY_MIN_DECAY, HY_MAX_DECAY, HY_CH, dtype=np.float32).astype(np.float64))
    decay = np.exp(-t * deltas[None, :]).astype(np.float32)
    return zp, decay


def _rope_table(cos, sin, half, tm):
    S, width = cos.shape
    low = (np.arange(width) % (2 * half)) < half
    tab = np.zeros((3, S + tm, width), np.float32)
    tab[0, :S] = cos
    tab[0, S:] = 1.0
    tab[1, :S] = np.where(low[None, :], 0.0, sin)
    tab[2, :S] = np.where(low[None, :], -sin, 0.0)
    return tab


def _axial_rope_table(head_dim, tm):
    rows = SEQ // GRID_W
    nf = head_dim // 4
    row = np.repeat(np.arange(rows), GRID_W).astype(np.float64)
    col = np.tile(np.arange(GRID_W), rows).astype(np.float64)
    inv = ROPE_BASE ** (-np.arange(nf, dtype=np.float64) / nf)
    ang = np.stack([row[:, None] * inv, col[:, None] * inv], axis=1)
    a = np.broadcast_to(ang[:, :, None, :], (SEQ, 2, 2, nf)).reshape(SEQ, head_dim)
    reps = LANES // head_dim
    a = np.tile(a, (1, reps))
    return _rope_table(np.cos(a), np.sin(a), nf, tm)


def _seq_rope_table(head_dim, tm):
    inv = 1.0 / (ROPE_BASE ** np.linspace(0.0, 1.0, head_dim // 2, dtype=np.float32).astype(np.float64))
    ang = np.arange(SEQ, dtype=np.float64)[:, None] * inv
    a = np.concatenate([ang, ang], axis=1)
    return _rope_table(np.cos(a), np.sin(a), head_dim // 2, tm)


def _group_mean_matrix():
    g = np.arange(SEG) // DA_HD
    return (g[:, None] == g[None, :]).astype(np.float32) / DA_HD


def _router_weights(w_grp, b_grp, w_rt, b_rt):
    pad = ROUTE_W - 2 * N_EXPERTS
    wr = jnp.concatenate([w_rt, jnp.repeat(w_grp, EXP_PER_GROUP, axis=1),
                          jnp.zeros((D_MODEL, pad), f32)], axis=1)
    br = jnp.concatenate([b_rt, jnp.repeat(b_grp, EXP_PER_GROUP), jnp.zeros((pad,), f32)])[None, :]
    return wr, br


def _expert_weights(w_gate, w_up, w_down):
    wgu = jnp.concatenate([w_gate, w_up], axis=-1).reshape(N_EXPERTS, D_MODEL, 2 * D_EXPERT).astype(bf16)
    wd = w_down.reshape(N_EXPERTS, D_EXPERT, D_MODEL).astype(bf16)
    return wgu, wd


def kernel(x, c, ctx, c_ctx, ada_w, ada_b, e_w_in, e_w_out, hy_conv_w, hy_conv_b, hy_f_w1, hy_f_b1, hy_f_w2, hy_f_b2, hy_f_w3, hy_f_freq, hy_bias, da_q_norm, da_k_norm, da_lam, da_subln, o_w_in, o_w_out, ret_decay, ret_gn, gq_q_norm, gq_k_norm, gq_sink, moe_w_grp, moe_b_grp, moe_w_rt, moe_b_rt, moe_w_gate, moe_w_up, moe_w_down):
    assert x.shape == (BATCH, SEQ, D_MODEL) and ctx.shape == (BATCH, CTX_LEN, D_MODEL)
    h = jnp.concatenate([x.reshape(T_LAT, D_MODEL), ctx.reshape(T_CTX, D_MODEL)], axis=0)
    c_rows = jnp.concatenate([c, c_ctx[None, :], jnp.zeros((MOD_ROWS - BATCH - 1, D_MODEL), f32)], axis=0)
    mod = ada_modulation(c_rows, ada_w, ada_b)

    gmat = jnp.asarray(_group_mean_matrix()).astype(bf16)
    ax_tab = jnp.asarray(_axial_rope_table(DA_HD, PROJ_TM))
    r1_tab = jnp.asarray(_seq_rope_table(RET_DK, PROJ_TM))
    ones = jnp.ones((SEG,), f32)

    lam_init0 = 0.8 - 0.6 * math.exp(-0.3 * 0)
    reps = SEG // DA_HD
    gain0 = jnp.concatenate([ones, ones, ones, jnp.tile(da_q_norm[0], reps) * DA_HD ** -0.5,
                             jnp.tile(da_k_norm[0], reps), ones])[None, :]
    proj0 = in_projection("even", h, mod[0], e_w_in[0].astype(bf16), gain0, gmat, [ax_tab])

    w3r = hy_f_w3[0].reshape(HY_FILT_HID, 4, HY_CH).transpose(1, 0, 2)
    w1p = jnp.concatenate([hy_f_w1[0], jnp.zeros((FEAT_PAD - HY_EMB, HY_FILT_HID), f32)], axis=0)
    y_hy = None
    for L, blk0 in ((SEQ, 0), (CTX_LEN, T_LAT // CTX_LEN)):
        zfeat, decay = _filter_features(L)
        cm, sm = _dft_tables(L)
        cm = jnp.asarray(cm).astype(bf16)
        sm = jnp.asarray(sm).astype(bf16)
        spec, nyq = hyena_filter_spectra(L, jnp.asarray(zfeat), w1p, hy_f_b1[0][None, :], hy_f_w2[0],
                                         hy_f_b2[0][None, :], w3r, hy_f_freq[0], jnp.asarray(decay), cm, sm)
        y_hy = hyena_mix(L, blk0, proj0, hy_conv_w[0], hy_conv_b[0][None, :], spec, nyq, hy_bias[0], cm, sm,
                         prev_out=y_hy)

    subln = da_subln[0][None, :]
    y_da = diff_attention(proj0, da_lam[0], subln, lam_init0, "latent")
    y_da = diff_attention(proj0, da_lam[0], subln, lam_init0, "context", prev_out=y_da)

    wr0, br0 = _router_weights(moe_w_grp[0], moe_b_grp[0], moe_w_rt[0], moe_b_rt[0])
    h, v, comb = out_projection(T_ALL, y_hy, y_da, e_w_out[0].astype(bf16), h, mod[0], wr0, br0)
    wgu0, wd0 = _expert_weights(moe_w_gate[0], moe_w_up[0], moe_w_down[0])
    h = moe(T_ALL, v, wgu0, wd0, comb, h, mod[0])

    w_in1 = jnp.concatenate([o_w_in[0], jnp.zeros((D_MODEL, PROJ_W - o_w_in.shape[2]), f32)], axis=1).astype(bf16)
    kq = GQ_KV * GQ_HD
    gain1 = jnp.concatenate([ones, ones * RET_DK ** -0.5, ones, ones,
                             jnp.tile(gq_q_norm[0], reps) * GQ_HD ** -0.5,
                             jnp.tile(gq_k_norm[0], kq // GQ_HD), jnp.ones((SEG - kq,), f32)])[None, :]
    proj1 = in_projection("odd", h, mod[1], w_in1, gain1, gmat, [ax_tab, r1_tab])
    y_ret = retention(proj1, ret_decay[0], ret_gn[0][None, :])
    y_gq = window_gqa(proj1, gq_sink[0])
    wr1, br1 = _router_weights(moe_w_grp[1], moe_b_grp[1], moe_w_rt[1], moe_b_rt[1])
    h_lat, v, comb = out_projection(T_LAT, y_ret, y_gq, o_w_out[0].astype(bf16), h, mod[1], wr1, br1)
    wgu1, wd1 = _expert_weights(moe_w_gate[1], moe_w_up[1], moe_w_down[1])
    out = moe(T_LAT, v, wgu1, wd1, comb, h_lat, mod[1])
    return out.reshape(BATCH, SEQ, D_MODEL)
```

```python
import functools
import math

import numpy as np
import jax
import jax.numpy as jnp
from jax import lax
from jax.experimental import pallas as pl
from jax.experimental.pallas import tpu as pltpu

f32 = jnp.float32
bf16 = jnp.bfloat16

D_MODEL = 1024
BATCH = 8
SEQ = 2048
DEPTH = 2
GRID_W = 64
CTX_LEN = 256
EPS = 1e-6
NEG_INF = -1e30
ROPE_BASE = 10000.0
HY_CH = D_MODEL // 2
HY_EMB = 33
HY_FILT_HID = 64
HY_MAX_DECAY = math.log(1e-2) / 0.3
HY_MIN_DECAY = math.log(1e-2) / 1.5
DA_HEADS = 4
DA_HD = D_MODEL // 16
RET_HEADS = 4
RET_DK = D_MODEL // 8
RET_CHUNK = 128
GQ_KV = 2
GQ_GROUP = 4
GQ_HD = D_MODEL // 16
WINDOW = 128
N_GROUPS = 4
EXP_PER_GROUP = 8
N_EXPERTS = N_GROUPS * EXP_PER_GROUP
D_EXPERT = D_MODEL // 4

T_LAT = BATCH * SEQ
T_CTX = BATCH * CTX_LEN
T_ALL = T_LAT + T_CTX
PROJ_W = 3072
SEG = 512
CTX_MOD_ROW = BATCH
MOD_ROWS = 16

LANES = 128
VMEM_LIMIT_BYTES = 56 * 1024 * 1024


def _cparams(*sem):
    return pltpu.CompilerParams(dimension_semantics=sem, vmem_limit_bytes=VMEM_LIMIT_BYTES)


def _dot(a, b):
    return jnp.dot(a, b, preferred_element_type=f32)


def _dot_nt(a, b):
    return lax.dot_general(a, b, (((1,), (1,)), ((), ())), preferred_element_type=f32)


def _split(x):
    hi = x.astype(bf16)
    lo = (x - hi.astype(f32)).astype(bf16)
    return hi, lo


def _dot3(a, b):
    ah, al = _split(a)
    bh, bl = _split(b)
    return _dot(ah, bh) + _dot(al, bh) + _dot(ah, bl)


def _dot2(a, b_bf16):
    ah, al = _split(a)
    return _dot(ah, b_bf16) + _dot(al, b_bf16)


def _silu(x):
    return x * jax.nn.sigmoid(x)


def _rms(x):
    return x * lax.rsqrt(jnp.mean(x * x, axis=-1, keepdims=True) + EPS)


def _const_spec(shape):
    nd = len(shape)
    return pl.BlockSpec(shape, lambda *_: (0,) * nd)


def _const_spec1(shape):
    nd = len(shape)
    return pl.BlockSpec(shape, lambda *_: (0,) * nd, pipeline_mode=pl.Buffered(1))


ADA_TN = 1536


def _ada_kernel(c_ref, w_ref, b_ref, o_ref):
    x = _silu(c_ref[...])
    o_ref[0] = _dot3(x, w_ref[0]) + b_ref[0]


def ada_modulation(c_rows, ada_w, ada_b):
    n = 6 * D_MODEL
    return pl.pallas_call(
        _ada_kernel,
        grid=(DEPTH, n // ADA_TN),
        in_specs=[
            pl.BlockSpec((MOD_ROWS, D_MODEL), lambda l, j: (0, 0)),
            pl.BlockSpec((1, D_MODEL, ADA_TN), lambda l, j: (l, 0, j)),
            pl.BlockSpec((1, 1, ADA_TN), lambda l, j: (l, 0, j)),
        ],
        out_specs=pl.BlockSpec((1, MOD_ROWS, ADA_TN), lambda l, j: (l, 0, j)),
        out_shape=jax.ShapeDtypeStruct((DEPTH, MOD_ROWS, n), f32),
        compiler_params=_cparams("arbitrary", "arbitrary"),
        name="ada_modulation",
    )(c_rows, ada_w, ada_b.reshape(DEPTH, 1, n))


PROJ_TM = 512


def _mod_row(i, tm):
    return jnp.minimum((i * tm) // SEQ, CTX_MOD_ROW)


def _tile4(t):
    return jnp.concatenate([t, t, t, t], axis=1)


def _group_norm64(y, gmat):
    ms = _dot2(y * y, gmat)
    return y * lax.rsqrt(ms + EPS)


def _rope(y, tab, shift):
    w = y.shape[1]
    return y * tab[0] + pltpu.roll(y, shift, 1) * tab[1] + pltpu.roll(y, w - shift, 1) * tab[2]


def _stacked_specs(tm, width):
    n_lat = T_LAT // tm
    return [pl.BlockSpec((tm, width), lambda i: (jnp.minimum(i, n_lat - 1), 0)),
            pl.BlockSpec((tm, width), lambda i: (jnp.maximum(i - n_lat, 0), 0))]


def _stacked_tile(i, tm, lat_ref, ctx_ref):
    return jnp.where(i < T_LAT // tm, lat_ref[...], ctx_ref[...])


def _inproj_kernel(layer_kind, *refs):
    i = pl.program_id(0)
    if layer_kind == "even":
        x_ref, c_ref, mod_ref, w_ref, gain_ref, gmat_ref, ax_ref, o_ref = refs
        h = _stacked_tile(i, PROJ_TM, x_ref, c_ref)
    else:
        h_ref, mod_ref, w_ref, gain_ref, gmat_ref, ax_ref, r1_ref, o_ref = refs
        h = h_ref[...]
    r = _mod_row(i, PROJ_TM)
    sh = mod_ref[pl.ds(r, 1), pl.ds(0, D_MODEL)]
    sc = mod_ref[pl.ds(r, 1), pl.ds(D_MODEL, D_MODEL)]
    u = (_rms(h) * (1.0 + sc) + sh).astype(bf16)

    def seg(j):
        return _dot(u, w_ref[:, j * SEG:(j + 1) * SEG])

    def put(j, y):
        o_ref[:, j * SEG:(j + 1) * SEG] = y.astype(bf16)

    def gain(j):
        return gain_ref[:, j * SEG:(j + 1) * SEG]

    gmat = gmat_ref[...]
    ax = ax_ref[...]
    ax4 = (_tile4(ax[0]), _tile4(ax[1]), _tile4(ax[2]))
    if layer_kind == "even":
        for j in (0, 1, 2, 5):
            put(j, seg(j))
        for j in (3, 4):
            put(j, _rope(_group_norm64(seg(j), gmat) * gain(j), ax4, DA_HD // 4))
    else:
        r1 = r1_ref[...]
        r14 = (_tile4(r1[0]), _tile4(r1[1]), _tile4(r1[2]))
        for j in (0, 1):
            put(j, _rope(seg(j) * gain(j), r14, RET_DK // 2))
        for j in (2, 3):
            put(j, seg(j))
        put(4, _rope(_group_norm64(seg(4), gmat) * gain(4), ax4, GQ_HD // 4))
        y = seg(5)
        kw = GQ_KV * GQ_HD
        yk = _rope(_group_norm64(y[:, :kw], gmat[:kw, :kw]) * gain(5)[:, :kw], ax, GQ_HD // 4)
        o_ref[:, 5 * SEG:5 * SEG + kw] = yk.astype(bf16)
        o_ref[:, 5 * SEG + kw:6 * SEG] = y[:, kw:].astype(bf16)


def in_projection(layer_kind, hs, mod, w, gain, gmat, tables):
    n_lat_tiles = T_LAT // PROJ_TM
    n_pos_tiles = SEQ // PROJ_TM

    def tab_map(i):
        return (0, jnp.where(i < n_lat_tiles, i % n_pos_tiles, n_pos_tiles), 0)

    tab_specs = [pl.BlockSpec((3, PROJ_TM, LANES), tab_map) for _ in tables]
    if layer_kind == "even":
        h_specs = _stacked_specs(PROJ_TM, D_MODEL)
    else:
        h_specs = [pl.BlockSpec((PROJ_TM, D_MODEL), lambda i: (i, 0))]
    return pl.pallas_call(
        functools.partial(_inproj_kernel, layer_kind),
        grid=(T_ALL // PROJ_TM,),
        in_specs=h_specs + [
            _const_spec((MOD_ROWS, 6 * D_MODEL)),
            _const_spec((D_MODEL, PROJ_W)),
            _const_spec((1, PROJ_W)),
            _const_spec((SEG, SEG)),
        ] + tab_specs,
        out_specs=pl.BlockSpec((PROJ_TM, PROJ_W), lambda i: (i, 0)),
        out_shape=jax.ShapeDtypeStruct((T_ALL, PROJ_W), bf16),
        compiler_params=_cparams("parallel"),
        name="in_projection_" + layer_kind,
    )(*hs, mod, w, gain, gmat, *tables)


HY_TC = 256
HY_FREQ_CHUNK = 512
FEAT_PAD = 64


def _alt_sign(shape, axis):
    idx = lax.broadcasted_iota(jnp.int32, shape, axis)
    return jnp.where((idx & 1) == 0, 1.0, -1.0).astype(f32)


def _filter_kernel(L, z_ref, w1_ref, b1_ref, w2_ref, b2_ref, wf_ref, wb_ref, freq_ref, dec_ref, c_ref, s_ref,
                   spec_ref, nyq_ref):
    hid = jnp.sin(freq_ref[0:1, :] * (_dot3(z_ref[...], w1_ref[...]) + b1_ref[...]))
    hid = jnp.sin(freq_ref[1:2, :] * (_dot3(hid, w2_ref[...]) + b2_ref[...]))
    dec = dec_ref[...]
    fwd = _dot3(hid, wf_ref[0]) * dec
    bwd = _dot3(hid, wb_ref[0]) * dec
    row = lax.broadcasted_iota(jnp.int32, fwd.shape, 0)
    bwd = jnp.where(row == 0, 0.0, bwd)
    even = fwd + bwd
    odd = bwd - fwd
    wk = jnp.where(row == 0, 0.5 / L, 1.0 / L).astype(f32)
    spec_ref[0, 0] = _dot(c_ref[...], even.astype(bf16)) * wk
    spec_ref[0, 1] = _dot(s_ref[...], odd.astype(bf16)) * wk
    nyq = jnp.sum(even * _alt_sign(even.shape, 0), axis=0, keepdims=True) * (0.5 / L)
    nyq_ref[0] = jnp.broadcast_to(nyq, (8, nyq.shape[1]))


def hyena_filter_spectra(L, zfeat, w1, b1, w2, b2, w3r, freq, decay, cmat, smat):
    nct = HY_CH // HY_TC
    return pl.pallas_call(
        functools.partial(_filter_kernel, L),
        grid=(2, nct),
        in_specs=[
            _const_spec((L, FEAT_PAD)),
            _const_spec((FEAT_PAD, HY_FILT_HID)),
            _const_spec((1, HY_FILT_HID)),
            _const_spec((HY_FILT_HID, HY_FILT_HID)),
            _const_spec((1, HY_FILT_HID)),
            pl.BlockSpec((1, HY_FILT_HID, HY_TC), lambda n, c: (2 * n, 0, c)),
            pl.BlockSpec((1, HY_FILT_HID, HY_TC), lambda n, c: (2 * n + 1, 0, c)),
            _const_spec((2, HY_FILT_HID)),
            pl.BlockSpec((L, HY_TC), lambda n, c: (0, c)),
            _const_spec1((L, L)),
            _const_spec1((L, L)),
        ],
        out_specs=[
            pl.BlockSpec((1, 2, L, HY_TC), lambda n, c: (n, 0, 0, c)),
            pl.BlockSpec((1, 8, HY_TC), lambda n, c: (n, 0, c)),
        ],
        out_shape=[
            jax.ShapeDtypeStruct((2, 2, L, HY_CH), f32),
            jax.ShapeDtypeStruct((2, 8, HY_CH), f32),
        ],
        compiler_params=_cparams("arbitrary", "arbitrary"),
        name="hyena_filter_L%d" % L,
    )(zfeat, w1, b1, w2, b2, w3r, w3r, freq, decay, cmat, smat)


def _conv3(u, w, b):
    L = u.shape[0]
    row = lax.broadcasted_iota(jnp.int32, u.shape, 0)
    prev = jnp.where(row == 0, 0.0, pltpu.roll(u, 1, 0))
    nxt = jnp.where(row == L - 1, 0.0, pltpu.roll(u, L - 1, 0))
    return prev * w[0:1, :] + u * w[1:2, :] + nxt * w[2:3, :] + b


def _hyena_kernel(v_ref, x1_ref, x2_ref, wv_ref, w1_ref, w2_ref, bv_ref, b1_ref, b2_ref, spec_ref, nyq_ref,
                  bias_ref, c_ref, s_ref, *rest):
    o_ref, yr_ref, yi_ref = rest[-3:]
    L = v_ref.shape[0]
    fch = min(L, HY_FREQ_CHUNK)
    z = _conv3(v_ref[...].astype(f32), wv_ref[...], bv_ref[...])
    gate_refs = ((x1_ref, w1_ref, b1_ref), (x2_ref, w2_ref, b2_ref))
    alt = _alt_sign(z.shape, 0)
    for n in range(2):
        zb = z.astype(bf16)
        for k in range(L // fch):
            rows = slice(k * fch, (k + 1) * fch)
            a = _dot(c_ref[rows, :], zb)
            b = _dot(s_ref[rows, :], zb)
            hr = spec_ref[n, 0, rows, :]
            hi = spec_ref[n, 1, rows, :]
            yr_ref[rows, :] = (a * hr + b * hi).astype(bf16)
            yi_ref[rows, :] = (a * hi - b * hr).astype(bf16)
        x_nyq = jnp.sum(z * alt, axis=0, keepdims=True)
        y = (_dot(c_ref[...], yr_ref[...]) - _dot(s_ref[...], yi_ref[...])
             + alt * (x_nyq * nyq_ref[n, 0:1, :]))
        x_ref, w_ref, b_ref = gate_refs[n]
        gate = _conv3(x_ref[...].astype(f32), w_ref[...], b_ref[...])
        z = gate * (y + z * bias_ref[n:n + 1, :])
    o_ref[...] = z.astype(bf16)


def hyena_mix(L, row_block0, proj, conv_w, conv_b, spec, nyq, bias, cmat, smat):
    nct = HY_CH // HY_TC
    nseg = HY_CH // HY_TC

    def col(k):
        return lambda c, b: (row_block0 + b, k * nseg + c)

    def par(k):
        return lambda c, b: (0, k * nseg + c)

    in_specs = (
        [pl.BlockSpec((L, HY_TC), col(k)) for k in range(3)]
        + [pl.BlockSpec((3, HY_TC), par(k)) for k in range(3)]
        + [pl.BlockSpec((1, HY_TC), par(k)) for k in range(3)]
        + [
            pl.BlockSpec((2, 2, L, HY_TC), lambda c, b: (0, 0, 0, c), pipeline_mode=pl.Buffered(1)),
            pl.BlockSpec((2, 8, HY_TC), lambda c, b: (0, 0, c)),
            pl.BlockSpec((2, HY_TC), lambda c, b: (0, c)),
            _const_spec1((L, L)),
            _const_spec1((L, L)),
        ]
    )
    args = [proj, proj, proj, conv_w, conv_w, conv_w, conv_b, conv_b, conv_b, spec, nyq, bias, cmat, smat]
    return pl.pallas_call(
        _hyena_kernel,
        grid=(nct, BATCH),
        in_specs=in_specs,
        out_specs=pl.BlockSpec((L, HY_TC), lambda c, b: (b, c)),
        out_shape=jax.ShapeDtypeStruct((BATCH * L, HY_CH), bf16),
        scratch_shapes=[pltpu.VMEM((L, HY_TC), bf16), pltpu.VMEM((L, HY_TC), bf16)],
        compiler_params=_cparams("arbitrary", "arbitrary"),
        name="hyena_mix_L%d" % L,
    )(*args)


DA_TQ = 256


def _diff_attn_kernel(lam_init, q_ref, kc_ref, vc_ref, kl_ref, vl_ref, lam_ref, subln_ref, o_ref):
    i = pl.program_id(1)
    n_lat_blocks = SEQ // DA_TQ

    @pl.when(i < n_lat_blocks)
    def _():
        _diff_attn_body(lam_init, q_ref, (kc_ref, vc_ref, kl_ref, vl_ref), lam_ref, subln_ref, o_ref)

    @pl.when(i == n_lat_blocks)
    def _():
        _diff_attn_body(lam_init, q_ref, (kc_ref, vc_ref), lam_ref, subln_ref, o_ref)


def _diff_attn_body(lam_init, q_ref, kv_refs, lam_ref, subln_ref, o_ref):
    n_src = len(kv_refs) // 2
    lp = lam_ref[...]
    lam = (jnp.exp(jnp.sum(lp[0:1] * lp[1:2], axis=-1, keepdims=True))
           - jnp.exp(jnp.sum(lp[2:3] * lp[3:4], axis=-1, keepdims=True)) + lam_init)
    q = q_ref[...]
    lane = lax.broadcasted_iota(jnp.int32, (q.shape[0], 2 * DA_HD), 1)
    hw = 2 * DA_HD
    for h in range(DA_HEADS):
        qh = q[:, h * hw:(h + 1) * hw]
        ks = [kv_refs[2 * s][:, h * hw:(h + 1) * hw] for s in range(n_src)]
        vs = [kv_refs[2 * s + 1][:, h * hw:(h + 1) * hw] for s in range(n_src)]
        es = []
        invs = []
        for m in range(2):
            qm = jnp.where((lane < DA_HD) == (m == 0), qh, jnp.zeros_like(qh))
            ss = [_dot_nt(qm, k) for k in ks]
            mx = functools.reduce(jnp.maximum, [jnp.max(s, axis=-1, keepdims=True) for s in ss])
            e = [jnp.exp(s - mx) for s in ss]
            den = functools.reduce(jnp.add, [jnp.sum(x, axis=-1, keepdims=True) for x in e])
            es.append(e)
            invs.append(1.0 / den)
        c0 = invs[0]
        c1 = lam * invs[1]
        oh = functools.reduce(jnp.add, [
            _dot((es[0][s] * c0 - es[1][s] * c1).astype(bf16), vs[s]) for s in range(n_src)])
        oh = _rms(oh) * subln_ref[...] * (1.0 - lam_init)
        o_ref[:, h * hw:(h + 1) * hw] = oh.astype(bf16)


def diff_attention(proj, lam_p, subln, lam_init):
    assert CTX_LEN == DA_TQ
    width = DA_HEADS * 2 * DA_HD
    qcol, kcol, vcol = 3, 4, 5
    ctx_blk0 = T_LAT // CTX_LEN
    nq = SEQ // DA_TQ

    def q_rows(b, i):
        return jnp.where(i < nq, b * nq + i, ctx_blk0 + b)

    return pl.pallas_call(
        functools.partial(_diff_attn_kernel, lam_init),
        grid=(BATCH, nq + 1),
        in_specs=[
            pl.BlockSpec((DA_TQ, width), lambda b, i: (q_rows(b, i), qcol)),
            pl.BlockSpec((CTX_LEN, width), lambda b, i: (ctx_blk0 + b, kcol)),
            pl.BlockSpec((CTX_LEN, width), lambda b, i: (ctx_blk0 + b, vcol)),
            pl.BlockSpec((SEQ, width), lambda b, i: (b, kcol)),
            pl.BlockSpec((SEQ, width), lambda b, i: (b, vcol)),
            _const_spec((4, DA_HD)),
            _const_spec((1, 2 * DA_HD)),
        ],
        out_specs=pl.BlockSpec((DA_TQ, width), lambda b, i: (q_rows(b, i), 0)),
        out_shape=jax.ShapeDtypeStruct((T_ALL, width), bf16),
        compiler_params=_cparams("parallel", "arbitrary"),
        name="diff_attention",
    )(proj, proj, proj, proj, proj, lam_p, subln)


def _log_sigmoid(x):
    return jnp.minimum(x, 0.0) - jnp.log(1.0 + jnp.exp(-jnp.abs(x)))


def _retention_kernel(q_ref, k_ref, v_ref, g_ref, kc_ref, vc_ref, decay_ref, gn_ref, o_ref, sf_ref):
    h = pl.program_id(1)
    ch = RET_CHUNK
    nchunk = SEQ // ch
    lgs = _log_sigmoid(decay_ref[...])
    sel = lax.broadcasted_iota(jnp.int32, lgs.shape, 1) == h
    lg = jnp.sum(jnp.where(sel, lgs, 0.0), axis=-1, keepdims=True)
    lgf = lg[0:1, :]
    lgb = lg[1:2, :]
    ri = lax.broadcasted_iota(jnp.int32, (ch, ch), 0).astype(f32)
    ci = lax.broadcasted_iota(jnp.int32, (ch, ch), 1).astype(f32)
    rel = ri - ci
    dsum = (jnp.where(rel >= 0, jnp.exp(jnp.maximum(rel, 0.0) * lgf), 0.0)
            + jnp.where(rel <= 0, jnp.exp(jnp.maximum(-rel, 0.0) * lgb), 0.0))
    zeta_f = jnp.exp((ch - 1 - ri) * lgf)
    zeta_b = jnp.exp(ri * lgb)
    xi_f = jnp.exp((ri + 1.0) * lgf)
    xi_b = jnp.exp((ch - ri) * lgb)
    gch_f = jnp.exp(ch * lgf)
    gch_b = jnp.exp(ch * lgb)

    def kv_state(kw, v):
        return _dot(kw.T.astype(bf16), v)

    kc = kc_ref[...].astype(f32)
    vc = vc_ref[...]
    cr = lax.broadcasted_iota(jnp.int32, kc.shape, 0).astype(f32)
    s_f0 = kv_state(kc * jnp.exp((CTX_LEN - 1 - cr) * lgf), vc)
    s_b0 = kv_state(kc * jnp.exp(cr * lgb), vc)

    def chunk(n):
        return pl.ds(pl.multiple_of(n * ch, ch), ch)

    def fwd(n, s):
        sf_ref[n] = s
        kn = k_ref[chunk(n), :].astype(f32)
        return gch_f * s + kv_state(kn * zeta_f, v_ref[chunk(n), :])

    lax.fori_loop(0, nchunk, fwd, s_f0)

    gn = gn_ref[...]

    def bwd(t, s):
        n = nchunk - 1 - t
        qn = q_ref[chunk(n), :]
        kn = k_ref[chunk(n), :]
        vn = v_ref[chunk(n), :]
        att = _dot_nt(qn, kn) * dsum
        o = (_dot(att.astype(bf16), vn)
             + xi_f * _dot(qn, sf_ref[n].astype(bf16))
             + xi_b * _dot(qn, s.astype(bf16)))
        mu = jnp.mean(o, axis=-1, keepdims=True)
        oc = o - mu
        var = jnp.mean(oc * oc, axis=-1, keepdims=True)
        y = oc * lax.rsqrt(var + EPS) * gn * _silu(g_ref[chunk(n), :].astype(f32))
        o_ref[chunk(n), :] = y.astype(bf16)
        return gch_b * s + kv_state(kn.astype(f32) * zeta_b, vn)

    lax.fori_loop(0, nchunk, bwd, s_b0)


def retention(proj, decay, gn_w):
    dk = RET_DK
    ctx_blk0 = T_LAT // CTX_LEN
    return pl.pallas_call(
        _retention_kernel,
        grid=(BATCH, RET_HEADS),
        in_specs=[
            pl.BlockSpec((SEQ, dk), lambda b, h: (b, h)),
            pl.BlockSpec((SEQ, dk), lambda b, h: (b, RET_HEADS + h)),
            pl.BlockSpec((SEQ, dk), lambda b, h: (b, 2 * RET_HEADS + h)),
            pl.BlockSpec((SEQ, dk), lambda b, h: (b, 3 * RET_HEADS + h)),
            pl.BlockSpec((CTX_LEN, dk), lambda b, h: (ctx_blk0 + b, RET_HEADS + h)),
            pl.BlockSpec((CTX_LEN, dk), lambda b, h: (ctx_blk0 + b, 2 * RET_HEADS + h)),
            _const_spec((2, RET_HEADS)),
            pl.BlockSpec((1, dk), lambda b, h: (0, h)),
        ],
        out_specs=pl.BlockSpec((SEQ, dk), lambda b, h: (b, h)),
        out_shape=jax.ShapeDtypeStruct((T_LAT, RET_HEADS * dk), bf16),
        scratch_shapes=[pltpu.VMEM((SEQ // RET_CHUNK, dk, dk), f32)],
        compiler_params=_cparams("parallel", "arbitrary"),
        name="retention",
    )(proj, proj, proj, proj, proj, proj, decay, gn_w)


GQ_TQ = 128
GQ_SPAN = 3 * GQ_TQ


def _gqa_kernel(q_ref, kv_ref, kvc_ref, sink_ref, o_ref):
    n = pl.program_id(1)
    start = pl.multiple_of(jnp.clip((n - 1) * GQ_TQ, 0, SEQ - GQ_SPAN), GQ_TQ)
    kw = GQ_KV * GQ_HD
    kl_all = kv_ref[pl.ds(start, GQ_SPAN), pl.ds(0, kw)]
    vl_all = kv_ref[pl.ds(start, GQ_SPAN), pl.ds(kw, kw)]
    kc_all = kvc_ref[:, 0:kw]
    vc_all = kvc_ref[:, kw:2 * kw]
    qpos = n * GQ_TQ + lax.broadcasted_iota(jnp.int32, (GQ_TQ, GQ_SPAN), 0)
    kpos = start + lax.broadcasted_iota(jnp.int32, (GQ_TQ, GQ_SPAN), 1)
    mask = jnp.abs(kpos - qpos) <= WINDOW
    q = q_ref[...]
    hd = GQ_HD
    for kv in range(GQ_KV):
        kl = kl_all[:, kv * hd:(kv + 1) * hd]
        vl = vl_all[:, kv * hd:(kv + 1) * hd]
        kc = kc_all[:, kv * hd:(kv + 1) * hd]
        vc = vc_all[:, kv * hd:(kv + 1) * hd]
        for g in range(GQ_GROUP):
            hidx = kv * GQ_GROUP + g
            qh = q[:, hidx * hd:(hidx + 1) * hd]
            sink = sink_ref[hidx]
            s_c = _dot_nt(qh, kc)
            s_l = jnp.where(mask, _dot_nt(qh, kl), NEG_INF)
            mx = jnp.maximum(jnp.maximum(jnp.max(s_c, axis=-1, keepdims=True),
                                         jnp.max(s_l, axis=-1, keepdims=True)), sink)
            e_c = jnp.exp(s_c - mx)
            e_l = jnp.exp(s_l - mx)
            den = (jnp.sum(e_c, axis=-1, keepdims=True) + jnp.sum(e_l, axis=-1, keepdims=True)
                   + jnp.exp(sink - mx))
            inv = 1.0 / den
            o = _dot((e_c * inv).astype(bf16), vc) + _dot((e_l * inv).astype(bf16), vl)
            o_ref[:, hidx * hd:(hidx + 1) * hd] = o.astype(bf16)


def window_gqa(proj, sink):
    width = GQ_KV * GQ_GROUP * GQ_HD
    nq = SEQ // GQ_TQ
    kvw = 2 * GQ_KV * GQ_HD
    kv_col = (5 * SEG) // kvw
    ctx_blk0 = T_LAT // CTX_LEN
    return pl.pallas_call(
        _gqa_kernel,
        grid=(BATCH, nq),
        in_specs=[
            pl.BlockSpec((GQ_TQ, width), lambda b, n: (b * nq + n, 4)),
            pl.BlockSpec((SEQ, kvw), lambda b, n: (b, kv_col)),
            pl.BlockSpec((CTX_LEN, kvw), lambda b, n: (ctx_blk0 + b, kv_col)),
            pl.BlockSpec(memory_space=pltpu.SMEM),
        ],
        out_specs=pl.BlockSpec((GQ_TQ, width), lambda b, n: (b * nq + n, 0)),
        out_shape=jax.ShapeDtypeStruct((T_LAT, width), bf16),
        compiler_params=_cparams("parallel", "arbitrary"),
        name="window_gqa",
    )(proj, proj, proj, sink)


OUT_TM = 512
ROUTE_W = LANES
MOE_TB = 256


def _route(logits):
    lane_i = lax.broadcasted_iota(jnp.int32, logits.shape, 1)
    lane = lane_i.astype(f32)
    big = float(1 << 20)
    valid = lane_i < N_EXPERTS
    le = logits
    lgx = pltpu.roll(logits, ROUTE_W - N_EXPERTS, 1)
    lgx = jnp.where(valid, lgx, NEG_INF)
    gmax = jnp.max(lgx, axis=-1, keepdims=True)
    grp = (lane_i // EXP_PER_GROUP).astype(f32)
    g_sel = jnp.min(jnp.where(lgx == gmax, grp, big), axis=-1, keepdims=True)
    p_grp = float(EXP_PER_GROUP) / jnp.sum(jnp.exp(lgx - gmax), axis=-1, keepdims=True)
    lm = jnp.where(valid, jnp.where(grp == g_sel, le, NEG_INF), NEG_INF)
    v1 = jnp.max(lm, axis=-1, keepdims=True)
    i1 = jnp.min(jnp.where(lm == v1, lane, big), axis=-1, keepdims=True)
    lm2 = jnp.where(lane == i1, NEG_INF, lm)
    v2 = jnp.max(lm2, axis=-1, keepdims=True)
    i2 = jnp.min(jnp.where(lm2 == v2, lane, big), axis=-1, keepdims=True)
    e2 = jnp.exp(v2 - v1)
    w1 = p_grp / (1.0 + e2)
    w2 = w1 * e2
    return jnp.where(lane == i1, w1, 0.0) + jnp.where(lane == i2, w2, 0.0)


def _outproj_kernel(stacked, *refs):
    i = pl.program_id(0)
    if stacked:
        (ya_ref, yac_ref, yb_ref, w_ref, x_ref, c_ref, mod_ref, wr_ref, br_ref,
         hn_ref, v_ref, comb_ref, cnt_ref) = refs
        ya = _stacked_tile(i, OUT_TM, ya_ref, yac_ref)
        h = _stacked_tile(i, OUT_TM, x_ref, c_ref)
    else:
        ya_ref, yb_ref, w_ref, h_ref, mod_ref, wr_ref, br_ref, hn_ref, v_ref, comb_ref, cnt_ref = refs
        ya = ya_ref[...]
        h = h_ref[...]
    r = _mod_row(i, OUT_TM)
    g1 = mod_ref[pl.ds(r, 1), pl.ds(2 * D_MODEL, D_MODEL)]
    sh2 = mod_ref[pl.ds(r, 1), pl.ds(3 * D_MODEL, D_MODEL)]
    sc2 = mod_ref[pl.ds(r, 1), pl.ds(4 * D_MODEL, D_MODEL)]
    half = ya.shape[1]
    m = _dot(ya, w_ref[0:half, :]) + _dot(yb_ref[...], w_ref[half:2 * half, :])
    hn = h + g1 * m
    hn_ref[...] = hn
    v = _rms(hn) * (1.0 + sc2) + sh2
    v_ref[...] = v.astype(bf16)
    comb = _route(_dot3(v, wr_ref[...]) + br_ref[...])
    comb_ref[...] = comb
    for s in range(OUT_TM // MOE_TB):
        cnt = jnp.sum((comb[s * MOE_TB:(s + 1) * MOE_TB] != 0.0).astype(f32), axis=0, keepdims=True)
        cnt_ref[s] = jnp.broadcast_to(cnt, (8, ROUTE_W)).astype(jnp.int32)


def out_projection(n_rows, yas, yb, w, hs, mod, wr, br):
    stacked = len(yas) == 2
    assert stacked == (len(hs) == 2) and (not stacked or n_rows == T_ALL)
    half = yb.shape[1]
    row_spec = lambda width: [pl.BlockSpec((OUT_TM, width), lambda i: (i, 0))]
    return pl.pallas_call(
        functools.partial(_outproj_kernel, stacked),
        grid=(n_rows // OUT_TM,),
        in_specs=(
            (_stacked_specs(OUT_TM, half) if stacked else row_spec(half))
            + row_spec(half)
            + [_const_spec((2 * half, D_MODEL))]
            + (_stacked_specs(OUT_TM, D_MODEL) if stacked else row_spec(D_MODEL))
            + [_const_spec((MOD_ROWS, 6 * D_MODEL)),
               _const_spec((D_MODEL, ROUTE_W)),
               _const_spec((1, ROUTE_W))]
        ),
        out_specs=[
            pl.BlockSpec((OUT_TM, D_MODEL), lambda i: (i, 0)),
            pl.BlockSpec((OUT_TM, D_MODEL), lambda i: (i, 0)),
            pl.BlockSpec((OUT_TM, ROUTE_W), lambda i: (i, 0)),
            pl.BlockSpec((OUT_TM // MOE_TB, 8, ROUTE_W), lambda i: (i, 0, 0)),
        ],
        out_shape=[
            jax.ShapeDtypeStruct((n_rows, D_MODEL), f32),
            jax.ShapeDtypeStruct((n_rows, D_MODEL), bf16),
            jax.ShapeDtypeStruct((n_rows, ROUTE_W), f32),
            jax.ShapeDtypeStruct((n_rows // MOE_TB, 8, ROUTE_W), jnp.int32),
        ],
        compiler_params=_cparams("parallel"),
        name="out_projection",
    )(*yas, yb, w, *hs, mod, wr, br)


MOE_UNIT = 16
MOE_TG = 256
MOE_TOP = 2
MOE_RLOC = MOE_TOP * MOE_TB + N_EXPERTS * MOE_UNIT
MOE_META = 128


def _moe_rows(n_blk):
    return n_blk * MOE_RLOC + N_EXPERTS * MOE_TG


def _moe_plan_kernel(n_blk, cnt_ref, units_ref, dst_ref, tile_exp_ref, meta_ref):
    def clear(k, c):
        tile_exp_ref[k] = 0
        return c

    lax.fori_loop(0, tile_exp_ref.shape[0], clear, 0)

    def clear_meta(k, c):
        meta_ref[k] = 0
        return c

    lax.fori_loop(0, MOE_META, clear_meta, 0)

    def per_expert(e, goff):
        def per_blk(t, acc):
            u = (cnt_ref[t * N_EXPERTS + e] + (MOE_UNIT - 1)) // MOE_UNIT
            units_ref[t * N_EXPERTS + e] = u
            dst_ref[t * N_EXPERTS + e] = goff + acc
            return acc + u * MOE_UNIT

        n_e = lax.fori_loop(0, n_blk, per_blk, 0)
        nt = (n_e + (MOE_TG - 1)) // MOE_TG
        t0 = goff // MOE_TG

        def mark(k, c):
            tile_exp_ref[t0 + k] = e
            return c

        lax.fori_loop(0, nt, mark, 0)
        meta_ref[1 + e] = goff + n_e
        meta_ref[1 + N_EXPERTS + e] = (nt * MOE_TG - n_e) // MOE_UNIT
        return goff + nt * MOE_TG

    total = lax.fori_loop(0, N_EXPERTS, per_expert, 0)
    meta_ref[0] = total // MOE_TG


def moe_plan(n_blk, counts):
    smem = pl.BlockSpec(memory_space=pltpu.SMEM)
    n_tiles = _moe_rows(n_blk) // MOE_TG
    return pl.pallas_call(
        functools.partial(_moe_plan_kernel, n_blk),
        in_specs=[smem],
        out_specs=[smem, smem, smem, smem],
        out_shape=[
            jax.ShapeDtypeStruct((n_blk * N_EXPERTS,), jnp.int32),
            jax.ShapeDtypeStruct((n_blk * N_EXPERTS,), jnp.int32),
            jax.ShapeDtypeStruct((n_tiles,), jnp.int32),
            jax.ShapeDtypeStruct((MOE_META,), jnp.int32),
        ],
        name="moe_plan",
    )(counts)


def _block_routes(comb, ltri, utri):
    oh = comb != 0.0
    ohf = jnp.where(oh, 1.0, 0.0)
    rank = _dot(ltri, ohf.astype(bf16))
    cnt = jnp.sum(ohf, axis=0, keepdims=True)
    units = jnp.floor((cnt + (MOE_UNIT - 1.0)) * (1.0 / MOE_UNIT))
    seg = _dot(jnp.broadcast_to(units, (8, ROUTE_W)).astype(bf16), utri)[0:1, :] * MOE_UNIT
    dest = seg + rank
    big = float(1 << 20)
    d_a = jnp.min(jnp.where(oh, dest, big), axis=-1, keepdims=True)
    d_b = jnp.max(jnp.where(oh, dest, -1.0), axis=-1, keepdims=True)
    w_a = jnp.sum(jnp.where(oh, jnp.where(dest == d_a, comb, 0.0), 0.0), axis=-1, keepdims=True)
    w_b = jnp.sum(jnp.where(oh, jnp.where(dest == d_b, comb, 0.0), 0.0), axis=-1, keepdims=True)
    second = d_b != d_a
    return d_a, jnp.where(second, d_b, -1.0), w_a, jnp.where(second, w_b, 0.0)


def _one_hot_rows(d):
    r = lax.broadcasted_iota(jnp.int32, (d.shape[0], MOE_RLOC), 1).astype(f32)
    return jnp.where(r == d, 1.0, 0.0).astype(bf16)


def _unit_copies(t, units_ref, dst_ref, local, remote, sem, to_remote, wait):
    def per_expert(e, row):
        nu = units_ref[t * N_EXPERTS + e]
        base = dst_ref[t * N_EXPERTS + e]

        def per_unit(u, c):
            lo = local.at[pl.ds(pl.multiple_of(row + u * MOE_UNIT, MOE_UNIT), MOE_UNIT)]
            ro = remote.at[pl.ds(pl.multiple_of(base + u * MOE_UNIT, MOE_UNIT), MOE_UNIT)]
            cp = pltpu.make_async_copy(lo, ro, sem) if to_remote else pltpu.make_async_copy(ro, lo, sem)
            if wait:
                cp.wait()
            else:
                cp.start()
            return c

        lax.fori_loop(0, nu, per_unit, 0)
        return row + nu * MOE_UNIT

    lax.fori_loop(0, N_EXPERTS, per_expert, 0)


def _gap_copies(meta_ref, zero_ref, remote, sem, wait):
    def per_expert(e, c):
        start = meta_ref[1 + e]

        def per_unit(u, c2):
            ro = remote.at[pl.ds(pl.multiple_of(start + u * MOE_UNIT, MOE_UNIT), MOE_UNIT)]
            cp = pltpu.make_async_copy(zero_ref, ro, sem)
            if wait:
                cp.wait()
            else:
                cp.start()
            return c2

        lax.fori_loop(0, meta_ref[1 + N_EXPERTS + e], per_unit, 0)
        return c

    lax.fori_loop(0, N_EXPERTS, per_expert, 0)


def _moe_dispatch_kernel(n_blk, units_ref, dst_ref, meta_ref, x_ref, comb_ref, ltri_ref, utri_ref, xs_ref,
                         buf_ref, zero_ref, sem_ref):
    t = pl.program_id(0)
    slot = t % 2
    d_a, d_b, _, _ = _block_routes(comb_ref[...], ltri_ref[...], utri_ref[...])
    p = _one_hot_rows(d_a) + _one_hot_rows(d_b)
    rows = lax.dot_general(p, x_ref[...], (((0,), (0,)), ((), ())), preferred_element_type=f32)
    buf_ref[slot] = rows.astype(bf16)
    _unit_copies(t, units_ref, dst_ref, buf_ref.at[slot], xs_ref, sem_ref.at[slot], True, False)

    @pl.when(t > 0)
    def _():
        _unit_copies(t - 1, units_ref, dst_ref, buf_ref.at[1 - slot], xs_ref, sem_ref.at[1 - slot], True, True)

    @pl.when(t == n_blk - 1)
    def _():
        zero_ref[...] = jnp.zeros_like(zero_ref)
        _gap_copies(meta_ref, zero_ref, xs_ref, sem_ref.at[2], False)
        _unit_copies(t, units_ref, dst_ref, buf_ref.at[slot], xs_ref, sem_ref.at[slot], True, True)
        _gap_copies(meta_ref, zero_ref, xs_ref, sem_ref.at[2], True)


def moe_dispatch(n_blk, units, dst, meta, x, comb, ltri, utri):
    return pl.pallas_call(
        functools.partial(_moe_dispatch_kernel, n_blk),
        grid_spec=pltpu.PrefetchScalarGridSpec(
            num_scalar_prefetch=3,
            grid=(n_blk,),
            in_specs=[
                pl.BlockSpec((MOE_TB, D_MODEL), lambda t, *_: (t, 0)),
                pl.BlockSpec((MOE_TB, ROUTE_W), lambda t, *_: (t, 0)),
                pl.BlockSpec((MOE_TB, MOE_TB), lambda t, *_: (0, 0)),
                pl.BlockSpec((ROUTE_W, ROUTE_W), lambda t, *_: (0, 0)),
            ],
            out_specs=pl.BlockSpec(memory_space=pl.ANY),
            scratch_shapes=[
                pltpu.VMEM((2, MOE_RLOC, D_MODEL), bf16),
                pltpu.VMEM((MOE_UNIT, D_MODEL), bf16),
                pltpu.SemaphoreType.DMA((3,)),
            ],
        ),
        out_shape=jax.ShapeDtypeStruct((_moe_rows(n_blk), D_MODEL), bf16),
        compiler_params=_cparams("arbitrary"),
        name="moe_dispatch",
    )(units, dst, meta, x, comb, ltri, utri)


def _moe_expert_kernel(tile_exp_ref, meta_ref, xs_ref, wg_ref, wu_ref, wd_ref, ys_ref, wgb_ref, wub_ref, wdb_ref):
    i = pl.program_id(0)

    @pl.when(i < meta_ref[0])
    def _():
        prev = tile_exp_ref[jnp.maximum(i - 1, 0)]

        @pl.when(jnp.logical_or(i == 0, tile_exp_ref[i] != prev))
        def _():
            wgb_ref[...] = wg_ref[0].astype(bf16)
            wub_ref[...] = wu_ref[0].astype(bf16)
            wdb_ref[...] = wd_ref[0].astype(bf16)

        x = xs_ref[...]
        a = _silu(_dot(x, wgb_ref[...])) * _dot(x, wub_ref[...])
        ys_ref[...] = _dot(a.astype(bf16), wdb_ref[...]).astype(bf16)


def moe_experts(n_blk, tile_exp, meta, xs, w_gate, w_up, w_down):
    n_tiles = _moe_rows(n_blk) // MOE_TG

    def row_map(i, te, meta):
        return (jnp.minimum(i, meta[0] - 1), 0)

    def w_map(i, te, meta):
        return (te[jnp.minimum(i, meta[0] - 1)], 0, 0)

    return pl.pallas_call(
        _moe_expert_kernel,
        grid_spec=pltpu.PrefetchScalarGridSpec(
            num_scalar_prefetch=2,
            grid=(n_tiles,),
            in_specs=[
                pl.BlockSpec((MOE_TG, D_MODEL), row_map),
                pl.BlockSpec((1, D_MODEL, D_EXPERT), w_map),
                pl.BlockSpec((1, D_MODEL, D_EXPERT), w_map),
                pl.BlockSpec((1, D_EXPERT, D_MODEL), w_map),
            ],
            out_specs=pl.BlockSpec((MOE_TG, D_MODEL), row_map),
            scratch_shapes=[
                pltpu.VMEM((D_MODEL, D_EXPERT), bf16),
                pltpu.VMEM((D_MODEL, D_EXPERT), bf16),
                pltpu.VMEM((D_EXPERT, D_MODEL), bf16),
            ],
        ),
        out_shape=jax.ShapeDtypeStruct((_moe_rows(n_blk), D_MODEL), bf16),
        compiler_params=_cparams("arbitrary"),
        name="moe_experts",
    )(tile_exp, meta, xs, w_gate, w_up, w_down)


def _moe_combine_kernel(n_blk, units_ref, dst_ref, ys_ref, comb_ref, h_ref, mod_ref, ltri_ref, utri_ref, o_ref,
                        buf_ref, sem_ref):
    t = pl.program_id(0)
    slot = t % 2

    @pl.when(t == 0)
    def _():
        buf_ref[...] = jnp.zeros_like(buf_ref)
        _unit_copies(0, units_ref, dst_ref, buf_ref.at[0], ys_ref, sem_ref.at[0], False, False)

    @pl.when(t + 1 < n_blk)
    def _():
        _unit_copies(t + 1, units_ref, dst_ref, buf_ref.at[1 - slot], ys_ref, sem_ref.at[1 - slot], False, False)

    _unit_copies(t, units_ref, dst_ref, buf_ref.at[slot], ys_ref, sem_ref.at[slot], False, True)
    d_a, d_b, w_a, w_b = _block_routes(comb_ref[...], ltri_ref[...], utri_ref[...])
    ys = buf_ref[slot]
    m = w_a * _dot(_one_hot_rows(d_a), ys) + w_b * _dot(_one_hot_rows(d_b), ys)
    r = _mod_row(t, MOE_TB)
    g2 = mod_ref[pl.ds(r, 1), pl.ds(5 * D_MODEL, D_MODEL)]
    o_ref[...] = h_ref[...] + g2 * m


def moe_combine(n_blk, units, dst, ys, comb, h, mod, ltri, utri):
    return pl.pallas_call(
        functools.partial(_moe_combine_kernel, n_blk),
        grid_spec=pltpu.PrefetchScalarGridSpec(
            num_scalar_prefetch=2,
            grid=(n_blk,),
            in_specs=[
                pl.BlockSpec(memory_space=pl.ANY),
                pl.BlockSpec((MOE_TB, ROUTE_W), lambda t, *_: (t, 0)),
                pl.BlockSpec((MOE_TB, D_MODEL), lambda t, *_: (t, 0)),
                pl.BlockSpec((MOD_ROWS, 6 * D_MODEL), lambda t, *_: (0, 0)),
                pl.BlockSpec((MOE_TB, MOE_TB), lambda t, *_: (0, 0)),
                pl.BlockSpec((ROUTE_W, ROUTE_W), lambda t, *_: (0, 0)),
            ],
            out_specs=pl.BlockSpec((MOE_TB, D_MODEL), lambda t, *_: (t, 0)),
            scratch_shapes=[
                pltpu.VMEM((2, MOE_RLOC, D_MODEL), bf16),
                pltpu.SemaphoreType.DMA((2,)),
            ],
        ),
        out_shape=jax.ShapeDtypeStruct((n_blk * MOE_TB, D_MODEL), f32),
        compiler_params=_cparams("arbitrary"),
        name="moe_combine",
    )(units, dst, ys, comb, h, mod, ltri, utri)


def sparse_moe(n_rows, v, comb, counts, h, mod, w_gate, w_up, w_down, ltri, utri):
    n_blk = n_rows // MOE_TB
    cnt = counts[:, 0, :N_EXPERTS].reshape(n_blk * N_EXPERTS)
    units, dst, tile_exp, meta = moe_plan(n_blk, cnt)
    xs = moe_dispatch(n_blk, units, dst, meta, v, comb, ltri, utri)
    ys = moe_experts(n_blk, tile_exp, meta, xs,
                     w_gate.reshape(N_EXPERTS, D_MODEL, D_EXPERT), w_up.reshape(N_EXPERTS, D_MODEL, D_EXPERT),
                     w_down.reshape(N_EXPERTS, D_EXPERT, D_MODEL))
    return moe_combine(n_blk, units, dst, ys, comb, h, mod, ltri, utri)


def _dft_tables(L):
    k = np.arange(L, dtype=np.int64)
    ang = (2.0 * np.pi / (2 * L)) * ((k[:, None] * k[None, :]) % (2 * L)).astype(np.float64)
    return np.cos(ang).astype(np.float32), np.sin(ang).astype(np.float32)


def _filter_features(L):
    bands = (HY_EMB - 1) // 2
    t = np.linspace(0.0, 1.0, L, dtype=np.float32).astype(np.float64)[:, None]
    w = (2.0 * np.pi / L) * np.arange(L, dtype=np.float64)[:, None]
    fb = np.linspace(1e-4, bands - 1, bands, dtype=np.float32).astype(np.float64)[None, :]
    z = np.concatenate([t, np.cos(fb * w), -np.sin(fb * w)], axis=-1)
    zp = np.zeros((L, FEAT_PAD), np.float32)
    zp[:, :HY_EMB] = z
    deltas = np.abs(np.linspace(HY_MIN_DECAY, HY_MAX_DECAY, HY_CH, dtype=np.float32).astype(np.float64))
    decay = np.exp(-t * deltas[None, :]).astype(np.float32)
    return zp, decay


def _rope_table(cos, sin, half, tm):
    S, width = cos.shape
    low = (np.arange(width) % (2 * half)) < half
    tab = np.zeros((3, S + tm, width), np.float32)
    tab[0, :S] = cos
    tab[0, S:] = 1.0
    tab[1, :S] = np.where(low[None, :], 0.0, sin)
    tab[2, :S] = np.where(low[None, :], -sin, 0.0)
    return tab


def _axial_rope_table(head_dim, tm):
    rows = SEQ // GRID_W
    nf = head_dim // 4
    row = np.repeat(np.arange(rows), GRID_W).astype(np.float64)
    col = np.tile(np.arange(GRID_W), rows).astype(np.float64)
    inv = ROPE_BASE ** (-np.arange(nf, dtype=np.float64) / nf)
    ang = np.stack([row[:, None] * inv, col[:, None] * inv], axis=1)
    a = np.broadcast_to(ang[:, :, None, :], (SEQ, 2, 2, nf)).reshape(SEQ, head_dim)
    reps = LANES // head_dim
    a = np.tile(a, (1, reps))
    return _rope_table(np.cos(a), np.sin(a), nf, tm)


def _seq_rope_table(head_dim, tm):
    inv = 1.0 / (ROPE_BASE ** np.linspace(0.0, 1.0, head_dim // 2, dtype=np.float32).astype(np.float64))
    ang = np.arange(SEQ, dtype=np.float64)[:, None] * inv
    a = np.concatenate([ang, ang], axis=1)
    return _rope_table(np.cos(a), np.sin(a), head_dim // 2, tm)


def _group_mean_matrix():
    g = np.arange(SEG) // DA_HD
    return (g[:, None] == g[None, :]).astype(np.float32) / DA_HD


def _router_weights(w_grp, b_grp, w_rt, b_rt):
    pad = ROUTE_W - 2 * N_EXPERTS
    wr = jnp.concatenate([w_rt, jnp.repeat(w_grp, EXP_PER_GROUP, axis=1),
                          jnp.zeros((D_MODEL, pad), f32)], axis=1)
    br = jnp.concatenate([b_rt, jnp.repeat(b_grp, EXP_PER_GROUP), jnp.zeros((pad,), f32)])[None, :]
    return wr, br


def _strict_lower(n):
    i = np.arange(n)
    return (i[None, :] < i[:, None]).astype(np.float32)


def kernel(x, c, ctx, c_ctx, ada_w, ada_b, e_w_in, e_w_out, hy_conv_w, hy_conv_b, hy_f_w1, hy_f_b1, hy_f_w2, hy_f_b2, hy_f_w3, hy_f_freq, hy_bias, da_q_norm, da_k_norm, da_lam, da_subln, o_w_in, o_w_out, ret_decay, ret_gn, gq_q_norm, gq_k_norm, gq_sink, moe_w_grp, moe_b_grp, moe_w_rt, moe_b_rt, moe_w_gate, moe_w_up, moe_w_down):
    assert x.shape == (BATCH, SEQ, D_MODEL) and ctx.shape == (BATCH, CTX_LEN, D_MODEL)
    x_rows = x.reshape(T_LAT, D_MODEL)
    ctx_rows = ctx.reshape(T_CTX, D_MODEL)
    c_rows = jnp.concatenate([c, c_ctx[None, :], jnp.zeros((MOD_ROWS - BATCH - 1, D_MODEL), f32)], axis=0)
    mod = ada_modulation(c_rows, ada_w, ada_b)

    gmat = jnp.asarray(_group_mean_matrix()).astype(bf16)
    ax_tab = jnp.asarray(_axial_rope_table(DA_HD, PROJ_TM))
    r1_tab = jnp.asarray(_seq_rope_table(RET_DK, PROJ_TM))
    ones = jnp.ones((SEG,), f32)

    lam_init0 = 0.8 - 0.6 * math.exp(-0.3 * 0)
    reps = SEG // DA_HD
    gain0 = jnp.concatenate([ones, ones, ones, jnp.tile(da_q_norm[0], reps) * DA_HD ** -0.5,
                             jnp.tile(da_k_norm[0], reps), ones])[None, :]
    proj0 = in_projection("even", [x_rows, ctx_rows], mod[0], e_w_in[0].astype(bf16), gain0, gmat, [ax_tab])

    w3r = hy_f_w3[0].reshape(HY_FILT_HID, 4, HY_CH).transpose(1, 0, 2)
    w1p = jnp.concatenate([hy_f_w1[0], jnp.zeros((FEAT_PAD - HY_EMB, HY_FILT_HID), f32)], axis=0)
    y_hy = []
    for L, blk0 in ((SEQ, 0), (CTX_LEN, T_LAT // CTX_LEN)):
        zfeat, decay = _filter_features(L)
        cm, sm = _dft_tables(L)
        cm = jnp.asarray(cm).astype(bf16)
        sm = jnp.asarray(sm).astype(bf16)
        spec, nyq = hyena_filter_spectra(L, jnp.asarray(zfeat), w1p, hy_f_b1[0][None, :], hy_f_w2[0],
                                         hy_f_b2[0][None, :], w3r, hy_f_freq[0], jnp.asarray(decay), cm, sm)
        y_hy.append(hyena_mix(L, blk0, proj0, hy_conv_w[0], hy_conv_b[0][None, :], spec, nyq, hy_bias[0], cm, sm))

    y_da = diff_attention(proj0, da_lam[0], da_subln[0][None, :], lam_init0)

    wr0, br0 = _router_weights(moe_w_grp[0], moe_b_grp[0], moe_w_rt[0], moe_b_rt[0])
    ltri = jnp.asarray(_strict_lower(MOE_TB)).astype(bf16)
    utri = jnp.asarray(_strict_lower(ROUTE_W).T).astype(bf16)
    h, v, comb, counts = out_projection(T_ALL, y_hy, y_da, e_w_out[0].astype(bf16), [x_rows, ctx_rows], mod[0],
                                        wr0, br0)
    h = sparse_moe(T_ALL, v, comb, counts, h, mod[0], moe_w_gate[0], moe_w_up[0], moe_w_down[0], ltri, utri)

    w_in1 = jnp.concatenate([o_w_in[0], jnp.zeros((D_MODEL, PROJ_W - o_w_in.shape[2]), f32)], axis=1).astype(bf16)
    kq = GQ_KV * GQ_HD
    gain1 = jnp.concatenate([ones, ones * RET_DK ** -0.5, ones, ones,
                             jnp.tile(gq_q_norm[0], reps) * GQ_HD ** -0.5,
                             jnp.tile(gq_k_norm[0], kq // GQ_HD), jnp.ones((SEG - kq,), f32)])[None, :]
    proj1 = in_projection("odd", [h], mod[1], w_in1, gain1, gmat, [ax_tab, r1_tab])
    y_ret = retention(proj1, ret_decay[0], ret_gn[0][None, :])
    y_gq = window_gqa(proj1, gq_sink[0])
    wr1, br1 = _router_weights(moe_w_grp[1], moe_b_grp[1], moe_w_rt[1], moe_b_rt[1])
    h_lat, v, comb, counts = out_projection(T_LAT, [y_ret], y_gq, o_w_out[0].astype(bf16), [h], mod[1], wr1, br1)
    out = sparse_moe(T_LAT, v, comb, counts, h_lat, mod[1], moe_w_gate[1], moe_w_up[1], moe_w_down[1], ltri, utri)
    return out.reshape(BATCH, SEQ, D_MODEL)
```

```python
import functools
import math

import numpy as np
import jax
import jax.numpy as jnp
from jax import lax
from jax.experimental import pallas as pl
from jax.experimental.pallas import tpu as pltpu

f32 = jnp.float32
bf16 = jnp.bfloat16

D_MODEL = 1024
BATCH = 8
SEQ = 2048
DEPTH = 2
GRID_W = 64
CTX_LEN = 256
EPS = 1e-6
NEG_INF = -1e30
LOG2E = math.log2(math.e)
ROPE_BASE = 10000.0
HY_CH = D_MODEL // 2
HY_EMB = 33
HY_FILT_HID = 64
HY_MAX_DECAY = math.log(1e-2) / 0.3
HY_MIN_DECAY = math.log(1e-2) / 1.5
DA_HEADS = 4
DA_HD = D_MODEL // 16
RET_HEADS = 4
RET_DK = D_MODEL // 8
RET_CHUNK = 128
GQ_KV = 2
GQ_GROUP = 4
GQ_HD = D_MODEL // 16
WINDOW = 128
N_GROUPS = 4
EXP_PER_GROUP = 8
N_EXPERTS = N_GROUPS * EXP_PER_GROUP
D_EXPERT = D_MODEL // 4

T_LAT = BATCH * SEQ
T_CTX = BATCH * CTX_LEN
T_ALL = T_LAT + T_CTX
PROJ_W = 3072
SEG = 512
CTX_MOD_ROW = BATCH
MOD_ROWS = 16

LANES = 128
VMEM_LIMIT_BYTES = 56 * 1024 * 1024


def _cparams(*sem):
    return pltpu.CompilerParams(dimension_semantics=sem, vmem_limit_bytes=VMEM_LIMIT_BYTES)


def _dot(a, b):
    return jnp.dot(a, b, preferred_element_type=f32)


def _dot_nt(a, b):
    return lax.dot_general(a, b, (((1,), (1,)), ((), ())), preferred_element_type=f32)


def _split(x):
    hi = x.astype(bf16)
    lo = (x - hi.astype(f32)).astype(bf16)
    return hi, lo


def _dot3(a, b):
    ah, al = _split(a)
    bh, bl = _split(b)
    return _dot(ah, bh) + _dot(al, bh) + _dot(ah, bl)


def _dot2(a, b_bf16):
    ah, al = _split(a)
    return _dot(ah, b_bf16) + _dot(al, b_bf16)


def _silu(x):
    return x * jax.nn.sigmoid(x)


def _rms(x):
    return x * lax.rsqrt(jnp.mean(x * x, axis=-1, keepdims=True) + EPS)


def _const_spec(shape):
    nd = len(shape)
    return pl.BlockSpec(shape, lambda *_: (0,) * nd)


def _const_spec1(shape):
    nd = len(shape)
    return pl.BlockSpec(shape, lambda *_: (0,) * nd, pipeline_mode=pl.Buffered(1))


ADA_TN = 1536


def _ada_kernel(c_ref, w_ref, b_ref, o_ref):
    x = _silu(c_ref[...])
    o_ref[0] = _dot3(x, w_ref[0]) + b_ref[0]


def ada_modulation(c_rows, ada_w, ada_b):
    n = 6 * D_MODEL
    return pl.pallas_call(
        _ada_kernel,
        grid=(DEPTH, n // ADA_TN),
        in_specs=[
            pl.BlockSpec((MOD_ROWS, D_MODEL), lambda l, j: (0, 0)),
            pl.BlockSpec((1, D_MODEL, ADA_TN), lambda l, j: (l, 0, j)),
            pl.BlockSpec((1, 1, ADA_TN), lambda l, j: (l, 0, j)),
        ],
        out_specs=pl.BlockSpec((1, MOD_ROWS, ADA_TN), lambda l, j: (l, 0, j)),
        out_shape=jax.ShapeDtypeStruct((DEPTH, MOD_ROWS, n), f32),
        compiler_params=_cparams("arbitrary", "arbitrary"),
        name="ada_modulation",
    )(c_rows, ada_w, ada_b.reshape(DEPTH, 1, n))


PROJ_TM = 512


def _mod_row(i, tm):
    return jnp.minimum((i * tm) // SEQ, CTX_MOD_ROW)


def _tile4(t):
    return jnp.concatenate([t, t, t, t], axis=1)


def _group_norm64(y, gmat):
    ms = _dot2(y * y, gmat)
    return y * lax.rsqrt(ms + EPS)


def _rope(y, tab, shift):
    w = y.shape[1]
    return y * tab[0] + pltpu.roll(y, shift, 1) * tab[1] + pltpu.roll(y, w - shift, 1) * tab[2]


def _stacked_specs(tm, width):
    n_lat = T_LAT // tm
    return [pl.BlockSpec((tm, width), lambda i: (jnp.minimum(i, n_lat - 1), 0)),
            pl.BlockSpec((tm, width), lambda i: (jnp.maximum(i - n_lat, 0), 0))]


def _stacked_tile(i, tm, lat_ref, ctx_ref):
    return jnp.where(i < T_LAT // tm, lat_ref[...], ctx_ref[...])


def _inproj_kernel(layer_kind, *refs):
    i = pl.program_id(0)
    if layer_kind == "even":
        x_ref, c_ref, mod_ref, w_ref, gain_ref, gmat_ref, ax_ref, o_ref = refs
        h = _stacked_tile(i, PROJ_TM, x_ref, c_ref)
    else:
        h_ref, mod_ref, w_ref, gain_ref, gmat_ref, ax_ref, r1_ref, o_ref = refs
        h = h_ref[...]
    r = _mod_row(i, PROJ_TM)
    sh = mod_ref[pl.ds(r, 1), pl.ds(0, D_MODEL)]
    sc = mod_ref[pl.ds(r, 1), pl.ds(D_MODEL, D_MODEL)]
    u = (_rms(h) * (1.0 + sc) + sh).astype(bf16)

    def seg(j):
        return _dot(u, w_ref[:, j * SEG:(j + 1) * SEG])

    def put(j, y):
        o_ref[:, j * SEG:(j + 1) * SEG] = y.astype(bf16)

    def gain(j):
        return gain_ref[:, j * SEG:(j + 1) * SEG]

    gmat = gmat_ref[...]
    ax = ax_ref[...]
    ax4 = (_tile4(ax[0]), _tile4(ax[1]), _tile4(ax[2]))
    if layer_kind == "even":
        for j in (0, 1, 2, 5):
            put(j, seg(j))
        for j in (3, 4):
            put(j, _rope(_group_norm64(seg(j), gmat) * gain(j), ax4, DA_HD // 4))
    else:
        r1 = r1_ref[...]
        r14 = (_tile4(r1[0]), _tile4(r1[1]), _tile4(r1[2]))
        for j in (0, 1):
            put(j, _rope(seg(j) * gain(j), r14, RET_DK // 2))
        for j in (2, 3):
            put(j, seg(j))
        put(4, _rope(_group_norm64(seg(4), gmat) * gain(4), ax4, GQ_HD // 4))
        y = seg(5)
        kw = GQ_KV * GQ_HD
        yk = _rope(_group_norm64(y[:, :kw], gmat[:kw, :kw]) * gain(5)[:, :kw], ax, GQ_HD // 4)
        yv = y[:, kw:2 * kw]
        pieces = (yk, yv, pltpu.roll(yk, GQ_HD, 1), pltpu.roll(yv, GQ_HD, 1))
        for p, piece in enumerate(pieces):
            o_ref[:, 5 * SEG + p * kw:5 * SEG + (p + 1) * kw] = piece.astype(bf16)


def in_projection(layer_kind, hs, mod, w, gain, gmat, tables):
    n_lat_tiles = T_LAT // PROJ_TM
    n_pos_tiles = SEQ // PROJ_TM

    def tab_map(i):
        return (0, jnp.where(i < n_lat_tiles, i % n_pos_tiles, n_pos_tiles), 0)

    tab_specs = [pl.BlockSpec((3, PROJ_TM, LANES), tab_map) for _ in tables]
    if layer_kind == "even":
        h_specs = _stacked_specs(PROJ_TM, D_MODEL)
    else:
        h_specs = [pl.BlockSpec((PROJ_TM, D_MODEL), lambda i: (i, 0))]
    return pl.pallas_call(
        functools.partial(_inproj_kernel, layer_kind),
        grid=(T_ALL // PROJ_TM,),
        in_specs=h_specs + [
            _const_spec((MOD_ROWS, 6 * D_MODEL)),
            _const_spec((D_MODEL, PROJ_W)),
            _const_spec((1, PROJ_W)),
            _const_spec((SEG, SEG)),
        ] + tab_specs,
        out_specs=pl.BlockSpec((PROJ_TM, PROJ_W), lambda i: (i, 0)),
        out_shape=jax.ShapeDtypeStruct((T_ALL, PROJ_W), bf16),
        compiler_params=_cparams("parallel"),
        name="in_projection_" + layer_kind,
    )(*hs, mod, w, gain, gmat, *tables)


HY_TC = 256
HY_FREQ_CHUNK = 512
FEAT_PAD = 64


def _alt_sign(shape, axis):
    idx = lax.broadcasted_iota(jnp.int32, shape, axis)
    return jnp.where((idx & 1) == 0, 1.0, -1.0).astype(f32)


def _filter_kernel(L, z_ref, w1_ref, b1_ref, w2_ref, b2_ref, wf_ref, wb_ref, freq_ref, dec_ref, c_ref, s_ref,
                   spec_ref, nyq_ref):
    hid = jnp.sin(freq_ref[0:1, :] * (_dot3(z_ref[...], w1_ref[...]) + b1_ref[...]))
    hid = jnp.sin(freq_ref[1:2, :] * (_dot3(hid, w2_ref[...]) + b2_ref[...]))
    dec = dec_ref[...]
    fwd = _dot3(hid, wf_ref[0]) * dec
    bwd = _dot3(hid, wb_ref[0]) * dec
    row = lax.broadcasted_iota(jnp.int32, fwd.shape, 0)
    bwd = jnp.where(row == 0, 0.0, bwd)
    even = fwd + bwd
    odd = bwd - fwd
    wk = jnp.where(row == 0, 0.5 / L, 1.0 / L).astype(f32)
    spec_ref[0, 0] = _dot(c_ref[...], even.astype(bf16)) * wk
    spec_ref[0, 1] = _dot(s_ref[...], odd.astype(bf16)) * wk
    nyq = jnp.sum(even * _alt_sign(even.shape, 0), axis=0, keepdims=True) * (0.5 / L)
    nyq_ref[0] = jnp.broadcast_to(nyq, (8, nyq.shape[1]))


def hyena_filter_spectra(L, zfeat, w1, b1, w2, b2, w3r, freq, decay, cmat, smat):
    nct = HY_CH // HY_TC
    return pl.pallas_call(
        functools.partial(_filter_kernel, L),
        grid=(2, nct),
        in_specs=[
            _const_spec((L, FEAT_PAD)),
            _const_spec((FEAT_PAD, HY_FILT_HID)),
            _const_spec((1, HY_FILT_HID)),
            _const_spec((HY_FILT_HID, HY_FILT_HID)),
            _const_spec((1, HY_FILT_HID)),
            pl.BlockSpec((1, HY_FILT_HID, HY_TC), lambda n, c: (2 * n, 0, c)),
            pl.BlockSpec((1, HY_FILT_HID, HY_TC), lambda n, c: (2 * n + 1, 0, c)),
            _const_spec((2, HY_FILT_HID)),
            pl.BlockSpec((L, HY_TC), lambda n, c: (0, c)),
            _const_spec1((L, L)),
            _const_spec1((L, L)),
        ],
        out_specs=[
            pl.BlockSpec((1, 2, L, HY_TC), lambda n, c: (n, 0, 0, c)),
            pl.BlockSpec((1, 8, HY_TC), lambda n, c: (n, 0, c)),
        ],
        out_shape=[
            jax.ShapeDtypeStruct((2, 2, L, HY_CH), f32),
            jax.ShapeDtypeStruct((2, 8, HY_CH), f32),
        ],
        compiler_params=_cparams("arbitrary", "arbitrary"),
        name="hyena_filter_L%d" % L,
    )(zfeat, w1, b1, w2, b2, w3r, w3r, freq, decay, cmat, smat)


def _conv3(u, w, b):
    L = u.shape[0]
    row = lax.broadcasted_iota(jnp.int32, u.shape, 0)
    prev = jnp.where(row == 0, 0.0, pltpu.roll(u, 1, 0))
    nxt = jnp.where(row == L - 1, 0.0, pltpu.roll(u, L - 1, 0))
    return prev * w[0:1, :] + u * w[1:2, :] + nxt * w[2:3, :] + b


def _hyena_kernel(v_ref, x1_ref, x2_ref, wv_ref, w1_ref, w2_ref, bv_ref, b1_ref, b2_ref, spec_ref, nyq_ref,
                  bias_ref, c_ref, s_ref, *rest):
    o_ref, yr_ref, yi_ref = rest[-3:]
    L = v_ref.shape[0]
    fch = min(L, HY_FREQ_CHUNK)
    z = _conv3(v_ref[...].astype(f32), wv_ref[...], bv_ref[...])
    gate_refs = ((x1_ref, w1_ref, b1_ref), (x2_ref, w2_ref, b2_ref))
    alt = _alt_sign(z.shape, 0)
    for n in range(2):
        zb = z.astype(bf16)
        for k in range(L // fch):
            rows = slice(k * fch, (k + 1) * fch)
            a = _dot(c_ref[rows, :], zb)
            b = _dot(s_ref[rows, :], zb)
            hr = spec_ref[n, 0, rows, :]
            hi = spec_ref[n, 1, rows, :]
            yr_ref[rows, :] = (a * hr + b * hi).astype(bf16)
            yi_ref[rows, :] = (a * hi - b * hr).astype(bf16)
        x_nyq = jnp.sum(z * alt, axis=0, keepdims=True)
        y = (_dot(c_ref[...], yr_ref[...]) - _dot(s_ref[...], yi_ref[...])
             + alt * (x_nyq * nyq_ref[n, 0:1, :]))
        x_ref, w_ref, b_ref = gate_refs[n]
        gate = _conv3(x_ref[...].astype(f32), w_ref[...], b_ref[...])
        z = gate * (y + z * bias_ref[n:n + 1, :])
    o_ref[...] = z.astype(bf16)


def hyena_mix(L, row_block0, proj, conv_w, conv_b, spec, nyq, bias, cmat, smat):
    nct = HY_CH // HY_TC
    nseg = HY_CH // HY_TC

    def col(k):
        return lambda c, b: (row_block0 + b, k * nseg + c)

    def par(k):
        return lambda c, b: (0, k * nseg + c)

    in_specs = (
        [pl.BlockSpec((L, HY_TC), col(k)) for k in range(3)]
        + [pl.BlockSpec((3, HY_TC), par(k)) for k in range(3)]
        + [pl.BlockSpec((1, HY_TC), par(k)) for k in range(3)]
        + [
            pl.BlockSpec((2, 2, L, HY_TC), lambda c, b: (0, 0, 0, c), pipeline_mode=pl.Buffered(1)),
            pl.BlockSpec((2, 8, HY_TC), lambda c, b: (0, 0, c)),
            pl.BlockSpec((2, HY_TC), lambda c, b: (0, c)),
            _const_spec1((L, L)),
            _const_spec1((L, L)),
        ]
    )
    args = [proj, proj, proj, conv_w, conv_w, conv_w, conv_b, conv_b, conv_b, spec, nyq, bias, cmat, smat]
    return pl.pallas_call(
        _hyena_kernel,
        grid=(nct, BATCH),
        in_specs=in_specs,
        out_specs=pl.BlockSpec((L, HY_TC), lambda c, b: (b, c)),
        out_shape=jax.ShapeDtypeStruct((BATCH * L, HY_CH), bf16),
        scratch_shapes=[pltpu.VMEM((L, HY_TC), bf16), pltpu.VMEM((L, HY_TC), bf16)],
        compiler_params=_cparams("arbitrary", "arbitrary"),
        name="hyena_mix_L%d" % L,
    )(*args)


DA_TQ = 256


def _diff_attn_kernel(lam_init, q_ref, kc_ref, vc_ref, kl_ref, vl_ref, lam_ref, subln_ref, o_ref):
    i = pl.program_id(1)
    n_lat_blocks = SEQ // DA_TQ

    @pl.when(i < n_lat_blocks)
    def _():
        _diff_attn_body(lam_init, q_ref, (kc_ref, vc_ref, kl_ref, vl_ref), lam_ref, subln_ref, o_ref)

    @pl.when(i == n_lat_blocks)
    def _():
        _diff_attn_body(lam_init, q_ref, (kc_ref, vc_ref), lam_ref, subln_ref, o_ref)


def _diff_attn_body(lam_init, q_ref, kv_refs, lam_ref, subln_ref, o_ref):
    n_src = len(kv_refs) // 2
    lp = lam_ref[...]
    lam = (jnp.exp(jnp.sum(lp[0:1] * lp[1:2], axis=-1, keepdims=True))
           - jnp.exp(jnp.sum(lp[2:3] * lp[3:4], axis=-1, keepdims=True)) + lam_init)
    q = q_ref[...]
    lane = lax.broadcasted_iota(jnp.int32, (q.shape[0], 2 * DA_HD), 1)
    hw = 2 * DA_HD
    for h in range(DA_HEADS):
        qh = q[:, h * hw:(h + 1) * hw]
        ks = [kv_refs[2 * s][:, h * hw:(h + 1) * hw] for s in range(n_src)]
        vs = [kv_refs[2 * s + 1][:, h * hw:(h + 1) * hw] for s in range(n_src)]
        es = []
        invs = []
        for m in range(2):
            qm = jnp.where((lane < DA_HD) == (m == 0), qh, jnp.zeros_like(qh))
            ss = [_dot_nt(qm, k) for k in ks]
            mx = functools.reduce(jnp.maximum, [jnp.max(s, axis=-1, keepdims=True) for s in ss])
            e = [jnp.exp2(s - mx) for s in ss]
            den = functools.reduce(jnp.add, [jnp.sum(x, axis=-1, keepdims=True) for x in e])
            es.append(functools.reduce(jnp.add, [_dot(e[s].astype(bf16), vs[s]) for s in range(n_src)]))
            invs.append(1.0 / den)
        oh = es[0] * invs[0] - es[1] * (lam * invs[1])
        oh = _rms(oh) * subln_ref[...] * (1.0 - lam_init)
        o_ref[:, h * hw:(h + 1) * hw] = oh.astype(bf16)


def diff_attention(proj, lam_p, subln, lam_init):
    assert CTX_LEN == DA_TQ
    width = DA_HEADS * 2 * DA_HD
    qcol, kcol, vcol = 3, 4, 5
    ctx_blk0 = T_LAT // CTX_LEN
    nq = SEQ // DA_TQ

    def q_rows(b, i):
        return jnp.where(i < nq, b * nq + i, ctx_blk0 + b)

    return pl.pallas_call(
        functools.partial(_diff_attn_kernel, lam_init),
        grid=(BATCH, nq + 1),
        in_specs=[
            pl.BlockSpec((DA_TQ, width), lambda b, i: (q_rows(b, i), qcol)),
            pl.BlockSpec((CTX_LEN, width), lambda b, i: (ctx_blk0 + b, kcol)),
            pl.BlockSpec((CTX_LEN, width), lambda b, i: (ctx_blk0 + b, vcol)),
            pl.BlockSpec((SEQ, width), lambda b, i: (b, kcol)),
            pl.BlockSpec((SEQ, width), lambda b, i: (b, vcol)),
            _const_spec((4, DA_HD)),
            _const_spec((1, 2 * DA_HD)),
        ],
        out_specs=pl.BlockSpec((DA_TQ, width), lambda b, i: (q_rows(b, i), 0)),
        out_shape=jax.ShapeDtypeStruct((T_ALL, width), bf16),
        compiler_params=_cparams("parallel", "arbitrary"),
        name="diff_attention",
    )(proj, proj, proj, proj, proj, lam_p, subln)


def _log_sigmoid(x):
    return jnp.minimum(x, 0.0) - jnp.log(1.0 + jnp.exp(-jnp.abs(x)))


def _retention_kernel(q_ref, k_ref, v_ref, g_ref, kc_ref, vc_ref, decay_ref, gn_ref, o_ref, st_ref):
    h = pl.program_id(1)
    ch = RET_CHUNK
    nchunk = SEQ // ch
    lgs = _log_sigmoid(decay_ref[...])
    sel = lax.broadcasted_iota(jnp.int32, lgs.shape, 1) == h
    lg = jnp.sum(jnp.where(sel, lgs, 0.0), axis=-1, keepdims=True)
    lgf = lg[0:1, :]
    lgb = lg[1:2, :]
    ri = lax.broadcasted_iota(jnp.int32, (ch, ch), 0).astype(f32)
    ci = lax.broadcasted_iota(jnp.int32, (ch, ch), 1).astype(f32)
    rel = ri - ci
    dsum = (jnp.where(rel >= 0, jnp.exp(jnp.maximum(rel, 0.0) * lgf), 0.0)
            + jnp.where(rel <= 0, jnp.exp(jnp.maximum(-rel, 0.0) * lgb), 0.0))
    zeta_f = jnp.exp((ch - 1 - ci) * lgf)
    zeta_b = jnp.exp(ci * lgb)
    xi_f = jnp.exp((ri + 1.0) * lgf)
    xi_b = jnp.exp((ch - ri) * lgb)
    gch_f = jnp.exp(ch * lgf)
    gch_b = jnp.exp(ch * lgb)
    dk = q_ref.shape[1]

    kct = kc_ref[...].astype(f32).T
    vc = vc_ref[...]
    cl = lax.broadcasted_iota(jnp.int32, kct.shape, 1).astype(f32)
    s_f = _dot((kct * jnp.exp((CTX_LEN - 1 - cl) * lgf)).astype(bf16), vc)
    s_b = _dot((kct * jnp.exp(cl * lgb)).astype(bf16), vc)

    def rows(n):
        return slice(n * ch, (n + 1) * ch)

    u_f, u_b = [], []
    for n in range(nchunk):
        kt = k_ref[rows(n), :].astype(f32).T
        vn = v_ref[rows(n), :]
        u_f.append(_dot((kt * zeta_f).astype(bf16), vn))
        u_b.append(_dot((kt * zeta_b).astype(bf16), vn))

    for n in range(nchunk):
        st_ref[n, 0:dk, :] = s_f.astype(bf16)
        s_f = gch_f * s_f + u_f[n]
    for n in reversed(range(nchunk)):
        st_ref[n, dk:2 * dk, :] = s_b.astype(bf16)
        s_b = gch_b * s_b + u_b[n]

    gn = gn_ref[...]
    for n in range(nchunk):
        qn = q_ref[rows(n), :]
        att = _dot_nt(qn, k_ref[rows(n), :]) * dsum
        qf = qn.astype(f32)
        lhs = jnp.concatenate([att.astype(bf16), (qf * xi_f).astype(bf16), (qf * xi_b).astype(bf16)], axis=1)
        rhs = jnp.concatenate([v_ref[rows(n), :], st_ref[n]], axis=0)
        o = _dot(lhs, rhs)
        mu = jnp.mean(o, axis=-1, keepdims=True)
        oc = o - mu
        var = jnp.mean(oc * oc, axis=-1, keepdims=True)
        y = oc * lax.rsqrt(var + EPS) * gn * _silu(g_ref[rows(n), :].astype(f32))
        o_ref[rows(n), :] = y.astype(bf16)


def retention(proj, decay, gn_w):
    dk = RET_DK
    ctx_blk0 = T_LAT // CTX_LEN
    return pl.pallas_call(
        _retention_kernel,
        grid=(BATCH, RET_HEADS),
        in_specs=[
            pl.BlockSpec((SEQ, dk), lambda b, h: (b, h)),
            pl.BlockSpec((SEQ, dk), lambda b, h: (b, RET_HEADS + h)),
            pl.BlockSpec((SEQ, dk), lambda b, h: (b, 2 * RET_HEADS + h)),
            pl.BlockSpec((SEQ, dk), lambda b, h: (b, 3 * RET_HEADS + h)),
            pl.BlockSpec((CTX_LEN, dk), lambda b, h: (ctx_blk0 + b, RET_HEADS + h)),
            pl.BlockSpec((CTX_LEN, dk), lambda b, h: (ctx_blk0 + b, 2 * RET_HEADS + h)),
            _const_spec((2, RET_HEADS)),
            pl.BlockSpec((1, dk), lambda b, h: (0, h)),
        ],
        out_specs=pl.BlockSpec((SEQ, dk), lambda b, h: (b, h)),
        out_shape=jax.ShapeDtypeStruct((T_LAT, RET_HEADS * dk), bf16),
        scratch_shapes=[pltpu.VMEM((SEQ // RET_CHUNK, 2 * dk, dk), bf16)],
        compiler_params=_cparams("parallel", "arbitrary"),
        name="retention",
    )(proj, proj, proj, proj, proj, proj, decay, gn_w)


GQ_TQ = 128
GQ_SPAN = 3 * GQ_TQ


def _gqa_kernel(q_ref, kv_ref, kvc_ref, sink_ref, o_ref):
    n = pl.program_id(1)
    start = pl.multiple_of(jnp.clip((n - 1) * GQ_TQ, 0, SEQ - GQ_SPAN), GQ_TQ)
    pw = 2 * GQ_HD
    n_heads = GQ_KV * GQ_GROUP
    n_keys = CTX_LEN + GQ_SPAN
    col = lax.broadcasted_iota(jnp.int32, (GQ_TQ, n_keys), 1)
    qpos = n * GQ_TQ + lax.broadcasted_iota(jnp.int32, (GQ_TQ, n_keys), 0)
    dist = jnp.where(col < CTX_LEN, 0, jnp.abs(start + col - CTX_LEN - qpos))
    mask = dist <= WINDOW
    q = q_ref[...]
    lower = lax.broadcasted_iota(jnp.int32, (GQ_TQ, pw), 1) < GQ_HD
    outs = [None] * n_heads
    for swapped in (0, 1):
        kcol = slice(2 * swapped * pw, (2 * swapped + 1) * pw)
        vcol = slice((2 * swapped + 1) * pw, (2 * swapped + 2) * pw)
        k = jnp.concatenate([kvc_ref[:, kcol], kv_ref[pl.ds(start, GQ_SPAN), kcol]], axis=0)
        v = jnp.concatenate([kvc_ref[:, vcol], kv_ref[pl.ds(start, GQ_SPAN), vcol]], axis=0)
        heads = [h for h in range(n_heads) if ((h // GQ_GROUP) == (h % 2)) == (swapped == 0)]
        qs = jnp.concatenate(
            [jnp.where(lower == (h % 2 == 0), q[:, (h // 2) * pw:(h // 2 + 1) * pw], jnp.zeros((), bf16))
             for h in heads], axis=0)
        s = _dot_nt(qs, k)
        ps = []
        for i, h in enumerate(heads):
            sh = jnp.where(mask, s[i * GQ_TQ:(i + 1) * GQ_TQ], NEG_INF)
            sink = sink_ref[h]
            mx = jnp.maximum(jnp.max(sh, axis=-1, keepdims=True), sink)
            e = jnp.exp(sh - mx)
            inv = 1.0 / (jnp.sum(e, axis=-1, keepdims=True) + jnp.exp(sink - mx))
            ps.append((e * inv).astype(bf16))
        o = _dot(jnp.concatenate(ps, axis=0), v)
        for i, h in enumerate(heads):
            outs[h] = o[i * GQ_TQ:(i + 1) * GQ_TQ]
    o_ref[...] = jnp.concatenate(
        [jnp.where(lower, outs[2 * j], outs[2 * j + 1]) for j in range(n_heads // 2)], axis=1).astype(bf16)


def window_gqa(proj, sink):
    width = GQ_KV * GQ_GROUP * GQ_HD
    nq = SEQ // GQ_TQ
    kvw = 4 * GQ_KV * GQ_HD
    kv_col = (5 * SEG) // kvw
    ctx_blk0 = T_LAT // CTX_LEN
    return pl.pallas_call(
        _gqa_kernel,
        grid=(BATCH, nq),
        in_specs=[
            pl.BlockSpec((GQ_TQ, width), lambda b, n: (b * nq + n, 4)),
            pl.BlockSpec((SEQ, kvw), lambda b, n: (b, kv_col)),
            pl.BlockSpec((CTX_LEN, kvw), lambda b, n: (ctx_blk0 + b, kv_col)),
            pl.BlockSpec(memory_space=pltpu.SMEM),
        ],
        out_specs=pl.BlockSpec((GQ_TQ, width), lambda b, n: (b * nq + n, 0)),
        out_shape=jax.ShapeDtypeStruct((T_LAT, width), bf16),
        compiler_params=_cparams("parallel", "arbitrary"),
        name="window_gqa",
    )(proj, proj, proj, sink)


OUT_TM = 512
ROUTE_W = LANES
MOE_TB = 256


def _route(logits):
    lane_i = lax.broadcasted_iota(jnp.int32, logits.shape, 1)
    lane = lane_i.astype(f32)
    big = float(1 << 20)
    valid = lane_i < N_EXPERTS
    le = logits
    lgx = pltpu.roll(logits, ROUTE_W - N_EXPERTS, 1)
    lgx = jnp.where(valid, lgx, NEG_INF)
    gmax = jnp.max(lgx, axis=-1, keepdims=True)
    grp = (lane_i // EXP_PER_GROUP).astype(f32)
    g_sel = jnp.min(jnp.where(lgx == gmax, grp, big), axis=-1, keepdims=True)
    p_grp = float(EXP_PER_GROUP) / jnp.sum(jnp.exp(lgx - gmax), axis=-1, keepdims=True)
    lm = jnp.where(valid, jnp.where(grp == g_sel, le, NEG_INF), NEG_INF)
    v1 = jnp.max(lm, axis=-1, keepdims=True)
    i1 = jnp.min(jnp.where(lm == v1, lane, big), axis=-1, keepdims=True)
    lm2 = jnp.where(lane == i1, NEG_INF, lm)
    v2 = jnp.max(lm2, axis=-1, keepdims=True)
    i2 = jnp.min(jnp.where(lm2 == v2, lane, big), axis=-1, keepdims=True)
    e2 = jnp.exp(v2 - v1)
    w1 = p_grp / (1.0 + e2)
    w2 = w1 * e2
    return jnp.where(lane == i1, w1, 0.0) + jnp.where(lane == i2, w2, 0.0)


def _outproj_kernel(stacked, *refs):
    i = pl.program_id(0)
    if stacked:
        (ya_ref, yac_ref, yb_ref, w_ref, x_ref, c_ref, mod_ref, wr_ref, br_ref,
         hn_ref, v_ref, comb_ref, cnt_ref) = refs
        ya = _stacked_tile(i, OUT_TM, ya_ref, yac_ref)
        h = _stacked_tile(i, OUT_TM, x_ref, c_ref)
    else:
        ya_ref, yb_ref, w_ref, h_ref, mod_ref, wr_ref, br_ref, hn_ref, v_ref, comb_ref, cnt_ref = refs
        ya = ya_ref[...]
        h = h_ref[...]
    r = _mod_row(i, OUT_TM)
    g1 = mod_ref[pl.ds(r, 1), pl.ds(2 * D_MODEL, D_MODEL)]
    sh2 = mod_ref[pl.ds(r, 1), pl.ds(3 * D_MODEL, D_MODEL)]
    sc2 = mod_ref[pl.ds(r, 1), pl.ds(4 * D_MODEL, D_MODEL)]
    half = ya.shape[1]
    m = _dot(ya, w_ref[0:half, :]) + _dot(yb_ref[...], w_ref[half:2 * half, :])
    hn = h + g1 * m
    hn_ref[...] = hn
    v = _rms(hn) * (1.0 + sc2) + sh2
    v_ref[...] = v.astype(bf16)
    comb = _route(_dot3(v, wr_ref[...]) + br_ref[...])
    comb_ref[...] = comb
    for s in range(OUT_TM // MOE_TB):
        cnt = jnp.sum((comb[s * MOE_TB:(s + 1) * MOE_TB] != 0.0).astype(f32), axis=0, keepdims=True)
        cnt_ref[s] = jnp.broadcast_to(cnt, (8, ROUTE_W)).astype(jnp.int32)


def out_projection(n_rows, yas, yb, w, hs, mod, wr, br):
    stacked = len(yas) == 2
    assert stacked == (len(hs) == 2) and (not stacked or n_rows == T_ALL)
    half = yb.shape[1]
    row_spec = lambda width: [pl.BlockSpec((OUT_TM, width), lambda i: (i, 0))]
    return pl.pallas_call(
        functools.partial(_outproj_kernel, stacked),
        grid=(n_rows // OUT_TM,),
        in_specs=(
            (_stacked_specs(OUT_TM, half) if stacked else row_spec(half))
            + row_spec(half)
            + [_const_spec((2 * half, D_MODEL))]
            + (_stacked_specs(OUT_TM, D_MODEL) if stacked else row_spec(D_MODEL))
            + [_const_spec((MOD_ROWS, 6 * D_MODEL)),
               _const_spec((D_MODEL, ROUTE_W)),
               _const_spec((1, ROUTE_W))]
        ),
        out_specs=[
            pl.BlockSpec((OUT_TM, D_MODEL), lambda i: (i, 0)),
            pl.BlockSpec((OUT_TM, D_MODEL), lambda i: (i, 0)),
            pl.BlockSpec((OUT_TM, ROUTE_W), lambda i: (i, 0)),
            pl.BlockSpec((OUT_TM // MOE_TB, 8, ROUTE_W), lambda i: (i, 0, 0)),
        ],
        out_shape=[
            jax.ShapeDtypeStruct((n_rows, D_MODEL), f32),
            jax.ShapeDtypeStruct((n_rows, D_MODEL), bf16),
            jax.ShapeDtypeStruct((n_rows, ROUTE_W), f32),
            jax.ShapeDtypeStruct((n_rows // MOE_TB, 8, ROUTE_W), jnp.int32),
        ],
        compiler_params=_cparams("parallel"),
        name="out_projection",
    )(*yas, yb, w, *hs, mod, wr, br)


MOE_UNIT = 16
MOE_TG = 512
MOE_TOP = 2
MOE_RLOC = MOE_TOP * MOE_TB + N_EXPERTS * MOE_UNIT
MOE_META = 128


def _moe_rows(n_blk):
    return n_blk * MOE_RLOC + N_EXPERTS * MOE_TG


def _moe_plan_kernel(n_blk, cnt_ref, units_ref, dst_ref, tile_exp_ref, meta_ref):
    def clear(k, c):
        tile_exp_ref[k] = 0
        return c

    lax.fori_loop(0, tile_exp_ref.shape[0], clear, 0)

    def clear_meta(k, c):
        meta_ref[k] = 0
        return c

    lax.fori_loop(0, MOE_META, clear_meta, 0)

    def per_expert(e, goff):
        def per_blk(t, acc):
            u = (cnt_ref[t * N_EXPERTS + e] + (MOE_UNIT - 1)) // MOE_UNIT
            units_ref[t * N_EXPERTS + e] = u
            dst_ref[t * N_EXPERTS + e] = goff + acc
            return acc + u * MOE_UNIT

        n_e = lax.fori_loop(0, n_blk, per_blk, 0)
        nt = (n_e + (MOE_TG - 1)) // MOE_TG
        t0 = goff // MOE_TG

        def mark(k, c):
            tile_exp_ref[t0 + k] = e
            return c

        lax.fori_loop(0, nt, mark, 0)
        meta_ref[1 + e] = goff + n_e
        meta_ref[1 + N_EXPERTS + e] = (nt * MOE_TG - n_e) // MOE_UNIT
        return goff + nt * MOE_TG

    total = lax.fori_loop(0, N_EXPERTS, per_expert, 0)
    meta_ref[0] = total // MOE_TG


def moe_plan(n_blk, counts):
    smem = pl.BlockSpec(memory_space=pltpu.SMEM)
    n_tiles = _moe_rows(n_blk) // MOE_TG
    return pl.pallas_call(
        functools.partial(_moe_plan_kernel, n_blk),
        in_specs=[smem],
        out_specs=[smem, smem, smem, smem],
        out_shape=[
            jax.ShapeDtypeStruct((n_blk * N_EXPERTS,), jnp.int32),
            jax.ShapeDtypeStruct((n_blk * N_EXPERTS,), jnp.int32),
            jax.ShapeDtypeStruct((n_tiles,), jnp.int32),
            jax.ShapeDtypeStruct((MOE_META,), jnp.int32),
        ],
        name="moe_plan",
    )(counts)


def _block_routes(comb, ltri, utri):
    oh = comb != 0.0
    ohf = jnp.where(oh, 1.0, 0.0)
    rank = _dot(ltri, ohf.astype(bf16))
    cnt = jnp.sum(ohf, axis=0, keepdims=True)
    units = jnp.floor((cnt + (MOE_UNIT - 1.0)) * (1.0 / MOE_UNIT))
    seg = _dot(jnp.broadcast_to(units, (8, ROUTE_W)).astype(bf16), utri)[0:1, :] * MOE_UNIT
    dest = seg + rank
    big = float(1 << 20)
    d_a = jnp.min(jnp.where(oh, dest, big), axis=-1, keepdims=True)
    d_b = jnp.max(jnp.where(oh, dest, -1.0), axis=-1, keepdims=True)
    w_a = jnp.sum(jnp.where(oh, jnp.where(dest == d_a, comb, 0.0), 0.0), axis=-1, keepdims=True)
    w_b = jnp.sum(jnp.where(oh, jnp.where(dest == d_b, comb, 0.0), 0.0), axis=-1, keepdims=True)
    second = d_b != d_a
    return d_a, jnp.where(second, d_b, -1.0), w_a, jnp.where(second, w_b, 0.0)


def _one_hot_rows(d):
    r = lax.broadcasted_iota(jnp.int32, (d.shape[0], MOE_RLOC), 1).astype(f32)
    return jnp.where(r == d, 1.0, 0.0).astype(bf16)


def _unit_copies(t, units_ref, dst_ref, local, remote, sem, to_remote, wait):
    def per_expert(e, row):
        nu = units_ref[t * N_EXPERTS + e]
        base = dst_ref[t * N_EXPERTS + e]

        def per_unit(u, c):
            lo = local.at[pl.ds(pl.multiple_of(row + u * MOE_UNIT, MOE_UNIT), MOE_UNIT)]
            ro = remote.at[pl.ds(pl.multiple_of(base + u * MOE_UNIT, MOE_UNIT), MOE_UNIT)]
            cp = pltpu.make_async_copy(lo, ro, sem) if to_remote else pltpu.make_async_copy(ro, lo, sem)
            if wait:
                cp.wait()
            else:
                cp.start()
            return c

        lax.fori_loop(0, nu, per_unit, 0)
        return row + nu * MOE_UNIT

    lax.fori_loop(0, N_EXPERTS, per_expert, 0)


def _gap_copies(meta_ref, zero_ref, remote, sem, wait):
    def per_expert(e, c):
        start = meta_ref[1 + e]

        def per_unit(u, c2):
            ro = remote.at[pl.ds(pl.multiple_of(start + u * MOE_UNIT, MOE_UNIT), MOE_UNIT)]
            cp = pltpu.make_async_copy(zero_ref, ro, sem)
            if wait:
                cp.wait()
            else:
                cp.start()
            return c2

        lax.fori_loop(0, meta_ref[1 + N_EXPERTS + e], per_unit, 0)
        return c

    lax.fori_loop(0, N_EXPERTS, per_expert, 0)


def _moe_dispatch_kernel(n_blk, units_ref, dst_ref, meta_ref, x_ref, comb_ref, ltri_ref, utri_ref, xs_ref,
                         buf_ref, zero_ref, sem_ref):
    t = pl.program_id(0)
    slot = t % 2
    d_a, d_b, _, _ = _block_routes(comb_ref[...], ltri_ref[...], utri_ref[...])
    p = _one_hot_rows(d_a) + _one_hot_rows(d_b)
    rows = lax.dot_general(p, x_ref[...], (((0,), (0,)), ((), ())), preferred_element_type=f32)
    buf_ref[slot] = rows.astype(bf16)
    _unit_copies(t, units_ref, dst_ref, buf_ref.at[slot], xs_ref, sem_ref.at[slot], True, False)

    @pl.when(t > 0)
    def _():
        _unit_copies(t - 1, units_ref, dst_ref, buf_ref.at[1 - slot], xs_ref, sem_ref.at[1 - slot], True, True)

    @pl.when(t == n_blk - 1)
    def _():
        zero_ref[...] = jnp.zeros_like(zero_ref)
        _gap_copies(meta_ref, zero_ref, xs_ref, sem_ref.at[2], False)
        _unit_copies(t, units_ref, dst_ref, buf_ref.at[slot], xs_ref, sem_ref.at[slot], True, True)
        _gap_copies(meta_ref, zero_ref, xs_ref, sem_ref.at[2], True)


def moe_dispatch(n_blk, units, dst, meta, x, comb, ltri, utri):
    return pl.pallas_call(
        functools.partial(_moe_dispatch_kernel, n_blk),
        grid_spec=pltpu.PrefetchScalarGridSpec(
            num_scalar_prefetch=3,
            grid=(n_blk,),
            in_specs=[
                pl.BlockSpec((MOE_TB, D_MODEL), lambda t, *_: (t, 0)),
                pl.BlockSpec((MOE_TB, ROUTE_W), lambda t, *_: (t, 0)),
                pl.BlockSpec((MOE_TB, MOE_TB), lambda t, *_: (0, 0)),
                pl.BlockSpec((ROUTE_W, ROUTE_W), lambda t, *_: (0, 0)),
            ],
            out_specs=pl.BlockSpec(memory_space=pl.ANY),
            scratch_shapes=[
                pltpu.VMEM((2, MOE_RLOC, D_MODEL), bf16),
                pltpu.VMEM((MOE_UNIT, D_MODEL), bf16),
                pltpu.SemaphoreType.DMA((3,)),
            ],
        ),
        out_shape=jax.ShapeDtypeStruct((_moe_rows(n_blk), D_MODEL), bf16),
        compiler_params=_cparams("arbitrary"),
        name="moe_dispatch",
    )(units, dst, meta, x, comb, ltri, utri)


def _moe_expert_kernel(tile_exp_ref, meta_ref, xs_ref, wg_ref, wu_ref, wd_ref, ys_ref, wgb_ref, wub_ref, wdb_ref):
    i = pl.program_id(0)

    @pl.when(i < meta_ref[0])
    def _():
        prev = tile_exp_ref[jnp.maximum(i - 1, 0)]

        @pl.when(jnp.logical_or(i == 0, tile_exp_ref[i] != prev))
        def _():
            wgb_ref[...] = wg_ref[0].astype(bf16)
            wub_ref[...] = wu_ref[0].astype(bf16)
            wdb_ref[...] = wd_ref[0].astype(bf16)

        x = xs_ref[...]
        a = _silu(_dot(x, wgb_ref[...])) * _dot(x, wub_ref[...])
        ys_ref[...] = _dot(a.astype(bf16), wdb_ref[...]).astype(bf16)


def moe_experts(n_blk, layer, tile_exp, meta, xs, w_gate, w_up, w_down):
    n_tiles = _moe_rows(n_blk) // MOE_TG

    def row_map(i, te, meta):
        return (jnp.minimum(i, meta[0] - 1), 0)

    def w_map(i, te, meta):
        return (layer * N_EXPERTS + te[jnp.minimum(i, meta[0] - 1)], 0, 0)

    return pl.pallas_call(
        _moe_expert_kernel,
        grid_spec=pltpu.PrefetchScalarGridSpec(
            num_scalar_prefetch=2,
            grid=(n_tiles,),
            in_specs=[
                pl.BlockSpec((MOE_TG, D_MODEL), row_map),
                pl.BlockSpec((1, D_MODEL, D_EXPERT), w_map),
                pl.BlockSpec((1, D_MODEL, D_EXPERT), w_map),
                pl.BlockSpec((1, D_EXPERT, D_MODEL), w_map),
            ],
            out_specs=pl.BlockSpec((MOE_TG, D_MODEL), row_map),
            scratch_shapes=[
                pltpu.VMEM((D_MODEL, D_EXPERT), bf16),
                pltpu.VMEM((D_MODEL, D_EXPERT), bf16),
                pltpu.VMEM((D_EXPERT, D_MODEL), bf16),
            ],
        ),
        out_shape=jax.ShapeDtypeStruct((_moe_rows(n_blk), D_MODEL), bf16),
        compiler_params=_cparams("arbitrary"),
        name="moe_experts",
    )(tile_exp, meta, xs, w_gate, w_up, w_down)


def _moe_combine_kernel(n_blk, units_ref, dst_ref, ys_ref, comb_ref, h_ref, mod_ref, ltri_ref, utri_ref, o_ref,
                        buf_ref, sem_ref):
    t = pl.program_id(0)
    slot = t % 2

    @pl.when(t == 0)
    def _():
        buf_ref[...] = jnp.zeros_like(buf_ref)
        _unit_copies(0, units_ref, dst_ref, buf_ref.at[0], ys_ref, sem_ref.at[0], False, False)

    @pl.when(t + 1 < n_blk)
    def _():
        _unit_copies(t + 1, units_ref, dst_ref, buf_ref.at[1 - slot], ys_ref, sem_ref.at[1 - slot], False, False)

    _unit_copies(t, units_ref, dst_ref, buf_ref.at[slot], ys_ref, sem_ref.at[slot], False, True)
    d_a, d_b, w_a, w_b = _block_routes(comb_ref[...], ltri_ref[...], utri_ref[...])
    ys = buf_ref[slot]
    m = w_a * _dot(_one_hot_rows(d_a), ys) + w_b * _dot(_one_hot_rows(d_b), ys)
    r = _mod_row(t, MOE_TB)
    g2 = mod_ref[pl.ds(r, 1), pl.ds(5 * D_MODEL, D_MODEL)]
    o_ref[...] = h_ref[...] + g2 * m


def moe_combine(n_blk, units, dst, ys, comb, h, mod, ltri, utri):
    return pl.pallas_call(
        functools.partial(_moe_combine_kernel, n_blk),
        grid_spec=pltpu.PrefetchScalarGridSpec(
            num_scalar_prefetch=2,
            grid=(n_blk,),
            in_specs=[
                pl.BlockSpec(memory_space=pl.ANY),
                pl.BlockSpec((MOE_TB, ROUTE_W), lambda t, *_: (t, 0)),
                pl.BlockSpec((MOE_TB, D_MODEL), lambda t, *_: (t, 0)),
                pl.BlockSpec((MOD_ROWS, 6 * D_MODEL), lambda t, *_: (0, 0)),
                pl.BlockSpec((MOE_TB, MOE_TB), lambda t, *_: (0, 0)),
                pl.BlockSpec((ROUTE_W, ROUTE_W), lambda t, *_: (0, 0)),
            ],
            out_specs=pl.BlockSpec((MOE_TB, D_MODEL), lambda t, *_: (t, 0)),
            scratch_shapes=[
                pltpu.VMEM((2, MOE_RLOC, D_MODEL), bf16),
                pltpu.SemaphoreType.DMA((2,)),
            ],
        ),
        out_shape=jax.ShapeDtypeStruct((n_blk * MOE_TB, D_MODEL), f32),
        compiler_params=_cparams("arbitrary"),
        name="moe_combine",
    )(units, dst, ys, comb, h, mod, ltri, utri)


def sparse_moe(n_rows, layer, v, comb, counts, h, mod, w_gate, w_up, w_down, ltri, utri):
    n_blk = n_rows // MOE_TB
    cnt = counts[:, 0, :N_EXPERTS].reshape(n_blk * N_EXPERTS)
    units, dst, tile_exp, meta = moe_plan(n_blk, cnt)
    xs = moe_dispatch(n_blk, units, dst, meta, v, comb, ltri, utri)
    ys = moe_experts(n_blk, layer, tile_exp, meta, xs,
                     w_gate.reshape(DEPTH * N_EXPERTS, D_MODEL, D_EXPERT),
                     w_up.reshape(DEPTH * N_EXPERTS, D_MODEL, D_EXPERT),
                     w_down.reshape(DEPTH * N_EXPERTS, D_EXPERT, D_MODEL))
    return moe_combine(n_blk, units, dst, ys, comb, h, mod, ltri, utri)


def _dft_tables(L):
    k = np.arange(L, dtype=np.int64)
    ang = (2.0 * np.pi / (2 * L)) * ((k[:, None] * k[None, :]) % (2 * L)).astype(np.float64)
    return np.cos(ang).astype(np.float32), np.sin(ang).astype(np.float32)


def _filter_features(L):
    bands = (HY_EMB - 1) // 2
    t = np.linspace(0.0, 1.0, L, dtype=np.float32).astype(np.float64)[:, None]
    w = (2.0 * np.pi / L) * np.arange(L, dtype=np.float64)[:, None]
    fb = np.linspace(1e-4, bands - 1, bands, dtype=np.float32).astype(np.float64)[None, :]
    z = np.concatenate([t, np.cos(fb * w), -np.sin(fb * w)], axis=-1)
    zp = np.zeros((L, FEAT_PAD), np.float32)
    zp[:, :HY_EMB] = z
    deltas = np.abs(np.linspace(HY_MIN_DECAY, HY_MAX_DECAY, HY_CH, dtype=np.float32).astype(np.float64))
    decay = np.exp(-t * deltas[None, :]).astype(np.float32)
    return zp, decay


def _rope_table(cos, sin, half, tm):
    S, width = cos.shape
    low = (np.arange(width) % (2 * half)) < half
    tab = np.zeros((3, S + tm, width), np.float32)
    tab[0, :S] = cos
    tab[0, S:] = 1.0
    tab[1, :S] = np.where(low[None, :], 0.0, sin)
    tab[2, :S] = np.where(low[None, :], -sin, 0.0)
    return tab


def _axial_rope_table(head_dim, tm):
    rows = SEQ // GRID_W
    nf = head_dim // 4
    row = np.repeat(np.arange(rows), GRID_W).astype(np.float64)
    col = np.tile(np.arange(GRID_W), rows).astype(np.float64)
    inv = ROPE_BASE ** (-np.arange(nf, dtype=np.float64) / nf)
    ang = np.stack([row[:, None] * inv, col[:, None] * inv], axis=1)
    a = np.broadcast_to(ang[:, :, None, :], (SEQ, 2, 2, nf)).reshape(SEQ, head_dim)
    reps = LANES // head_dim
    a = np.tile(a, (1, reps))
    return _rope_table(np.cos(a), np.sin(a), nf, tm)


def _seq_rope_table(head_dim, tm):
    inv = 1.0 / (ROPE_BASE ** np.linspace(0.0, 1.0, head_dim // 2, dtype=np.float32).astype(np.float64))
    ang = np.arange(SEQ, dtype=np.float64)[:, None] * inv
    a = np.concatenate([ang, ang], axis=1)
    return _rope_table(np.cos(a), np.sin(a), head_dim // 2, tm)


def _group_mean_matrix():
    g = np.arange(SEG) // DA_HD
    return (g[:, None] == g[None, :]).astype(np.float32) / DA_HD


def _router_weights(w_grp, b_grp, w_rt, b_rt):
    pad = ROUTE_W - 2 * N_EXPERTS
    wr = jnp.concatenate([w_rt, jnp.repeat(w_grp, EXP_PER_GROUP, axis=1),
                          jnp.zeros((D_MODEL, pad), f32)], axis=1)
    br = jnp.concatenate([b_rt, jnp.repeat(b_grp, EXP_PER_GROUP), jnp.zeros((pad,), f32)])[None, :]
    return wr, br


def _strict_lower(n):
    i = np.arange(n)
    return (i[None, :] < i[:, None]).astype(np.float32)


def kernel(x, c, ctx, c_ctx, ada_w, ada_b, e_w_in, e_w_out, hy_conv_w, hy_conv_b, hy_f_w1, hy_f_b1, hy_f_w2, hy_f_b2, hy_f_w3, hy_f_freq, hy_bias, da_q_norm, da_k_norm, da_lam, da_subln, o_w_in, o_w_out, ret_decay, ret_gn, gq_q_norm, gq_k_norm, gq_sink, moe_w_grp, moe_b_grp, moe_w_rt, moe_b_rt, moe_w_gate, moe_w_up, moe_w_down):
    assert x.shape == (BATCH, SEQ, D_MODEL) and ctx.shape == (BATCH, CTX_LEN, D_MODEL)
    x_rows = x.reshape(T_LAT, D_MODEL)
    ctx_rows = ctx.reshape(T_CTX, D_MODEL)
    c_rows = jnp.concatenate([c, c_ctx[None, :], jnp.zeros((MOD_ROWS - BATCH - 1, D_MODEL), f32)], axis=0)
    mod = ada_modulation(c_rows, ada_w, ada_b)

    gmat = jnp.asarray(_group_mean_matrix()).astype(bf16)
    ax_tab = jnp.asarray(_axial_rope_table(DA_HD, PROJ_TM))
    r1_tab = jnp.asarray(_seq_rope_table(RET_DK, PROJ_TM))
    ones = jnp.ones((SEG,), f32)

    lam_init0 = 0.8 - 0.6 * math.exp(-0.3 * 0)
    reps = SEG // DA_HD
    gain0 = jnp.concatenate([ones, ones, ones, jnp.tile(da_q_norm[0], reps) * (DA_HD ** -0.5 * LOG2E),
                             jnp.tile(da_k_norm[0], reps), ones])[None, :]
    proj0 = in_projection("even", [x_rows, ctx_rows], mod[0], e_w_in[0].astype(bf16), gain0, gmat, [ax_tab])

    w3r = hy_f_w3[0].reshape(HY_FILT_HID, 4, HY_CH).transpose(1, 0, 2)
    w1p = jnp.concatenate([hy_f_w1[0], jnp.zeros((FEAT_PAD - HY_EMB, HY_FILT_HID), f32)], axis=0)
    y_hy = []
    for L, blk0 in ((SEQ, 0), (CTX_LEN, T_LAT // CTX_LEN)):
        zfeat, decay = _filter_features(L)
        cm, sm = _dft_tables(L)
        cm = jnp.asarray(cm).astype(bf16)
        sm = jnp.asarray(sm).astype(bf16)
        spec, nyq = hyena_filter_spectra(L, jnp.asarray(zfeat), w1p, hy_f_b1[0][None, :], hy_f_w2[0],
                                         hy_f_b2[0][None, :], w3r, hy_f_freq[0], jnp.asarray(decay), cm, sm)
        y_hy.append(hyena_mix(L, blk0, proj0, hy_conv_w[0], hy_conv_b[0][None, :], spec, nyq, hy_bias[0], cm, sm))

    y_da = diff_attention(proj0, da_lam[0], da_subln[0][None, :], lam_init0)

    wr0, br0 = _router_weights(moe_w_grp[0], moe_b_grp[0], moe_w_rt[0], moe_b_rt[0])
    ltri = jnp.asarray(_strict_lower(MOE_TB)).astype(bf16)
    utri = jnp.asarray(_strict_lower(ROUTE_W).T).astype(bf16)
    h, v, comb, counts = out_projection(T_ALL, y_hy, y_da, e_w_out[0].astype(bf16), [x_rows, ctx_rows], mod[0],
                                        wr0, br0)
    h = sparse_moe(T_ALL, 0, v, comb, counts, h, mod[0], moe_w_gate, moe_w_up, moe_w_down, ltri, utri)

    w_in1 = jnp.concatenate([o_w_in[0], jnp.zeros((D_MODEL, PROJ_W - o_w_in.shape[2]), f32)], axis=1).astype(bf16)
    kq = GQ_KV * GQ_HD
    gain1 = jnp.concatenate([ones, ones * RET_DK ** -0.5, ones, ones,
                             jnp.tile(gq_q_norm[0], reps) * GQ_HD ** -0.5,
                             jnp.tile(gq_k_norm[0], kq // GQ_HD), jnp.ones((SEG - kq,), f32)])[None, :]
    proj1 = in_projection("odd", [h], mod[1], w_in1, gain1, gmat, [ax_tab, r1_tab])
    y_ret = retention(proj1, ret_decay[0], ret_gn[0][None, :])
    y_gq = window_gqa(proj1, gq_sink[0])
    wr1, br1 = _router_weights(moe_w_grp[1], moe_b_grp[1], moe_w_rt[1], moe_b_rt[1])
    h_lat, v, comb, counts = out_projection(T_LAT, [y_ret], y_gq, o_w_out[0].astype(bf16), [h], mod[1], wr1, br1)
    out = sparse_moe(T_LAT, 1, v, comb, counts, h_lat, mod[1], moe_w_gate, moe_w_up, moe_w_down, ltri, utri)
    return out.reshape(BATCH, SEQ, D_MODEL)
```

```python
import functools
import math

import numpy as np
import jax
import jax.numpy as jnp
from jax import lax
from jax.experimental import pallas as pl
from jax.experimental.pallas import tpu as pltpu

f32 = jnp.float32
bf16 = jnp.bfloat16

D_MODEL = 1024
BATCH = 8
SEQ = 2048
DEPTH = 2
GRID_W = 64
CTX_LEN = 256
EPS = 1e-6
NEG_INF = -1e30
LOG2E = math.log2(math.e)
ROPE_BASE = 10000.0
HY_CH = D_MODEL // 2
HY_EMB = 33
HY_FILT_HID = 64
HY_MAX_DECAY = math.log(1e-2) / 0.3
HY_MIN_DECAY = math.log(1e-2) / 1.5
DA_HEADS = 4
DA_HD = D_MODEL // 16
RET_HEADS = 4
RET_DK = D_MODEL // 8
RET_CHUNK = 128
GQ_KV = 2
GQ_GROUP = 4
GQ_HD = D_MODEL // 16
WINDOW = 128
N_GROUPS = 4
EXP_PER_GROUP = 8
N_EXPERTS = N_GROUPS * EXP_PER_GROUP
D_EXPERT = D_MODEL // 4

T_LAT = BATCH * SEQ
T_CTX = BATCH * CTX_LEN
T_ALL = T_LAT + T_CTX
PROJ_W = 3072
SEG = 512
CTX_MOD_ROW = BATCH
MOD_ROWS = 16

LANES = 128
VMEM_LIMIT_BYTES = 56 * 1024 * 1024


def _cparams(*sem):
    return pltpu.CompilerParams(dimension_semantics=sem, vmem_limit_bytes=VMEM_LIMIT_BYTES)


def _dot(a, b):
    return jnp.dot(a, b, preferred_element_type=f32)


def _dot_nt(a, b):
    return lax.dot_general(a, b, (((1,), (1,)), ((), ())), preferred_element_type=f32)


def _split(x):
    hi = x.astype(bf16)
    lo = (x - hi.astype(f32)).astype(bf16)
    return hi, lo


def _dot3(a, b):
    ah, al = _split(a)
    bh, bl = _split(b)
    return _dot(ah, bh) + _dot(al, bh) + _dot(ah, bl)


def _dot2(a, b_bf16):
    ah, al = _split(a)
    return _dot(ah, b_bf16) + _dot(al, b_bf16)


def _silu(x):
    return x * jax.nn.sigmoid(x)


def _rms(x):
    return x * lax.rsqrt(jnp.mean(x * x, axis=-1, keepdims=True) + EPS)


def _const_spec(shape):
    nd = len(shape)
    return pl.BlockSpec(shape, lambda *_: (0,) * nd)


def _const_spec1(shape):
    nd = len(shape)
    return pl.BlockSpec(shape, lambda *_: (0,) * nd, pipeline_mode=pl.Buffered(1))


ADA_TN = 1536


def _ada_kernel(c_ref, w_ref, b_ref, o_ref):
    x = _silu(c_ref[...])
    o_ref[0] = _dot3(x, w_ref[0]) + b_ref[0]


def ada_modulation(c_rows, ada_w, ada_b):
    n = 6 * D_MODEL
    return pl.pallas_call(
        _ada_kernel,
        grid=(DEPTH, n // ADA_TN),
        in_specs=[
            pl.BlockSpec((MOD_ROWS, D_MODEL), lambda l, j: (0, 0)),
            pl.BlockSpec((1, D_MODEL, ADA_TN), lambda l, j: (l, 0, j)),
            pl.BlockSpec((1, 1, ADA_TN), lambda l, j: (l, 0, j)),
        ],
        out_specs=pl.BlockSpec((1, MOD_ROWS, ADA_TN), lambda l, j: (l, 0, j)),
        out_shape=jax.ShapeDtypeStruct((DEPTH, MOD_ROWS, n), f32),
        compiler_params=_cparams("arbitrary", "arbitrary"),
        name="ada_modulation",
    )(c_rows, ada_w, ada_b.reshape(DEPTH, 1, n))


PROJ_TM = 512


def _mod_row(i, tm):
    return jnp.minimum((i * tm) // SEQ, CTX_MOD_ROW)


def _tile4(t):
    return jnp.concatenate([t, t, t, t], axis=1)


def _group_norm64(y, gmat):
    ms = _dot2(y * y, gmat)
    return y * lax.rsqrt(ms + EPS)


def _rope(y, tab, shift):
    w = y.shape[1]
    return y * tab[0] + pltpu.roll(y, shift, 1) * tab[1] + pltpu.roll(y, w - shift, 1) * tab[2]


def _stacked_specs(tm, width):
    n_lat = T_LAT // tm
    return [pl.BlockSpec((tm, width), lambda i: (jnp.minimum(i, n_lat - 1), 0)),
            pl.BlockSpec((tm, width), lambda i: (jnp.maximum(i - n_lat, 0), 0))]


def _stacked_tile(i, tm, lat_ref, ctx_ref):
    return jnp.where(i < T_LAT // tm, lat_ref[...], ctx_ref[...])


def _inproj_kernel(layer_kind, *refs):
    i = pl.program_id(0)
    if layer_kind == "even":
        x_ref, c_ref, mod_ref, w_ref, gain_ref, gmat_ref, ax_ref, o_ref = refs
        h = _stacked_tile(i, PROJ_TM, x_ref, c_ref)
    else:
        h_ref, mod_ref, w_ref, gain_ref, gmat_ref, ax_ref, r1_ref, o_ref = refs
        h = h_ref[...]
    r = _mod_row(i, PROJ_TM)
    sh = mod_ref[pl.ds(r, 1), pl.ds(0, D_MODEL)]
    sc = mod_ref[pl.ds(r, 1), pl.ds(D_MODEL, D_MODEL)]
    u = (_rms(h) * (1.0 + sc) + sh).astype(bf16)

    def seg(j):
        return _dot(u, w_ref[:, j * SEG:(j + 1) * SEG])

    def put(j, y):
        o_ref[:, j * SEG:(j + 1) * SEG] = y.astype(bf16)

    def gain(j):
        return gain_ref[:, j * SEG:(j + 1) * SEG]

    gmat = gmat_ref[...]
    ax = ax_ref[...]
    ax4 = (_tile4(ax[0]), _tile4(ax[1]), _tile4(ax[2]))
    if layer_kind == "even":
        for j in (0, 1, 2, 5):
            put(j, seg(j))
        for j in (3, 4):
            put(j, _rope(_group_norm64(seg(j), gmat) * gain(j), ax4, DA_HD // 4))
    else:
        r1 = r1_ref[...]
        r14 = (_tile4(r1[0]), _tile4(r1[1]), _tile4(r1[2]))
        for j in (0, 1):
            put(j, _rope(seg(j) * gain(j), r14, RET_DK // 2))
        for j in (2, 3):
            put(j, seg(j))
        put(4, _rope(_group_norm64(seg(4), gmat) * gain(4), ax4, GQ_HD // 4))
        y = seg(5)
        kw = GQ_KV * GQ_HD
        yk = _rope(_group_norm64(y[:, :kw], gmat[:kw, :kw]) * gain(5)[:, :kw], ax, GQ_HD // 4)
        yv = y[:, kw:2 * kw]
        pieces = (yk, yv, pltpu.roll(yk, GQ_HD, 1), pltpu.roll(yv, GQ_HD, 1))
        for p, piece in enumerate(pieces):
            o_ref[:, 5 * SEG + p * kw:5 * SEG + (p + 1) * kw] = piece.astype(bf16)


def in_projection(layer_kind, hs, mod, w, gain, gmat, tables):
    n_lat_tiles = T_LAT // PROJ_TM
    n_pos_tiles = SEQ // PROJ_TM

    def tab_map(i):
        return (0, jnp.where(i < n_lat_tiles, i % n_pos_tiles, n_pos_tiles), 0)

    tab_specs = [pl.BlockSpec((3, PROJ_TM, LANES), tab_map) for _ in tables]
    if layer_kind == "even":
        h_specs = _stacked_specs(PROJ_TM, D_MODEL)
    else:
        h_specs = [pl.BlockSpec((PROJ_TM, D_MODEL), lambda i: (i, 0))]
    return pl.pallas_call(
        functools.partial(_inproj_kernel, layer_kind),
        grid=(T_ALL // PROJ_TM,),
        in_specs=h_specs + [
            _const_spec((MOD_ROWS, 6 * D_MODEL)),
            _const_spec((D_MODEL, PROJ_W)),
            _const_spec((1, PROJ_W)),
            _const_spec((SEG, SEG)),
        ] + tab_specs,
        out_specs=pl.BlockSpec((PROJ_TM, PROJ_W), lambda i: (i, 0)),
        out_shape=jax.ShapeDtypeStruct((T_ALL, PROJ_W), bf16),
        compiler_params=_cparams("parallel"),
        name="in_projection_" + layer_kind,
    )(*hs, mod, w, gain, gmat, *tables)


HY_TC = 256
HY_FREQ_CHUNK = 512
FEAT_PAD = 64


def _alt_sign(shape, axis):
    idx = lax.broadcasted_iota(jnp.int32, shape, axis)
    return jnp.where((idx & 1) == 0, 1.0, -1.0).astype(f32)


def _filter_kernel(L, z_ref, w1_ref, b1_ref, w2_ref, b2_ref, wf_ref, wb_ref, freq_ref, dec_ref, c_ref, s_ref,
                   spec_ref, nyq_ref):
    hid = jnp.sin(freq_ref[0:1, :] * (_dot3(z_ref[...], w1_ref[...]) + b1_ref[...]))
    hid = jnp.sin(freq_ref[1:2, :] * (_dot3(hid, w2_ref[...]) + b2_ref[...]))
    dec = dec_ref[...]
    fwd = _dot3(hid, wf_ref[0]) * dec
    bwd = _dot3(hid, wb_ref[0]) * dec
    row = lax.broadcasted_iota(jnp.int32, fwd.shape, 0)
    bwd = jnp.where(row == 0, 0.0, bwd)
    even = fwd + bwd
    odd = bwd - fwd
    wk = jnp.where(row == 0, 0.5 / L, 1.0 / L).astype(f32)
    spec_ref[0, 0] = _dot(c_ref[...], even.astype(bf16)) * wk
    spec_ref[0, 1] = _dot(s_ref[...], odd.astype(bf16)) * wk
    nyq = jnp.sum(even * _alt_sign(even.shape, 0), axis=0, keepdims=True) * (0.5 / L)
    nyq_ref[0] = jnp.broadcast_to(nyq, (8, nyq.shape[1]))


def hyena_filter_spectra(L, zfeat, w1, b1, w2, b2, w3r, freq, decay, cmat, smat):
    nct = HY_CH // HY_TC
    return pl.pallas_call(
        functools.partial(_filter_kernel, L),
        grid=(2, nct),
        in_specs=[
            _const_spec((L, FEAT_PAD)),
            _const_spec((FEAT_PAD, HY_FILT_HID)),
            _const_spec((1, HY_FILT_HID)),
            _const_spec((HY_FILT_HID, HY_FILT_HID)),
            _const_spec((1, HY_FILT_HID)),
            pl.BlockSpec((1, HY_FILT_HID, HY_TC), lambda n, c: (2 * n, 0, c)),
            pl.BlockSpec((1, HY_FILT_HID, HY_TC), lambda n, c: (2 * n + 1, 0, c)),
            _const_spec((2, HY_FILT_HID)),
            pl.BlockSpec((L, HY_TC), lambda n, c: (0, c)),
            _const_spec1((L, L)),
            _const_spec1((L, L)),
        ],
        out_specs=[
            pl.BlockSpec((1, 2, L, HY_TC), lambda n, c: (n, 0, 0, c)),
            pl.BlockSpec((1, 8, HY_TC), lambda n, c: (n, 0, c)),
        ],
        out_shape=[
            jax.ShapeDtypeStruct((2, 2, L, HY_CH), f32),
            jax.ShapeDtypeStruct((2, 8, HY_CH), f32),
        ],
        compiler_params=_cparams("arbitrary", "arbitrary"),
        name="hyena_filter_L%d" % L,
    )(zfeat, w1, b1, w2, b2, w3r, w3r, freq, decay, cmat, smat)


def _conv3(u, w, b):
    L = u.shape[0]
    row = lax.broadcasted_iota(jnp.int32, u.shape, 0)
    prev = jnp.where(row == 0, 0.0, pltpu.roll(u, 1, 0))
    nxt = jnp.where(row == L - 1, 0.0, pltpu.roll(u, L - 1, 0))
    return prev * w[0:1, :] + u * w[1:2, :] + nxt * w[2:3, :] + b


def _hyena_kernel(v_ref, x1_ref, x2_ref, wv_ref, w1_ref, w2_ref, bv_ref, b1_ref, b2_ref, spec_ref, nyq_ref,
                  bias_ref, c_ref, s_ref, *rest):
    o_ref, yr_ref, yi_ref = rest[-3:]
    L = v_ref.shape[0]
    fch = min(L, HY_FREQ_CHUNK)
    z = _conv3(v_ref[...].astype(f32), wv_ref[...], bv_ref[...])
    gate_refs = ((x1_ref, w1_ref, b1_ref), (x2_ref, w2_ref, b2_ref))
    alt = _alt_sign(z.shape, 0)
    for n in range(2):
        zb = z.astype(bf16)
        for k in range(L // fch):
            rows = slice(k * fch, (k + 1) * fch)
            a = _dot(c_ref[rows, :], zb)
            b = _dot(s_ref[rows, :], zb)
            hr = spec_ref[n, 0, rows, :]
            hi = spec_ref[n, 1, rows, :]
            yr_ref[rows, :] = (a * hr + b * hi).astype(bf16)
            yi_ref[rows, :] = (a * hi - b * hr).astype(bf16)
        x_nyq = jnp.sum(z * alt, axis=0, keepdims=True)
        y = (_dot(c_ref[...], yr_ref[...]) - _dot(s_ref[...], yi_ref[...])
             + alt * (x_nyq * nyq_ref[n, 0:1, :]))
        x_ref, w_ref, b_ref = gate_refs[n]
        gate = _conv3(x_ref[...].astype(f32), w_ref[...], b_ref[...])
        z = gate * (y + z * bias_ref[n:n + 1, :])
    o_ref[...] = z.astype(bf16)


def hyena_mix(L, row_block0, proj, conv_w, conv_b, spec, nyq, bias, cmat, smat):
    nct = HY_CH // HY_TC
    nseg = HY_CH // HY_TC

    def col(k):
        return lambda c, b: (row_block0 + b, k * nseg + c)

    def par(k):
        return lambda c, b: (0, k * nseg + c)

    in_specs = (
        [pl.BlockSpec((L, HY_TC), col(k)) for k in range(3)]
        + [pl.BlockSpec((3, HY_TC), par(k)) for k in range(3)]
        + [pl.BlockSpec((1, HY_TC), par(k)) for k in range(3)]
        + [
            pl.BlockSpec((2, 2, L, HY_TC), lambda c, b: (0, 0, 0, c), pipeline_mode=pl.Buffered(1)),
            pl.BlockSpec((2, 8, HY_TC), lambda c, b: (0, 0, c)),
            pl.BlockSpec((2, HY_TC), lambda c, b: (0, c)),
            _const_spec1((L, L)),
            _const_spec1((L, L)),
        ]
    )
    args = [proj, proj, proj, conv_w, conv_w, conv_w, conv_b, conv_b, conv_b, spec, nyq, bias, cmat, smat]
    return pl.pallas_call(
        _hyena_kernel,
        grid=(nct, BATCH),
        in_specs=in_specs,
        out_specs=pl.BlockSpec((L, HY_TC), lambda c, b: (b, c)),
        out_shape=jax.ShapeDtypeStruct((BATCH * L, HY_CH), bf16),
        scratch_shapes=[pltpu.VMEM((L, HY_TC), bf16), pltpu.VMEM((L, HY_TC), bf16)],
        compiler_params=_cparams("arbitrary", "arbitrary"),
        name="hyena_mix_L%d" % L,
    )(*args)


DA_TQ = 256


def _diff_attn_kernel(lam_init, q_ref, kc_ref, vc_ref, kl_ref, vl_ref, lam_ref, subln_ref, o_ref):
    i = pl.program_id(1)
    n_lat_blocks = SEQ // DA_TQ

    @pl.when(i < n_lat_blocks)
    def _():
        _diff_attn_body(lam_init, q_ref, (kc_ref, vc_ref, kl_ref, vl_ref), lam_ref, subln_ref, o_ref)

    @pl.when(i == n_lat_blocks)
    def _():
        _diff_attn_body(lam_init, q_ref, (kc_ref, vc_ref), lam_ref, subln_ref, o_ref)


def _diff_attn_body(lam_init, q_ref, kv_refs, lam_ref, subln_ref, o_ref):
    n_src = len(kv_refs) // 2
    lp = lam_ref[...]
    lam = (jnp.exp(jnp.sum(lp[0:1] * lp[1:2], axis=-1, keepdims=True))
           - jnp.exp(jnp.sum(lp[2:3] * lp[3:4], axis=-1, keepdims=True)) + lam_init)
    q = q_ref[...]
    tq = q.shape[0]
    lower = lax.broadcasted_iota(jnp.int32, (tq, 2 * DA_HD), 1) < DA_HD
    zero = jnp.zeros((), bf16)
    hw = 2 * DA_HD
    outs = []
    for h in range(DA_HEADS):
        qh = q[:, h * hw:(h + 1) * hw]
        ks = [kv_refs[2 * s][:, h * hw:(h + 1) * hw] for s in range(n_src)]
        vs = [kv_refs[2 * s + 1][:, h * hw:(h + 1) * hw] for s in range(n_src)]
        qs = jnp.concatenate([jnp.where(lower, qh, zero), jnp.where(lower, zero, qh)], axis=0)
        ss = [_dot_nt(qs, k) for k in ks]
        mx = functools.reduce(jnp.maximum, [jnp.max(s, axis=-1, keepdims=True) for s in ss])
        es = [jnp.exp2(s - mx) for s in ss]
        den = functools.reduce(jnp.add, [jnp.sum(e, axis=-1, keepdims=True) for e in es])
        pv = functools.reduce(jnp.add, [_dot(es[s].astype(bf16), vs[s]) for s in range(n_src)])
        pv = pv * (1.0 / den)
        oh = pv[:tq] - lam * pv[tq:]
        outs.append(_rms(oh) * subln_ref[...] * (1.0 - lam_init))
    o_ref[...] = jnp.concatenate(outs, axis=1).astype(bf16)


def diff_attention(proj, lam_p, subln, lam_init):
    assert CTX_LEN == DA_TQ
    width = DA_HEADS * 2 * DA_HD
    qcol, kcol, vcol = 3, 4, 5
    ctx_blk0 = T_LAT // CTX_LEN
    nq = SEQ // DA_TQ

    def q_rows(b, i):
        return jnp.where(i < nq, b * nq + i, ctx_blk0 + b)

    return pl.pallas_call(
        functools.partial(_diff_attn_kernel, lam_init),
        grid=(BATCH, nq + 1),
        in_specs=[
            pl.BlockSpec((DA_TQ, width), lambda b, i: (q_rows(b, i), qcol)),
            pl.BlockSpec((CTX_LEN, width), lambda b, i: (ctx_blk0 + b, kcol)),
            pl.BlockSpec((CTX_LEN, width), lambda b, i: (ctx_blk0 + b, vcol)),
            pl.BlockSpec((SEQ, width), lambda b, i: (b, kcol)),
            pl.BlockSpec((SEQ, width), lambda b, i: (b, vcol)),
            _const_spec((4, DA_HD)),
            _const_spec((1, 2 * DA_HD)),
        ],
        out_specs=pl.BlockSpec((DA_TQ, width), lambda b, i: (q_rows(b, i), 0)),
        out_shape=jax.ShapeDtypeStruct((T_ALL, width), bf16),
        compiler_params=_cparams("parallel", "arbitrary"),
        name="diff_attention",
    )(proj, proj, proj, proj, proj, lam_p, subln)


def _log_sigmoid(x):
    return jnp.minimum(x, 0.0) - jnp.log(1.0 + jnp.exp(-jnp.abs(x)))


def _retention_kernel(q_ref, k_ref, v_ref, g_ref, kc_ref, vc_ref, decay_ref, gn_ref, o_ref, st_ref):
    h = pl.program_id(1)
    ch = RET_CHUNK
    nchunk = SEQ // ch
    lgs = _log_sigmoid(decay_ref[...])
    sel = lax.broadcasted_iota(jnp.int32, lgs.shape, 1) == h
    lg = jnp.sum(jnp.where(sel, lgs, 0.0), axis=-1, keepdims=True)
    lgf = lg[0:1, :]
    lgb = lg[1:2, :]
    ri = lax.broadcasted_iota(jnp.int32, (ch, ch), 0).astype(f32)
    ci = lax.broadcasted_iota(jnp.int32, (ch, ch), 1).astype(f32)
    rel = ri - ci
    dsum = (jnp.where(rel >= 0, jnp.exp(jnp.maximum(rel, 0.0) * lgf), 0.0)
            + jnp.where(rel <= 0, jnp.exp(jnp.maximum(-rel, 0.0) * lgb), 0.0))
    zeta_f = jnp.exp((ch - 1 - ci) * lgf)
    zeta_b = jnp.exp(ci * lgb)
    xi_f = jnp.exp((ri + 1.0) * lgf)
    xi_b = jnp.exp((ch - ri) * lgb)
    gch_f = jnp.exp(ch * lgf)
    gch_b = jnp.exp(ch * lgb)
    dk = q_ref.shape[1]

    kct = kc_ref[...].astype(f32).T
    vc = vc_ref[...]
    cl = lax.broadcasted_iota(jnp.int32, kct.shape, 1).astype(f32)
    s_f = _dot((kct * jnp.exp((CTX_LEN - 1 - cl) * lgf)).astype(bf16), vc)
    s_b = _dot((kct * jnp.exp(cl * lgb)).astype(bf16), vc)

    def rows(n):
        return slice(n * ch, (n + 1) * ch)

    u_f, u_b = [], []
    for n in range(nchunk):
        kt = k_ref[rows(n), :].astype(f32).T
        vn = v_ref[rows(n), :]
        u_f.append(_dot((kt * zeta_f).astype(bf16), vn))
        u_b.append(_dot((kt * zeta_b).astype(bf16), vn))

    for n in range(nchunk):
        st_ref[n, 0:dk, :] = s_f.astype(bf16)
        s_f = gch_f * s_f + u_f[n]
    for n in reversed(range(nchunk)):
        st_ref[n, dk:2 * dk, :] = s_b.astype(bf16)
        s_b = gch_b * s_b + u_b[n]

    gn = gn_ref[...]
    for n in range(nchunk):
        qn = q_ref[rows(n), :]
        att = _dot_nt(qn, k_ref[rows(n), :]) * dsum
        qf = qn.astype(f32)
        lhs = jnp.concatenate([att.astype(bf16), (qf * xi_f).astype(bf16), (qf * xi_b).astype(bf16)], axis=1)
        rhs = jnp.concatenate([v_ref[rows(n), :], st_ref[n]], axis=0)
        o = _dot(lhs, rhs)
        mu = jnp.mean(o, axis=-1, keepdims=True)
        oc = o - mu
        var = jnp.mean(oc * oc, axis=-1, keepdims=True)
        y = oc * lax.rsqrt(var + EPS) * gn * _silu(g_ref[rows(n), :].astype(f32))
        o_ref[rows(n), :] = y.astype(bf16)


def retention(proj, decay, gn_w):
    dk = RET_DK
    ctx_blk0 = T_LAT // CTX_LEN
    return pl.pallas_call(
        _retention_kernel,
        grid=(BATCH, RET_HEADS),
        in_specs=[
            pl.BlockSpec((SEQ, dk), lambda b, h: (b, h)),
            pl.BlockSpec((SEQ, dk), lambda b, h: (b, RET_HEADS + h)),
            pl.BlockSpec((SEQ, dk), lambda b, h: (b, 2 * RET_HEADS + h)),
            pl.BlockSpec((SEQ, dk), lambda b, h: (b, 3 * RET_HEADS + h)),
            pl.BlockSpec((CTX_LEN, dk), lambda b, h: (ctx_blk0 + b, RET_HEADS + h)),
            pl.BlockSpec((CTX_LEN, dk), lambda b, h: (ctx_blk0 + b, 2 * RET_HEADS + h)),
            _const_spec((2, RET_HEADS)),
            pl.BlockSpec((1, dk), lambda b, h: (0, h)),
        ],
        out_specs=pl.BlockSpec((SEQ, dk), lambda b, h: (b, h)),
        out_shape=jax.ShapeDtypeStruct((T_LAT, RET_HEADS * dk), bf16),
        scratch_shapes=[pltpu.VMEM((SEQ // RET_CHUNK, 2 * dk, dk), bf16)],
        compiler_params=_cparams("parallel", "arbitrary"),
        name="retention",
    )(proj, proj, proj, proj, proj, proj, decay, gn_w)


GQ_TQ = 128
GQ_SPAN = 3 * GQ_TQ


def _gqa_kernel(q_ref, kv_ref, kvc_ref, sink_ref, o_ref):
    n = pl.program_id(1)
    start = pl.multiple_of(jnp.clip((n - 1) * GQ_TQ, 0, SEQ - GQ_SPAN), GQ_TQ)
    pw = 2 * GQ_HD
    n_heads = GQ_KV * GQ_GROUP
    n_keys = CTX_LEN + GQ_SPAN
    col = lax.broadcasted_iota(jnp.int32, (GQ_TQ, n_keys), 1)
    qpos = n * GQ_TQ + lax.broadcasted_iota(jnp.int32, (GQ_TQ, n_keys), 0)
    dist = jnp.where(col < CTX_LEN, 0, jnp.abs(start + col - CTX_LEN - qpos))
    mask = dist <= WINDOW
    q = q_ref[...]
    lower = lax.broadcasted_iota(jnp.int32, (GQ_TQ, pw), 1) < GQ_HD
    outs = [None] * n_heads
    for swapped in (0, 1):
        kcol = slice(2 * swapped * pw, (2 * swapped + 1) * pw)
        vcol = slice((2 * swapped + 1) * pw, (2 * swapped + 2) * pw)
        k = jnp.concatenate([kvc_ref[:, kcol], kv_ref[pl.ds(start, GQ_SPAN), kcol]], axis=0)
        v = jnp.concatenate([kvc_ref[:, vcol], kv_ref[pl.ds(start, GQ_SPAN), vcol]], axis=0)
        heads = [h for h in range(n_heads) if ((h // GQ_GROUP) == (h % 2)) == (swapped == 0)]
        qs = jnp.concatenate(
            [jnp.where(lower == (h % 2 == 0), q[:, (h // 2) * pw:(h // 2 + 1) * pw], jnp.zeros((), bf16))
             for h in heads], axis=0)
        s = _dot_nt(qs, k)
        ps = []
        for i, h in enumerate(heads):
            sh = jnp.where(mask, s[i * GQ_TQ:(i + 1) * GQ_TQ], NEG_INF)
            sink = sink_ref[h]
            mx = jnp.maximum(jnp.max(sh, axis=-1, keepdims=True), sink)
            e = jnp.exp(sh - mx)
            inv = 1.0 / (jnp.sum(e, axis=-1, keepdims=True) + jnp.exp(sink - mx))
            ps.append((e * inv).astype(bf16))
        o = _dot(jnp.concatenate(ps, axis=0), v)
        for i, h in enumerate(heads):
            outs[h] = o[i * GQ_TQ:(i + 1) * GQ_TQ]
    o_ref[...] = jnp.concatenate(
        [jnp.where(lower, outs[2 * j], outs[2 * j + 1]) for j in range(n_heads // 2)], axis=1).astype(bf16)


def window_gqa(proj, sink):
    width = GQ_KV * GQ_GROUP * GQ_HD
    nq = SEQ // GQ_TQ
    kvw = 4 * GQ_KV * GQ_HD
    kv_col = (5 * SEG) // kvw
    ctx_blk0 = T_LAT // CTX_LEN
    return pl.pallas_call(
        _gqa_kernel,
        grid=(BATCH, nq),
        in_specs=[
            pl.BlockSpec((GQ_TQ, width), lambda b, n: (b * nq + n, 4)),
            pl.BlockSpec((SEQ, kvw), lambda b, n: (b, kv_col)),
            pl.BlockSpec((CTX_LEN, kvw), lambda b, n: (ctx_blk0 + b, kv_col)),
            pl.BlockSpec(memory_space=pltpu.SMEM),
        ],
        out_specs=pl.BlockSpec((GQ_TQ, width), lambda b, n: (b * nq + n, 0)),
        out_shape=jax.ShapeDtypeStruct((T_LAT, width), bf16),
        compiler_params=_cparams("parallel", "arbitrary"),
        name="window_gqa",
    )(proj, proj, proj, sink)


OUT_TM = 512
ROUTE_W = LANES
MOE_TB = 256


def _route(logits):
    lane_i = lax.broadcasted_iota(jnp.int32, logits.shape, 1)
    lane = lane_i.astype(f32)
    big = float(1 << 20)
    valid = lane_i < N_EXPERTS
    le = logits
    lgx = pltpu.roll(logits, ROUTE_W - N_EXPERTS, 1)
    lgx = jnp.where(valid, lgx, NEG_INF)
    gmax = jnp.max(lgx, axis=-1, keepdims=True)
    grp = (lane_i // EXP_PER_GROUP).astype(f32)
    g_sel = jnp.min(jnp.where(lgx == gmax, grp, big), axis=-1, keepdims=True)
    p_grp = float(EXP_PER_GROUP) / jnp.sum(jnp.exp(lgx - gmax), axis=-1, keepdims=True)
    lm = jnp.where(valid, jnp.where(grp == g_sel, le, NEG_INF), NEG_INF)
    v1 = jnp.max(lm, axis=-1, keepdims=True)
    i1 = jnp.min(jnp.where(lm == v1, lane, big), axis=-1, keepdims=True)
    lm2 = jnp.where(lane == i1, NEG_INF, lm)
    v2 = jnp.max(lm2, axis=-1, keepdims=True)
    i2 = jnp.min(jnp.where(lm2 == v2, lane, big), axis=-1, keepdims=True)
    e2 = jnp.exp(v2 - v1)
    w1 = p_grp / (1.0 + e2)
    w2 = w1 * e2
    return jnp.where(lane == i1, w1, 0.0) + jnp.where(lane == i2, w2, 0.0)


def _outproj_kernel(stacked, *refs):
    i = pl.program_id(0)
    if stacked:
        (ya_ref, yac_ref, yb_ref, w_ref, x_ref, c_ref, mod_ref, wr_ref, br_ref,
         hn_ref, v_ref, comb_ref, cnt_ref) = refs
        ya = _stacked_tile(i, OUT_TM, ya_ref, yac_ref)
        h = _stacked_tile(i, OUT_TM, x_ref, c_ref)
    else:
        ya_ref, yb_ref, w_ref, h_ref, mod_ref, wr_ref, br_ref, hn_ref, v_ref, comb_ref, cnt_ref = refs
        ya = ya_ref[...]
        h = h_ref[...]
    r = _mod_row(i, OUT_TM)
    g1 = mod_ref[pl.ds(r, 1), pl.ds(2 * D_MODEL, D_MODEL)]
    sh2 = mod_ref[pl.ds(r, 1), pl.ds(3 * D_MODEL, D_MODEL)]
    sc2 = mod_ref[pl.ds(r, 1), pl.ds(4 * D_MODEL, D_MODEL)]
    half = ya.shape[1]
    m = _dot(ya, w_ref[0:half, :]) + _dot(yb_ref[...], w_ref[half:2 * half, :])
    hn = h + g1 * m
    hn_ref[...] = hn
    v = _rms(hn) * (1.0 + sc2) + sh2
    v_ref[...] = v.astype(bf16)
    comb = _route(_dot3(v, wr_ref[...]) + br_ref[...])
    comb_ref[...] = comb
    for s in range(OUT_TM // MOE_TB):
        cnt = jnp.sum((comb[s * MOE_TB:(s + 1) * MOE_TB] != 0.0).astype(f32), axis=0, keepdims=True)
        cnt_ref[s] = jnp.broadcast_to(cnt, (8, ROUTE_W)).astype(jnp.int32)


def out_projection(n_rows, yas, yb, w, hs, mod, wr, br):
    stacked = len(yas) == 2
    assert stacked == (len(hs) == 2) and (not stacked or n_rows == T_ALL)
    half = yb.shape[1]
    row_spec = lambda width: [pl.BlockSpec((OUT_TM, width), lambda i: (i, 0))]
    return pl.pallas_call(
        functools.partial(_outproj_kernel, stacked),
        grid=(n_rows // OUT_TM,),
        in_specs=(
            (_stacked_specs(OUT_TM, half) if stacked else row_spec(half))
            + row_spec(half)
            + [_const_spec((2 * half, D_MODEL))]
            + (_stacked_specs(OUT_TM, D_MODEL) if stacked else row_spec(D_MODEL))
            + [_const_spec((MOD_ROWS, 6 * D_MODEL)),
               _const_spec((D_MODEL, ROUTE_W)),
               _const_spec((1, ROUTE_W))]
        ),
        out_specs=[
            pl.BlockSpec((OUT_TM, D_MODEL), lambda i: (i, 0)),
            pl.BlockSpec((OUT_TM, D_MODEL), lambda i: (i, 0)),
            pl.BlockSpec((OUT_TM, ROUTE_W), lambda i: (i, 0)),
            pl.BlockSpec((OUT_TM // MOE_TB, 8, ROUTE_W), lambda i: (i, 0, 0)),
        ],
        out_shape=[
            jax.ShapeDtypeStruct((n_rows, D_MODEL), f32),
            jax.ShapeDtypeStruct((n_rows, D_MODEL), bf16),
            jax.ShapeDtypeStruct((n_rows, ROUTE_W), f32),
            jax.ShapeDtypeStruct((n_rows // MOE_TB, 8, ROUTE_W), jnp.int32),
        ],
        compiler_params=_cparams("parallel"),
        name="out_projection",
    )(*yas, yb, w, *hs, mod, wr, br)


MOE_UNIT = 16
MOE_TG = 512
MOE_TOP = 2
MOE_RLOC = MOE_TOP * MOE_TB + N_EXPERTS * MOE_UNIT
MOE_NUNIT = MOE_RLOC // MOE_UNIT
MOE_META = 128


def _moe_rows(n_blk):
    return n_blk * MOE_RLOC + N_EXPERTS * MOE_TG


def _moe_plan_kernel(n_blk, cnt_ref, units_ref, dst_ref, tile_exp_ref, meta_ref):
    def clear(k, c):
        tile_exp_ref[k] = 0
        return c

    lax.fori_loop(0, tile_exp_ref.shape[0], clear, 0)

    def clear_meta(k, c):
        meta_ref[k] = 0
        return c

    lax.fori_loop(0, MOE_META, clear_meta, 0)

    def per_expert(e, goff):
        def per_blk(t, acc):
            u = (cnt_ref[t * N_EXPERTS + e] + (MOE_UNIT - 1)) // MOE_UNIT
            units_ref[t * N_EXPERTS + e] = u
            dst_ref[t * N_EXPERTS + e] = goff + acc
            return acc + u * MOE_UNIT

        n_e = lax.fori_loop(0, n_blk, per_blk, 0)
        nt = (n_e + (MOE_TG - 1)) // MOE_TG
        t0 = goff // MOE_TG

        def mark(k, c):
            tile_exp_ref[t0 + k] = e
            return c

        lax.fori_loop(0, nt, mark, 0)
        meta_ref[1 + e] = goff + n_e
        meta_ref[1 + N_EXPERTS + e] = (nt * MOE_TG - n_e) // MOE_UNIT
        return goff + nt * MOE_TG

    total = lax.fori_loop(0, N_EXPERTS, per_expert, 0)
    meta_ref[0] = total // MOE_TG


def moe_plan(n_blk, counts):
    smem = pl.BlockSpec(memory_space=pltpu.SMEM)
    n_tiles = _moe_rows(n_blk) // MOE_TG
    return pl.pallas_call(
        functools.partial(_moe_plan_kernel, n_blk),
        in_specs=[smem],
        out_specs=[smem, smem, smem, smem],
        out_shape=[
            jax.ShapeDtypeStruct((n_blk * N_EXPERTS,), jnp.int32),
            jax.ShapeDtypeStruct((n_blk * N_EXPERTS,), jnp.int32),
            jax.ShapeDtypeStruct((n_tiles,), jnp.int32),
            jax.ShapeDtypeStruct((MOE_META,), jnp.int32),
        ],
        name="moe_plan",
    )(counts)


def _block_routes(comb, ltri, utri):
    oh = comb != 0.0
    ohf = jnp.where(oh, 1.0, 0.0)
    rank = _dot(ltri, ohf.astype(bf16))
    cnt = jnp.sum(ohf, axis=0, keepdims=True)
    units = jnp.floor((cnt + (MOE_UNIT - 1.0)) * (1.0 / MOE_UNIT))
    seg = _dot(jnp.broadcast_to(units, (8, ROUTE_W)).astype(bf16), utri)[0:1, :] * MOE_UNIT
    dest = seg + rank
    big = float(1 << 20)
    d_a = jnp.min(jnp.where(oh, dest, big), axis=-1, keepdims=True)
    d_b = jnp.max(jnp.where(oh, dest, -1.0), axis=-1, keepdims=True)
    w_a = jnp.sum(jnp.where(oh, jnp.where(dest == d_a, comb, 0.0), 0.0), axis=-1, keepdims=True)
    w_b = jnp.sum(jnp.where(oh, jnp.where(dest == d_b, comb, 0.0), 0.0), axis=-1, keepdims=True)
    second = d_b != d_a
    return d_a, jnp.where(second, d_b, -1.0), w_a, jnp.where(second, w_b, 0.0)


def _one_hot_rows(d):
    r = lax.broadcasted_iota(jnp.int32, (d.shape[0], MOE_RLOC), 1).astype(f32)
    return jnp.where(r == d, 1.0, 0.0).astype(bf16)


def _unit_table(t, units_ref, dst_ref, tab_ref, pad_row0):
    def per_expert(e, j):
        nu = units_ref[t * N_EXPERTS + e]
        base = dst_ref[t * N_EXPERTS + e]

        def per_unit(u, c):
            tab_ref[j + u] = base + u * MOE_UNIT
            return c

        lax.fori_loop(0, nu, per_unit, 0)
        return j + nu

    n_used = lax.fori_loop(0, N_EXPERTS, per_expert, 0)

    def pad(j, c):
        tab_ref[j] = pad_row0 + j * MOE_UNIT
        return c

    lax.fori_loop(n_used, MOE_NUNIT, pad, 0)
    return n_used


def _unit_rows(ref, row):
    return ref.at[pl.ds(pl.multiple_of(row, MOE_UNIT), MOE_UNIT)]


def _wait_all_units(local, remote, sem):
    pltpu.make_async_copy(local, remote.at[pl.ds(0, MOE_RLOC)], sem).wait()


def _gap_copies(meta_ref, zero_ref, remote, sem, wait):
    def per_expert(e, c):
        start = meta_ref[1 + e]

        def per_unit(u, c2):
            ro = remote.at[pl.ds(pl.multiple_of(start + u * MOE_UNIT, MOE_UNIT), MOE_UNIT)]
            cp = pltpu.make_async_copy(zero_ref, ro, sem)
            if wait:
                cp.wait()
            else:
                cp.start()
            return c2

        lax.fori_loop(0, meta_ref[1 + N_EXPERTS + e], per_unit, 0)
        return c

    lax.fori_loop(0, N_EXPERTS, per_expert, 0)


def _moe_dispatch_kernel(n_blk, units_ref, dst_ref, meta_ref, x_ref, comb_ref, ltri_ref, utri_ref, xs_ref,
                         buf_ref, zero_ref, tab_ref, sem_ref):
    t = pl.program_id(0)
    slot = t % 2
    d_a, d_b, _, _ = _block_routes(comb_ref[...], ltri_ref[...], utri_ref[...])
    p = _one_hot_rows(d_a) + _one_hot_rows(d_b)
    rows = lax.dot_general(p, x_ref[...], (((0,), (0,)), ((), ())), preferred_element_type=f32)
    buf_ref[slot] = rows.astype(bf16)
    _unit_table(t, units_ref, dst_ref, tab_ref, _moe_rows(n_blk) + slot * MOE_RLOC)
    for j in range(MOE_NUNIT):
        pltpu.make_async_copy(buf_ref.at[slot, pl.ds(j * MOE_UNIT, MOE_UNIT)], _unit_rows(xs_ref, tab_ref[j]),
                              sem_ref.at[slot]).start()

    @pl.when(t > 0)
    def _():
        _wait_all_units(buf_ref.at[1 - slot], xs_ref, sem_ref.at[1 - slot])

    @pl.when(t == n_blk - 1)
    def _():
        zero_ref[...] = jnp.zeros_like(zero_ref)
        _gap_copies(meta_ref, zero_ref, xs_ref, sem_ref.at[2], False)
        _wait_all_units(buf_ref.at[slot], xs_ref, sem_ref.at[slot])
        _gap_copies(meta_ref, zero_ref, xs_ref, sem_ref.at[2], True)


def moe_dispatch(n_blk, units, dst, meta, x, comb, ltri, utri):
    return pl.pallas_call(
        functools.partial(_moe_dispatch_kernel, n_blk),
        grid_spec=pltpu.PrefetchScalarGridSpec(
            num_scalar_prefetch=3,
            grid=(n_blk,),
            in_specs=[
                pl.BlockSpec((MOE_TB, D_MODEL), lambda t, *_: (t, 0)),
                pl.BlockSpec((MOE_TB, ROUTE_W), lambda t, *_: (t, 0)),
                pl.BlockSpec((MOE_TB, MOE_TB), lambda t, *_: (0, 0)),
                pl.BlockSpec((ROUTE_W, ROUTE_W), lambda t, *_: (0, 0)),
            ],
            out_specs=pl.BlockSpec(memory_space=pl.ANY),
            scratch_shapes=[
                pltpu.VMEM((2, MOE_RLOC, D_MODEL), bf16),
                pltpu.VMEM((MOE_UNIT, D_MODEL), bf16),
                pltpu.SMEM((MOE_NUNIT,), jnp.int32),
                pltpu.SemaphoreType.DMA((3,)),
            ],
        ),
        out_shape=jax.ShapeDtypeStruct((_moe_rows(n_blk) + 2 * MOE_RLOC, D_MODEL), bf16),
        compiler_params=_cparams("arbitrary"),
        name="moe_dispatch",
    )(units, dst, meta, x, comb, ltri, utri)


def _moe_expert_kernel(tile_exp_ref, meta_ref, xs_ref, wg_ref, wu_ref, wd_ref, ys_ref, wgb_ref, wub_ref, wdb_ref):
    i = pl.program_id(0)

    @pl.when(i < meta_ref[0])
    def _():
        prev = tile_exp_ref[jnp.maximum(i - 1, 0)]

        @pl.when(jnp.logical_or(i == 0, tile_exp_ref[i] != prev))
        def _():
            wgb_ref[...] = wg_ref[0].astype(bf16)
            wub_ref[...] = wu_ref[0].astype(bf16)
            wdb_ref[...] = wd_ref[0].astype(bf16)

        x = xs_ref[...]
        a = _silu(_dot(x, wgb_ref[...])) * _dot(x, wub_ref[...])
        ys_ref[...] = _dot(a.astype(bf16), wdb_ref[...]).astype(bf16)


def moe_experts(n_blk, layer, tile_exp, meta, xs, w_gate, w_up, w_down):
    n_tiles = _moe_rows(n_blk) // MOE_TG

    def row_map(i, te, meta):
        return (jnp.minimum(i, meta[0] - 1), 0)

    def w_map(i, te, meta):
        return (layer * N_EXPERTS + te[jnp.minimum(i, meta[0] - 1)], 0, 0)

    return pl.pallas_call(
        _moe_expert_kernel,
        grid_spec=pltpu.PrefetchScalarGridSpec(
            num_scalar_prefetch=2,
            grid=(n_tiles,),
            in_specs=[
                pl.BlockSpec((MOE_TG, D_MODEL), row_map),
                pl.BlockSpec((1, D_MODEL, D_EXPERT), w_map),
                pl.BlockSpec((1, D_MODEL, D_EXPERT), w_map),
                pl.BlockSpec((1, D_EXPERT, D_MODEL), w_map),
            ],
            out_specs=pl.BlockSpec((MOE_TG, D_MODEL), row_map),
            scratch_shapes=[
                pltpu.VMEM((D_MODEL, D_EXPERT), bf16),
                pltpu.VMEM((D_MODEL, D_EXPERT), bf16),
                pltpu.VMEM((D_EXPERT, D_MODEL), bf16),
            ],
        ),
        out_shape=jax.ShapeDtypeStruct((_moe_rows(n_blk), D_MODEL), bf16),
        compiler_params=_cparams("arbitrary"),
        name="moe_experts",
    )(tile_exp, meta, xs, w_gate, w_up, w_down)


def _moe_combine_kernel(n_blk, units_ref, dst_ref, ys_ref, zeros_ref, comb_ref, h_ref, mod_ref, ltri_ref, utri_ref,
                        o_ref, buf_ref, tab_ref, sem_ref):
    t = pl.program_id(0)
    slot = t % 2

    def gather(tt, s):
        n_used = _unit_table(tt, units_ref, dst_ref, tab_ref, 0)
        for j in range(MOE_NUNIT):
            dst = buf_ref.at[s, pl.ds(j * MOE_UNIT, MOE_UNIT)]

            @pl.when(j < n_used)
            def _():
                pltpu.make_async_copy(_unit_rows(ys_ref, tab_ref[j]), dst, sem_ref.at[s]).start()

            @pl.when(j >= n_used)
            def _():
                pltpu.make_async_copy(zeros_ref, dst, sem_ref.at[s]).start()

    @pl.when(t == 0)
    def _():
        gather(0, 0)

    @pl.when(t + 1 < n_blk)
    def _():
        gather(t + 1, 1 - slot)

    _wait_all_units(buf_ref.at[slot], ys_ref, sem_ref.at[slot])
    d_a, d_b, w_a, w_b = _block_routes(comb_ref[...], ltri_ref[...], utri_ref[...])
    p = jnp.concatenate([_one_hot_rows(d_a), _one_hot_rows(d_b)], axis=0)
    picked = _dot(p, buf_ref[slot])
    m = w_a * picked[:MOE_TB] + w_b * picked[MOE_TB:]
    r = _mod_row(t, MOE_TB)
    g2 = mod_ref[pl.ds(r, 1), pl.ds(5 * D_MODEL, D_MODEL)]
    o_ref[...] = h_ref[...] + g2 * m


def moe_combine(n_blk, units, dst, ys, comb, h, mod, ltri, utri):
    return pl.pallas_call(
        functools.partial(_moe_combine_kernel, n_blk),
        grid_spec=pltpu.PrefetchScalarGridSpec(
            num_scalar_prefetch=2,
            grid=(n_blk,),
            in_specs=[
                pl.BlockSpec(memory_space=pl.ANY),
                pl.BlockSpec(memory_space=pl.ANY),
                pl.BlockSpec((MOE_TB, ROUTE_W), lambda t, *_: (t, 0)),
                pl.BlockSpec((MOE_TB, D_MODEL), lambda t, *_: (t, 0)),
                pl.BlockSpec((MOD_ROWS, 6 * D_MODEL), lambda t, *_: (0, 0)),
                pl.BlockSpec((MOE_TB, MOE_TB), lambda t, *_: (0, 0)),
                pl.BlockSpec((ROUTE_W, ROUTE_W), lambda t, *_: (0, 0)),
            ],
            out_specs=pl.BlockSpec((MOE_TB, D_MODEL), lambda t, *_: (t, 0)),
            scratch_shapes=[
                pltpu.VMEM((2, MOE_RLOC, D_MODEL), bf16),
                pltpu.SMEM((MOE_NUNIT,), jnp.int32),
                pltpu.SemaphoreType.DMA((2,)),
            ],
        ),
        out_shape=jax.ShapeDtypeStruct((n_blk * MOE_TB, D_MODEL), f32),
        compiler_params=_cparams("arbitrary"),
        name="moe_combine",
    )(units, dst, ys, jnp.zeros((MOE_UNIT, D_MODEL), bf16), comb, h, mod, ltri, utri)


def sparse_moe(n_rows, layer, v, comb, counts, h, mod, w_gate, w_up, w_down, ltri, utri):
    n_blk = n_rows // MOE_TB
    cnt = counts[:, 0, :N_EXPERTS].reshape(n_blk * N_EXPERTS)
    units, dst, tile_exp, meta = moe_plan(n_blk, cnt)
    xs = moe_dispatch(n_blk, units, dst, meta, v, comb, ltri, utri)
    ys = moe_experts(n_blk, layer, tile_exp, meta, xs,
                     w_gate.reshape(DEPTH * N_EXPERTS, D_MODEL, D_EXPERT),
                     w_up.reshape(DEPTH * N_EXPERTS, D_MODEL, D_EXPERT),
                     w_down.reshape(DEPTH * N_EXPERTS, D_EXPERT, D_MODEL))
    return moe_combine(n_blk, units, dst, ys, comb, h, mod, ltri, utri)


def _dft_tables(L):
    k = np.arange(L, dtype=np.int64)
    ang = (2.0 * np.pi / (2 * L)) * ((k[:, None] * k[None, :]) % (2 * L)).astype(np.float64)
    return np.cos(ang).astype(np.float32), np.sin(ang).astype(np.float32)


def _filter_features(L):
    bands = (HY_EMB - 1) // 2
    t = np.linspace(0.0, 1.0, L, dtype=np.float32).astype(np.float64)[:, None]
    w = (2.0 * np.pi / L) * np.arange(L, dtype=np.float64)[:, None]
    fb = np.linspace(1e-4, bands - 1, bands, dtype=np.float32).astype(np.float64)[None, :]
    z = np.concatenate([t, np.cos(fb * w), -np.sin(fb * w)], axis=-1)
    zp = np.zeros((L, FEAT_PAD), np.float32)
    zp[:, :HY_EMB] = z
    deltas = np.abs(np.linspace(HY_MIN_DECAY, HY_MAX_DECAY, HY_CH, dtype=np.float32).astype(np.float64))
    decay = np.exp(-t * deltas[None, :]).astype(np.float32)
    return zp, decay


def _rope_table(cos, sin, half, tm):
    S, width = cos.shape
    low = (np.arange(width) % (2 * half)) < half
    tab = np.zeros((3, S + tm, width), np.float32)
    tab[0, :S] = cos
    tab[0, S:] = 1.0
    tab[1, :S] = np.where(low[None, :], 0.0, sin)
    tab[2, :S] = np.where(low[None, :], -sin, 0.0)
    return tab


def _axial_rope_table(head_dim, tm):
    rows = SEQ // GRID_W
    nf = head_dim // 4
    row = np.repeat(np.arange(rows), GRID_W).astype(np.float64)
    col = np.tile(np.arange(GRID_W), rows).astype(np.float64)
    inv = ROPE_BASE ** (-np.arange(nf, dtype=np.float64) / nf)
    ang = np.stack([row[:, None] * inv, col[:, None] * inv], axis=1)
    a = np.broadcast_to(ang[:, :, None, :], (SEQ, 2, 2, nf)).reshape(SEQ, head_dim)
    reps = LANES // head_dim
    a = np.tile(a, (1, reps))
    return _rope_table(np.cos(a), np.sin(a), nf, tm)


def _seq_rope_table(head_dim, tm):
    inv = 1.0 / (ROPE_BASE ** np.linspace(0.0, 1.0, head_dim // 2, dtype=np.float32).astype(np.float64))
    ang = np.arange(SEQ, dtype=np.float64)[:, None] * inv
    a = np.concatenate([ang, ang], axis=1)
    return _rope_table(np.cos(a), np.sin(a), head_dim // 2, tm)


def _group_mean_matrix():
    g = np.arange(SEG) // DA_HD
    return (g[:, None] == g[None, :]).astype(np.float32) / DA_HD


def _router_weights(w_grp, b_grp, w_rt, b_rt):
    pad = ROUTE_W - 2 * N_EXPERTS
    wr = jnp.concatenate([w_rt, jnp.repeat(w_grp, EXP_PER_GROUP, axis=1),
                          jnp.zeros((D_MODEL, pad), f32)], axis=1)
    br = jnp.concatenate([b_rt, jnp.repeat(b_grp, EXP_PER_GROUP), jnp.zeros((pad,), f32)])[None, :]
    return wr, br


def _strict_lower(n):
    i = np.arange(n)
    return (i[None, :] < i[:, None]).astype(np.float32)


def kernel(x, c, ctx, c_ctx, ada_w, ada_b, e_w_in, e_w_out, hy_conv_w, hy_conv_b, hy_f_w1, hy_f_b1, hy_f_w2, hy_f_b2, hy_f_w3, hy_f_freq, hy_bias, da_q_norm, da_k_norm, da_lam, da_subln, o_w_in, o_w_out, ret_decay, ret_gn, gq_q_norm, gq_k_norm, gq_sink, moe_w_grp, moe_b_grp, moe_w_rt, moe_b_rt, moe_w_gate, moe_w_up, moe_w_down):
    assert x.shape == (BATCH, SEQ, D_MODEL) and ctx.shape == (BATCH, CTX_LEN, D_MODEL)
    x_rows = x.reshape(T_LAT, D_MODEL)
    ctx_rows = ctx.reshape(T_CTX, D_MODEL)
    c_rows = jnp.concatenate([c, c_ctx[None, :], jnp.zeros((MOD_ROWS - BATCH - 1, D_MODEL), f32)], axis=0)
    mod = ada_modulation(c_rows, ada_w, ada_b)

    gmat = jnp.asarray(_group_mean_matrix()).astype(bf16)
    ax_tab = jnp.asarray(_axial_rope_table(DA_HD, PROJ_TM))
    r1_tab = jnp.asarray(_seq_rope_table(RET_DK, PROJ_TM))
    ones = jnp.ones((SEG,), f32)

    lam_init0 = 0.8 - 0.6 * math.exp(-0.3 * 0)
    reps = SEG // DA_HD
    gain0 = jnp.concatenate([ones, ones, ones, jnp.tile(da_q_norm[0], reps) * (DA_HD ** -0.5 * LOG2E),
                             jnp.tile(da_k_norm[0], reps), ones])[None, :]
    proj0 = in_projection("even", [x_rows, ctx_rows], mod[0], e_w_in[0].astype(bf16), gain0, gmat, [ax_tab])

    w3r = hy_f_w3[0].reshape(HY_FILT_HID, 4, HY_CH).transpose(1, 0, 2)
    w1p = jnp.concatenate([hy_f_w1[0], jnp.zeros((FEAT_PAD - HY_EMB, HY_FILT_HID), f32)], axis=0)
    y_hy = []
    for L, blk0 in ((SEQ, 0), (CTX_LEN, T_LAT // CTX_LEN)):
        zfeat, decay = _filter_features(L)
        cm, sm = _dft_tables(L)
        cm = jnp.asarray(cm).astype(bf16)
        sm = jnp.asarray(sm).astype(bf16)
        spec, nyq = hyena_filter_spectra(L, jnp.asarray(zfeat), w1p, hy_f_b1[0][None, :], hy_f_w2[0],
                                         hy_f_b2[0][None, :], w3r, hy_f_freq[0], jnp.asarray(decay), cm, sm)
        y_hy.append(hyena_mix(L, blk0, proj0, hy_conv_w[0], hy_conv_b[0][None, :], spec, nyq, hy_bias[0], cm, sm))

    y_da = diff_attention(proj0, da_lam[0], da_subln[0][None, :], lam_init0)

    wr0, br0 = _router_weights(moe_w_grp[0], moe_b_grp[0], moe_w_rt[0], moe_b_rt[0])
    ltri = jnp.asarray(_strict_lower(MOE_TB)).astype(bf16)
    utri = jnp.asarray(_strict_lower(ROUTE_W).T).astype(bf16)
    h, v, comb, counts = out_projection(T_ALL, y_hy, y_da, e_w_out[0].astype(bf16), [x_rows, ctx_rows], mod[0],
                                        wr0, br0)
    h = sparse_moe(T_ALL, 0, v, comb, counts, h, mod[0], moe_w_gate, moe_w_up, moe_w_down, ltri, utri)

    w_in1 = jnp.concatenate([o_w_in[0], jnp.zeros((D_MODEL, PROJ_W - o_w_in.shape[2]), f32)], axis=1).astype(bf16)
    kq = GQ_KV * GQ_HD
    gain1 = jnp.concatenate([ones, ones * RET_DK ** -0.5, ones, ones,
                             jnp.tile(gq_q_norm[0], reps) * GQ_HD ** -0.5,
                             jnp.tile(gq_k_norm[0], kq // GQ_HD), jnp.ones((SEG - kq,), f32)])[None, :]
    proj1 = in_projection("odd", [h], mod[1], w_in1, gain1, gmat, [ax_tab, r1_tab])
    y_ret = retention(proj1, ret_decay[0], ret_gn[0][None, :])
    y_gq = window_gqa(proj1, gq_sink[0])
    wr1, br1 = _router_weights(moe_w_grp[1], moe_b_grp[1], moe_w_rt[1], moe_b_rt[1])
    h_lat, v, comb, counts = out_projection(T_LAT, [y_ret], y_gq, o_w_out[0].astype(bf16), [h], mod[1], wr1, br1)
    out = sparse_moe(T_LAT, 1, v, comb, counts, h_lat, mod[1], moe_w_gate, moe_w_up, moe_w_down, ltri, utri)
    return out.reshape(BATCH, SEQ, D_MODEL)
```

```python
import functools
import math

import numpy as np
import jax
import jax.numpy as jnp
from jax import lax
from jax.experimental import pallas as pl
from jax.experimental.pallas import tpu as pltpu

f32 = jnp.float32
bf16 = jnp.bfloat16

D_MODEL = 1024
BATCH = 8
SEQ = 2048
DEPTH = 2
GRID_W = 64
CTX_LEN = 256
EPS = 1e-6
NEG_INF = -1e30
LOG2E = math.log2(math.e)
ROPE_BASE = 10000.0
HY_CH = D_MODEL // 2
HY_EMB = 33
HY_FILT_HID = 64
HY_MAX_DECAY = math.log(1e-2) / 0.3
HY_MIN_DECAY = math.log(1e-2) / 1.5
DA_HEADS = 4
DA_HD = D_MODEL // 16
RET_HEADS = 4
RET_DK = D_MODEL // 8
RET_CHUNK = 128
GQ_KV = 2
GQ_GROUP = 4
GQ_HD = D_MODEL // 16
WINDOW = 128
N_GROUPS = 4
EXP_PER_GROUP = 8
N_EXPERTS = N_GROUPS * EXP_PER_GROUP
D_EXPERT = D_MODEL // 4

T_LAT = BATCH * SEQ
T_CTX = BATCH * CTX_LEN
T_ALL = T_LAT + T_CTX
PROJ_W = 3072
SEG = 512
CTX_MOD_ROW = BATCH
MOD_ROWS = 16

LANES = 128
VMEM_LIMIT_BYTES = 56 * 1024 * 1024


def _cparams(*sem):
    return pltpu.CompilerParams(dimension_semantics=sem, vmem_limit_bytes=VMEM_LIMIT_BYTES)


def _dot(a, b):
    return jnp.dot(a, b, preferred_element_type=f32)


def _dot_nt(a, b):
    return lax.dot_general(a, b, (((1,), (1,)), ((), ())), preferred_element_type=f32)


def _split(x):
    hi = x.astype(bf16)
    lo = (x - hi.astype(f32)).astype(bf16)
    return hi, lo


def _dot3(a, b):
    ah, al = _split(a)
    bh, bl = _split(b)
    return _dot(ah, bh) + _dot(al, bh) + _dot(ah, bl)


def _silu(x):
    return x * jax.nn.sigmoid(x)


def _rms(x):
    return x * lax.rsqrt(jnp.mean(x * x, axis=-1, keepdims=True) + EPS)


def _const_spec(shape):
    nd = len(shape)
    return pl.BlockSpec(shape, lambda *_: (0,) * nd)


def _const_spec1(shape):
    nd = len(shape)
    return pl.BlockSpec(shape, lambda *_: (0,) * nd, pipeline_mode=pl.Buffered(1))


ADA_TN = 1536


def _ada_kernel(c_ref, w_ref, b_ref, o_ref):
    x = _silu(c_ref[...])
    o_ref[0] = _dot3(x, w_ref[0]) + b_ref[0]


def ada_modulation(c_rows, ada_w, ada_b):
    n = 6 * D_MODEL
    return pl.pallas_call(
        _ada_kernel,
        grid=(DEPTH, n // ADA_TN),
        in_specs=[
            pl.BlockSpec((MOD_ROWS, D_MODEL), lambda l, j: (0, 0)),
            pl.BlockSpec((1, D_MODEL, ADA_TN), lambda l, j: (l, 0, j)),
            pl.BlockSpec((1, 1, ADA_TN), lambda l, j: (l, 0, j)),
        ],
        out_specs=pl.BlockSpec((1, MOD_ROWS, ADA_TN), lambda l, j: (l, 0, j)),
        out_shape=jax.ShapeDtypeStruct((DEPTH, MOD_ROWS, n), f32),
        compiler_params=_cparams("arbitrary", "arbitrary"),
        name="ada_modulation",
    )(c_rows, ada_w, ada_b.reshape(DEPTH, 1, n))


PROJ_TM = 512


def _mod_row(i, tm):
    return jnp.minimum((i * tm) // SEQ, CTX_MOD_ROW)


def _tile4(t):
    return jnp.concatenate([t, t, t, t], axis=1)


def _group_norm64(y, gmat):
    ms = _dot((y * y).astype(bf16), gmat)
    return y * lax.rsqrt(ms + EPS)


def _rope(y, tab, shift):
    w = y.shape[1]
    return y * tab[0] + pltpu.roll(y, shift, 1) * tab[1] + pltpu.roll(y, w - shift, 1) * tab[2]


def _stacked_specs(tm, width):
    n_lat = T_LAT // tm
    return [pl.BlockSpec((tm, width), lambda i: (jnp.minimum(i, n_lat - 1), 0)),
            pl.BlockSpec((tm, width), lambda i: (jnp.maximum(i - n_lat, 0), 0))]


def _stacked_tile(i, tm, lat_ref, ctx_ref):
    return jnp.where(i < T_LAT // tm, lat_ref[...], ctx_ref[...])


def _inproj_kernel(layer_kind, *refs):
    i = pl.program_id(0)
    if layer_kind == "even":
        x_ref, c_ref, mod_ref, w_ref, gain_ref, gmat_ref, ax_ref, o_ref = refs
        h = _stacked_tile(i, PROJ_TM, x_ref, c_ref)
    else:
        h_ref, mod_ref, w_ref, gain_ref, gmat_ref, ax_ref, r1_ref, o_ref = refs
        h = h_ref[...]
    r = _mod_row(i, PROJ_TM)
    sh = mod_ref[pl.ds(r, 1), pl.ds(0, D_MODEL)]
    sc = mod_ref[pl.ds(r, 1), pl.ds(D_MODEL, D_MODEL)]
    u = (_rms(h) * (1.0 + sc) + sh).astype(bf16)

    def seg(j):
        return _dot(u, w_ref[:, j * SEG:(j + 1) * SEG])

    def put(j, y):
        o_ref[:, j * SEG:(j + 1) * SEG] = y.astype(bf16)

    def gain(j):
        return gain_ref[:, j * SEG:(j + 1) * SEG]

    gmat = gmat_ref[...]
    ax = ax_ref[...]
    ax4 = (_tile4(ax[0]), _tile4(ax[1]), _tile4(ax[2]))
    if layer_kind == "even":
        for j in (3, 4):
            put(j, _rope(_group_norm64(seg(j), gmat) * gain(j), ax4, DA_HD // 4))
        for j in (0, 1, 2, 5):
            put(j, seg(j))
    else:
        r1 = r1_ref[...]
        r14 = (_tile4(r1[0]), _tile4(r1[1]), _tile4(r1[2]))
        put(4, _rope(_group_norm64(seg(4), gmat) * gain(4), ax4, GQ_HD // 4))
        y = seg(5)
        kw = GQ_KV * GQ_HD
        yk = _rope(_group_norm64(y[:, :kw], gmat[:kw, :kw]) * gain(5)[:, :kw], ax, GQ_HD // 4)
        yv = y[:, kw:2 * kw]
        pieces = (yk, yv, pltpu.roll(yk, GQ_HD, 1), pltpu.roll(yv, GQ_HD, 1))
        for p, piece in enumerate(pieces):
            o_ref[:, 5 * SEG + p * kw:5 * SEG + (p + 1) * kw] = piece.astype(bf16)
        for j in (0, 1):
            put(j, _rope(seg(j) * gain(j), r14, RET_DK // 2))
        for j in (2, 3):
            put(j, seg(j))


def in_projection(layer_kind, hs, mod, w, gain, gmat, tables):
    n_lat_tiles = T_LAT // PROJ_TM
    n_pos_tiles = SEQ // PROJ_TM

    def tab_map(i):
        return (0, jnp.where(i < n_lat_tiles, i % n_pos_tiles, n_pos_tiles), 0)

    tab_specs = [pl.BlockSpec((3, PROJ_TM, LANES), tab_map) for _ in tables]
    if layer_kind == "even":
        h_specs = _stacked_specs(PROJ_TM, D_MODEL)
    else:
        h_specs = [pl.BlockSpec((PROJ_TM, D_MODEL), lambda i: (i, 0))]
    return pl.pallas_call(
        functools.partial(_inproj_kernel, layer_kind),
        grid=(T_ALL // PROJ_TM,),
        in_specs=h_specs + [
            _const_spec((MOD_ROWS, 6 * D_MODEL)),
            _const_spec((D_MODEL, PROJ_W)),
            _const_spec((1, PROJ_W)),
            _const_spec((SEG, SEG)),
        ] + tab_specs,
        out_specs=pl.BlockSpec((PROJ_TM, PROJ_W), lambda i: (i, 0)),
        out_shape=jax.ShapeDtypeStruct((T_ALL, PROJ_W), bf16),
        compiler_params=_cparams("parallel"),
        name="in_projection_" + layer_kind,
    )(*hs, mod, w, gain, gmat, *tables)


HY_TC = 256
HY_NB = 2
HY_FREQ_CHUNK = 512
FEAT_PAD = 64


def _alt_sign(shape, axis):
    idx = lax.broadcasted_iota(jnp.int32, shape, axis)
    return jnp.where((idx & 1) == 0, 1.0, -1.0).astype(f32)


def _filter_kernel(L, z_ref, w1_ref, b1_ref, w2_ref, b2_ref, wf_ref, wb_ref, freq_ref, dec_ref, c_ref, s_ref,
                   spec_ref, nyq_ref):
    hid = jnp.sin(freq_ref[0:1, :] * (_dot3(z_ref[...], w1_ref[...]) + b1_ref[...]))
    hid = jnp.sin(freq_ref[1:2, :] * (_dot3(hid, w2_ref[...]) + b2_ref[...]))
    dec = dec_ref[...]
    fwd = _dot3(hid, wf_ref[0]) * dec
    bwd = _dot3(hid, wb_ref[0]) * dec
    row = lax.broadcasted_iota(jnp.int32, fwd.shape, 0)
    bwd = jnp.where(row == 0, 0.0, bwd)
    even = fwd + bwd
    odd = bwd - fwd
    wk = jnp.where(row == 0, 0.5 / L, 1.0 / L).astype(f32)
    spec_ref[0, 0] = _dot(c_ref[...], even.astype(bf16)) * wk
    spec_ref[0, 1] = _dot(s_ref[...], odd.astype(bf16)) * wk
    nyq = jnp.sum(even * _alt_sign(even.shape, 0), axis=0, keepdims=True) * (0.5 / L)
    nyq_ref[0] = jnp.broadcast_to(nyq, (8, nyq.shape[1]))


def hyena_filter_spectra(L, zfeat, w1, b1, w2, b2, w3r, freq, decay, cmat, smat):
    nct = HY_CH // HY_TC
    return pl.pallas_call(
        functools.partial(_filter_kernel, L),
        grid=(2, nct),
        in_specs=[
            _const_spec((L, FEAT_PAD)),
            _const_spec((FEAT_PAD, HY_FILT_HID)),
            _const_spec((1, HY_FILT_HID)),
            _const_spec((HY_FILT_HID, HY_FILT_HID)),
            _const_spec((1, HY_FILT_HID)),
            pl.BlockSpec((1, HY_FILT_HID, HY_TC), lambda n, c: (2 * n, 0, c)),
            pl.BlockSpec((1, HY_FILT_HID, HY_TC), lambda n, c: (2 * n + 1, 0, c)),
            _const_spec((2, HY_FILT_HID)),
            pl.BlockSpec((L, HY_TC), lambda n, c: (0, c)),
            _const_spec1((L, L)),
            _const_spec1((L, L)),
        ],
        out_specs=[
            pl.BlockSpec((1, 2, L, HY_TC), lambda n, c: (n, 0, 0, c)),
            pl.BlockSpec((1, 8, HY_TC), lambda n, c: (n, 0, c)),
        ],
        out_shape=[
            jax.ShapeDtypeStruct((2, 2, L, HY_CH), f32),
            jax.ShapeDtypeStruct((2, 8, HY_CH), f32),
        ],
        compiler_params=_cparams("arbitrary", "arbitrary"),
        name="hyena_filter_L%d" % L,
    )(zfeat, w1, b1, w2, b2, w3r, w3r, freq, decay, cmat, smat)


def _conv3(u, w, b):
    L = u.shape[0]
    row = lax.broadcasted_iota(jnp.int32, u.shape, 0)
    prev = jnp.where(row == 0, 0.0, pltpu.roll(u, 1, 0))
    nxt = jnp.where(row == L - 1, 0.0, pltpu.roll(u, L - 1, 0))
    return prev * w[0:1, :] + u * w[1:2, :] + nxt * w[2:3, :] + b


def _hyena_kernel(v_ref, x1_ref, x2_ref, wv_ref, w1_ref, w2_ref, bv_ref, b1_ref, b2_ref, spec_ref, nyq_ref,
                  bias_ref, c_ref, s_ref, o_ref, z_ref, yr_ref, yi_ref):
    L = c_ref.shape[0]
    fch = min(L, HY_FREQ_CHUNK)

    def side_by_side(ref, w_ref, b_ref):
        return jnp.concatenate(
            [_conv3(ref[s * L:(s + 1) * L, :].astype(f32), w_ref[...], b_ref[...]) for s in range(HY_NB)], axis=1)

    def rep(x):
        return jnp.concatenate([x] * HY_NB, axis=1)

    z_ref[...] = side_by_side(v_ref, wv_ref, bv_ref)
    gate_refs = ((x1_ref, w1_ref, b1_ref), (x2_ref, w2_ref, b2_ref))
    alt = _alt_sign((fch, HY_NB * HY_TC), 0)
    for n in range(2):
        zb = z_ref[...].astype(bf16)
        x_nyq = jnp.zeros((1, HY_NB * HY_TC), f32)
        for k in range(L // fch):
            rows = slice(k * fch, (k + 1) * fch)
            a = _dot(c_ref[rows, :], zb)
            b = _dot(s_ref[rows, :], zb)
            hr = rep(spec_ref[n, 0, rows, :])
            hi = rep(spec_ref[n, 1, rows, :])
            yr_ref[rows, :] = (a * hr + b * hi).astype(bf16)
            yi_ref[rows, :] = (a * hi - b * hr).astype(bf16)
            x_nyq = x_nyq + jnp.sum(z_ref[rows, :] * alt, axis=0, keepdims=True)
        gate = side_by_side(*gate_refs[n])
        nyq_term = x_nyq * rep(nyq_ref[n, 0:1, :])
        bias = rep(bias_ref[n:n + 1, :])
        for k in range(L // fch):
            rows = slice(k * fch, (k + 1) * fch)
            y = _dot(c_ref[rows, :], yr_ref[...]) - _dot(s_ref[rows, :], yi_ref[...]) + alt * nyq_term
            z_ref[rows, :] = gate[rows, :] * (y + z_ref[rows, :] * bias)
    for s in range(HY_NB):
        o_ref[s * L:(s + 1) * L, :] = z_ref[:, s * HY_TC:(s + 1) * HY_TC].astype(bf16)


def hyena_mix(L, row_block0, proj, conv_w, conv_b, spec, nyq, bias, cmat, smat):
    assert row_block0 % HY_NB == 0 and BATCH % HY_NB == 0
    nct = HY_CH // HY_TC
    nseg = HY_CH // HY_TC
    rows = HY_NB * L

    def col(k):
        return lambda c, b: (row_block0 // HY_NB + b, k * nseg + c)

    def par(k):
        return lambda c, b: (0, k * nseg + c)

    in_specs = (
        [pl.BlockSpec((rows, HY_TC), col(k), pipeline_mode=pl.Buffered(1)) for k in range(3)]
        + [pl.BlockSpec((3, HY_TC), par(k)) for k in range(3)]
        + [pl.BlockSpec((1, HY_TC), par(k)) for k in range(3)]
        + [
            pl.BlockSpec((2, 2, L, HY_TC), lambda c, b: (0, 0, 0, c), pipeline_mode=pl.Buffered(1)),
            pl.BlockSpec((2, 8, HY_TC), lambda c, b: (0, 0, c)),
            pl.BlockSpec((2, HY_TC), lambda c, b: (0, c)),
            _const_spec1((L, L)),
            _const_spec1((L, L)),
        ]
    )
    args = [proj, proj, proj, conv_w, conv_w, conv_w, conv_b, conv_b, conv_b, spec, nyq, bias, cmat, smat]
    return pl.pallas_call(
        _hyena_kernel,
        grid=(nct, BATCH // HY_NB),
        in_specs=in_specs,
        out_specs=pl.BlockSpec((rows, HY_TC), lambda c, b: (b, c)),
        out_shape=jax.ShapeDtypeStruct((BATCH * L, HY_CH), bf16),
        scratch_shapes=[pltpu.VMEM((L, HY_NB * HY_TC), f32),
                        pltpu.VMEM((L, HY_NB * HY_TC), bf16),
                        pltpu.VMEM((L, HY_NB * HY_TC), bf16)],
        compiler_params=_cparams("arbitrary", "arbitrary"),
        name="hyena_mix_L%d" % L,
    )(*args)


DA_TQ = 512


def _diff_attn_kernel(lam_init, q_ref, qc_ref, kc_ref, vc_ref, kl_ref, vl_ref, lam_ref, subln_ref, o_ref, oc_ref):
    i = pl.program_id(1)
    n_lat_blocks = SEQ // DA_TQ

    @pl.when(i < n_lat_blocks)
    def _():
        _diff_attn_body(lam_init, q_ref, (kc_ref, vc_ref, kl_ref, vl_ref), lam_ref, subln_ref, o_ref)

    @pl.when(i == n_lat_blocks)
    def _():
        _diff_attn_body(lam_init, qc_ref, (kc_ref, vc_ref), lam_ref, subln_ref, oc_ref)


def _diff_attn_body(lam_init, q_ref, kv_refs, lam_ref, subln_ref, o_ref):
    n_src = len(kv_refs) // 2
    lp = lam_ref[...]
    lam = (jnp.exp(jnp.sum(lp[0:1] * lp[1:2], axis=-1, keepdims=True))
           - jnp.exp(jnp.sum(lp[2:3] * lp[3:4], axis=-1, keepdims=True)) + lam_init)
    q = q_ref[...]
    tq = q.shape[0]
    lower = lax.broadcasted_iota(jnp.int32, (tq, 2 * DA_HD), 1) < DA_HD
    zero = jnp.zeros((), bf16)
    hw = 2 * DA_HD
    outs = []
    for h in range(DA_HEADS):
        qh = q[:, h * hw:(h + 1) * hw]
        ks = [kv_refs[2 * s][:, h * hw:(h + 1) * hw] for s in range(n_src)]
        vs = [kv_refs[2 * s + 1][:, h * hw:(h + 1) * hw] for s in range(n_src)]
        qs = jnp.concatenate([jnp.where(lower, qh, zero), jnp.where(lower, zero, qh)], axis=0)
        ss = [_dot_nt(qs, k) for k in ks]
        mx = functools.reduce(jnp.maximum, [jnp.max(s, axis=-1, keepdims=True) for s in ss])
        es = [jnp.exp2(s - mx) for s in ss]
        den = functools.reduce(jnp.add, [jnp.sum(e, axis=-1, keepdims=True) for e in es])
        pv = functools.reduce(jnp.add, [_dot(es[s].astype(bf16), vs[s]) for s in range(n_src)])
        pv = pv * (1.0 / den)
        oh = pv[:tq] - lam * pv[tq:]
        outs.append(_rms(oh) * subln_ref[...] * (1.0 - lam_init))
    o_ref[...] = jnp.concatenate(outs, axis=1).astype(bf16)


def diff_attention(proj, lam_p, subln, lam_init):
    width = DA_HEADS * 2 * DA_HD
    qcol, kcol, vcol = 3, 4, 5
    ctx_blk0 = T_LAT // CTX_LEN
    nq = SEQ // DA_TQ

    def lat_rows(b, i):
        return b * nq + jnp.minimum(i, nq - 1)

    return pl.pallas_call(
        functools.partial(_diff_attn_kernel, lam_init),
        grid=(BATCH, nq + 1),
        in_specs=[
            pl.BlockSpec((DA_TQ, width), lambda b, i: (lat_rows(b, i), qcol)),
            pl.BlockSpec((CTX_LEN, width), lambda b, i: (ctx_blk0 + b, qcol)),
            pl.BlockSpec((CTX_LEN, width), lambda b, i: (ctx_blk0 + b, kcol)),
            pl.BlockSpec((CTX_LEN, width), lambda b, i: (ctx_blk0 + b, vcol)),
            pl.BlockSpec((SEQ, width), lambda b, i: (b, kcol)),
            pl.BlockSpec((SEQ, width), lambda b, i: (b, vcol)),
            _const_spec((4, DA_HD)),
            _const_spec((1, 2 * DA_HD)),
        ],
        out_specs=[
            pl.BlockSpec((DA_TQ, width), lambda b, i: (lat_rows(b, i), 0)),
            pl.BlockSpec((CTX_LEN, width), lambda b, i: (b, 0)),
        ],
        out_shape=[
            jax.ShapeDtypeStruct((T_LAT, width), bf16),
            jax.ShapeDtypeStruct((T_CTX, width), bf16),
        ],
        compiler_params=_cparams("parallel", "arbitrary"),
        name="diff_attention",
    )(proj, proj, proj, proj, proj, proj, lam_p, subln)


def _log_sigmoid(x):
    return jnp.minimum(x, 0.0) - jnp.log(1.0 + jnp.exp(-jnp.abs(x)))


def _retention_kernel(q_ref, k_ref, v_ref, g_ref, kc_ref, vc_ref, decay_ref, gn_ref, o_ref, st_ref):
    h = pl.program_id(1)
    ch = RET_CHUNK
    nchunk = SEQ // ch
    lgs = _log_sigmoid(decay_ref[...])
    sel = lax.broadcasted_iota(jnp.int32, lgs.shape, 1) == h
    lg = jnp.sum(jnp.where(sel, lgs, 0.0), axis=-1, keepdims=True)
    lgf = lg[0:1, :]
    lgb = lg[1:2, :]
    ri = lax.broadcasted_iota(jnp.int32, (ch, ch), 0).astype(f32)
    ci = lax.broadcasted_iota(jnp.int32, (ch, ch), 1).astype(f32)
    rel = ri - ci
    dsum = (jnp.where(rel >= 0, jnp.exp(jnp.maximum(rel, 0.0) * lgf), 0.0)
            + jnp.where(rel <= 0, jnp.exp(jnp.maximum(-rel, 0.0) * lgb), 0.0))
    zeta_f = jnp.exp((ch - 1 - ci) * lgf)
    zeta_b = jnp.exp(ci * lgb)
    xi_f = jnp.exp((ri + 1.0) * lgf)
    xi_b = jnp.exp((ch - ri) * lgb)
    gch_f = jnp.exp(ch * lgf)
    gch_b = jnp.exp(ch * lgb)
    dk = q_ref.shape[1]

    kct = kc_ref[...].astype(f32).T
    vc = vc_ref[...]
    cl = lax.broadcasted_iota(jnp.int32, kct.shape, 1).astype(f32)
    s_f = _dot((kct * jnp.exp((CTX_LEN - 1 - cl) * lgf)).astype(bf16), vc)
    s_b = _dot((kct * jnp.exp(cl * lgb)).astype(bf16), vc)

    def rows(n):
        return slice(n * ch, (n + 1) * ch)

    u_f, u_b = [], []
    for n in range(nchunk):
        kt = k_ref[rows(n), :].astype(f32).T
        vn = v_ref[rows(n), :]
        u_f.append(_dot((kt * zeta_f).astype(bf16), vn))
        u_b.append(_dot((kt * zeta_b).astype(bf16), vn))

    for n in range(nchunk):
        st_ref[n, 0:dk, :] = s_f.astype(bf16)
        s_f = gch_f * s_f + u_f[n]
    for n in reversed(range(nchunk)):
        st_ref[n, dk:2 * dk, :] = s_b.astype(bf16)
        s_b = gch_b * s_b + u_b[n]

    gn = gn_ref[...]
    for n in range(nchunk):
        qn = q_ref[rows(n), :]
        att = _dot_nt(qn, k_ref[rows(n), :]) * dsum
        qf = qn.astype(f32)
        lhs = jnp.concatenate([att.astype(bf16), (qf * xi_f).astype(bf16), (qf * xi_b).astype(bf16)], axis=1)
        rhs = jnp.concatenate([v_ref[rows(n), :], st_ref[n]], axis=0)
        o = _dot(lhs, rhs)
        mu = jnp.mean(o, axis=-1, keepdims=True)
        oc = o - mu
        var = jnp.mean(oc * oc, axis=-1, keepdims=True)
        y = oc * lax.rsqrt(var + EPS) * gn * _silu(g_ref[rows(n), :].astype(f32))
        o_ref[rows(n), :] = y.astype(bf16)


def retention(proj, decay, gn_w):
    dk = RET_DK
    ctx_blk0 = T_LAT // CTX_LEN
    return pl.pallas_call(
        _retention_kernel,
        grid=(BATCH, RET_HEADS),
        in_specs=[
            pl.BlockSpec((SEQ, dk), lambda b, h: (b, h)),
            pl.BlockSpec((SEQ, dk), lambda b, h: (b, RET_HEADS + h)),
            pl.BlockSpec((SEQ, dk), lambda b, h: (b, 2 * RET_HEADS + h)),
            pl.BlockSpec((SEQ, dk), lambda b, h: (b, 3 * RET_HEADS + h)),
            pl.BlockSpec((CTX_LEN, dk), lambda b, h: (ctx_blk0 + b, RET_HEADS + h)),
            pl.BlockSpec((CTX_LEN, dk), lambda b, h: (ctx_blk0 + b, 2 * RET_HEADS + h)),
            _const_spec((2, RET_HEADS)),
            pl.BlockSpec((1, dk), lambda b, h: (0, h)),
        ],
        out_specs=pl.BlockSpec((SEQ, dk), lambda b, h: (b, h)),
        out_shape=jax.ShapeDtypeStruct((T_LAT, RET_HEADS * dk), bf16),
        scratch_shapes=[pltpu.VMEM((SEQ // RET_CHUNK, 2 * dk, dk), bf16)],
        compiler_params=_cparams("parallel", "arbitrary"),
        name="retention",
    )(proj, proj, proj, proj, proj, proj, decay, gn_w)


GQ_TQ = 128
GQ_SPAN = 3 * GQ_TQ


def _gqa_kernel(q_ref, kv_ref, kvc_ref, sink_ref, o_ref):
    n = pl.program_id(1)
    start = pl.multiple_of(jnp.clip((n - 1) * GQ_TQ, 0, SEQ - GQ_SPAN), GQ_TQ)
    pw = 2 * GQ_HD
    n_heads = GQ_KV * GQ_GROUP
    n_keys = CTX_LEN + GQ_SPAN
    col = lax.broadcasted_iota(jnp.int32, (GQ_TQ, n_keys), 1)
    qpos = n * GQ_TQ + lax.broadcasted_iota(jnp.int32, (GQ_TQ, n_keys), 0)
    dist = jnp.where(col < CTX_LEN, 0, jnp.abs(start + col - CTX_LEN - qpos))
    mask = dist <= WINDOW
    q = q_ref[...]
    lower = lax.broadcasted_iota(jnp.int32, (GQ_TQ, pw), 1) < GQ_HD
    outs = [None] * n_heads
    for swapped in (0, 1):
        kcol = slice(2 * swapped * pw, (2 * swapped + 1) * pw)
        vcol = slice((2 * swapped + 1) * pw, (2 * swapped + 2) * pw)
        k = jnp.concatenate([kvc_ref[:, kcol], kv_ref[pl.ds(start, GQ_SPAN), kcol]], axis=0)
        v = jnp.concatenate([kvc_ref[:, vcol], kv_ref[pl.ds(start, GQ_SPAN), vcol]], axis=0)
        heads = [h for h in range(n_heads) if ((h // GQ_GROUP) == (h % 2)) == (swapped == 0)]
        qs = jnp.concatenate(
            [jnp.where(lower == (h % 2 == 0), q[:, (h // 2) * pw:(h // 2 + 1) * pw], jnp.zeros((), bf16))
             for h in heads], axis=0)
        s = _dot_nt(qs, k)
        ps = []
        for i, h in enumerate(heads):
            sh = jnp.where(mask, s[i * GQ_TQ:(i + 1) * GQ_TQ], NEG_INF)
            sink = sink_ref[h]
            mx = jnp.maximum(jnp.max(sh, axis=-1, keepdims=True), sink)
            e = jnp.exp(sh - mx)
            inv = 1.0 / (jnp.sum(e, axis=-1, keepdims=True) + jnp.exp(sink - mx))
            ps.append((e * inv).astype(bf16))
        o = _dot(jnp.concatenate(ps, axis=0), v)
        for i, h in enumerate(heads):
            outs[h] = o[i * GQ_TQ:(i + 1) * GQ_TQ]
    o_ref[...] = jnp.concatenate(
        [jnp.where(lower, outs[2 * j], outs[2 * j + 1]) for j in range(n_heads // 2)], axis=1).astype(bf16)


def window_gqa(proj, sink):
    width = GQ_KV * GQ_GROUP * GQ_HD
    nq = SEQ // GQ_TQ
    kvw = 4 * GQ_KV * GQ_HD
    kv_col = (5 * SEG) // kvw
    ctx_blk0 = T_LAT // CTX_LEN
    return pl.pallas_call(
        _gqa_kernel,
        grid=(BATCH, nq),
        in_specs=[
            pl.BlockSpec((GQ_TQ, width), lambda b, n: (b * nq + n, 4)),
            pl.BlockSpec((SEQ, kvw), lambda b, n: (b, kv_col)),
            pl.BlockSpec((CTX_LEN, kvw), lambda b, n: (ctx_blk0 + b, kv_col)),
            pl.BlockSpec(memory_space=pltpu.SMEM),
        ],
        out_specs=pl.BlockSpec((GQ_TQ, width), lambda b, n: (b * nq + n, 0)),
        out_shape=jax.ShapeDtypeStruct((T_LAT, width), bf16),
        compiler_params=_cparams("parallel", "arbitrary"),
        name="window_gqa",
    )(proj, proj, proj, sink)


OUT_TM = 512
ROUTE_W = LANES
MOE_TB = 256


def _route(logits):
    lane_i = lax.broadcasted_iota(jnp.int32, logits.shape, 1)
    lane = lane_i.astype(f32)
    big = float(1 << 20)
    valid = lane_i < N_EXPERTS
    le = logits
    lgx = pltpu.roll(logits, ROUTE_W - N_EXPERTS, 1)
    lgx = jnp.where(valid, lgx, NEG_INF)
    gmax = jnp.max(lgx, axis=-1, keepdims=True)
    grp = (lane_i // EXP_PER_GROUP).astype(f32)
    g_sel = jnp.min(jnp.where(lgx == gmax, grp, big), axis=-1, keepdims=True)
    p_grp = float(EXP_PER_GROUP) / jnp.sum(jnp.exp(lgx - gmax), axis=-1, keepdims=True)
    lm = jnp.where(valid, jnp.where(grp == g_sel, le, NEG_INF), NEG_INF)
    v1 = jnp.max(lm, axis=-1, keepdims=True)
    i1 = jnp.min(jnp.where(lm == v1, lane, big), axis=-1, keepdims=True)
    lm2 = jnp.where(lane == i1, NEG_INF, lm)
    v2 = jnp.max(lm2, axis=-1, keepdims=True)
    i2 = jnp.min(jnp.where(lm2 == v2, lane, big), axis=-1, keepdims=True)
    e2 = jnp.exp(v2 - v1)
    w1 = p_grp / (1.0 + e2)
    w2 = w1 * e2
    return jnp.where(lane == i1, w1, 0.0) + jnp.where(lane == i2, w2, 0.0)


def _outproj_kernel(stacked, *refs):
    i = pl.program_id(0)
    if stacked:
        (ya_ref, yac_ref, yb_ref, ybc_ref, w_ref, x_ref, c_ref, mod_ref, wrh_ref, wrl_ref, br_ref,
         hn_ref, v_ref, comb_ref, cnt_ref) = refs
        ya = _stacked_tile(i, OUT_TM, ya_ref, yac_ref)
        yb = _stacked_tile(i, OUT_TM, yb_ref, ybc_ref)
        h = _stacked_tile(i, OUT_TM, x_ref, c_ref)
    else:
        ya_ref, yb_ref, w_ref, h_ref, mod_ref, wrh_ref, wrl_ref, br_ref, hn_ref, v_ref, comb_ref, cnt_ref = refs
        ya = ya_ref[...]
        yb = yb_ref[...]
        h = h_ref[...]
    r = _mod_row(i, OUT_TM)
    g1 = mod_ref[pl.ds(r, 1), pl.ds(2 * D_MODEL, D_MODEL)]
    sh2 = mod_ref[pl.ds(r, 1), pl.ds(3 * D_MODEL, D_MODEL)]
    sc2 = mod_ref[pl.ds(r, 1), pl.ds(4 * D_MODEL, D_MODEL)]
    half = ya.shape[1]
    m = _dot(ya, w_ref[0:half, :]) + _dot(yb, w_ref[half:2 * half, :])
    hn = h + g1 * m
    hn_ref[...] = hn
    v = _rms(hn) * (1.0 + sc2) + sh2
    v_ref[...] = v.astype(bf16)
    vh, vl = _split(v)
    wrh = wrh_ref[...]
    comb = _route(_dot(vh, wrh) + _dot(vl, wrh) + _dot(vh, wrl_ref[...]) + br_ref[...])
    comb_ref[...] = comb
    for s in range(OUT_TM // MOE_TB):
        cnt = jnp.sum((comb[s * MOE_TB:(s + 1) * MOE_TB] != 0.0).astype(f32), axis=0, keepdims=True)
        cnt_ref[s] = jnp.broadcast_to(cnt, (8, ROUTE_W)).astype(jnp.int32)


def out_projection(n_rows, yas, ybs, w, hs, mod, wr, br):
    stacked = len(yas) == 2
    assert stacked == (len(hs) == 2) == (len(ybs) == 2) and (not stacked or n_rows == T_ALL)
    half = ybs[0].shape[1]
    row_spec = lambda width: [pl.BlockSpec((OUT_TM, width), lambda i: (i, 0))]
    rows = lambda width: _stacked_specs(OUT_TM, width) if stacked else row_spec(width)
    wr_hi = wr.astype(bf16)
    wr_lo = (wr - wr_hi.astype(f32)).astype(bf16)
    return pl.pallas_call(
        functools.partial(_outproj_kernel, stacked),
        grid=(n_rows // OUT_TM,),
        in_specs=(
            rows(half) + rows(half)
            + [_const_spec((2 * half, D_MODEL))]
            + rows(D_MODEL)
            + [_const_spec((MOD_ROWS, 6 * D_MODEL)),
               _const_spec((D_MODEL, ROUTE_W)),
               _const_spec((D_MODEL, ROUTE_W)),
               _const_spec((1, ROUTE_W))]
        ),
        out_specs=[
            pl.BlockSpec((OUT_TM, D_MODEL), lambda i: (i, 0)),
            pl.BlockSpec((OUT_TM, D_MODEL), lambda i: (i, 0)),
            pl.BlockSpec((OUT_TM, ROUTE_W), lambda i: (i, 0)),
            pl.BlockSpec((OUT_TM // MOE_TB, 8, ROUTE_W), lambda i: (i, 0, 0)),
        ],
        out_shape=[
            jax.ShapeDtypeStruct((n_rows, D_MODEL), f32),
            jax.ShapeDtypeStruct((n_rows, D_MODEL), bf16),
            jax.ShapeDtypeStruct((n_rows, ROUTE_W), f32),
            jax.ShapeDtypeStruct((n_rows // MOE_TB, 8, ROUTE_W), jnp.int32),
        ],
        compiler_params=_cparams("parallel"),
        name="out_projection",
    )(*yas, *ybs, w, *hs, mod, wr_hi, wr_lo, br)


MOE_UNIT = 16
MOE_TG = 512
MOE_TOP = 2
MOE_RLOC = MOE_TOP * MOE_TB + N_EXPERTS * MOE_UNIT
MOE_NUNIT = MOE_RLOC // MOE_UNIT
MOE_META = 128


def _moe_rows(n_blk):
    return n_blk * MOE_RLOC + N_EXPERTS * MOE_TG


def _moe_plan_kernel(n_blk, cnt_ref, units_ref, dst_ref, tile_exp_ref, meta_ref):
    def clear(k, c):
        tile_exp_ref[k] = 0
        return c

    lax.fori_loop(0, tile_exp_ref.shape[0], clear, 0)

    def clear_meta(k, c):
        meta_ref[k] = 0
        return c

    lax.fori_loop(0, MOE_META, clear_meta, 0)

    def per_expert(e, goff):
        def per_blk(t, acc):
            u = (cnt_ref[t * N_EXPERTS + e] + (MOE_UNIT - 1)) // MOE_UNIT
            units_ref[t * N_EXPERTS + e] = u
            dst_ref[t * N_EXPERTS + e] = goff + acc
            return acc + u * MOE_UNIT

        n_e = lax.fori_loop(0, n_blk, per_blk, 0)
        nt = (n_e + (MOE_TG - 1)) // MOE_TG
        t0 = goff // MOE_TG

        def mark(k, c):
            tile_exp_ref[t0 + k] = e
            return c

        lax.fori_loop(0, nt, mark, 0)
        meta_ref[1 + e] = goff + n_e
        meta_ref[1 + N_EXPERTS + e] = (nt * MOE_TG - n_e) // MOE_UNIT
        return goff + nt * MOE_TG

    total = lax.fori_loop(0, N_EXPERTS, per_expert, 0)
    meta_ref[0] = total // MOE_TG


def moe_plan(n_blk, counts):
    smem = pl.BlockSpec(memory_space=pltpu.SMEM)
    n_tiles = _moe_rows(n_blk) // MOE_TG
    return pl.pallas_call(
        functools.partial(_moe_plan_kernel, n_blk),
        in_specs=[smem],
        out_specs=[smem, smem, smem, smem],
        out_shape=[
            jax.ShapeDtypeStruct((n_blk * N_EXPERTS,), jnp.int32),
            jax.ShapeDtypeStruct((n_blk * N_EXPERTS,), jnp.int32),
            jax.ShapeDtypeStruct((n_tiles,), jnp.int32),
            jax.ShapeDtypeStruct((MOE_META,), jnp.int32),
        ],
        name="moe_plan",
    )(counts)


def _block_routes(comb, ltri, utri):
    oh = comb != 0.0
    ohf = jnp.where(oh, 1.0, 0.0)
    rank = _dot(ltri, ohf.astype(bf16))
    cnt = jnp.sum(ohf, axis=0, keepdims=True)
    units = jnp.floor((cnt + (MOE_UNIT - 1.0)) * (1.0 / MOE_UNIT))
    seg = _dot(jnp.broadcast_to(units, (8, ROUTE_W)).astype(bf16), utri)[0:1, :] * MOE_UNIT
    dest = seg + rank
    big = float(1 << 20)
    d_a = jnp.min(jnp.where(oh, dest, big), axis=-1, keepdims=True)
    d_b = jnp.max(jnp.where(oh, dest, -1.0), axis=-1, keepdims=True)
    w_a = jnp.sum(jnp.where(oh, jnp.where(dest == d_a, comb, 0.0), 0.0), axis=-1, keepdims=True)
    w_b = jnp.sum(jnp.where(oh, jnp.where(dest == d_b, comb, 0.0), 0.0), axis=-1, keepdims=True)
    second = d_b != d_a
    return d_a, jnp.where(second, d_b, -1.0), w_a, jnp.where(second, w_b, 0.0)


def _one_hot_rows(d):
    r = lax.broadcasted_iota(jnp.int32, (d.shape[0], MOE_RLOC), 1).astype(f32)
    return jnp.where(r == d, 1.0, 0.0).astype(bf16)


def _unit_table(t, units_ref, dst_ref, tab_ref, pad_row0):
    def per_expert(e, j):
        nu = units_ref[t * N_EXPERTS + e]
        base = dst_ref[t * N_EXPERTS + e]

        def per_unit(u, c):
            tab_ref[j + u] = base + u * MOE_UNIT
            return c

        lax.fori_loop(0, nu, per_unit, 0)
        return j + nu

    n_used = lax.fori_loop(0, N_EXPERTS, per_expert, 0)

    def pad(j, c):
        tab_ref[j] = pad_row0 + j * MOE_UNIT
        return c

    lax.fori_loop(n_used, MOE_NUNIT, pad, 0)
    return n_used


def _unit_rows(ref, row):
    return ref.at[pl.ds(pl.multiple_of(row, MOE_UNIT), MOE_UNIT)]


def _wait_all_units(local, remote, sem):
    pltpu.make_async_copy(local, remote.at[pl.ds(0, MOE_RLOC)], sem).wait()


def _gap_copies(meta_ref, zero_ref, remote, sem, wait):
    def per_expert(e, c):
        start = meta_ref[1 + e]

        def per_unit(u, c2):
            ro = remote.at[pl.ds(pl.multiple_of(start + u * MOE_UNIT, MOE_UNIT), MOE_UNIT)]
            cp = pltpu.make_async_copy(zero_ref, ro, sem)
            if wait:
                cp.wait()
            else:
                cp.start()
            return c2

        lax.fori_loop(0, meta_ref[1 + N_EXPERTS + e], per_unit, 0)
        return c

    lax.fori_loop(0, N_EXPERTS, per_expert, 0)


def _moe_dispatch_kernel(n_blk, units_ref, dst_ref, meta_ref, x_ref, comb_ref, ltri_ref, utri_ref, xs_ref,
                         buf_ref, zero_ref, tab_ref, sem_ref):
    t = pl.program_id(0)
    slot = t % 2
    d_a, d_b, _, _ = _block_routes(comb_ref[...], ltri_ref[...], utri_ref[...])
    p = _one_hot_rows(d_a) + _one_hot_rows(d_b)
    rows = lax.dot_general(p, x_ref[...], (((0,), (0,)), ((), ())), preferred_element_type=f32)
    buf_ref[slot] = rows.astype(bf16)
    _unit_table(t, units_ref, dst_ref, tab_ref, _moe_rows(n_blk) + slot * MOE_RLOC)
    for j in range(MOE_NUNIT):
        pltpu.make_async_copy(buf_ref.at[slot, pl.ds(j * MOE_UNIT, MOE_UNIT)], _unit_rows(xs_ref, tab_ref[j]),
                              sem_ref.at[slot]).start()

    @pl.when(t > 0)
    def _():
        _wait_all_units(buf_ref.at[1 - slot], xs_ref, sem_ref.at[1 - slot])

    @pl.when(t == n_blk - 1)
    def _():
        zero_ref[...] = jnp.zeros_like(zero_ref)
        _gap_copies(meta_ref, zero_ref, xs_ref, sem_ref.at[2], False)
        _wait_all_units(buf_ref.at[slot], xs_ref, sem_ref.at[slot])
        _gap_copies(meta_ref, zero_ref, xs_ref, sem_ref.at[2], True)


def moe_dispatch(n_blk, units, dst, meta, x, comb, ltri, utri):
    return pl.pallas_call(
        functools.partial(_moe_dispatch_kernel, n_blk),
        grid_spec=pltpu.PrefetchScalarGridSpec(
            num_scalar_prefetch=3,
            grid=(n_blk,),
            in_specs=[
                pl.BlockSpec((MOE_TB, D_MODEL), lambda t, *_: (t, 0)),
                pl.BlockSpec((MOE_TB, ROUTE_W), lambda t, *_: (t, 0)),
                pl.BlockSpec((MOE_TB, MOE_TB), lambda t, *_: (0, 0)),
                pl.BlockSpec((ROUTE_W, ROUTE_W), lambda t, *_: (0, 0)),
            ],
            out_specs=pl.BlockSpec(memory_space=pl.ANY),
            scratch_shapes=[
                pltpu.VMEM((2, MOE_RLOC, D_MODEL), bf16),
                pltpu.VMEM((MOE_UNIT, D_MODEL), bf16),
                pltpu.SMEM((MOE_NUNIT,), jnp.int32),
                pltpu.SemaphoreType.DMA((3,)),
            ],
        ),
        out_shape=jax.ShapeDtypeStruct((_moe_rows(n_blk) + 2 * MOE_RLOC, D_MODEL), bf16),
        compiler_params=_cparams("arbitrary"),
        name="moe_dispatch",
    )(units, dst, meta, x, comb, ltri, utri)


def _moe_expert_kernel(tile_exp_ref, meta_ref, xs_ref, wg_ref, wu_ref, wd_ref, ys_ref, wgub_ref, wdb_ref):
    i = pl.program_id(0)

    @pl.when(i < meta_ref[0])
    def _():
        prev = tile_exp_ref[jnp.maximum(i - 1, 0)]

        @pl.when(jnp.logical_or(i == 0, tile_exp_ref[i] != prev))
        def _():
            wgub_ref[:, :D_EXPERT] = wg_ref[0].astype(bf16)
            wgub_ref[:, D_EXPERT:] = wu_ref[0].astype(bf16)
            wdb_ref[...] = wd_ref[0].astype(bf16)

        gu = _dot(xs_ref[...], wgub_ref[...])
        a = _silu(gu[:, :D_EXPERT]) * gu[:, D_EXPERT:]
        ys_ref[...] = _dot(a.astype(bf16), wdb_ref[...]).astype(bf16)


def moe_experts(n_blk, layer, tile_exp, meta, xs, w_gate, w_up, w_down):
    n_tiles = _moe_rows(n_blk) // MOE_TG

    def row_map(i, te, meta):
        return (jnp.minimum(i, meta[0] - 1), 0)

    def w_map(i, te, meta):
        return (layer * N_EXPERTS + te[jnp.minimum(i, meta[0] - 1)], 0, 0)

    return pl.pallas_call(
        _moe_expert_kernel,
        grid_spec=pltpu.PrefetchScalarGridSpec(
            num_scalar_prefetch=2,
            grid=(n_tiles,),
            in_specs=[
                pl.BlockSpec((MOE_TG, D_MODEL), row_map),
                pl.BlockSpec((1, D_MODEL, D_EXPERT), w_map),
                pl.BlockSpec((1, D_MODEL, D_EXPERT), w_map),
                pl.BlockSpec((1, D_EXPERT, D_MODEL), w_map),
            ],
            out_specs=pl.BlockSpec((MOE_TG, D_MODEL), row_map),
            scratch_shapes=[
                pltpu.VMEM((D_MODEL, 2 * D_EXPERT), bf16),
                pltpu.VMEM((D_EXPERT, D_MODEL), bf16),
            ],
        ),
        out_shape=jax.ShapeDtypeStruct((_moe_rows(n_blk), D_MODEL), bf16),
        compiler_params=_cparams("arbitrary"),
        name="moe_experts",
    )(tile_exp, meta, xs, w_gate, w_up, w_down)


def _moe_combine_kernel(n_blk, units_ref, dst_ref, ys_ref, zeros_ref, comb_ref, h_ref, mod_ref, ltri_ref, utri_ref,
                        o_ref, buf_ref, tab_ref, sem_ref):
    t = pl.program_id(0)
    slot = t % 2

    def gather(tt, s):
        n_used = _unit_table(tt, units_ref, dst_ref, tab_ref, 0)
        for j in range(MOE_NUNIT):
            dst = buf_ref.at[s, pl.ds(j * MOE_UNIT, MOE_UNIT)]

            @pl.when(j < n_used)
            def _():
                pltpu.make_async_copy(_unit_rows(ys_ref, tab_ref[j]), dst, sem_ref.at[s]).start()

            @pl.when(j >= n_used)
            def _():
                pltpu.make_async_copy(zeros_ref, dst, sem_ref.at[s]).start()

    @pl.when(t == 0)
    def _():
        gather(0, 0)

    @pl.when(t + 1 < n_blk)
    def _():
        gather(t + 1, 1 - slot)

    _wait_all_units(buf_ref.at[slot], ys_ref, sem_ref.at[slot])
    d_a, d_b, w_a, w_b = _block_routes(comb_ref[...], ltri_ref[...], utri_ref[...])
    p = jnp.concatenate([_one_hot_rows(d_a), _one_hot_rows(d_b)], axis=0)
    picked = _dot(p, buf_ref[slot])
    m = w_a * picked[:MOE_TB] + w_b * picked[MOE_TB:]
    r = _mod_row(t, MOE_TB)
    g2 = mod_ref[pl.ds(r, 1), pl.ds(5 * D_MODEL, D_MODEL)]
    o_ref[...] = h_ref[...] + g2 * m


def moe_combine(n_blk, units, dst, ys, comb, h, mod, ltri, utri):
    return pl.pallas_call(
        functools.partial(_moe_combine_kernel, n_blk),
        grid_spec=pltpu.PrefetchScalarGridSpec(
            num_scalar_prefetch=2,
            grid=(n_blk,),
            in_specs=[
                pl.BlockSpec(memory_space=pl.ANY),
                pl.BlockSpec(memory_space=pl.ANY),
                pl.BlockSpec((MOE_TB, ROUTE_W), lambda t, *_: (t, 0)),
                pl.BlockSpec((MOE_TB, D_MODEL), lambda t, *_: (t, 0)),
                pl.BlockSpec((MOD_ROWS, 6 * D_MODEL), lambda t, *_: (0, 0)),
                pl.BlockSpec((MOE_TB, MOE_TB), lambda t, *_: (0, 0)),
                pl.BlockSpec((ROUTE_W, ROUTE_W), lambda t, *_: (0, 0)),
            ],
            out_specs=pl.BlockSpec((MOE_TB, D_MODEL), lambda t, *_: (t, 0)),
            scratch_shapes=[
                pltpu.VMEM((2, MOE_RLOC, D_MODEL), bf16),
                pltpu.SMEM((MOE_NUNIT,), jnp.int32),
                pltpu.SemaphoreType.DMA((2,)),
            ],
        ),
        out_shape=jax.ShapeDtypeStruct((n_blk * MOE_TB, D_MODEL), f32),
        compiler_params=_cparams("arbitrary"),
        name="moe_combine",
    )(units, dst, ys, jnp.zeros((MOE_UNIT, D_MODEL), bf16), comb, h, mod, ltri, utri)


def sparse_moe(n_rows, layer, v, comb, counts, h, mod, w_gate, w_up, w_down, ltri, utri):
    n_blk = n_rows // MOE_TB
    cnt = counts[:, 0, :N_EXPERTS].reshape(n_blk * N_EXPERTS)
    units, dst, tile_exp, meta = moe_plan(n_blk, cnt)
    xs = moe_dispatch(n_blk, units, dst, meta, v, comb, ltri, utri)
    ys = moe_experts(n_blk, layer, tile_exp, meta, xs,
                     w_gate.reshape(DEPTH * N_EXPERTS, D_MODEL, D_EXPERT),
                     w_up.reshape(DEPTH * N_EXPERTS, D_MODEL, D_EXPERT),
                     w_down.reshape(DEPTH * N_EXPERTS, D_EXPERT, D_MODEL))
    return moe_combine(n_blk, units, dst, ys, comb, h, mod, ltri, utri)


def _dft_tables(L):
    k = np.arange(L, dtype=np.int64)
    ang = (2.0 * np.pi / (2 * L)) * ((k[:, None] * k[None, :]) % (2 * L)).astype(np.float64)
    return np.cos(ang).astype(np.float32), np.sin(ang).astype(np.float32)


def _filter_features(L):
    bands = (HY_EMB - 1) // 2
    t = np.linspace(0.0, 1.0, L, dtype=np.float32).astype(np.float64)[:, None]
    w = (2.0 * np.pi / L) * np.arange(L, dtype=np.float64)[:, None]
    fb = np.linspace(1e-4, bands - 1, bands, dtype=np.float32).astype(np.float64)[None, :]
    z = np.concatenate([t, np.cos(fb * w), -np.sin(fb * w)], axis=-1)
    zp = np.zeros((L, FEAT_PAD), np.float32)
    zp[:, :HY_EMB] = z
    deltas = np.abs(np.linspace(HY_MIN_DECAY, HY_MAX_DECAY, HY_CH, dtype=np.float32).astype(np.float64))
    decay = np.exp(-t * deltas[None, :]).astype(np.float32)
    return zp, decay


def _rope_table(cos, sin, half, tm):
    S, width = cos.shape
    low = (np.arange(width) % (2 * half)) < half
    tab = np.zeros((3, S + tm, width), np.float32)
    tab[0, :S] = cos
    tab[0, S:] = 1.0
    tab[1, :S] = np.where(low[None, :], 0.0, sin)
    tab[2, :S] = np.where(low[None, :], -sin, 0.0)
    return tab


def _axial_rope_table(head_dim, tm):
    rows = SEQ // GRID_W
    nf = head_dim // 4
    row = np.repeat(np.arange(rows), GRID_W).astype(np.float64)
    col = np.tile(np.arange(GRID_W), rows).astype(np.float64)
    inv = ROPE_BASE ** (-np.arange(nf, dtype=np.float64) / nf)
    ang = np.stack([row[:, None] * inv, col[:, None] * inv], axis=1)
    a = np.broadcast_to(ang[:, :, None, :], (SEQ, 2, 2, nf)).reshape(SEQ, head_dim)
    reps = LANES // head_dim
    a = np.tile(a, (1, reps))
    return _rope_table(np.cos(a), np.sin(a), nf, tm)


def _seq_rope_table(head_dim, tm):
    inv = 1.0 / (ROPE_BASE ** np.linspace(0.0, 1.0, head_dim // 2, dtype=np.float32).astype(np.float64))
    ang = np.arange(SEQ, dtype=np.float64)[:, None] * inv
    a = np.concatenate([ang, ang], axis=1)
    return _rope_table(np.cos(a), np.sin(a), head_dim // 2, tm)


def _group_mean_matrix():
    g = np.arange(SEG) // DA_HD
    return (g[:, None] == g[None, :]).astype(np.float32) / DA_HD


def _router_weights(w_grp, b_grp, w_rt, b_rt):
    pad = ROUTE_W - 2 * N_EXPERTS
    wr = jnp.concatenate([w_rt, jnp.repeat(w_grp, EXP_PER_GROUP, axis=1),
                          jnp.zeros((D_MODEL, pad), f32)], axis=1)
    br = jnp.concatenate([b_rt, jnp.repeat(b_grp, EXP_PER_GROUP), jnp.zeros((pad,), f32)])[None, :]
    return wr, br


def _strict_lower(n):
    i = np.arange(n)
    return (i[None, :] < i[:, None]).astype(np.float32)


def kernel(x, c, ctx, c_ctx, ada_w, ada_b, e_w_in, e_w_out, hy_conv_w, hy_conv_b, hy_f_w1, hy_f_b1, hy_f_w2, hy_f_b2, hy_f_w3, hy_f_freq, hy_bias, da_q_norm, da_k_norm, da_lam, da_subln, o_w_in, o_w_out, ret_decay, ret_gn, gq_q_norm, gq_k_norm, gq_sink, moe_w_grp, moe_b_grp, moe_w_rt, moe_b_rt, moe_w_gate, moe_w_up, moe_w_down):
    assert x.shape == (BATCH, SEQ, D_MODEL) and ctx.shape == (BATCH, CTX_LEN, D_MODEL)
    x_rows = x.reshape(T_LAT, D_MODEL)
    ctx_rows = ctx.reshape(T_CTX, D_MODEL)
    c_rows = jnp.concatenate([c, c_ctx[None, :], jnp.zeros((MOD_ROWS - BATCH - 1, D_MODEL), f32)], axis=0)
    mod = ada_modulation(c_rows, ada_w, ada_b)

    gmat = jnp.asarray(_group_mean_matrix()).astype(bf16)
    ax_tab = jnp.asarray(_axial_rope_table(DA_HD, PROJ_TM))
    r1_tab = jnp.asarray(_seq_rope_table(RET_DK, PROJ_TM))
    ones = jnp.ones((SEG,), f32)

    lam_init0 = 0.8 - 0.6 * math.exp(-0.3 * 0)
    reps = SEG // DA_HD
    gain0 = jnp.concatenate([ones, ones, ones, jnp.tile(da_q_norm[0], reps) * (DA_HD ** -0.5 * LOG2E),
                             jnp.tile(da_k_norm[0], reps), ones])[None, :]
    proj0 = in_projection("even", [x_rows, ctx_rows], mod[0], e_w_in[0].astype(bf16), gain0, gmat, [ax_tab])

    w3r = hy_f_w3[0].reshape(HY_FILT_HID, 4, HY_CH).transpose(1, 0, 2)
    w1p = jnp.concatenate([hy_f_w1[0], jnp.zeros((FEAT_PAD - HY_EMB, HY_FILT_HID), f32)], axis=0)
    y_hy = []
    for L, blk0 in ((SEQ, 0), (CTX_LEN, T_LAT // CTX_LEN)):
        zfeat, decay = _filter_features(L)
        cm, sm = _dft_tables(L)
        cm = jnp.asarray(cm).astype(bf16)
        sm = jnp.asarray(sm).astype(bf16)
        spec, nyq = hyena_filter_spectra(L, jnp.asarray(zfeat), w1p, hy_f_b1[0][None, :], hy_f_w2[0],
                                         hy_f_b2[0][None, :], w3r, hy_f_freq[0], jnp.asarray(decay), cm, sm)
        y_hy.append(hyena_mix(L, blk0, proj0, hy_conv_w[0], hy_conv_b[0][None, :], spec, nyq, hy_bias[0], cm, sm))

    y_da = diff_attention(proj0, da_lam[0], da_subln[0][None, :], lam_init0)

    wr0, br0 = _router_weights(moe_w_grp[0], moe_b_grp[0], moe_w_rt[0], moe_b_rt[0])
    ltri = jnp.asarray(_strict_lower(MOE_TB)).astype(bf16)
    utri = jnp.asarray(_strict_lower(ROUTE_W).T).astype(bf16)
    h, v, comb, counts = out_projection(T_ALL, y_hy, y_da, e_w_out[0].astype(bf16), [x_rows, ctx_rows], mod[0],
                                        wr0, br0)
    h = sparse_moe(T_ALL, 0, v, comb, counts, h, mod[0], moe_w_gate, moe_w_up, moe_w_down, ltri, utri)

    w_in1 = jnp.concatenate([o_w_in[0], jnp.zeros((D_MODEL, PROJ_W - o_w_in.shape[2]), f32)], axis=1).astype(bf16)
    kq = GQ_KV * GQ_HD
    gain1 = jnp.concatenate([ones, ones * RET_DK ** -0.5, ones, ones,
                             jnp.tile(gq_q_norm[0], reps) * GQ_HD ** -0.5,
                             jnp.tile(gq_k_norm[0], kq // GQ_HD), jnp.ones((SEG - kq,), f32)])[None, :]
    proj1 = in_projection("odd", [h], mod[1], w_in1, gain1, gmat, [ax_tab, r1_tab])
    y_ret = retention(proj1, ret_decay[0], ret_gn[0][None, :])
    y_gq = window_gqa(proj1, gq_sink[0])
    wr1, br1 = _router_weights(moe_w_grp[1], moe_b_grp[1], moe_w_rt[1], moe_b_rt[1])
    h_lat, v, comb, counts = out_projection(T_LAT, [y_ret], [y_gq], o_w_out[0].astype(bf16), [h], mod[1], wr1, br1)
    out = sparse_moe(T_LAT, 1, v, comb, counts, h_lat, mod[1], moe_w_gate, moe_w_up, moe_w_down, ltri, utri)
    return out.reshape(BATCH, SEQ, D_MODEL)
```

```python
import functools
import math

import numpy as np
import jax
import jax.numpy as jnp
from jax import lax
from jax.experimental import pallas as pl
from jax.experimental.pallas import tpu as pltpu

f32 = jnp.float32
bf16 = jnp.bfloat16

D_MODEL = 1024
BATCH = 8
SEQ = 2048
DEPTH = 2
GRID_W = 64
CTX_LEN = 256
EPS = 1e-6
NEG_INF = -1e30
LOG2E = math.log2(math.e)
ROPE_BASE = 10000.0
HY_CH = D_MODEL // 2
HY_EMB = 33
HY_FILT_HID = 64
HY_MAX_DECAY = math.log(1e-2) / 0.3
HY_MIN_DECAY = math.log(1e-2) / 1.5
DA_HEADS = 4
DA_HD = D_MODEL // 16
RET_HEADS = 4
RET_DK = D_MODEL // 8
RET_CHUNK = 128
GQ_KV = 2
GQ_GROUP = 4
GQ_HD = D_MODEL // 16
WINDOW = 128
N_GROUPS = 4
EXP_PER_GROUP = 8
N_EXPERTS = N_GROUPS * EXP_PER_GROUP
D_EXPERT = D_MODEL // 4

T_LAT = BATCH * SEQ
T_CTX = BATCH * CTX_LEN
T_ALL = T_LAT + T_CTX
PROJ_W = 3072
SEG = 512
CTX_MOD_ROW = BATCH
MOD_ROWS = 16

LANES = 128
VMEM_LIMIT_BYTES = 56 * 1024 * 1024


def _cparams(*sem):
    return pltpu.CompilerParams(dimension_semantics=sem, vmem_limit_bytes=VMEM_LIMIT_BYTES)


def _dot(a, b):
    return jnp.dot(a, b, preferred_element_type=f32)


def _dot_nt(a, b):
    return lax.dot_general(a, b, (((1,), (1,)), ((), ())), preferred_element_type=f32)


def _split(x):
    hi = x.astype(bf16)
    lo = (x - hi.astype(f32)).astype(bf16)
    return hi, lo


def _dot3(a, b):
    ah, al = _split(a)
    bh, bl = _split(b)
    return _dot(ah, bh) + _dot(al, bh) + _dot(ah, bl)


def _silu(x):
    return x * jax.nn.sigmoid(x)


def _rms(x):
    return x * lax.rsqrt(jnp.mean(x * x, axis=-1, keepdims=True) + EPS)


def _const_spec(shape):
    nd = len(shape)
    return pl.BlockSpec(shape, lambda *_: (0,) * nd)


def _const_spec1(shape):
    nd = len(shape)
    return pl.BlockSpec(shape, lambda *_: (0,) * nd, pipeline_mode=pl.Buffered(1))


ADA_TN = 1536


def _ada_kernel(c_ref, w_ref, b_ref, o_ref):
    x = _silu(c_ref[...])
    o_ref[0] = _dot3(x, w_ref[0]) + b_ref[0]


def ada_modulation(c_rows, ada_w, ada_b):
    n = 6 * D_MODEL
    return pl.pallas_call(
        _ada_kernel,
        grid=(DEPTH, n // ADA_TN),
        in_specs=[
            pl.BlockSpec((MOD_ROWS, D_MODEL), lambda l, j: (0, 0)),
            pl.BlockSpec((1, D_MODEL, ADA_TN), lambda l, j: (l, 0, j)),
            pl.BlockSpec((1, 1, ADA_TN), lambda l, j: (l, 0, j)),
        ],
        out_specs=pl.BlockSpec((1, MOD_ROWS, ADA_TN), lambda l, j: (l, 0, j)),
        out_shape=jax.ShapeDtypeStruct((DEPTH, MOD_ROWS, n), f32),
        compiler_params=_cparams("arbitrary", "arbitrary"),
        name="ada_modulation",
    )(c_rows, ada_w, ada_b.reshape(DEPTH, 1, n))


PROJ_TM = 512


def _mod_row(i, tm):
    return jnp.minimum((i * tm) // SEQ, CTX_MOD_ROW)


def _tile4(t):
    return jnp.concatenate([t, t, t, t], axis=1)


def _group_norm64(y, gmat):
    ms = _dot((y * y).astype(bf16), gmat)
    return y * lax.rsqrt(ms + EPS)


def _rope(y, tab, shift):
    w = y.shape[1]
    return y * tab[0] + pltpu.roll(y, shift, 1) * tab[1] + pltpu.roll(y, w - shift, 1) * tab[2]


def _stacked_specs(tm, width):
    n_lat = T_LAT // tm
    return [pl.BlockSpec((tm, width), lambda i: (jnp.minimum(i, n_lat - 1), 0)),
            pl.BlockSpec((tm, width), lambda i: (jnp.maximum(i - n_lat, 0), 0))]


def _stacked_tile(i, tm, lat_ref, ctx_ref):
    return jnp.where(i < T_LAT // tm, lat_ref[...], ctx_ref[...])


def _inproj_kernel(layer_kind, *refs):
    i = pl.program_id(0)
    if layer_kind == "even":
        x_ref, c_ref, mod_ref, w_ref, gain_ref, gmat_ref, ax_ref, o_ref = refs
        h = _stacked_tile(i, PROJ_TM, x_ref, c_ref)
    else:
        h_ref, mod_ref, w_ref, gain_ref, gmat_ref, ax_ref, r1_ref, o_ref = refs
        h = h_ref[...]
    r = _mod_row(i, PROJ_TM)
    sh = mod_ref[pl.ds(r, 1), pl.ds(0, D_MODEL)]
    sc = mod_ref[pl.ds(r, 1), pl.ds(D_MODEL, D_MODEL)]
    u = (_rms(h) * (1.0 + sc) + sh).astype(bf16)

    def seg(j):
        return _dot(u, w_ref[:, j * SEG:(j + 1) * SEG])

    def put(j, y):
        o_ref[:, j * SEG:(j + 1) * SEG] = y.astype(bf16)

    def gain(j):
        return gain_ref[:, j * SEG:(j + 1) * SEG]

    gmat = gmat_ref[...]
    ax = ax_ref[...]
    ax4 = (_tile4(ax[0]), _tile4(ax[1]), _tile4(ax[2]))
    if layer_kind == "even":
        for j in (3, 4):
            put(j, _rope(_group_norm64(seg(j), gmat) * gain(j), ax4, DA_HD // 4))
        for j in (0, 1, 2, 5):
            put(j, seg(j))
    else:
        r1 = r1_ref[...]
        r14 = (_tile4(r1[0]), _tile4(r1[1]), _tile4(r1[2]))
        put(4, _rope(_group_norm64(seg(4), gmat) * gain(4), ax4, GQ_HD // 4))
        y = seg(5)
        kw = GQ_KV * GQ_HD
        yk = _rope(_group_norm64(y[:, :kw], gmat[:kw, :kw]) * gain(5)[:, :kw], ax, GQ_HD // 4)
        yv = y[:, kw:2 * kw]
        pieces = (yk, yv, pltpu.roll(yk, GQ_HD, 1), pltpu.roll(yv, GQ_HD, 1))
        for p, piece in enumerate(pieces):
            o_ref[:, 5 * SEG + p * kw:5 * SEG + (p + 1) * kw] = piece.astype(bf16)
        for j in (0, 1):
            put(j, _rope(seg(j) * gain(j), r14, RET_DK // 2))
        for j in (2, 3):
            put(j, seg(j))


def in_projection(layer_kind, hs, mod, w, gain, gmat, tables):
    n_lat_tiles = T_LAT // PROJ_TM
    n_pos_tiles = SEQ // PROJ_TM

    def tab_map(i):
        return (0, jnp.where(i < n_lat_tiles, i % n_pos_tiles, n_pos_tiles), 0)

    tab_specs = [pl.BlockSpec((3, PROJ_TM, LANES), tab_map) for _ in tables]
    if layer_kind == "even":
        h_specs = _stacked_specs(PROJ_TM, D_MODEL)
    else:
        h_specs = [pl.BlockSpec((PROJ_TM, D_MODEL), lambda i: (i, 0))]
    return pl.pallas_call(
        functools.partial(_inproj_kernel, layer_kind),
        grid=(T_ALL // PROJ_TM,),
        in_specs=h_specs + [
            _const_spec((MOD_ROWS, 6 * D_MODEL)),
            _const_spec((D_MODEL, PROJ_W)),
            _const_spec((1, PROJ_W)),
            _const_spec((SEG, SEG)),
        ] + tab_specs,
        out_specs=pl.BlockSpec((PROJ_TM, PROJ_W), lambda i: (i, 0)),
        out_shape=jax.ShapeDtypeStruct((T_ALL, PROJ_W), bf16),
        compiler_params=_cparams("parallel"),
        name="in_projection_" + layer_kind,
    )(*hs, mod, w, gain, gmat, *tables)


HY_TC = 256
HY_FREQ_CHUNK = 512
FEAT_PAD = 64


def _alt_sign(shape, axis):
    idx = lax.broadcasted_iota(jnp.int32, shape, axis)
    return jnp.where((idx & 1) == 0, 1.0, -1.0).astype(f32)


def _filter_kernel(L, z_ref, w1_ref, b1_ref, w2_ref, b2_ref, wf_ref, wb_ref, freq_ref, dec_ref, c_ref, s_ref,
                   spec_ref, nyq_ref):
    hid = jnp.sin(freq_ref[0:1, :] * (_dot3(z_ref[...], w1_ref[...]) + b1_ref[...]))
    hid = jnp.sin(freq_ref[1:2, :] * (_dot3(hid, w2_ref[...]) + b2_ref[...]))
    dec = dec_ref[...]
    fwd = _dot3(hid, wf_ref[0]) * dec
    bwd = _dot3(hid, wb_ref[0]) * dec
    row = lax.broadcasted_iota(jnp.int32, fwd.shape, 0)
    bwd = jnp.where(row == 0, 0.0, bwd)
    even = fwd + bwd
    odd = bwd - fwd
    wk = jnp.where(row == 0, 0.5 / L, 1.0 / L).astype(f32)
    spec_ref[0, 0] = _dot(c_ref[...], even.astype(bf16)) * wk
    spec_ref[0, 1] = _dot(s_ref[...], odd.astype(bf16)) * wk
    nyq = jnp.sum(even * _alt_sign(even.shape, 0), axis=0, keepdims=True) * (0.5 / L)
    nyq_ref[0] = jnp.broadcast_to(nyq, (8, nyq.shape[1]))


def hyena_filter_spectra(L, zfeat, w1, b1, w2, b2, w3r, freq, decay, cmat, smat):
    nct = HY_CH // HY_TC
    return pl.pallas_call(
        functools.partial(_filter_kernel, L),
        grid=(2, nct),
        in_specs=[
            _const_spec((L, FEAT_PAD)),
            _const_spec((FEAT_PAD, HY_FILT_HID)),
            _const_spec((1, HY_FILT_HID)),
            _const_spec((HY_FILT_HID, HY_FILT_HID)),
            _const_spec((1, HY_FILT_HID)),
            pl.BlockSpec((1, HY_FILT_HID, HY_TC), lambda n, c: (2 * n, 0, c)),
            pl.BlockSpec((1, HY_FILT_HID, HY_TC), lambda n, c: (2 * n + 1, 0, c)),
            _const_spec((2, HY_FILT_HID)),
            pl.BlockSpec((L, HY_TC), lambda n, c: (0, c)),
            _const_spec1((L, L)),
            _const_spec1((L, L)),
        ],
        out_specs=[
            pl.BlockSpec((1, 2, L, HY_TC), lambda n, c: (n, 0, 0, c)),
            pl.BlockSpec((1, 8, HY_TC), lambda n, c: (n, 0, c)),
        ],
        out_shape=[
            jax.ShapeDtypeStruct((2, 2, L, HY_CH), f32),
            jax.ShapeDtypeStruct((2, 8, HY_CH), f32),
        ],
        compiler_params=_cparams("arbitrary", "arbitrary"),
        name="hyena_filter_L%d" % L,
    )(zfeat, w1, b1, w2, b2, w3r, w3r, freq, decay, cmat, smat)


def _conv3(u, w, b):
    L = u.shape[0]
    row = lax.broadcasted_iota(jnp.int32, u.shape, 0)
    prev = jnp.where(row == 0, 0.0, pltpu.roll(u, 1, 0))
    nxt = jnp.where(row == L - 1, 0.0, pltpu.roll(u, L - 1, 0))
    return prev * w[0:1, :] + u * w[1:2, :] + nxt * w[2:3, :] + b


def _hyena_kernel(v_ref, x1_ref, x2_ref, wv_ref, w1_ref, w2_ref, bv_ref, b1_ref, b2_ref, spec_ref, nyq_ref,
                  bias_ref, c_ref, s_ref, o_ref, yr_ref, yi_ref):
    L = v_ref.shape[0]
    fch = min(L, HY_FREQ_CHUNK)
    z = _conv3(v_ref[...].astype(f32), wv_ref[...], bv_ref[...])
    gate_refs = ((x1_ref, w1_ref, b1_ref), (x2_ref, w2_ref, b2_ref))
    alt = _alt_sign(z.shape, 0)
    for n in range(2):
        zb = z.astype(bf16)
        for k in range(L // fch):
            rows = slice(k * fch, (k + 1) * fch)
            a = _dot(c_ref[rows, :], zb)
            b = _dot(s_ref[rows, :], zb)
            hr = spec_ref[n, 0, rows, :]
            hi = spec_ref[n, 1, rows, :]
            yr_ref[rows, :] = (a * hr + b * hi).astype(bf16)
            yi_ref[rows, :] = (a * hi - b * hr).astype(bf16)
        x_nyq = jnp.sum(z * alt, axis=0, keepdims=True)
        y = (_dot(c_ref[...], yr_ref[...]) - _dot(s_ref[...], yi_ref[...])
             + alt * (x_nyq * nyq_ref[n, 0:1, :]))
        x_ref, w_ref, b_ref = gate_refs[n]
        gate = _conv3(x_ref[...].astype(f32), w_ref[...], b_ref[...])
        z = gate * (y + z * bias_ref[n:n + 1, :])
    o_ref[...] = z.astype(bf16)


def hyena_mix(L, row_block0, proj, conv_w, conv_b, spec, nyq, bias, cmat, smat):
    nct = HY_CH // HY_TC
    nseg = HY_CH // HY_TC

    def col(k):
        return lambda c, b: (row_block0 + b, k * nseg + c)

    def par(k):
        return lambda c, b: (0, k * nseg + c)

    in_specs = (
        [pl.BlockSpec((L, HY_TC), col(k)) for k in range(3)]
        + [pl.BlockSpec((3, HY_TC), par(k)) for k in range(3)]
        + [pl.BlockSpec((1, HY_TC), par(k)) for k in range(3)]
        + [
            pl.BlockSpec((2, 2, L, HY_TC), lambda c, b: (0, 0, 0, c), pipeline_mode=pl.Buffered(1)),
            pl.BlockSpec((2, 8, HY_TC), lambda c, b: (0, 0, c)),
            pl.BlockSpec((2, HY_TC), lambda c, b: (0, c)),
            _const_spec1((L, L)),
            _const_spec1((L, L)),
        ]
    )
    args = [proj, proj, proj, conv_w, conv_w, conv_w, conv_b, conv_b, conv_b, spec, nyq, bias, cmat, smat]
    return pl.pallas_call(
        _hyena_kernel,
        grid=(nct, BATCH),
        in_specs=in_specs,
        out_specs=pl.BlockSpec((L, HY_TC), lambda c, b: (b, c)),
        out_shape=jax.ShapeDtypeStruct((BATCH * L, HY_CH), bf16),
        scratch_shapes=[pltpu.VMEM((L, HY_TC), bf16), pltpu.VMEM((L, HY_TC), bf16)],
        compiler_params=_cparams("arbitrary", "arbitrary"),
        name="hyena_mix_L%d" % L,
    )(*args)


DA_TQ = 512


def _diff_attn_kernel(lam_init, q_ref, qc_ref, kc_ref, vc_ref, kl_ref, vl_ref, lam_ref, subln_ref, o_ref, oc_ref):
    i = pl.program_id(1)
    n_lat_blocks = SEQ // DA_TQ

    @pl.when(i < n_lat_blocks)
    def _():
        _diff_attn_body(lam_init, q_ref, (kc_ref, vc_ref, kl_ref, vl_ref), lam_ref, subln_ref, o_ref)

    @pl.when(i == n_lat_blocks)
    def _():
        _diff_attn_body(lam_init, qc_ref, (kc_ref, vc_ref), lam_ref, subln_ref, oc_ref)


def _diff_attn_body(lam_init, q_ref, kv_refs, lam_ref, subln_ref, o_ref):
    n_src = len(kv_refs) // 2
    lp = lam_ref[...]
    lam = (jnp.exp(jnp.sum(lp[0:1] * lp[1:2], axis=-1, keepdims=True))
           - jnp.exp(jnp.sum(lp[2:3] * lp[3:4], axis=-1, keepdims=True)) + lam_init)
    q = q_ref[...]
    tq = q.shape[0]
    lower = lax.broadcasted_iota(jnp.int32, (tq, 2 * DA_HD), 1) < DA_HD
    zero = jnp.zeros((), bf16)
    hw = 2 * DA_HD
    outs = []
    for h in range(DA_HEADS):
        qh = q[:, h * hw:(h + 1) * hw]
        ks = [kv_refs[2 * s][:, h * hw:(h + 1) * hw] for s in range(n_src)]
        vs = [kv_refs[2 * s + 1][:, h * hw:(h + 1) * hw] for s in range(n_src)]
        qs = jnp.concatenate([jnp.where(lower, qh, zero), jnp.where(lower, zero, qh)], axis=0)
        ss = [_dot_nt(qs, k) for k in ks]
        mx = functools.reduce(jnp.maximum, [jnp.max(s, axis=-1, keepdims=True) for s in ss])
        es = [jnp.exp2(s - mx) for s in ss]
        den = functools.reduce(jnp.add, [jnp.sum(e, axis=-1, keepdims=True) for e in es])
        pv = functools.reduce(jnp.add, [_dot(es[s].astype(bf16), vs[s]) for s in range(n_src)])
        pv = pv * (1.0 / den)
        oh = pv[:tq] - lam * pv[tq:]
        outs.append(_rms(oh) * subln_ref[...] * (1.0 - lam_init))
    o_ref[...] = jnp.concatenate(outs, axis=1).astype(bf16)


def diff_attention(proj, lam_p, subln, lam_init):
    width = DA_HEADS * 2 * DA_HD
    qcol, kcol, vcol = 3, 4, 5
    ctx_blk0 = T_LAT // CTX_LEN
    nq = SEQ // DA_TQ

    def lat_rows(b, i):
        return b * nq + jnp.minimum(i, nq - 1)

    return pl.pallas_call(
        functools.partial(_diff_attn_kernel, lam_init),
        grid=(BATCH, nq + 1),
        in_specs=[
            pl.BlockSpec((DA_TQ, width), lambda b, i: (lat_rows(b, i), qcol)),
            pl.BlockSpec((CTX_LEN, width), lambda b, i: (ctx_blk0 + b, qcol)),
            pl.BlockSpec((CTX_LEN, width), lambda b, i: (ctx_blk0 + b, kcol)),
            pl.BlockSpec((CTX_LEN, width), lambda b, i: (ctx_blk0 + b, vcol)),
            pl.BlockSpec((SEQ, width), lambda b, i: (b, kcol)),
            pl.BlockSpec((SEQ, width), lambda b, i: (b, vcol)),
            _const_spec((4, DA_HD)),
            _const_spec((1, 2 * DA_HD)),
        ],
        out_specs=[
            pl.BlockSpec((DA_TQ, width), lambda b, i: (lat_rows(b, i), 0)),
            pl.BlockSpec((CTX_LEN, width), lambda b, i: (b, 0)),
        ],
        out_shape=[
            jax.ShapeDtypeStruct((T_LAT, width), bf16),
            jax.ShapeDtypeStruct((T_CTX, width), bf16),
        ],
        compiler_params=_cparams("parallel", "arbitrary"),
        name="diff_attention",
    )(proj, proj, proj, proj, proj, proj, lam_p, subln)


def _log_sigmoid(x):
    return jnp.minimum(x, 0.0) - jnp.log(1.0 + jnp.exp(-jnp.abs(x)))


def _retention_kernel(q_ref, k_ref, v_ref, g_ref, kc_ref, vc_ref, decay_ref, gn_ref, o_ref, st_ref):
    h = pl.program_id(1)
    ch = RET_CHUNK
    nchunk = SEQ // ch
    lgs = _log_sigmoid(decay_ref[...])
    sel = lax.broadcasted_iota(jnp.int32, lgs.shape, 1) == h
    lg = jnp.sum(jnp.where(sel, lgs, 0.0), axis=-1, keepdims=True)
    lgf = lg[0:1, :]
    lgb = lg[1:2, :]
    ri = lax.broadcasted_iota(jnp.int32, (ch, ch), 0).astype(f32)
    ci = lax.broadcasted_iota(jnp.int32, (ch, ch), 1).astype(f32)
    rel = ri - ci
    dsum = (jnp.where(rel >= 0, jnp.exp(jnp.maximum(rel, 0.0) * lgf), 0.0)
            + jnp.where(rel <= 0, jnp.exp(jnp.maximum(-rel, 0.0) * lgb), 0.0))
    zeta_f = jnp.exp((ch - 1 - ci) * lgf)
    zeta_b = jnp.exp(ci * lgb)
    xi_f = jnp.exp((ri + 1.0) * lgf)
    xi_b = jnp.exp((ch - ri) * lgb)
    gch_f = jnp.exp(ch * lgf)
    gch_b = jnp.exp(ch * lgb)
    dk = q_ref.shape[1]

    kct = kc_ref[...].astype(f32).T
    vc = vc_ref[...]
    cl = lax.broadcasted_iota(jnp.int32, kct.shape, 1).astype(f32)
    s_f = _dot((kct * jnp.exp((CTX_LEN - 1 - cl) * lgf)).astype(bf16), vc)
    s_b = _dot((kct * jnp.exp(cl * lgb)).astype(bf16), vc)

    def rows(n):
        return slice(n * ch, (n + 1) * ch)

    u_f, u_b = [], []
    for n in range(nchunk):
        kt = k_ref[rows(n), :].astype(f32).T
        vn = v_ref[rows(n), :]
        u_f.append(_dot((kt * zeta_f).astype(bf16), vn))
        u_b.append(_dot((kt * zeta_b).astype(bf16), vn))

    for n in range(nchunk):
        st_ref[n, 0:dk, :] = s_f.astype(bf16)
        s_f = gch_f * s_f + u_f[n]
    for n in reversed(range(nchunk)):
        st_ref[n, dk:2 * dk, :] = s_b.astype(bf16)
        s_b = gch_b * s_b + u_b[n]

    gn = gn_ref[...]
    for n in range(nchunk):
        qn = q_ref[rows(n), :]
        att = _dot_nt(qn, k_ref[rows(n), :]) * dsum
        qf = qn.astype(f32)
        lhs = jnp.concatenate([att.astype(bf16), (qf * xi_f).astype(bf16), (qf * xi_b).astype(bf16)], axis=1)
        rhs = jnp.concatenate([v_ref[rows(n), :], st_ref[n]], axis=0)
        o = _dot(lhs, rhs)
        mu = jnp.mean(o, axis=-1, keepdims=True)
        oc = o - mu
        var = jnp.mean(oc * oc, axis=-1, keepdims=True)
        y = oc * lax.rsqrt(var + EPS) * gn * _silu(g_ref[rows(n), :].astype(f32))
        o_ref[rows(n), :] = y.astype(bf16)


def retention(proj, decay, gn_w):
    dk = RET_DK
    ctx_blk0 = T_LAT // CTX_LEN
    return pl.pallas_call(
        _retention_kernel,
        grid=(BATCH, RET_HEADS),
        in_specs=[
            pl.BlockSpec((SEQ, dk), lambda b, h: (b, h)),
            pl.BlockSpec((SEQ, dk), lambda b, h: (b, RET_HEADS + h)),
            pl.BlockSpec((SEQ, dk), lambda b, h: (b, 2 * RET_HEADS + h)),
            pl.BlockSpec((SEQ, dk), lambda b, h: (b, 3 * RET_HEADS + h)),
            pl.BlockSpec((CTX_LEN, dk), lambda b, h: (ctx_blk0 + b, RET_HEADS + h)),
            pl.BlockSpec((CTX_LEN, dk), lambda b, h: (ctx_blk0 + b, 2 * RET_HEADS + h)),
            _const_spec((2, RET_HEADS)),
            pl.BlockSpec((1, dk), lambda b, h: (0, h)),
        ],
        out_specs=pl.BlockSpec((SEQ, dk), lambda b, h: (b, h)),
        out_shape=jax.ShapeDtypeStruct((T_LAT, RET_HEADS * dk), bf16),
        scratch_shapes=[pltpu.VMEM((SEQ // RET_CHUNK, 2 * dk, dk), bf16)],
        compiler_params=_cparams("parallel", "arbitrary"),
        name="retention",
    )(proj, proj, proj, proj, proj, proj, decay, gn_w)


GQ_TQ = 128
GQ_SPAN = 3 * GQ_TQ


def _gqa_kernel(q_ref, kv_ref, kvc_ref, sink_ref, o_ref):
    n = pl.program_id(1)
    start = pl.multiple_of(jnp.clip((n - 1) * GQ_TQ, 0, SEQ - GQ_SPAN), GQ_TQ)
    pw = 2 * GQ_HD
    n_heads = GQ_KV * GQ_GROUP
    n_keys = CTX_LEN + GQ_SPAN
    col = lax.broadcasted_iota(jnp.int32, (GQ_TQ, n_keys), 1)
    qpos = n * GQ_TQ + lax.broadcasted_iota(jnp.int32, (GQ_TQ, n_keys), 0)
    dist = jnp.where(col < CTX_LEN, 0, jnp.abs(start + col - CTX_LEN - qpos))
    mask = dist <= WINDOW
    q = q_ref[...]
    lower = lax.broadcasted_iota(jnp.int32, (GQ_TQ, pw), 1) < GQ_HD
    outs = [None] * n_heads
    for swapped in (0, 1):
        kcol = slice(2 * swapped * pw, (2 * swapped + 1) * pw)
        vcol = slice((2 * swapped + 1) * pw, (2 * swapped + 2) * pw)
        k = jnp.concatenate([kvc_ref[:, kcol], kv_ref[pl.ds(start, GQ_SPAN), kcol]], axis=0)
        v = jnp.concatenate([kvc_ref[:, vcol], kv_ref[pl.ds(start, GQ_SPAN), vcol]], axis=0)
        heads = [h for h in range(n_heads) if ((h // GQ_GROUP) == (h % 2)) == (swapped == 0)]
        qs = jnp.concatenate(
            [jnp.where(lower == (h % 2 == 0), q[:, (h // 2) * pw:(h // 2 + 1) * pw], jnp.zeros((), bf16))
             for h in heads], axis=0)
        s = _dot_nt(qs, k)
        ps = []
        for i, h in enumerate(heads):
            sh = jnp.where(mask, s[i * GQ_TQ:(i + 1) * GQ_TQ], NEG_INF)
            sink = sink_ref[h]
            mx = jnp.maximum(jnp.max(sh, axis=-1, keepdims=True), sink)
            e = jnp.exp(sh - mx)
            inv = 1.0 / (jnp.sum(e, axis=-1, keepdims=True) + jnp.exp(sink - mx))
            ps.append((e * inv).astype(bf16))
        o = _dot(jnp.concatenate(ps, axis=0), v)
        for i, h in enumerate(heads):
            outs[h] = o[i * GQ_TQ:(i + 1) * GQ_TQ]
    o_ref[...] = jnp.concatenate(
        [jnp.where(lower, outs[2 * j], outs[2 * j + 1]) for j in range(n_heads // 2)], axis=1).astype(bf16)


def window_gqa(proj, sink):
    width = GQ_KV * GQ_GROUP * GQ_HD
    nq = SEQ // GQ_TQ
    kvw = 4 * GQ_KV * GQ_HD
    kv_col = (5 * SEG) // kvw
    ctx_blk0 = T_LAT // CTX_LEN
    return pl.pallas_call(
        _gqa_kernel,
        grid=(BATCH, nq),
        in_specs=[
            pl.BlockSpec((GQ_TQ, width), lambda b, n: (b * nq + n, 4)),
            pl.BlockSpec((SEQ, kvw), lambda b, n: (b, kv_col)),
            pl.BlockSpec((CTX_LEN, kvw), lambda b, n: (ctx_blk0 + b, kv_col)),
            pl.BlockSpec(memory_space=pltpu.SMEM),
        ],
        out_specs=pl.BlockSpec((GQ_TQ, width), lambda b, n: (b * nq + n, 0)),
        out_shape=jax.ShapeDtypeStruct((T_LAT, width), bf16),
        compiler_params=_cparams("parallel", "arbitrary"),
        name="window_gqa",
    )(proj, proj, proj, sink)


OUT_TM = 512
ROUTE_W = LANES
MOE_TB = 256


def _route(logits):
    lane_i = lax.broadcasted_iota(jnp.int32, logits.shape, 1)
    lane = lane_i.astype(f32)
    big = float(1 << 20)
    valid = lane_i < N_EXPERTS
    le = logits
    lgx = pltpu.roll(logits, ROUTE_W - N_EXPERTS, 1)
    lgx = jnp.where(valid, lgx, NEG_INF)
    gmax = jnp.max(lgx, axis=-1, keepdims=True)
    grp = (lane_i // EXP_PER_GROUP).astype(f32)
    g_sel = jnp.min(jnp.where(lgx == gmax, grp, big), axis=-1, keepdims=True)
    p_grp = float(EXP_PER_GROUP) / jnp.sum(jnp.exp(lgx - gmax), axis=-1, keepdims=True)
    lm = jnp.where(valid, jnp.where(grp == g_sel, le, NEG_INF), NEG_INF)
    v1 = jnp.max(lm, axis=-1, keepdims=True)
    i1 = jnp.min(jnp.where(lm == v1, lane, big), axis=-1, keepdims=True)
    lm2 = jnp.where(lane == i1, NEG_INF, lm)
    v2 = jnp.max(lm2, axis=-1, keepdims=True)
    i2 = jnp.min(jnp.where(lm2 == v2, lane, big), axis=-1, keepdims=True)
    e2 = jnp.exp(v2 - v1)
    w1 = p_grp / (1.0 + e2)
    w2 = w1 * e2
    return jnp.where(lane == i1, w1, 0.0) + jnp.where(lane == i2, w2, 0.0)


def _outproj_kernel(stacked, *refs):
    i = pl.program_id(0)
    if stacked:
        (ya_ref, yac_ref, yb_ref, ybc_ref, w_ref, x_ref, c_ref, mod_ref, wr_ref, br_ref,
         hn_ref, v_ref, comb_ref, cnt_ref) = refs
        ya = _stacked_tile(i, OUT_TM, ya_ref, yac_ref)
        yb = _stacked_tile(i, OUT_TM, yb_ref, ybc_ref)
        h = _stacked_tile(i, OUT_TM, x_ref, c_ref)
    else:
        ya_ref, yb_ref, w_ref, h_ref, mod_ref, wr_ref, br_ref, hn_ref, v_ref, comb_ref, cnt_ref = refs
        ya = ya_ref[...]
        yb = yb_ref[...]
        h = h_ref[...]
    r = _mod_row(i, OUT_TM)
    g1 = mod_ref[pl.ds(r, 1), pl.ds(2 * D_MODEL, D_MODEL)]
    sh2 = mod_ref[pl.ds(r, 1), pl.ds(3 * D_MODEL, D_MODEL)]
    sc2 = mod_ref[pl.ds(r, 1), pl.ds(4 * D_MODEL, D_MODEL)]
    half = ya.shape[1]
    m = _dot(ya, w_ref[0:half, :]) + _dot(yb, w_ref[half:2 * half, :])
    hn = h + g1 * m
    hn_ref[...] = hn
    v = _rms(hn) * (1.0 + sc2) + sh2
    v_ref[...] = v.astype(bf16)
    vh, vl = _split(v)
    prod = _dot(jnp.concatenate([vh, vl], axis=0), wr_ref[...])
    tm = v.shape[0]
    comb = _route(prod[:tm, :ROUTE_W] + prod[:tm, ROUTE_W:] + prod[tm:, :ROUTE_W] + br_ref[...])
    comb_ref[...] = comb
    for s in range(OUT_TM // MOE_TB):
        cnt = jnp.sum((comb[s * MOE_TB:(s + 1) * MOE_TB] != 0.0).astype(f32), axis=0, keepdims=True)
        cnt_ref[s] = jnp.broadcast_to(cnt, (8, ROUTE_W)).astype(jnp.int32)


def out_projection(n_rows, yas, ybs, w, hs, mod, wr, br):
    stacked = len(yas) == 2
    assert stacked == (len(hs) == 2) == (len(ybs) == 2) and (not stacked or n_rows == T_ALL)
    half = ybs[0].shape[1]
    row_spec = lambda width: [pl.BlockSpec((OUT_TM, width), lambda i: (i, 0))]
    rows = lambda width: _stacked_specs(OUT_TM, width) if stacked else row_spec(width)
    wr_hi = wr.astype(bf16)
    wr_lo = (wr - wr_hi.astype(f32)).astype(bf16)
    return pl.pallas_call(
        functools.partial(_outproj_kernel, stacked),
        grid=(n_rows // OUT_TM,),
        in_specs=(
            rows(half) + rows(half)
            + [_const_spec((2 * half, D_MODEL))]
            + rows(D_MODEL)
            + [_const_spec((MOD_ROWS, 6 * D_MODEL)),
               _const_spec((D_MODEL, 2 * ROUTE_W)),
               _const_spec((1, ROUTE_W))]
        ),
        out_specs=[
            pl.BlockSpec((OUT_TM, D_MODEL), lambda i: (i, 0)),
            pl.BlockSpec((OUT_TM, D_MODEL), lambda i: (i, 0)),
            pl.BlockSpec((OUT_TM, ROUTE_W), lambda i: (i, 0)),
            pl.BlockSpec((OUT_TM // MOE_TB, 8, ROUTE_W), lambda i: (i, 0, 0)),
        ],
        out_shape=[
            jax.ShapeDtypeStruct((n_rows, D_MODEL), f32),
            jax.ShapeDtypeStruct((n_rows, D_MODEL), bf16),
            jax.ShapeDtypeStruct((n_rows, ROUTE_W), f32),
            jax.ShapeDtypeStruct((n_rows // MOE_TB, 8, ROUTE_W), jnp.int32),
        ],
        compiler_params=_cparams("parallel"),
        name="out_projection",
    )(*yas, *ybs, w, *hs, mod, jnp.concatenate([wr_hi, wr_lo], axis=1), br)


MOE_UNIT = 16
MOE_TG = 512
MOE_TOP = 2
MOE_RLOC = MOE_TOP * MOE_TB + N_EXPERTS * MOE_UNIT
MOE_NUNIT = MOE_RLOC // MOE_UNIT
MOE_META = 128


def _moe_rows(n_blk):
    return n_blk * MOE_RLOC + N_EXPERTS * MOE_TG


MOE_PLAN_ROWS = 128
MOE_PLAN_TILES = 256


def _lane_pick(x, lane, k):
    return jnp.sum(jnp.where(lane == k, x, 0.0), axis=1, keepdims=True)


def _moe_plan_kernel(n_blk, cnt_ref, ltri_ref, utri_ref, tabd_ref, tabc_ref, te_ref, meta_ref):
    shape = (MOE_PLAN_ROWS, ROUTE_W)
    lane = lax.broadcasted_iota(jnp.int32, shape, 1)
    cnt = cnt_ref[...].astype(f32)
    units = jnp.floor((cnt + (MOE_UNIT - 1.0)) * (1.0 / MOE_UNIT))
    ub = units.astype(bf16)
    utri = utri_ref[...]
    pre = _dot(ltri_ref[...], ub)
    lstart = _dot(ub, utri)
    n_e = jnp.sum(units, axis=0, keepdims=True)
    upt = MOE_TG // MOE_UNIT
    tiles_e = jnp.floor((n_e + (upt - 1.0)) * (1.0 / upt))
    goff = _dot(jnp.broadcast_to(tiles_e, (8, ROUTE_W)).astype(bf16), utri)[0:1, :]
    a = goff * upt + pre - lstart
    n_used = jnp.sum(units, axis=1, keepdims=True)
    j = lane.astype(f32)
    acc = jnp.zeros(shape, f32)
    for e in range(N_EXPERTS):
        ls = _lane_pick(lstart, lane, e)
        u = _lane_pick(units, lane, e)
        acc = acc + jnp.where(j >= ls, jnp.where(j < ls + u, _lane_pick(a, lane, e), 0.0), 0.0)
    rows = (acc + j) * MOE_UNIT
    used = j < n_used
    parity = (lax.broadcasted_iota(jnp.int32, shape, 0) & 1).astype(f32)
    spare = _moe_rows(n_blk) + parity * MOE_RLOC + j * MOE_UNIT
    tabd_ref[...] = jnp.where(used, rows, spare).astype(jnp.int32)
    tabc_ref[...] = jnp.where(used, rows, _lane_pick(rows, lane, 0)).astype(jnp.int32)

    ends = goff + tiles_e
    lane1 = lax.broadcasted_iota(jnp.int32, (1, ROUTE_W), 1)
    ti = lax.broadcasted_iota(jnp.int32, (8, MOE_PLAN_TILES), 1).astype(f32)
    te = jnp.zeros((8, MOE_PLAN_TILES), f32)
    for e in range(N_EXPERTS):
        te = te + jnp.where(ti >= _lane_pick(ends, lane1, e), 1.0, 0.0)
    te_ref[...] = jnp.minimum(te, N_EXPERTS - 1.0).astype(jnp.int32)

    n_tiles = jnp.sum(tiles_e, axis=1, keepdims=True)
    gap_start = jnp.broadcast_to((goff * upt + n_e) * MOE_UNIT, (8, ROUTE_W))
    gap_units = jnp.broadcast_to(tiles_e * upt - n_e, (8, ROUTE_W))
    lane8 = lax.broadcasted_iota(jnp.int32, (8, ROUTE_W), 1)
    meta = jnp.where(lane8 == 0, n_tiles,
                     jnp.where(lane8 <= N_EXPERTS, pltpu.roll(gap_start, 1, 1),
                               jnp.where(lane8 <= 2 * N_EXPERTS, pltpu.roll(gap_units, 1 + N_EXPERTS, 1), 0.0)))
    meta_ref[...] = meta.astype(jnp.int32)


def moe_plan(n_blk, counts, ltri, utri):
    assert n_blk <= MOE_PLAN_ROWS and _moe_rows(n_blk) // MOE_TG <= MOE_PLAN_TILES
    shape = (MOE_PLAN_ROWS, ROUTE_W)
    tabd, tabc, te, meta = pl.pallas_call(
        functools.partial(_moe_plan_kernel, n_blk),
        out_shape=[
            jax.ShapeDtypeStruct(shape, jnp.int32),
            jax.ShapeDtypeStruct(shape, jnp.int32),
            jax.ShapeDtypeStruct((8, MOE_PLAN_TILES), jnp.int32),
            jax.ShapeDtypeStruct((8, ROUTE_W), jnp.int32),
        ],
        name="moe_plan",
    )(counts, ltri, utri)
    return tabd.reshape(-1), tabc.reshape(-1), te[0], meta[0]


def _block_routes(comb, ltri, utri):
    oh = comb != 0.0
    ohf = jnp.where(oh, 1.0, 0.0)
    rank = _dot(ltri, ohf.astype(bf16))
    cnt = jnp.sum(ohf, axis=0, keepdims=True)
    units = jnp.floor((cnt + (MOE_UNIT - 1.0)) * (1.0 / MOE_UNIT))
    seg = _dot(jnp.broadcast_to(units, (8, ROUTE_W)).astype(bf16), utri)[0:1, :] * MOE_UNIT
    dest = seg + rank
    big = float(1 << 20)
    d_a = jnp.min(jnp.where(oh, dest, big), axis=-1, keepdims=True)
    d_b = jnp.max(jnp.where(oh, dest, -1.0), axis=-1, keepdims=True)
    w_a = jnp.sum(jnp.where(oh, jnp.where(dest == d_a, comb, 0.0), 0.0), axis=-1, keepdims=True)
    w_b = jnp.sum(jnp.where(oh, jnp.where(dest == d_b, comb, 0.0), 0.0), axis=-1, keepdims=True)
    second = d_b != d_a
    return d_a, jnp.where(second, d_b, -1.0), w_a, jnp.where(second, w_b, 0.0)


def _one_hot_rows(d):
    r = lax.broadcasted_iota(jnp.int32, (d.shape[0], MOE_RLOC), 1).astype(f32)
    return jnp.where(r == d, 1.0, 0.0).astype(bf16)


def _unit_rows(ref, tab_ref, t, j):
    return ref.at[pl.ds(pl.multiple_of(tab_ref[t * ROUTE_W + j], MOE_UNIT), MOE_UNIT)]


def _wait_all_units(local, remote, sem):
    pltpu.make_async_copy(local, remote.at[pl.ds(0, MOE_RLOC)], sem).wait()


def _gap_copies(meta_ref, zero_ref, remote, sem, wait):
    def per_expert(e, c):
        start = meta_ref[1 + e]

        def per_unit(u, c2):
            ro = remote.at[pl.ds(pl.multiple_of(start + u * MOE_UNIT, MOE_UNIT), MOE_UNIT)]
            cp = pltpu.make_async_copy(zero_ref, ro, sem)
            if wait:
                cp.wait()
            else:
                cp.start()
            return c2

        lax.fori_loop(0, meta_ref[1 + N_EXPERTS + e], per_unit, 0)
        return c

    lax.fori_loop(0, N_EXPERTS, per_expert, 0)


def _moe_dispatch_kernel(n_blk, tab_ref, meta_ref, x_ref, comb_ref, ltri_ref, utri_ref, xs_ref,
                         buf_ref, zero_ref, sem_ref):
    t = pl.program_id(0)
    slot = t % 2
    d_a, d_b, _, _ = _block_routes(comb_ref[...], ltri_ref[...], utri_ref[...])
    p = _one_hot_rows(d_a) + _one_hot_rows(d_b)
    rows = lax.dot_general(p, x_ref[...], (((0,), (0,)), ((), ())), preferred_element_type=f32)
    buf_ref[slot] = rows.astype(bf16)
    for j in range(MOE_NUNIT):
        pltpu.make_async_copy(buf_ref.at[slot, pl.ds(j * MOE_UNIT, MOE_UNIT)], _unit_rows(xs_ref, tab_ref, t, j),
                              sem_ref.at[slot]).start()

    @pl.when(t > 0)
    def _():
        _wait_all_units(buf_ref.at[1 - slot], xs_ref, sem_ref.at[1 - slot])

    @pl.when(t == n_blk - 1)
    def _():
        zero_ref[...] = jnp.zeros_like(zero_ref)
        _gap_copies(meta_ref, zero_ref, xs_ref, sem_ref.at[2], False)
        _wait_all_units(buf_ref.at[slot], xs_ref, sem_ref.at[slot])
        _gap_copies(meta_ref, zero_ref, xs_ref, sem_ref.at[2], True)


def moe_dispatch(n_blk, tab, meta, x, comb, ltri, utri):
    return pl.pallas_call(
        functools.partial(_moe_dispatch_kernel, n_blk),
        grid_spec=pltpu.PrefetchScalarGridSpec(
            num_scalar_prefetch=2,
            grid=(n_blk,),
            in_specs=[
                pl.BlockSpec((MOE_TB, D_MODEL), lambda t, *_: (t, 0)),
                pl.BlockSpec((MOE_TB, ROUTE_W), lambda t, *_: (t, 0)),
                pl.BlockSpec((MOE_TB, MOE_TB), lambda t, *_: (0, 0)),
                pl.BlockSpec((ROUTE_W, ROUTE_W), lambda t, *_: (0, 0)),
            ],
            out_specs=pl.BlockSpec(memory_space=pl.ANY),
            scratch_shapes=[
                pltpu.VMEM((2, MOE_RLOC, D_MODEL), bf16),
                pltpu.VMEM((MOE_UNIT, D_MODEL), bf16),
                pltpu.SemaphoreType.DMA((3,)),
            ],
        ),
        out_shape=jax.ShapeDtypeStruct((_moe_rows(n_blk) + 2 * MOE_RLOC, D_MODEL), bf16),
        compiler_params=_cparams("arbitrary"),
        name="moe_dispatch",
    )(tab, meta, x, comb, ltri, utri)


def _moe_expert_kernel(tile_exp_ref, meta_ref, xs_ref, wg_ref, wu_ref, wd_ref, ys_ref, wgub_ref, wdb_ref):
    i = pl.program_id(0)

    @pl.when(i < meta_ref[0])
    def _():
        prev = tile_exp_ref[jnp.maximum(i - 1, 0)]

        @pl.when(jnp.logical_or(i == 0, tile_exp_ref[i] != prev))
        def _():
            wgub_ref[:, :D_EXPERT] = wg_ref[0].astype(bf16)
            wgub_ref[:, D_EXPERT:] = wu_ref[0].astype(bf16)
            wdb_ref[...] = wd_ref[0].astype(bf16)

        gu = _dot(xs_ref[...], wgub_ref[...])
        a = _silu(gu[:, :D_EXPERT]) * gu[:, D_EXPERT:]
        ys_ref[...] = _dot(a.astype(bf16), wdb_ref[...]).astype(bf16)


def moe_experts(n_blk, layer, tile_exp, meta, xs, w_gate, w_up, w_down):
    n_tiles = _moe_rows(n_blk) // MOE_TG

    def row_map(i, te, meta):
        return (jnp.minimum(i, meta[0] - 1), 0)

    def w_map(i, te, meta):
        return (layer * N_EXPERTS + te[jnp.minimum(i, meta[0] - 1)], 0, 0)

    return pl.pallas_call(
        _moe_expert_kernel,
        grid_spec=pltpu.PrefetchScalarGridSpec(
            num_scalar_prefetch=2,
            grid=(n_tiles,),
            in_specs=[
                pl.BlockSpec((MOE_TG, D_MODEL), row_map),
                pl.BlockSpec((1, D_MODEL, D_EXPERT), w_map),
                pl.BlockSpec((1, D_MODEL, D_EXPERT), w_map),
                pl.BlockSpec((1, D_EXPERT, D_MODEL), w_map),
            ],
            out_specs=pl.BlockSpec((MOE_TG, D_MODEL), row_map),
            scratch_shapes=[
                pltpu.VMEM((D_MODEL, 2 * D_EXPERT), bf16),
                pltpu.VMEM((D_EXPERT, D_MODEL), bf16),
            ],
        ),
        out_shape=jax.ShapeDtypeStruct((_moe_rows(n_blk), D_MODEL), bf16),
        compiler_params=_cparams("arbitrary"),
        name="moe_experts",
    )(tile_exp, meta, xs, w_gate, w_up, w_down)


def _moe_combine_kernel(n_blk, tab_ref, ys_ref, comb_ref, h_ref, mod_ref, ltri_ref, utri_ref,
                        o_ref, buf_ref, sem_ref):
    t = pl.program_id(0)
    slot = t % 2

    def gather(tt, s):
        for j in range(MOE_NUNIT):
            pltpu.make_async_copy(_unit_rows(ys_ref, tab_ref, tt, j), buf_ref.at[s, pl.ds(j * MOE_UNIT, MOE_UNIT)],
                                  sem_ref.at[s]).start()

    @pl.when(t == 0)
    def _():
        gather(0, 0)

    @pl.when(t + 1 < n_blk)
    def _():
        gather(t + 1, 1 - slot)

    _wait_all_units(buf_ref.at[slot], ys_ref, sem_ref.at[slot])
    d_a, d_b, w_a, w_b = _block_routes(comb_ref[...], ltri_ref[...], utri_ref[...])
    p = jnp.concatenate([_one_hot_rows(d_a), _one_hot_rows(d_b)], axis=0)
    picked = _dot(p, buf_ref[slot])
    m = w_a * picked[:MOE_TB] + w_b * picked[MOE_TB:]
    r = _mod_row(t, MOE_TB)
    g2 = mod_ref[pl.ds(r, 1), pl.ds(5 * D_MODEL, D_MODEL)]
    o_ref[...] = h_ref[...] + g2 * m


def moe_combine(n_blk, tab, ys, comb, h, mod, ltri, utri):
    return pl.pallas_call(
        functools.partial(_moe_combine_kernel, n_blk),
        grid_spec=pltpu.PrefetchScalarGridSpec(
            num_scalar_prefetch=1,
            grid=(n_blk,),
            in_specs=[
                pl.BlockSpec(memory_space=pl.ANY),
                pl.BlockSpec((MOE_TB, ROUTE_W), lambda t, *_: (t, 0)),
                pl.BlockSpec((MOE_TB, D_MODEL), lambda t, *_: (t, 0)),
                pl.BlockSpec((MOD_ROWS, 6 * D_MODEL), lambda t, *_: (0, 0)),
                pl.BlockSpec((MOE_TB, MOE_TB), lambda t, *_: (0, 0)),
                pl.BlockSpec((ROUTE_W, ROUTE_W), lambda t, *_: (0, 0)),
            ],
            out_specs=pl.BlockSpec((MOE_TB, D_MODEL), lambda t, *_: (t, 0)),
            scratch_shapes=[
                pltpu.VMEM((2, MOE_RLOC, D_MODEL), bf16),
                pltpu.SemaphoreType.DMA((2,)),
            ],
        ),
        out_shape=jax.ShapeDtypeStruct((n_blk * MOE_TB, D_MODEL), f32),
        compiler_params=_cparams("arbitrary"),
        name="moe_combine",
    )(tab, ys, comb, h, mod, ltri, utri)


def sparse_moe(n_rows, layer, v, comb, counts, h, mod, w_gate, w_up, w_down, ltri, ltri_plan, utri):
    n_blk = n_rows // MOE_TB
    cnt = jnp.pad(counts[:, 0, :], ((0, MOE_PLAN_ROWS - n_blk), (0, 0)))
    tab_d, tab_c, tile_exp, meta = moe_plan(n_blk, cnt, ltri_plan, utri)
    xs = moe_dispatch(n_blk, tab_d, meta, v, comb, ltri, utri)
    ys = moe_experts(n_blk, layer, tile_exp, meta, xs,
                     w_gate.reshape(DEPTH * N_EXPERTS, D_MODEL, D_EXPERT),
                     w_up.reshape(DEPTH * N_EXPERTS, D_MODEL, D_EXPERT),
                     w_down.reshape(DEPTH * N_EXPERTS, D_EXPERT, D_MODEL))
    return moe_combine(n_blk, tab_c, ys, comb, h, mod, ltri, utri)


def _dft_tables(L):
    k = np.arange(L, dtype=np.int64)
    ang = (2.0 * np.pi / (2 * L)) * ((k[:, None] * k[None, :]) % (2 * L)).astype(np.float64)
    return np.cos(ang).astype(np.float32), np.sin(ang).astype(np.float32)


def _filter_features(L):
    bands = (HY_EMB - 1) // 2
    t = np.linspace(0.0, 1.0, L, dtype=np.float32).astype(np.float64)[:, None]
    w = (2.0 * np.pi / L) * np.arange(L, dtype=np.float64)[:, None]
    fb = np.linspace(1e-4, bands - 1, bands, dtype=np.float32).astype(np.float64)[None, :]
    z = np.concatenate([t, np.cos(fb * w), -np.sin(fb * w)], axis=-1)
    zp = np.zeros((L, FEAT_PAD), np.float32)
    zp[:, :HY_EMB] = z
    deltas = np.abs(np.linspace(HY_MIN_DECAY, HY_MAX_DECAY, HY_CH, dtype=np.float32).astype(np.float64))
    decay = np.exp(-t * deltas[None, :]).astype(np.float32)
    return zp, decay


def _rope_table(cos, sin, half, tm):
    S, width = cos.shape
    low = (np.arange(width) % (2 * half)) < half
    tab = np.zeros((3, S + tm, width), np.float32)
    tab[0, :S] = cos
    tab[0, S:] = 1.0
    tab[1, :S] = np.where(low[None, :], 0.0, sin)
    tab[2, :S] = np.where(low[None, :], -sin, 0.0)
    return tab


def _axial_rope_table(head_dim, tm):
    rows = SEQ // GRID_W
    nf = head_dim // 4
    row = np.repeat(np.arange(rows), GRID_W).astype(np.float64)
    col = np.tile(np.arange(GRID_W), rows).astype(np.float64)
    inv = ROPE_BASE ** (-np.arange(nf, dtype=np.float64) / nf)
    ang = np.stack([row[:, None] * inv, col[:, None] * inv], axis=1)
    a = np.broadcast_to(ang[:, :, None, :], (SEQ, 2, 2, nf)).reshape(SEQ, head_dim)
    reps = LANES // head_dim
    a = np.tile(a, (1, reps))
    return _rope_table(np.cos(a), np.sin(a), nf, tm)


def _seq_rope_table(head_dim, tm):
    inv = 1.0 / (ROPE_BASE ** np.linspace(0.0, 1.0, head_dim // 2, dtype=np.float32).astype(np.float64))
    ang = np.arange(SEQ, dtype=np.float64)[:, None] * inv
    a = np.concatenate([ang, ang], axis=1)
    return _rope_table(np.cos(a), np.sin(a), head_dim // 2, tm)


def _group_mean_matrix():
    g = np.arange(SEG) // DA_HD
    return (g[:, None] == g[None, :]).astype(np.float32) / DA_HD


def _router_weights(w_grp, b_grp, w_rt, b_rt):
    pad = ROUTE_W - 2 * N_EXPERTS
    wr = jnp.concatenate([w_rt, jnp.repeat(w_grp, EXP_PER_GROUP, axis=1),
                          jnp.zeros((D_MODEL, pad), f32)], axis=1)
    br = jnp.concatenate([b_rt, jnp.repeat(b_grp, EXP_PER_GROUP), jnp.zeros((pad,), f32)])[None, :]
    return wr, br


def _strict_lower(n):
    i = np.arange(n)
    return (i[None, :] < i[:, None]).astype(np.float32)


def kernel(x, c, ctx, c_ctx, ada_w, ada_b, e_w_in, e_w_out, hy_conv_w, hy_conv_b, hy_f_w1, hy_f_b1, hy_f_w2, hy_f_b2, hy_f_w3, hy_f_freq, hy_bias, da_q_norm, da_k_norm, da_lam, da_subln, o_w_in, o_w_out, ret_decay, ret_gn, gq_q_norm, gq_k_norm, gq_sink, moe_w_grp, moe_b_grp, moe_w_rt, moe_b_rt, moe_w_gate, moe_w_up, moe_w_down):
    assert x.shape == (BATCH, SEQ, D_MODEL) and ctx.shape == (BATCH, CTX_LEN, D_MODEL)
    x_rows = x.reshape(T_LAT, D_MODEL)
    ctx_rows = ctx.reshape(T_CTX, D_MODEL)
    c_rows = jnp.concatenate([c, c_ctx[None, :], jnp.zeros((MOD_ROWS - BATCH - 1, D_MODEL), f32)], axis=0)
    mod = ada_modulation(c_rows, ada_w, ada_b)

    gmat = jnp.asarray(_group_mean_matrix()).astype(bf16)
    ax_tab = jnp.asarray(_axial_rope_table(DA_HD, PROJ_TM))
    r1_tab = jnp.asarray(_seq_rope_table(RET_DK, PROJ_TM))
    ones = jnp.ones((SEG,), f32)

    lam_init0 = 0.8 - 0.6 * math.exp(-0.3 * 0)
    reps = SEG // DA_HD
    gain0 = jnp.concatenate([ones, ones, ones, jnp.tile(da_q_norm[0], reps) * (DA_HD ** -0.5 * LOG2E),
                             jnp.tile(da_k_norm[0], reps), ones])[None, :]
    proj0 = in_projection("even", [x_rows, ctx_rows], mod[0], e_w_in[0].astype(bf16), gain0, gmat, [ax_tab])

    w3r = hy_f_w3[0].reshape(HY_FILT_HID, 4, HY_CH).transpose(1, 0, 2)
    w1p = jnp.concatenate([hy_f_w1[0], jnp.zeros((FEAT_PAD - HY_EMB, HY_FILT_HID), f32)], axis=0)
    y_hy = []
    for L, blk0 in ((SEQ, 0), (CTX_LEN, T_LAT // CTX_LEN)):
        zfeat, decay = _filter_features(L)
        cm, sm = _dft_tables(L)
        cm = jnp.asarray(cm).astype(bf16)
        sm = jnp.asarray(sm).astype(bf16)
        spec, nyq = hyena_filter_spectra(L, jnp.asarray(zfeat), w1p, hy_f_b1[0][None, :], hy_f_w2[0],
                                         hy_f_b2[0][None, :], w3r, hy_f_freq[0], jnp.asarray(decay), cm, sm)
        y_hy.append(hyena_mix(L, blk0, proj0, hy_conv_w[0], hy_conv_b[0][None, :], spec, nyq, hy_bias[0], cm, sm))

    y_da = diff_attention(proj0, da_lam[0], da_subln[0][None, :], lam_init0)

    wr0, br0 = _router_weights(moe_w_grp[0], moe_b_grp[0], moe_w_rt[0], moe_b_rt[0])
    ltri = jnp.asarray(_strict_lower(MOE_TB)).astype(bf16)
    ltri_plan = jnp.asarray(_strict_lower(MOE_PLAN_ROWS)).astype(bf16)
    utri = jnp.asarray(_strict_lower(ROUTE_W).T).astype(bf16)
    h, v, comb, counts = out_projection(T_ALL, y_hy, y_da, e_w_out[0].astype(bf16), [x_rows, ctx_rows], mod[0],
                                        wr0, br0)
    h = sparse_moe(T_ALL, 0, v, comb, counts, h, mod[0], moe_w_gate, moe_w_up, moe_w_down, ltri, ltri_plan, utri)

    w_in1 = jnp.concatenate([o_w_in[0], jnp.zeros((D_MODEL, PROJ_W - o_w_in.shape[2]), f32)], axis=1).astype(bf16)
    kq = GQ_KV * GQ_HD
    gain1 = jnp.concatenate([ones, ones * RET_DK ** -0.5, ones, ones,
                             jnp.tile(gq_q_norm[0], reps) * GQ_HD ** -0.5,
                             jnp.tile(gq_k_norm[0], kq // GQ_HD), jnp.ones((SEG - kq,), f32)])[None, :]
    proj1 = in_projection("odd", [h], mod[1], w_in1, gain1, gmat, [ax_tab, r1_tab])
    y_ret = retention(proj1, ret_decay[0], ret_gn[0][None, :])
    y_gq = window_gqa(proj1, gq_sink[0])
    wr1, br1 = _router_weights(moe_w_grp[1], moe_b_grp[1], moe_w_rt[1], moe_b_rt[1])
    h_lat, v, comb, counts = out_projection(T_LAT, [y_ret], [y_gq], o_w_out[0].astype(bf16), [h], mod[1], wr1, br1)
    out = sparse_moe(T_LAT, 1, v, comb, counts, h_lat, mod[1], moe_w_gate, moe_w_up, moe_w_down, ltri, ltri_plan,
                     utri)
    return out.reshape(BATCH, SEQ, D_MODEL)
```

```python
import functools
import math

import numpy as np
import jax
import jax.numpy as jnp
from jax import lax
from jax.experimental import pallas as pl
from jax.experimental.pallas import tpu as pltpu

f32 = jnp.float32
bf16 = jnp.bfloat16

D_MODEL = 1024
BATCH = 8
SEQ = 2048
DEPTH = 2
GRID_W = 64
CTX_LEN = 256
EPS = 1e-6
NEG_INF = -1e30
LOG2E = math.log2(math.e)
ROPE_BASE = 10000.0
HY_CH = D_MODEL // 2
HY_EMB = 33
HY_FILT_HID = 64
HY_MAX_DECAY = math.log(1e-2) / 0.3
HY_MIN_DECAY = math.log(1e-2) / 1.5
DA_HEADS = 4
DA_HD = D_MODEL // 16
RET_HEADS = 4
RET_DK = D_MODEL // 8
RET_CHUNK = 128
GQ_KV = 2
GQ_GROUP = 4
GQ_HD = D_MODEL // 16
WINDOW = 128
N_GROUPS = 4
EXP_PER_GROUP = 8
N_EXPERTS = N_GROUPS * EXP_PER_GROUP
D_EXPERT = D_MODEL // 4

T_LAT = BATCH * SEQ
T_CTX = BATCH * CTX_LEN
T_ALL = T_LAT + T_CTX
PROJ_W = 3072
SEG = 512
CTX_MOD_ROW = BATCH
MOD_ROWS = 16

LANES = 128
VMEM_LIMIT_BYTES = 56 * 1024 * 1024


def _cparams(*sem):
    return pltpu.CompilerParams(dimension_semantics=sem, vmem_limit_bytes=VMEM_LIMIT_BYTES)


def _dot(a, b):
    return jnp.dot(a, b, preferred_element_type=f32)


def _dot_nt(a, b):
    return lax.dot_general(a, b, (((1,), (1,)), ((), ())), preferred_element_type=f32)


def _split(x):
    hi = x.astype(bf16)
    lo = (x - hi.astype(f32)).astype(bf16)
    return hi, lo


def _dot3(a, b):
    ah, al = _split(a)
    bh, bl = _split(b)
    return _dot(ah, bh) + _dot(al, bh) + _dot(ah, bl)


def _silu(x):
    return x * jax.nn.sigmoid(x)


def _rms(x):
    return x * lax.rsqrt(jnp.mean(x * x, axis=-1, keepdims=True) + EPS)


def _const_spec(shape):
    nd = len(shape)
    return pl.BlockSpec(shape, lambda *_: (0,) * nd)


def _const_spec1(shape):
    nd = len(shape)
    return pl.BlockSpec(shape, lambda *_: (0,) * nd, pipeline_mode=pl.Buffered(1))


ADA_TN = 1536


def _ada_kernel(c_ref, w_ref, b_ref, o_ref):
    x = _silu(c_ref[...])
    o_ref[0] = _dot3(x, w_ref[0]) + b_ref[0]


def ada_modulation(c_rows, ada_w, ada_b):
    n = 6 * D_MODEL
    return pl.pallas_call(
        _ada_kernel,
        grid=(DEPTH, n // ADA_TN),
        in_specs=[
            pl.BlockSpec((MOD_ROWS, D_MODEL), lambda l, j: (0, 0)),
            pl.BlockSpec((1, D_MODEL, ADA_TN), lambda l, j: (l, 0, j)),
            pl.BlockSpec((1, 1, ADA_TN), lambda l, j: (l, 0, j)),
        ],
        out_specs=pl.BlockSpec((1, MOD_ROWS, ADA_TN), lambda l, j: (l, 0, j)),
        out_shape=jax.ShapeDtypeStruct((DEPTH, MOD_ROWS, n), f32),
        compiler_params=_cparams("arbitrary", "arbitrary"),
        name="ada_modulation",
    )(c_rows, ada_w, ada_b.reshape(DEPTH, 1, n))


PROJ_TM = 512


def _mod_row(i, tm):
    return jnp.minimum((i * tm) // SEQ, CTX_MOD_ROW)


def _tile4(t):
    return jnp.concatenate([t, t, t, t], axis=1)


def _group_norm64(y, gmat):
    ms = _dot((y * y).astype(bf16), gmat)
    return y * lax.rsqrt(ms + EPS)


def _rope(y, tab, shift):
    w = y.shape[1]
    return y * tab[0] + pltpu.roll(y, shift, 1) * tab[1] + pltpu.roll(y, w - shift, 1) * tab[2]


def _stacked_specs(tm, width):
    n_lat = T_LAT // tm
    return [pl.BlockSpec((tm, width), lambda i: (jnp.minimum(i, n_lat - 1), 0)),
            pl.BlockSpec((tm, width), lambda i: (jnp.maximum(i - n_lat, 0), 0))]


def _stacked_tile(i, tm, lat_ref, ctx_ref):
    return jnp.where(i < T_LAT // tm, lat_ref[...], ctx_ref[...])


def _inproj_kernel(layer_kind, *refs):
    i = pl.program_id(0)
    if layer_kind == "even":
        x_ref, c_ref, mod_ref, w_ref, gain_ref, gmat_ref, ax_ref, o_ref = refs
        h = _stacked_tile(i, PROJ_TM, x_ref, c_ref)
    else:
        h_ref, mod_ref, w_ref, gain_ref, gmat_ref, ax_ref, r1_ref, o_ref = refs
        h = h_ref[...]
    r = _mod_row(i, PROJ_TM)
    sh = mod_ref[pl.ds(r, 1), pl.ds(0, D_MODEL)]
    sc = mod_ref[pl.ds(r, 1), pl.ds(D_MODEL, D_MODEL)]
    u = (_rms(h) * (1.0 + sc) + sh).astype(bf16)

    def seg(j):
        return _dot(u, w_ref[:, j * SEG:(j + 1) * SEG])

    def put(j, y):
        o_ref[:, j * SEG:(j + 1) * SEG] = y.astype(bf16)

    def gain(j):
        return gain_ref[:, j * SEG:(j + 1) * SEG]

    gmat = gmat_ref[...]
    ax = ax_ref[...]
    ax4 = (_tile4(ax[0]), _tile4(ax[1]), _tile4(ax[2]))
    if layer_kind == "even":
        for j in (3, 4):
            put(j, _rope(_group_norm64(seg(j), gmat) * gain(j), ax4, DA_HD // 4))
        for j in (0, 1, 2, 5):
            put(j, seg(j))
    else:
        r1 = r1_ref[...]
        r14 = (_tile4(r1[0]), _tile4(r1[1]), _tile4(r1[2]))
        put(4, _rope(_group_norm64(seg(4), gmat) * gain(4), ax4, GQ_HD // 4))
        y = seg(5)
        kw = GQ_KV * GQ_HD
        yk = _rope(_group_norm64(y[:, :kw], gmat[:kw, :kw]) * gain(5)[:, :kw], ax, GQ_HD // 4)
        yv = y[:, kw:2 * kw]
        pieces = (yk, yv, pltpu.roll(yk, GQ_HD, 1), pltpu.roll(yv, GQ_HD, 1))
        for p, piece in enumerate(pieces):
            o_ref[:, 5 * SEG + p * kw:5 * SEG + (p + 1) * kw] = piece.astype(bf16)
        for j in (0, 1):
            put(j, _rope(seg(j) * gain(j), r14, RET_DK // 2))
        for j in (2, 3):
            put(j, seg(j))


def in_projection(layer_kind, hs, mod, w, gain, gmat, tables):
    n_lat_tiles = T_LAT // PROJ_TM
    n_pos_tiles = SEQ // PROJ_TM

    def tab_map(i):
        return (0, jnp.where(i < n_lat_tiles, i % n_pos_tiles, n_pos_tiles), 0)

    tab_specs = [pl.BlockSpec((3, PROJ_TM, LANES), tab_map) for _ in tables]
    if layer_kind == "even":
        h_specs = _stacked_specs(PROJ_TM, D_MODEL)
    else:
        h_specs = [pl.BlockSpec((PROJ_TM, D_MODEL), lambda i: (i, 0))]
    return pl.pallas_call(
        functools.partial(_inproj_kernel, layer_kind),
        grid=(T_ALL // PROJ_TM,),
        in_specs=h_specs + [
            _const_spec((MOD_ROWS, 6 * D_MODEL)),
            _const_spec((D_MODEL, PROJ_W)),
            _const_spec((1, PROJ_W)),
            _const_spec((SEG, SEG)),
        ] + tab_specs,
        out_specs=pl.BlockSpec((PROJ_TM, PROJ_W), lambda i: (i, 0)),
        out_shape=jax.ShapeDtypeStruct((T_ALL, PROJ_W), bf16),
        compiler_params=_cparams("parallel"),
        name="in_projection_" + layer_kind,
    )(*hs, mod, w, gain, gmat, *tables)


HY_TC = 256
HY_FREQ_CHUNK = 512
FEAT_PAD = 64


def _alt_sign(shape, axis):
    idx = lax.broadcasted_iota(jnp.int32, shape, axis)
    return jnp.where((idx & 1) == 0, 1.0, -1.0).astype(f32)


def _filter_kernel(L, z_ref, w1_ref, b1_ref, w2_ref, b2_ref, wf_ref, wb_ref, freq_ref, dec_ref, c_ref, s_ref,
                   spec_ref, nyq_ref):
    hid = jnp.sin(freq_ref[0:1, :] * (_dot3(z_ref[...], w1_ref[...]) + b1_ref[...]))
    hid = jnp.sin(freq_ref[1:2, :] * (_dot3(hid, w2_ref[...]) + b2_ref[...]))
    dec = dec_ref[...]
    fwd = _dot3(hid, wf_ref[0]) * dec
    bwd = _dot3(hid, wb_ref[0]) * dec
    row = lax.broadcasted_iota(jnp.int32, fwd.shape, 0)
    bwd = jnp.where(row == 0, 0.0, bwd)
    even = fwd + bwd
    odd = bwd - fwd
    wk = jnp.where(row == 0, 0.5 / L, 1.0 / L).astype(f32)
    spec_ref[0, 0] = _dot(c_ref[...], even.astype(bf16)) * wk
    spec_ref[0, 1] = _dot(s_ref[...], odd.astype(bf16)) * wk
    nyq = jnp.sum(even * _alt_sign(even.shape, 0), axis=0, keepdims=True) * (0.5 / L)
    nyq_ref[0] = jnp.broadcast_to(nyq, (8, nyq.shape[1]))


def hyena_filter_spectra(L, zfeat, w1, b1, w2, b2, w3r, freq, decay, cmat, smat):
    nct = HY_CH // HY_TC
    return pl.pallas_call(
        functools.partial(_filter_kernel, L),
        grid=(2, nct),
        in_specs=[
            _const_spec((L, FEAT_PAD)),
            _const_spec((FEAT_PAD, HY_FILT_HID)),
            _const_spec((1, HY_FILT_HID)),
            _const_spec((HY_FILT_HID, HY_FILT_HID)),
            _const_spec((1, HY_FILT_HID)),
            pl.BlockSpec((1, HY_FILT_HID, HY_TC), lambda n, c: (2 * n, 0, c)),
            pl.BlockSpec((1, HY_FILT_HID, HY_TC), lambda n, c: (2 * n + 1, 0, c)),
            _const_spec((2, HY_FILT_HID)),
            pl.BlockSpec((L, HY_TC), lambda n, c: (0, c)),
            _const_spec1((L, L)),
            _const_spec1((L, L)),
        ],
        out_specs=[
            pl.BlockSpec((1, 2, L, HY_TC), lambda n, c: (n, 0, 0, c)),
            pl.BlockSpec((1, 8, HY_TC), lambda n, c: (n, 0, c)),
        ],
        out_shape=[
            jax.ShapeDtypeStruct((2, 2, L, HY_CH), f32),
            jax.ShapeDtypeStruct((2, 8, HY_CH), f32),
        ],
        compiler_params=_cparams("arbitrary", "arbitrary"),
        name="hyena_filter_L%d" % L,
    )(zfeat, w1, b1, w2, b2, w3r, w3r, freq, decay, cmat, smat)


def _conv3(u, w, b):
    L = u.shape[0]
    row = lax.broadcasted_iota(jnp.int32, u.shape, 0)
    prev = jnp.where(row == 0, 0.0, pltpu.roll(u, 1, 0))
    nxt = jnp.where(row == L - 1, 0.0, pltpu.roll(u, L - 1, 0))
    return prev * w[0:1, :] + u * w[1:2, :] + nxt * w[2:3, :] + b


def _hyena_kernel(v_ref, x1_ref, x2_ref, wv_ref, w1_ref, w2_ref, bv_ref, b1_ref, b2_ref, spec_ref, nyq_ref,
                  bias_ref, c_ref, s_ref, o_ref, yr_ref, yi_ref):
    L = v_ref.shape[0]
    fch = min(L, HY_FREQ_CHUNK)
    z = _conv3(v_ref[...].astype(f32), wv_ref[...], bv_ref[...])
    gate_refs = ((x1_ref, w1_ref, b1_ref), (x2_ref, w2_ref, b2_ref))
    alt = _alt_sign(z.shape, 0)
    for n in range(2):
        zb = z.astype(bf16)
        for k in range(L // fch):
            rows = slice(k * fch, (k + 1) * fch)
            a = _dot(c_ref[rows, :], zb)
            b = _dot(s_ref[rows, :], zb)
            hr = spec_ref[n, 0, rows, :]
            hi = spec_ref[n, 1, rows, :]
            yr_ref[rows, :] = (a * hr + b * hi).astype(bf16)
            yi_ref[rows, :] = (a * hi - b * hr).astype(bf16)
        x_nyq = jnp.sum(z * alt, axis=0, keepdims=True)
        y = (_dot(c_ref[...], yr_ref[...]) - _dot(s_ref[...], yi_ref[...])
             + alt * (x_nyq * nyq_ref[n, 0:1, :]))
        x_ref, w_ref, b_ref = gate_refs[n]
        gate = _conv3(x_ref[...].astype(f32), w_ref[...], b_ref[...])
        z = gate * (y + z * bias_ref[n:n + 1, :])
    o_ref[...] = z.astype(bf16)


def hyena_mix(L, row_block0, proj, conv_w, conv_b, spec, nyq, bias, cmat, smat):
    nct = HY_CH // HY_TC
    nseg = HY_CH // HY_TC

    def col(k):
        return lambda c, b: (row_block0 + b, k * nseg + c)

    def par(k):
        return lambda c, b: (0, k * nseg + c)

    in_specs = (
        [pl.BlockSpec((L, HY_TC), col(k)) for k in range(3)]
        + [pl.BlockSpec((3, HY_TC), par(k)) for k in range(3)]
        + [pl.BlockSpec((1, HY_TC), par(k)) for k in range(3)]
        + [
            pl.BlockSpec((2, 2, L, HY_TC), lambda c, b: (0, 0, 0, c), pipeline_mode=pl.Buffered(1)),
            pl.BlockSpec((2, 8, HY_TC), lambda c, b: (0, 0, c)),
            pl.BlockSpec((2, HY_TC), lambda c, b: (0, c)),
            _const_spec1((L, L)),
            _const_spec1((L, L)),
        ]
    )
    args = [proj, proj, proj, conv_w, conv_w, conv_w, conv_b, conv_b, conv_b, spec, nyq, bias, cmat, smat]
    return pl.pallas_call(
        _hyena_kernel,
        grid=(nct, BATCH),
        in_specs=in_specs,
        out_specs=pl.BlockSpec((L, HY_TC), lambda c, b: (b, c)),
        out_shape=jax.ShapeDtypeStruct((BATCH * L, HY_CH), bf16),
        scratch_shapes=[pltpu.VMEM((L, HY_TC), bf16), pltpu.VMEM((L, HY_TC), bf16)],
        compiler_params=_cparams("arbitrary", "arbitrary"),
        name="hyena_mix_L%d" % L,
    )(*args)


DA_TQ = 512


def _diff_attn_kernel(lam_init, q_ref, qc_ref, kc_ref, vc_ref, kl_ref, vl_ref, lam_ref, subln_ref, o_ref, oc_ref):
    i = pl.program_id(1)
    n_lat_blocks = SEQ // DA_TQ

    @pl.when(i < n_lat_blocks)
    def _():
        _diff_attn_body(lam_init, q_ref, (kc_ref, vc_ref, kl_ref, vl_ref), lam_ref, subln_ref, o_ref)

    @pl.when(i == n_lat_blocks)
    def _():
        _diff_attn_body(lam_init, qc_ref, (kc_ref, vc_ref), lam_ref, subln_ref, oc_ref)


def _diff_attn_body(lam_init, q_ref, kv_refs, lam_ref, subln_ref, o_ref):
    n_src = len(kv_refs) // 2
    lp = lam_ref[...]
    lam = (jnp.exp(jnp.sum(lp[0:1] * lp[1:2], axis=-1, keepdims=True))
           - jnp.exp(jnp.sum(lp[2:3] * lp[3:4], axis=-1, keepdims=True)) + lam_init)
    q = q_ref[...]
    tq = q.shape[0]
    lower = lax.broadcasted_iota(jnp.int32, (tq, 2 * DA_HD), 1) < DA_HD
    zero = jnp.zeros((), bf16)
    hw = 2 * DA_HD
    outs = []
    for h in range(DA_HEADS):
        qh = q[:, h * hw:(h + 1) * hw]
        ks = [kv_refs[2 * s][:, h * hw:(h + 1) * hw] for s in range(n_src)]
        vs = [kv_refs[2 * s + 1][:, h * hw:(h + 1) * hw] for s in range(n_src)]
        qs = jnp.concatenate([jnp.where(lower, qh, zero), jnp.where(lower, zero, qh)], axis=0)
        ss = [_dot_nt(qs, k) for k in ks]
        mx = functools.reduce(jnp.maximum, [jnp.max(s, axis=-1, keepdims=True) for s in ss])
        es = [jnp.exp2(s - mx) for s in ss]
        den = functools.reduce(jnp.add, [jnp.sum(e, axis=-1, keepdims=True) for e in es])
        pv = functools.reduce(jnp.add, [_dot(es[s].astype(bf16), vs[s]) for s in range(n_src)])
        pv = pv * (1.0 / den)
        oh = pv[:tq] - lam * pv[tq:]
        outs.append(_rms(oh) * subln_ref[...] * (1.0 - lam_init))
    o_ref[...] = jnp.concatenate(outs, axis=1).astype(bf16)


def diff_attention(proj, lam_p, subln, lam_init):
    width = DA_HEADS * 2 * DA_HD
    qcol, kcol, vcol = 3, 4, 5
    ctx_blk0 = T_LAT // CTX_LEN
    nq = SEQ // DA_TQ

    def lat_rows(b, i):
        return b * nq + jnp.minimum(i, nq - 1)

    return pl.pallas_call(
        functools.partial(_diff_attn_kernel, lam_init),
        grid=(BATCH, nq + 1),
        in_specs=[
            pl.BlockSpec((DA_TQ, width), lambda b, i: (lat_rows(b, i), qcol)),
            pl.BlockSpec((CTX_LEN, width), lambda b, i: (ctx_blk0 + b, qcol)),
            pl.BlockSpec((CTX_LEN, width), lambda b, i: (ctx_blk0 + b, kcol)),
            pl.BlockSpec((CTX_LEN, width), lambda b, i: (ctx_blk0 + b, vcol)),
            pl.BlockSpec((SEQ, width), lambda b, i: (b, kcol)),
            pl.BlockSpec((SEQ, width), lambda b, i: (b, vcol)),
            _const_spec((4, DA_HD)),
            _const_spec((1, 2 * DA_HD)),
        ],
        out_specs=[
            pl.BlockSpec((DA_TQ, width), lambda b, i: (lat_rows(b, i), 0)),
            pl.BlockSpec((CTX_LEN, width), lambda b, i: (b, 0)),
        ],
        out_shape=[
            jax.ShapeDtypeStruct((T_LAT, width), bf16),
            jax.ShapeDtypeStruct((T_CTX, width), bf16),
        ],
        compiler_params=_cparams("parallel", "arbitrary"),
        name="diff_attention",
    )(proj, proj, proj, proj, proj, proj, lam_p, subln)


def _log_sigmoid(x):
    return jnp.minimum(x, 0.0) - jnp.log(1.0 + jnp.exp(-jnp.abs(x)))


def _retention_kernel(q_ref, k_ref, v_ref, g_ref, kc_ref, vc_ref, decay_ref, gn_ref, o_ref, st_ref):
    h = pl.program_id(1)
    ch = RET_CHUNK
    nchunk = SEQ // ch
    lgs = _log_sigmoid(decay_ref[...])
    sel = lax.broadcasted_iota(jnp.int32, lgs.shape, 1) == h
    lg = jnp.sum(jnp.where(sel, lgs, 0.0), axis=-1, keepdims=True)
    lgf = lg[0:1, :]
    lgb = lg[1:2, :]
    ri = lax.broadcasted_iota(jnp.int32, (ch, ch), 0).astype(f32)
    ci = lax.broadcasted_iota(jnp.int32, (ch, ch), 1).astype(f32)
    rel = ri - ci
    dsum = (jnp.where(rel >= 0, jnp.exp(jnp.maximum(rel, 0.0) * lgf), 0.0)
            + jnp.where(rel <= 0, jnp.exp(jnp.maximum(-rel, 0.0) * lgb), 0.0))
    zeta_f = jnp.exp((ch - 1 - ci) * lgf)
    zeta_b = jnp.exp(ci * lgb)
    xi_f = jnp.exp((ri + 1.0) * lgf)
    xi_b = jnp.exp((ch - ri) * lgb)
    gch_f = jnp.exp(ch * lgf)
    gch_b = jnp.exp(ch * lgb)
    dk = q_ref.shape[1]

    kct = kc_ref[...].astype(f32).T
    vc = vc_ref[...]
    cl = lax.broadcasted_iota(jnp.int32, kct.shape, 1).astype(f32)
    s_f = _dot((kct * jnp.exp((CTX_LEN - 1 - cl) * lgf)).astype(bf16), vc)
    s_b = _dot((kct * jnp.exp(cl * lgb)).astype(bf16), vc)

    def rows(n):
        return slice(n * ch, (n + 1) * ch)

    u_f, u_b = [], []
    for n in range(nchunk):
        kt = k_ref[rows(n), :].astype(f32).T
        vn = v_ref[rows(n), :]
        u_f.append(_dot((kt * zeta_f).astype(bf16), vn))
        u_b.append(_dot((kt * zeta_b).astype(bf16), vn))

    for n in range(nchunk):
        st_ref[n, 0:dk, :] = s_f.astype(bf16)
        s_f = gch_f * s_f + u_f[n]
    for n in reversed(range(nchunk)):
        st_ref[n, dk:2 * dk, :] = s_b.astype(bf16)
        s_b = gch_b * s_b + u_b[n]

    gn = gn_ref[...]
    for n in range(nchunk):
        qn = q_ref[rows(n), :]
        att = _dot_nt(qn, k_ref[rows(n), :]) * dsum
        qf = qn.astype(f32)
        lhs = jnp.concatenate([att.astype(bf16), (qf * xi_f).astype(bf16), (qf * xi_b).astype(bf16)], axis=1)
        rhs = jnp.concatenate([v_ref[rows(n), :], st_ref[n]], axis=0)
        o = _dot(lhs, rhs)
        mu = jnp.mean(o, axis=-1, keepdims=True)
        oc = o - mu
        var = jnp.mean(oc * oc, axis=-1, keepdims=True)
        y = oc * lax.rsqrt(var + EPS) * gn * _silu(g_ref[rows(n), :].astype(f32))
        o_ref[rows(n), :] = y.astype(bf16)


def retention(proj, decay, gn_w):
    dk = RET_DK
    ctx_blk0 = T_LAT // CTX_LEN
    return pl.pallas_call(
        _retention_kernel,
        grid=(BATCH, RET_HEADS),
        in_specs=[
            pl.BlockSpec((SEQ, dk), lambda b, h: (b, h)),
            pl.BlockSpec((SEQ, dk), lambda b, h: (b, RET_HEADS + h)),
            pl.BlockSpec((SEQ, dk), lambda b, h: (b, 2 * RET_HEADS + h)),
            pl.BlockSpec((SEQ, dk), lambda b, h: (b, 3 * RET_HEADS + h)),
            pl.BlockSpec((CTX_LEN, dk), lambda b, h: (ctx_blk0 + b, RET_HEADS + h)),
            pl.BlockSpec((CTX_LEN, dk), lambda b, h: (ctx_blk0 + b, 2 * RET_HEADS + h)),
            _const_spec((2, RET_HEADS)),
            pl.BlockSpec((1, dk), lambda b, h: (0, h)),
        ],
        out_specs=pl.BlockSpec((SEQ, dk), lambda b, h: (b, h)),
        out_shape=jax.ShapeDtypeStruct((T_LAT, RET_HEADS * dk), bf16),
        scratch_shapes=[pltpu.VMEM((SEQ // RET_CHUNK, 2 * dk, dk), bf16)],
        compiler_params=_cparams("parallel", "arbitrary"),
        name="retention",
    )(proj, proj, proj, proj, proj, proj, decay, gn_w)


GQ_TQ = 128
GQ_SPAN = 3 * GQ_TQ
GQ_NB = 4


def _gqa_kernel(q_ref, kv_ref, kvc_ref, sink_ref, o_ref):
    for b in range(GQ_NB):
        rows = slice(b * GQ_TQ, (b + 1) * GQ_TQ)
        o_ref[rows, :] = _gqa_block(pl.program_id(1) * GQ_NB + b, q_ref[rows, :], kv_ref, kvc_ref, sink_ref)


def _gqa_block(n, q, kv_ref, kvc_ref, sink_ref):
    start = pl.multiple_of(jnp.clip((n - 1) * GQ_TQ, 0, SEQ - GQ_SPAN), GQ_TQ)
    pw = 2 * GQ_HD
    n_heads = GQ_KV * GQ_GROUP
    kpos = start + lax.broadcasted_iota(jnp.int32, (GQ_TQ, GQ_SPAN), 1)
    qpos = n * GQ_TQ + lax.broadcasted_iota(jnp.int32, (GQ_TQ, GQ_SPAN), 0)
    mask = jnp.abs(kpos - qpos) <= WINDOW
    lower = lax.broadcasted_iota(jnp.int32, (GQ_TQ, pw), 1) < GQ_HD
    outs = [None] * n_heads
    for swapped in (0, 1):
        kcol = slice(2 * swapped * pw, (2 * swapped + 1) * pw)
        vcol = slice((2 * swapped + 1) * pw, (2 * swapped + 2) * pw)
        k = jnp.concatenate([kvc_ref[:, kcol], kv_ref[pl.ds(start, GQ_SPAN), kcol]], axis=0)
        v = jnp.concatenate([kvc_ref[:, vcol], kv_ref[pl.ds(start, GQ_SPAN), vcol]], axis=0)
        heads = [h for h in range(n_heads) if ((h // GQ_GROUP) == (h % 2)) == (swapped == 0)]
        qs = jnp.concatenate(
            [jnp.where(lower == (h % 2 == 0), q[:, (h // 2) * pw:(h // 2 + 1) * pw], jnp.zeros((), bf16))
             for h in heads], axis=0)
        s = _dot_nt(qs, k)
        es, invs = [], []
        for i, h in enumerate(heads):
            sh = s[i * GQ_TQ:(i + 1) * GQ_TQ]
            sh = jnp.concatenate([sh[:, :CTX_LEN], jnp.where(mask, sh[:, CTX_LEN:], NEG_INF)], axis=1)
            sink = sink_ref[h] * LOG2E
            mx = jnp.maximum(jnp.max(sh, axis=-1, keepdims=True), sink)
            e = jnp.exp2(sh - mx)
            invs.append(1.0 / (jnp.sum(e, axis=-1, keepdims=True) + jnp.exp2(sink - mx)))
            es.append(e.astype(bf16))
        o = _dot(jnp.concatenate(es, axis=0), v)
        for i, h in enumerate(heads):
            outs[h] = o[i * GQ_TQ:(i + 1) * GQ_TQ] * invs[i]
    return jnp.concatenate(
        [jnp.where(lower, outs[2 * j], outs[2 * j + 1]) for j in range(n_heads // 2)], axis=1).astype(bf16)


def window_gqa(proj, sink):
    width = GQ_KV * GQ_GROUP * GQ_HD
    nq = SEQ // (GQ_NB * GQ_TQ)
    kvw = 4 * GQ_KV * GQ_HD
    kv_col = (5 * SEG) // kvw
    ctx_blk0 = T_LAT // CTX_LEN
    return pl.pallas_call(
        _gqa_kernel,
        grid=(BATCH, nq),
        in_specs=[
            pl.BlockSpec((GQ_NB * GQ_TQ, width), lambda b, n: (b * nq + n, 4)),
            pl.BlockSpec((SEQ, kvw), lambda b, n: (b, kv_col)),
            pl.BlockSpec((CTX_LEN, kvw), lambda b, n: (ctx_blk0 + b, kv_col)),
            pl.BlockSpec(memory_space=pltpu.SMEM),
        ],
        out_specs=pl.BlockSpec((GQ_NB * GQ_TQ, width), lambda b, n: (b * nq + n, 0)),
        out_shape=jax.ShapeDtypeStruct((T_LAT, width), bf16),
        compiler_params=_cparams("parallel", "arbitrary"),
        name="window_gqa",
    )(proj, proj, proj, sink)


OUT_TM = 512
ROUTE_W = LANES
MOE_TB = 256


def _route(logits):
    lane_i = lax.broadcasted_iota(jnp.int32, logits.shape, 1)
    lane = lane_i.astype(f32)
    big = float(1 << 20)
    valid = lane_i < N_EXPERTS
    le = logits
    lgx = pltpu.roll(logits, ROUTE_W - N_EXPERTS, 1)
    lgx = jnp.where(valid, lgx, NEG_INF)
    gmax = jnp.max(lgx, axis=-1, keepdims=True)
    grp = (lane_i // EXP_PER_GROUP).astype(f32)
    g_sel = jnp.min(jnp.where(lgx == gmax, grp, big), axis=-1, keepdims=True)
    p_grp = float(EXP_PER_GROUP) / jnp.sum(jnp.exp(lgx - gmax), axis=-1, keepdims=True)
    lm = jnp.where(valid, jnp.where(grp == g_sel, le, NEG_INF), NEG_INF)
    v1 = jnp.max(lm, axis=-1, keepdims=True)
    i1 = jnp.min(jnp.where(lm == v1, lane, big), axis=-1, keepdims=True)
    lm2 = jnp.where(lane == i1, NEG_INF, lm)
    v2 = jnp.max(lm2, axis=-1, keepdims=True)
    i2 = jnp.min(jnp.where(lm2 == v2, lane, big), axis=-1, keepdims=True)
    e2 = jnp.exp(v2 - v1)
    w1 = p_grp / (1.0 + e2)
    w2 = w1 * e2
    return jnp.where(lane == i1, w1, 0.0) + jnp.where(lane == i2, w2, 0.0)


def _outproj_kernel(stacked, *refs):
    i = pl.program_id(0)
    if stacked:
        (ya_ref, yac_ref, yb_ref, ybc_ref, w_ref, x_ref, c_ref, mod_ref, wr_ref, br_ref,
         hn_ref, v_ref, comb_ref, cnt_ref) = refs
        ya = _stacked_tile(i, OUT_TM, ya_ref, yac_ref)
        yb = _stacked_tile(i, OUT_TM, yb_ref, ybc_ref)
        h = _stacked_tile(i, OUT_TM, x_ref, c_ref)
    else:
        ya_ref, yb_ref, w_ref, h_ref, mod_ref, wr_ref, br_ref, hn_ref, v_ref, comb_ref, cnt_ref = refs
        ya = ya_ref[...]
        yb = yb_ref[...]
        h = h_ref[...]
    r = _mod_row(i, OUT_TM)
    g1 = mod_ref[pl.ds(r, 1), pl.ds(2 * D_MODEL, D_MODEL)]
    sh2 = mod_ref[pl.ds(r, 1), pl.ds(3 * D_MODEL, D_MODEL)]
    sc2 = mod_ref[pl.ds(r, 1), pl.ds(4 * D_MODEL, D_MODEL)]
    half = ya.shape[1]
    m = _dot(ya, w_ref[0:half, :]) + _dot(yb, w_ref[half:2 * half, :])
    hn = h + g1 * m
    hn_ref[...] = hn
    v = _rms(hn) * (1.0 + sc2) + sh2
    v_ref[...] = v.astype(bf16)
    vh, vl = _split(v)
    prod = _dot(jnp.concatenate([vh, vl], axis=0), wr_ref[...])
    tm = v.shape[0]
    comb = _route(prod[:tm, :ROUTE_W] + prod[:tm, ROUTE_W:] + prod[tm:, :ROUTE_W] + br_ref[...])
    comb_ref[...] = comb
    for s in range(OUT_TM // MOE_TB):
        cnt = jnp.sum((comb[s * MOE_TB:(s + 1) * MOE_TB] != 0.0).astype(f32), axis=0, keepdims=True)
        cnt_ref[s] = jnp.broadcast_to(cnt, (8, ROUTE_W)).astype(jnp.int32)


def out_projection(n_rows, yas, ybs, w, hs, mod, wr, br):
    stacked = len(yas) == 2
    assert stacked == (len(hs) == 2) == (len(ybs) == 2) and (not stacked or n_rows == T_ALL)
    half = ybs[0].shape[1]
    row_spec = lambda width: [pl.BlockSpec((OUT_TM, width), lambda i: (i, 0))]
    rows = lambda width: _stacked_specs(OUT_TM, width) if stacked else row_spec(width)
    wr_hi = wr.astype(bf16)
    wr_lo = (wr - wr_hi.astype(f32)).astype(bf16)
    return pl.pallas_call(
        functools.partial(_outproj_kernel, stacked),
        grid=(n_rows // OUT_TM,),
        in_specs=(
            rows(half) + rows(half)
            + [_const_spec((2 * half, D_MODEL))]
            + rows(D_MODEL)
            + [_const_spec((MOD_ROWS, 6 * D_MODEL)),
               _const_spec((D_MODEL, 2 * ROUTE_W)),
               _const_spec((1, ROUTE_W))]
        ),
        out_specs=[
            pl.BlockSpec((OUT_TM, D_MODEL), lambda i: (i, 0)),
            pl.BlockSpec((OUT_TM, D_MODEL), lambda i: (i, 0)),
            pl.BlockSpec((OUT_TM, ROUTE_W), lambda i: (i, 0)),
            pl.BlockSpec((OUT_TM // MOE_TB, 8, ROUTE_W), lambda i: (i, 0, 0)),
        ],
        out_shape=[
            jax.ShapeDtypeStruct((n_rows, D_MODEL), f32),
            jax.ShapeDtypeStruct((n_rows, D_MODEL), bf16),
            jax.ShapeDtypeStruct((n_rows, ROUTE_W), f32),
            jax.ShapeDtypeStruct((n_rows // MOE_TB, 8, ROUTE_W), jnp.int32),
        ],
        compiler_params=_cparams("parallel"),
        name="out_projection",
    )(*yas, *ybs, w, *hs, mod, jnp.concatenate([wr_hi, wr_lo], axis=1), br)


MOE_UNIT = 16
MOE_TG = 512
MOE_TOP = 2
MOE_RLOC = MOE_TOP * MOE_TB + N_EXPERTS * MOE_UNIT
MOE_NUNIT = MOE_RLOC // MOE_UNIT
MOE_NB = 2
MOE_SPARE = 2 * MOE_NB
MOE_META = 128


def _moe_rows(n_blk):
    return n_blk * MOE_RLOC + N_EXPERTS * MOE_TG


MOE_PLAN_ROWS = 128
MOE_PLAN_TILES = 256


def _lane_pick(x, lane, k):
    return jnp.sum(jnp.where(lane == k, x, 0.0), axis=1, keepdims=True)


def _moe_plan_kernel(n_blk, cnt_ref, ltri_ref, utri_ref, tabd_ref, tabc_ref, te_ref, meta_ref):
    shape = (MOE_PLAN_ROWS, ROUTE_W)
    lane = lax.broadcasted_iota(jnp.int32, shape, 1)
    cnt = cnt_ref[...].astype(f32)
    units = jnp.floor((cnt + (MOE_UNIT - 1.0)) * (1.0 / MOE_UNIT))
    ub = units.astype(bf16)
    utri = utri_ref[...]
    pre = _dot(ltri_ref[...], ub)
    lstart = _dot(ub, utri)
    n_e = jnp.sum(units, axis=0, keepdims=True)
    upt = MOE_TG // MOE_UNIT
    tiles_e = jnp.floor((n_e + (upt - 1.0)) * (1.0 / upt))
    goff = _dot(jnp.broadcast_to(tiles_e, (8, ROUTE_W)).astype(bf16), utri)[0:1, :]
    a = goff * upt + pre - lstart
    n_used = jnp.sum(units, axis=1, keepdims=True)
    j = lane.astype(f32)
    acc = jnp.zeros(shape, f32)
    for e in range(N_EXPERTS):
        ls = _lane_pick(lstart, lane, e)
        u = _lane_pick(units, lane, e)
        acc = acc + jnp.where(j >= ls, jnp.where(j < ls + u, _lane_pick(a, lane, e), 0.0), 0.0)
    rows = (acc + j) * MOE_UNIT
    used = j < n_used
    region = (lax.broadcasted_iota(jnp.int32, shape, 0) & (MOE_SPARE - 1)).astype(f32)
    spare = _moe_rows(n_blk) + region * MOE_RLOC + j * MOE_UNIT
    tabd_ref[...] = jnp.where(used, rows, spare).astype(jnp.int32)
    tabc_ref[...] = jnp.where(used, rows, _lane_pick(rows, lane, 0)).astype(jnp.int32)

    ends = goff + tiles_e
    lane1 = lax.broadcasted_iota(jnp.int32, (1, ROUTE_W), 1)
    ti = lax.broadcasted_iota(jnp.int32, (8, MOE_PLAN_TILES), 1).astype(f32)
    te = jnp.zeros((8, MOE_PLAN_TILES), f32)
    for e in range(N_EXPERTS):
        te = te + jnp.where(ti >= _lane_pick(ends, lane1, e), 1.0, 0.0)
    te_ref[...] = jnp.minimum(te, N_EXPERTS - 1.0).astype(jnp.int32)

    n_tiles = jnp.sum(tiles_e, axis=1, keepdims=True)
    gap_start = jnp.broadcast_to((goff * upt + n_e) * MOE_UNIT, (8, ROUTE_W))
    gap_units = jnp.broadcast_to(tiles_e * upt - n_e, (8, ROUTE_W))
    lane8 = lax.broadcasted_iota(jnp.int32, (8, ROUTE_W), 1)
    meta = jnp.where(lane8 == 0, n_tiles,
                     jnp.where(lane8 <= N_EXPERTS, pltpu.roll(gap_start, 1, 1),
                               jnp.where(lane8 <= 2 * N_EXPERTS, pltpu.roll(gap_units, 1 + N_EXPERTS, 1), 0.0)))
    meta_ref[...] = meta.astype(jnp.int32)


def moe_plan(n_blk, counts, ltri, utri):
    assert n_blk <= MOE_PLAN_ROWS and _moe_rows(n_blk) // MOE_TG <= MOE_PLAN_TILES
    shape = (MOE_PLAN_ROWS, ROUTE_W)
    tabd, tabc, te, meta = pl.pallas_call(
        functools.partial(_moe_plan_kernel, n_blk),
        out_shape=[
            jax.ShapeDtypeStruct(shape, jnp.int32),
            jax.ShapeDtypeStruct(shape, jnp.int32),
            jax.ShapeDtypeStruct((8, MOE_PLAN_TILES), jnp.int32),
            jax.ShapeDtypeStruct((8, ROUTE_W), jnp.int32),
        ],
        name="moe_plan",
    )(counts, ltri, utri)
    return tabd.reshape(-1), tabc.reshape(-1), te[0], meta[0]


def _block_routes(comb, ltri, utri):
    oh = comb != 0.0
    ohf = jnp.where(oh, 1.0, 0.0)
    rank = _dot(ltri, ohf.astype(bf16))
    cnt = jnp.sum(ohf, axis=0, keepdims=True)
    units = jnp.floor((cnt + (MOE_UNIT - 1.0)) * (1.0 / MOE_UNIT))
    seg = _dot(jnp.broadcast_to(units, (8, ROUTE_W)).astype(bf16), utri)[0:1, :] * MOE_UNIT
    dest = seg + rank
    big = float(1 << 20)
    d_a = jnp.min(jnp.where(oh, dest, big), axis=-1, keepdims=True)
    d_b = jnp.max(jnp.where(oh, dest, -1.0), axis=-1, keepdims=True)
    w_a = jnp.sum(jnp.where(oh, jnp.where(dest == d_a, comb, 0.0), 0.0), axis=-1, keepdims=True)
    w_b = jnp.sum(jnp.where(oh, jnp.where(dest == d_b, comb, 0.0), 0.0), axis=-1, keepdims=True)
    second = d_b != d_a
    return d_a, jnp.where(second, d_b, -1.0), w_a, jnp.where(second, w_b, 0.0)


def _one_hot_rows(d):
    r = lax.broadcasted_iota(jnp.int32, (d.shape[0], MOE_RLOC), 1).astype(f32)
    return jnp.where(r == d, 1.0, 0.0).astype(bf16)


def _block_gather_matrix(comb, utri_tok, ltri_exp):
    comb_t = comb.T
    oh = comb_t != 0.0
    ohf = jnp.where(oh, 1.0, 0.0)
    rank = _dot(ohf.astype(bf16), utri_tok)
    cnt = jnp.sum(ohf, axis=1, keepdims=True)
    units = jnp.floor((cnt + (MOE_UNIT - 1.0)) * (1.0 / MOE_UNIT))
    seg = _dot(ltri_exp, jnp.broadcast_to(units, (ROUTE_W, ROUTE_W)).astype(bf16))[:, 0:1] * MOE_UNIT
    dest = seg + rank
    d_a = jnp.min(jnp.where(oh, dest, float(1 << 20)), axis=0, keepdims=True)
    d_b = jnp.max(jnp.where(oh, dest, -1.0), axis=0, keepdims=True)
    d_b = jnp.where(d_b != d_a, d_b, -1.0)
    r = lax.broadcasted_iota(jnp.int32, (MOE_RLOC, comb.shape[0]), 0).astype(f32)
    return jnp.where(r == d_a, 1.0, jnp.where(r == d_b, 1.0, 0.0)).astype(bf16)


def _unit_rows(ref, tab_ref, t, j):
    return ref.at[pl.ds(pl.multiple_of(tab_ref[t * ROUTE_W + j], MOE_UNIT), MOE_UNIT)]


def _wait_all_units(local, remote, sem):
    pltpu.make_async_copy(local, remote.at[pl.ds(0, MOE_RLOC)], sem).wait()


def _gap_copies(meta_ref, zero_ref, remote, sem, wait):
    def per_expert(e, c):
        start = meta_ref[1 + e]

        def per_unit(u, c2):
            ro = remote.at[pl.ds(pl.multiple_of(start + u * MOE_UNIT, MOE_UNIT), MOE_UNIT)]
            cp = pltpu.make_async_copy(zero_ref, ro, sem)
            if wait:
                cp.wait()
            else:
                cp.start()
            return c2

        lax.fori_loop(0, meta_ref[1 + N_EXPERTS + e], per_unit, 0)
        return c

    lax.fori_loop(0, N_EXPERTS, per_expert, 0)


def _moe_dispatch_kernel(n_blk, tab_ref, meta_ref, x_ref, comb_ref, utri_tok_ref, ltri_exp_ref, xs_ref,
                         buf_ref, zero_ref, sem_ref):
    step = pl.program_id(0)
    slot = step % 2
    for bb in range(MOE_NB):
        tok = slice(bb * MOE_TB, (bb + 1) * MOE_TB)
        p = _block_gather_matrix(comb_ref[tok, :], utri_tok_ref[...], ltri_exp_ref[...])
        buf_ref[slot, bb] = _dot(p, x_ref[tok, :]).astype(bf16)
    for bb in range(MOE_NB):
        for j in range(MOE_NUNIT):
            pltpu.make_async_copy(buf_ref.at[slot, bb, pl.ds(j * MOE_UNIT, MOE_UNIT)],
                                  _unit_rows(xs_ref, tab_ref, step * MOE_NB + bb, j), sem_ref.at[slot]).start()

    def wait_slot(s):
        for bb in range(MOE_NB):
            _wait_all_units(buf_ref.at[s, bb], xs_ref, sem_ref.at[s])

    @pl.when(step > 0)
    def _():
        wait_slot(1 - slot)

    @pl.when(step == n_blk // MOE_NB - 1)
    def _():
        zero_ref[...] = jnp.zeros_like(zero_ref)
        _gap_copies(meta_ref, zero_ref, xs_ref, sem_ref.at[2], False)
        wait_slot(slot)
        _gap_copies(meta_ref, zero_ref, xs_ref, sem_ref.at[2], True)


def moe_dispatch(n_blk, tab, meta, x, comb, utri_tok, ltri_exp):
    return pl.pallas_call(
        functools.partial(_moe_dispatch_kernel, n_blk),
        grid_spec=pltpu.PrefetchScalarGridSpec(
            num_scalar_prefetch=2,
            grid=(n_blk // MOE_NB,),
            in_specs=[
                pl.BlockSpec((MOE_NB * MOE_TB, D_MODEL), lambda t, *_: (t, 0)),
                pl.BlockSpec((MOE_NB * MOE_TB, ROUTE_W), lambda t, *_: (t, 0)),
                pl.BlockSpec((MOE_TB, MOE_TB), lambda t, *_: (0, 0)),
                pl.BlockSpec((ROUTE_W, ROUTE_W), lambda t, *_: (0, 0)),
            ],
            out_specs=pl.BlockSpec(memory_space=pl.ANY),
            scratch_shapes=[
                pltpu.VMEM((2, MOE_NB, MOE_RLOC, D_MODEL), bf16),
                pltpu.VMEM((MOE_UNIT, D_MODEL), bf16),
                pltpu.SemaphoreType.DMA((3,)),
            ],
        ),
        out_shape=jax.ShapeDtypeStruct((_moe_rows(n_blk) + MOE_SPARE * MOE_RLOC, D_MODEL), bf16),
        compiler_params=_cparams("arbitrary"),
        name="moe_dispatch",
    )(tab, meta, x, comb, utri_tok, ltri_exp)


def _moe_expert_kernel(tile_exp_ref, meta_ref, xs_ref, wg_ref, wu_ref, wd_ref, ys_ref, wgub_ref, wdb_ref):
    i = pl.program_id(0)

    @pl.when(i < meta_ref[0])
    def _():
        prev = tile_exp_ref[jnp.maximum(i - 1, 0)]

        @pl.when(jnp.logical_or(i == 0, tile_exp_ref[i] != prev))
        def _():
            wgub_ref[:, :D_EXPERT] = wg_ref[0].astype(bf16)
            wgub_ref[:, D_EXPERT:] = wu_ref[0].astype(bf16)
            wdb_ref[...] = wd_ref[0].astype(bf16)

        gu = _dot(xs_ref[...], wgub_ref[...])
        a = _silu(gu[:, :D_EXPERT]) * gu[:, D_EXPERT:]
        ys_ref[...] = _dot(a.astype(bf16), wdb_ref[...]).astype(bf16)


def moe_experts(n_blk, layer, tile_exp, meta, xs, w_gate, w_up, w_down):
    n_tiles = _moe_rows(n_blk) // MOE_TG

    def row_map(i, te, meta):
        return (jnp.minimum(i, meta[0] - 1), 0)

    def w_map(i, te, meta):
        return (layer * N_EXPERTS + te[jnp.minimum(i, meta[0] - 1)], 0, 0)

    return pl.pallas_call(
        _moe_expert_kernel,
        grid_spec=pltpu.PrefetchScalarGridSpec(
            num_scalar_prefetch=2,
            grid=(n_tiles,),
            in_specs=[
                pl.BlockSpec((MOE_TG, D_MODEL), row_map),
                pl.BlockSpec((1, D_MODEL, D_EXPERT), w_map),
                pl.BlockSpec((1, D_MODEL, D_EXPERT), w_map),
                pl.BlockSpec((1, D_EXPERT, D_MODEL), w_map),
            ],
            out_specs=pl.BlockSpec((MOE_TG, D_MODEL), row_map),
            scratch_shapes=[
                pltpu.VMEM((D_MODEL, 2 * D_EXPERT), bf16),
                pltpu.VMEM((D_EXPERT, D_MODEL), bf16),
            ],
        ),
        out_shape=jax.ShapeDtypeStruct((_moe_rows(n_blk), D_MODEL), bf16),
        compiler_params=_cparams("arbitrary"),
        name="moe_experts",
    )(tile_exp, meta, xs, w_gate, w_up, w_down)


def _moe_combine_kernel(n_blk, tab_ref, ys_ref, comb_ref, h_ref, mod_ref, ltri_ref, utri_ref,
                        o_ref, buf_ref, sem_ref):
    step = pl.program_id(0)
    slot = step % 2

    def gather(st, s):
        for bb in range(MOE_NB):
            for j in range(MOE_NUNIT):
                pltpu.make_async_copy(_unit_rows(ys_ref, tab_ref, st * MOE_NB + bb, j),
                                      buf_ref.at[s, bb, pl.ds(j * MOE_UNIT, MOE_UNIT)], sem_ref.at[s]).start()

    @pl.when(step == 0)
    def _():
        gather(0, 0)

    @pl.when(step + 1 < n_blk // MOE_NB)
    def _():
        gather(step + 1, 1 - slot)

    for bb in range(MOE_NB):
        _wait_all_units(buf_ref.at[slot, bb], ys_ref, sem_ref.at[slot])
    for bb in range(MOE_NB):
        tok = slice(bb * MOE_TB, (bb + 1) * MOE_TB)
        d_a, d_b, w_a, w_b = _block_routes(comb_ref[tok, :], ltri_ref[...], utri_ref[...])
        p = jnp.concatenate([_one_hot_rows(d_a), _one_hot_rows(d_b)], axis=0)
        picked = _dot(p, buf_ref[slot, bb])
        m = w_a * picked[:MOE_TB] + w_b * picked[MOE_TB:]
        r = _mod_row(step * MOE_NB + bb, MOE_TB)
        g2 = mod_ref[pl.ds(r, 1), pl.ds(5 * D_MODEL, D_MODEL)]
        o_ref[tok, :] = h_ref[tok, :] + g2 * m


def moe_combine(n_blk, tab, ys, comb, h, mod, ltri, utri):
    return pl.pallas_call(
        functools.partial(_moe_combine_kernel, n_blk),
        grid_spec=pltpu.PrefetchScalarGridSpec(
            num_scalar_prefetch=1,
            grid=(n_blk // MOE_NB,),
            in_specs=[
                pl.BlockSpec(memory_space=pl.ANY),
                pl.BlockSpec((MOE_NB * MOE_TB, ROUTE_W), lambda t, *_: (t, 0)),
                pl.BlockSpec((MOE_NB * MOE_TB, D_MODEL), lambda t, *_: (t, 0)),
                pl.BlockSpec((MOD_ROWS, 6 * D_MODEL), lambda t, *_: (0, 0)),
                pl.BlockSpec((MOE_TB, MOE_TB), lambda t, *_: (0, 0)),
                pl.BlockSpec((ROUTE_W, ROUTE_W), lambda t, *_: (0, 0)),
            ],
            out_specs=pl.BlockSpec((MOE_NB * MOE_TB, D_MODEL), lambda t, *_: (t, 0)),
            scratch_shapes=[
                pltpu.VMEM((2, MOE_NB, MOE_RLOC, D_MODEL), bf16),
                pltpu.SemaphoreType.DMA((2,)),
            ],
        ),
        out_shape=jax.ShapeDtypeStruct((n_blk * MOE_TB, D_MODEL), f32),
        compiler_params=_cparams("arbitrary"),
        name="moe_combine",
    )(tab, ys, comb, h, mod, ltri, utri)


def sparse_moe(n_rows, layer, v, comb, counts, h, mod, w_gate, w_up, w_down, ltri, ltri_plan, utri):
    n_blk = n_rows // MOE_TB
    cnt = jnp.pad(counts[:, 0, :], ((0, MOE_PLAN_ROWS - n_blk), (0, 0)))
    tab_d, tab_c, tile_exp, meta = moe_plan(n_blk, cnt, ltri_plan, utri)
    assert MOE_PLAN_ROWS == ROUTE_W
    xs = moe_dispatch(n_blk, tab_d, meta, v, comb, ltri.T, ltri_plan)
    ys = moe_experts(n_blk, layer, tile_exp, meta, xs,
                     w_gate.reshape(DEPTH * N_EXPERTS, D_MODEL, D_EXPERT),
                     w_up.reshape(DEPTH * N_EXPERTS, D_MODEL, D_EXPERT),
                     w_down.reshape(DEPTH * N_EXPERTS, D_EXPERT, D_MODEL))
    return moe_combine(n_blk, tab_c, ys, comb, h, mod, ltri, utri)


def _dft_tables(L):
    k = np.arange(L, dtype=np.int64)
    ang = (2.0 * np.pi / (2 * L)) * ((k[:, None] * k[None, :]) % (2 * L)).astype(np.float64)
    return np.cos(ang).astype(np.float32), np.sin(ang).astype(np.float32)


def _filter_features(L):
    bands = (HY_EMB - 1) // 2
    t = np.linspace(0.0, 1.0, L, dtype=np.float32).astype(np.float64)[:, None]
    w = (2.0 * np.pi / L) * np.arange(L, dtype=np.float64)[:, None]
    fb = np.linspace(1e-4, bands - 1, bands, dtype=np.float32).astype(np.float64)[None, :]
    z = np.concatenate([t, np.cos(fb * w), -np.sin(fb * w)], axis=-1)
    zp = np.zeros((L, FEAT_PAD), np.float32)
    zp[:, :HY_EMB] = z
    deltas = np.abs(np.linspace(HY_MIN_DECAY, HY_MAX_DECAY, HY_CH, dtype=np.float32).astype(np.float64))
    decay = np.exp(-t * deltas[None, :]).astype(np.float32)
    return zp, decay


def _rope_table(cos, sin, half, tm):
    S, width = cos.shape
    low = (np.arange(width) % (2 * half)) < half
    tab = np.zeros((3, S + tm, width), np.float32)
    tab[0, :S] = cos
    tab[0, S:] = 1.0
    tab[1, :S] = np.where(low[None, :], 0.0, sin)
    tab[2, :S] = np.where(low[None, :], -sin, 0.0)
    return tab


def _axial_rope_table(head_dim, tm):
    rows = SEQ // GRID_W
    nf = head_dim // 4
    row = np.repeat(np.arange(rows), GRID_W).astype(np.float64)
    col = np.tile(np.arange(GRID_W), rows).astype(np.float64)
    inv = ROPE_BASE ** (-np.arange(nf, dtype=np.float64) / nf)
    ang = np.stack([row[:, None] * inv, col[:, None] * inv], axis=1)
    a = np.broadcast_to(ang[:, :, None, :], (SEQ, 2, 2, nf)).reshape(SEQ, head_dim)
    reps = LANES // head_dim
    a = np.tile(a, (1, reps))
    return _rope_table(np.cos(a), np.sin(a), nf, tm)


def _seq_rope_table(head_dim, tm):
    inv = 1.0 / (ROPE_BASE ** np.linspace(0.0, 1.0, head_dim // 2, dtype=np.float32).astype(np.float64))
    ang = np.arange(SEQ, dtype=np.float64)[:, None] * inv
    a = np.concatenate([ang, ang], axis=1)
    return _rope_table(np.cos(a), np.sin(a), head_dim // 2, tm)


def _group_mean_matrix():
    g = np.arange(SEG) // DA_HD
    return (g[:, None] == g[None, :]).astype(np.float32) / DA_HD


def _router_weights(w_grp, b_grp, w_rt, b_rt):
    pad = ROUTE_W - 2 * N_EXPERTS
    wr = jnp.concatenate([w_rt, jnp.repeat(w_grp, EXP_PER_GROUP, axis=1),
                          jnp.zeros((D_MODEL, pad), f32)], axis=1)
    br = jnp.concatenate([b_rt, jnp.repeat(b_grp, EXP_PER_GROUP), jnp.zeros((pad,), f32)])[None, :]
    return wr, br


def _strict_lower(n):
    i = np.arange(n)
    return (i[None, :] < i[:, None]).astype(np.float32)


def kernel(x, c, ctx, c_ctx, ada_w, ada_b, e_w_in, e_w_out, hy_conv_w, hy_conv_b, hy_f_w1, hy_f_b1, hy_f_w2, hy_f_b2, hy_f_w3, hy_f_freq, hy_bias, da_q_norm, da_k_norm, da_lam, da_subln, o_w_in, o_w_out, ret_decay, ret_gn, gq_q_norm, gq_k_norm, gq_sink, moe_w_grp, moe_b_grp, moe_w_rt, moe_b_rt, moe_w_gate, moe_w_up, moe_w_down):
    assert x.shape == (BATCH, SEQ, D_MODEL) and ctx.shape == (BATCH, CTX_LEN, D_MODEL)
    x_rows = x.reshape(T_LAT, D_MODEL)
    ctx_rows = ctx.reshape(T_CTX, D_MODEL)
    c_rows = jnp.concatenate([c, c_ctx[None, :], jnp.zeros((MOD_ROWS - BATCH - 1, D_MODEL), f32)], axis=0)
    mod = ada_modulation(c_rows, ada_w, ada_b)

    gmat = jnp.asarray(_group_mean_matrix()).astype(bf16)
    ax_tab = jnp.asarray(_axial_rope_table(DA_HD, PROJ_TM))
    r1_tab = jnp.asarray(_seq_rope_table(RET_DK, PROJ_TM))
    ones = jnp.ones((SEG,), f32)

    lam_init0 = 0.8 - 0.6 * math.exp(-0.3 * 0)
    reps = SEG // DA_HD
    gain0 = jnp.concatenate([ones, ones, ones, jnp.tile(da_q_norm[0], reps) * (DA_HD ** -0.5 * LOG2E),
                             jnp.tile(da_k_norm[0], reps), ones])[None, :]
    proj0 = in_projection("even", [x_rows, ctx_rows], mod[0], e_w_in[0].astype(bf16), gain0, gmat, [ax_tab])

    w3r = hy_f_w3[0].reshape(HY_FILT_HID, 4, HY_CH).transpose(1, 0, 2)
    w1p = jnp.concatenate([hy_f_w1[0], jnp.zeros((FEAT_PAD - HY_EMB, HY_FILT_HID), f32)], axis=0)
    y_hy = []
    for L, blk0 in ((SEQ, 0), (CTX_LEN, T_LAT // CTX_LEN)):
        zfeat, decay = _filter_features(L)
        cm, sm = _dft_tables(L)
        cm = jnp.asarray(cm).astype(bf16)
        sm = jnp.asarray(sm).astype(bf16)
        spec, nyq = hyena_filter_spectra(L, jnp.asarray(zfeat), w1p, hy_f_b1[0][None, :], hy_f_w2[0],
                                         hy_f_b2[0][None, :], w3r, hy_f_freq[0], jnp.asarray(decay), cm, sm)
        y_hy.append(hyena_mix(L, blk0, proj0, hy_conv_w[0], hy_conv_b[0][None, :], spec, nyq, hy_bias[0], cm, sm))

    y_da = diff_attention(proj0, da_lam[0], da_subln[0][None, :], lam_init0)

    wr0, br0 = _router_weights(moe_w_grp[0], moe_b_grp[0], moe_w_rt[0], moe_b_rt[0])
    ltri = jnp.asarray(_strict_lower(MOE_TB)).astype(bf16)
    ltri_plan = jnp.asarray(_strict_lower(MOE_PLAN_ROWS)).astype(bf16)
    utri = jnp.asarray(_strict_lower(ROUTE_W).T).astype(bf16)
    h, v, comb, counts = out_projection(T_ALL, y_hy, y_da, e_w_out[0].astype(bf16), [x_rows, ctx_rows], mod[0],
                                        wr0, br0)
    h = sparse_moe(T_ALL, 0, v, comb, counts, h, mod[0], moe_w_gate, moe_w_up, moe_w_down, ltri, ltri_plan, utri)

    w_in1 = jnp.concatenate([o_w_in[0], jnp.zeros((D_MODEL, PROJ_W - o_w_in.shape[2]), f32)], axis=1).astype(bf16)
    kq = GQ_KV * GQ_HD
    gain1 = jnp.concatenate([ones, ones * RET_DK ** -0.5, ones, ones,
                             jnp.tile(gq_q_norm[0], reps) * (GQ_HD ** -0.5 * LOG2E),
                             jnp.tile(gq_k_norm[0], kq // GQ_HD), jnp.ones((SEG - kq,), f32)])[None, :]
    proj1 = in_projection("odd", [h], mod[1], w_in1, gain1, gmat, [ax_tab, r1_tab])
    y_ret = retention(proj1, ret_decay[0], ret_gn[0][None, :])
    y_gq = window_gqa(proj1, gq_sink[0])
    wr1, br1 = _router_weights(moe_w_grp[1], moe_b_grp[1], moe_w_rt[1], moe_b_rt[1])
    h_lat, v, comb, counts = out_projection(T_LAT, [y_ret], [y_gq], o_w_out[0].astype(bf16), [h], mod[1], wr1, br1)
    out = sparse_moe(T_LAT, 1, v, comb, counts, h_lat, mod[1], moe_w_gate, moe_w_up, moe_w_down, ltri, ltri_plan,
                     utri)
    return out.reshape(BATCH, SEQ, D_MODEL)
```

```python
import functools
import math

import numpy as np
import jax
import jax.numpy as jnp
from jax import lax
from jax.experimental import pallas as pl
from jax.experimental.pallas import tpu as pltpu

f32 = jnp.float32
bf16 = jnp.bfloat16

D_MODEL = 1024
BATCH = 8
SEQ = 2048
DEPTH = 2
GRID_W = 64
CTX_LEN = 256
EPS = 1e-6
NEG_INF = -1e30
LOG2E = math.log2(math.e)
ROPE_BASE = 10000.0
HY_CH = D_MODEL // 2
HY_EMB = 33
HY_FILT_HID = 64
HY_MAX_DECAY = math.log(1e-2) / 0.3
HY_MIN_DECAY = math.log(1e-2) / 1.5
DA_HEADS = 4
DA_HD = D_MODEL // 16
RET_HEADS = 4
RET_DK = D_MODEL // 8
RET_CHUNK = 128
GQ_KV = 2
GQ_GROUP = 4
GQ_HD = D_MODEL // 16
WINDOW = 128
N_GROUPS = 4
EXP_PER_GROUP = 8
N_EXPERTS = N_GROUPS * EXP_PER_GROUP
D_EXPERT = D_MODEL // 4

T_LAT = BATCH * SEQ
T_CTX = BATCH * CTX_LEN
T_ALL = T_LAT + T_CTX
PROJ_W = 3072
SEG = 512
CTX_MOD_ROW = BATCH
MOD_ROWS = 16

LANES = 128
VMEM_LIMIT_BYTES = 56 * 1024 * 1024


def _cparams(*sem):
    return pltpu.CompilerParams(dimension_semantics=sem, vmem_limit_bytes=VMEM_LIMIT_BYTES)


def _dot(a, b):
    return jnp.dot(a, b, preferred_element_type=f32)


def _dot_nt(a, b):
    return lax.dot_general(a, b, (((1,), (1,)), ((), ())), preferred_element_type=f32)


def _split(x):
    hi = x.astype(bf16)
    lo = (x - hi.astype(f32)).astype(bf16)
    return hi, lo


def _dot3(a, b):
    ah, al = _split(a)
    bh, bl = _split(b)
    return _dot(ah, bh) + _dot(al, bh) + _dot(ah, bl)


def _silu(x):
    return x * jax.nn.sigmoid(x)


def _rms(x):
    return x * lax.rsqrt(jnp.mean(x * x, axis=-1, keepdims=True) + EPS)


def _const_spec(shape):
    nd = len(shape)
    return pl.BlockSpec(shape, lambda *_: (0,) * nd)


def _const_spec1(shape):
    nd = len(shape)
    return pl.BlockSpec(shape, lambda *_: (0,) * nd, pipeline_mode=pl.Buffered(1))


ADA_TN = 1536


def _ada_kernel(c_ref, w_ref, b_ref, o_ref):
    x = _silu(c_ref[...])
    o_ref[0] = _dot3(x, w_ref[0]) + b_ref[0]


def ada_modulation(c_rows, ada_w, ada_b):
    n = 6 * D_MODEL
    return pl.pallas_call(
        _ada_kernel,
        grid=(DEPTH, n // ADA_TN),
        in_specs=[
            pl.BlockSpec((MOD_ROWS, D_MODEL), lambda l, j: (0, 0)),
            pl.BlockSpec((1, D_MODEL, ADA_TN), lambda l, j: (l, 0, j)),
            pl.BlockSpec((1, 1, ADA_TN), lambda l, j: (l, 0, j)),
        ],
        out_specs=pl.BlockSpec((1, MOD_ROWS, ADA_TN), lambda l, j: (l, 0, j)),
        out_shape=jax.ShapeDtypeStruct((DEPTH, MOD_ROWS, n), f32),
        compiler_params=_cparams("arbitrary", "arbitrary"),
        name="ada_modulation",
    )(c_rows, ada_w, ada_b.reshape(DEPTH, 1, n))


PROJ_TM = 1024


def _mod_row(i, tm):
    return jnp.minimum((i * tm) // SEQ, CTX_MOD_ROW)


def _tile4(t):
    return jnp.concatenate([t, t, t, t], axis=1)


def _group_norm64(y, gmat):
    ms = _dot((y * y).astype(bf16), gmat)
    return y * lax.rsqrt(ms + EPS)


def _rope(y, tab, shift):
    w = y.shape[1]
    return y * tab[0] + pltpu.roll(y, shift, 1) * tab[1] + pltpu.roll(y, w - shift, 1) * tab[2]


def _stacked_specs(tm, width):
    n_lat = T_LAT // tm
    return [pl.BlockSpec((tm, width), lambda i: (jnp.minimum(i, n_lat - 1), 0)),
            pl.BlockSpec((tm, width), lambda i: (jnp.maximum(i - n_lat, 0), 0))]


def _stacked_tile(i, tm, lat_ref, ctx_ref):
    return jnp.where(i < T_LAT // tm, lat_ref[...], ctx_ref[...])


def _inproj_kernel(layer_kind, *refs):
    i = pl.program_id(0)
    if layer_kind == "even":
        x_ref, c_ref, mod_ref, w_ref, gain_ref, gmat_ref, ax_ref, o_ref = refs
        h = _stacked_tile(i, PROJ_TM, x_ref, c_ref)
    else:
        h_ref, mod_ref, w_ref, gain_ref, gmat_ref, ax_ref, r1_ref, o_ref = refs
        h = h_ref[...]
    r = _mod_row(i, PROJ_TM)
    sh = mod_ref[pl.ds(r, 1), pl.ds(0, D_MODEL)]
    sc = mod_ref[pl.ds(r, 1), pl.ds(D_MODEL, D_MODEL)]
    u = (_rms(h) * (1.0 + sc) + sh).astype(bf16)

    def seg(j):
        return _dot(u, w_ref[:, j * SEG:(j + 1) * SEG])

    def put(j, y):
        o_ref[:, j * SEG:(j + 1) * SEG] = y.astype(bf16)

    def gain(j):
        return gain_ref[:, j * SEG:(j + 1) * SEG]

    gmat = gmat_ref[...]
    ax = ax_ref[...]
    ax4 = (_tile4(ax[0]), _tile4(ax[1]), _tile4(ax[2]))
    if layer_kind == "even":
        for j in (3, 4):
            put(j, _rope(_group_norm64(seg(j), gmat) * gain(j), ax4, DA_HD // 4))
        for j in (0, 1, 2, 5):
            put(j, seg(j))
    else:
        r1 = r1_ref[...]
        r14 = (_tile4(r1[0]), _tile4(r1[1]), _tile4(r1[2]))
        put(4, _rope(_group_norm64(seg(4), gmat) * gain(4), ax4, GQ_HD // 4))
        y = seg(5)
        kw = GQ_KV * GQ_HD
        yk = _rope(_group_norm64(y[:, :kw], gmat[:kw, :kw]) * gain(5)[:, :kw], ax, GQ_HD // 4)
        yv = y[:, kw:2 * kw]
        pieces = (yk, yv, pltpu.roll(yk, GQ_HD, 1), pltpu.roll(yv, GQ_HD, 1))
        for p, piece in enumerate(pieces):
            o_ref[:, 5 * SEG + p * kw:5 * SEG + (p + 1) * kw] = piece.astype(bf16)
        for j in (0, 1):
            put(j, _rope(seg(j) * gain(j), r14, RET_DK // 2))
        for j in (2, 3):
            put(j, seg(j))


def in_projection(layer_kind, hs, mod, w, gain, gmat, tables):
    n_lat_tiles = T_LAT // PROJ_TM
    n_pos_tiles = SEQ // PROJ_TM

    def tab_map(i):
        return (0, jnp.where(i < n_lat_tiles, i % n_pos_tiles, n_pos_tiles), 0)

    tab_specs = [pl.BlockSpec((3, PROJ_TM, LANES), tab_map) for _ in tables]
    if layer_kind == "even":
        h_specs = _stacked_specs(PROJ_TM, D_MODEL)
    else:
        h_specs = [pl.BlockSpec((PROJ_TM, D_MODEL), lambda i: (i, 0))]
    return pl.pallas_call(
        functools.partial(_inproj_kernel, layer_kind),
        grid=(T_ALL // PROJ_TM,),
        in_specs=h_specs + [
            _const_spec((MOD_ROWS, 6 * D_MODEL)),
            _const_spec((D_MODEL, PROJ_W)),
            _const_spec((1, PROJ_W)),
            _const_spec((SEG, SEG)),
        ] + tab_specs,
        out_specs=pl.BlockSpec((PROJ_TM, PROJ_W), lambda i: (i, 0)),
        out_shape=jax.ShapeDtypeStruct((T_ALL, PROJ_W), bf16),
        compiler_params=_cparams("parallel"),
        name="in_projection_" + layer_kind,
    )(*hs, mod, w, gain, gmat, *tables)


HY_TC = 256
HY_FREQ_CHUNK = 512
FEAT_PAD = 64


def _alt_sign(shape, axis):
    idx = lax.broadcasted_iota(jnp.int32, shape, axis)
    return jnp.where((idx & 1) == 0, 1.0, -1.0).astype(f32)


def _filter_kernel(L, z_ref, w1_ref, b1_ref, w2_ref, b2_ref, wf_ref, wb_ref, freq_ref, dec_ref, c_ref, s_ref,
                   spec_ref, nyq_ref):
    hid = jnp.sin(freq_ref[0:1, :] * (_dot3(z_ref[...], w1_ref[...]) + b1_ref[...]))
    hid = jnp.sin(freq_ref[1:2, :] * (_dot3(hid, w2_ref[...]) + b2_ref[...]))
    dec = dec_ref[...]
    fwd = _dot3(hid, wf_ref[0]) * dec
    bwd = _dot3(hid, wb_ref[0]) * dec
    row = lax.broadcasted_iota(jnp.int32, fwd.shape, 0)
    bwd = jnp.where(row == 0, 0.0, bwd)
    even = fwd + bwd
    odd = bwd - fwd
    wk = jnp.where(row == 0, 0.5 / L, 1.0 / L).astype(f32)
    spec_ref[0, 0] = _dot(c_ref[...], even.astype(bf16)) * wk
    spec_ref[0, 1] = _dot(s_ref[...], odd.astype(bf16)) * wk
    nyq = jnp.sum(even * _alt_sign(even.shape, 0), axis=0, keepdims=True) * (0.5 / L)
    nyq_ref[0] = jnp.broadcast_to(nyq, (8, nyq.shape[1]))


def hyena_filter_spectra(L, zfeat, w1, b1, w2, b2, w3r, freq, decay, cmat, smat):
    nct = HY_CH // HY_TC
    return pl.pallas_call(
        functools.partial(_filter_kernel, L),
        grid=(2, nct),
        in_specs=[
            _const_spec((L, FEAT_PAD)),
            _const_spec((FEAT_PAD, HY_FILT_HID)),
            _const_spec((1, HY_FILT_HID)),
            _const_spec((HY_FILT_HID, HY_FILT_HID)),
            _const_spec((1, HY_FILT_HID)),
            pl.BlockSpec((1, HY_FILT_HID, HY_TC), lambda n, c: (2 * n, 0, c)),
            pl.BlockSpec((1, HY_FILT_HID, HY_TC), lambda n, c: (2 * n + 1, 0, c)),
            _const_spec((2, HY_FILT_HID)),
            pl.BlockSpec((L, HY_TC), lambda n, c: (0, c)),
            _const_spec1((L, L)),
            _const_spec1((L, L)),
        ],
        out_specs=[
            pl.BlockSpec((1, 2, L, HY_TC), lambda n, c: (n, 0, 0, c)),
            pl.BlockSpec((1, 8, HY_TC), lambda n, c: (n, 0, c)),
        ],
        out_shape=[
            jax.ShapeDtypeStruct((2, 2, L, HY_CH), f32),
            jax.ShapeDtypeStruct((2, 8, HY_CH), f32),
        ],
        compiler_params=_cparams("arbitrary", "arbitrary"),
        name="hyena_filter_L%d" % L,
    )(zfeat, w1, b1, w2, b2, w3r, w3r, freq, decay, cmat, smat)


def _conv3(u, w, b):
    L = u.shape[0]
    row = lax.broadcasted_iota(jnp.int32, u.shape, 0)
    prev = jnp.where(row == 0, 0.0, pltpu.roll(u, 1, 0))
    nxt = jnp.where(row == L - 1, 0.0, pltpu.roll(u, L - 1, 0))
    return prev * w[0:1, :] + u * w[1:2, :] + nxt * w[2:3, :] + b


def _hyena_kernel(v_ref, x1_ref, x2_ref, wv_ref, w1_ref, w2_ref, bv_ref, b1_ref, b2_ref, spec_ref, nyq_ref,
                  bias_ref, c_ref, s_ref, o_ref, yr_ref, yi_ref):
    L = v_ref.shape[0]
    fch = min(L, HY_FREQ_CHUNK)
    z = _conv3(v_ref[...].astype(f32), wv_ref[...], bv_ref[...])
    gate_refs = ((x1_ref, w1_ref, b1_ref), (x2_ref, w2_ref, b2_ref))
    alt = _alt_sign(z.shape, 0)
    for n in range(2):
        zb = z.astype(bf16)
        for k in range(L // fch):
            rows = slice(k * fch, (k + 1) * fch)
            a = _dot(c_ref[rows, :], zb)
            b = _dot(s_ref[rows, :], zb)
            hr = spec_ref[n, 0, rows, :]
            hi = spec_ref[n, 1, rows, :]
            yr_ref[rows, :] = (a * hr + b * hi).astype(bf16)
            yi_ref[rows, :] = (a * hi - b * hr).astype(bf16)
        x_nyq = jnp.sum(z * alt, axis=0, keepdims=True)
        y = (_dot(c_ref[...], yr_ref[...]) - _dot(s_ref[...], yi_ref[...])
             + alt * (x_nyq * nyq_ref[n, 0:1, :]))
        x_ref, w_ref, b_ref = gate_refs[n]
        gate = _conv3(x_ref[...].astype(f32), w_ref[...], b_ref[...])
        z = gate * (y + z * bias_ref[n:n + 1, :])
    o_ref[...] = z.astype(bf16)


def hyena_mix(L, row_block0, proj, conv_w, conv_b, spec, nyq, bias, cmat, smat):
    nct = HY_CH // HY_TC
    nseg = HY_CH // HY_TC

    def col(k):
        return lambda c, b: (row_block0 + b, k * nseg + c)

    def par(k):
        return lambda c, b: (0, k * nseg + c)

    in_specs = (
        [pl.BlockSpec((L, HY_TC), col(k)) for k in range(3)]
        + [pl.BlockSpec((3, HY_TC), par(k)) for k in range(3)]
        + [pl.BlockSpec((1, HY_TC), par(k)) for k in range(3)]
        + [
            pl.BlockSpec((2, 2, L, HY_TC), lambda c, b: (0, 0, 0, c), pipeline_mode=pl.Buffered(1)),
            pl.BlockSpec((2, 8, HY_TC), lambda c, b: (0, 0, c)),
            pl.BlockSpec((2, HY_TC), lambda c, b: (0, c)),
            _const_spec1((L, L)),
            _const_spec1((L, L)),
        ]
    )
    args = [proj, proj, proj, conv_w, conv_w, conv_w, conv_b, conv_b, conv_b, spec, nyq, bias, cmat, smat]
    return pl.pallas_call(
        _hyena_kernel,
        grid=(nct, BATCH),
        in_specs=in_specs,
        out_specs=pl.BlockSpec((L, HY_TC), lambda c, b: (b, c)),
        out_shape=jax.ShapeDtypeStruct((BATCH * L, HY_CH), bf16),
        scratch_shapes=[pltpu.VMEM((L, HY_TC), bf16), pltpu.VMEM((L, HY_TC), bf16)],
        compiler_params=_cparams("arbitrary", "arbitrary"),
        name="hyena_mix_L%d" % L,
    )(*args)


DA_TQ = 512


def _diff_attn_kernel(lam_init, q_ref, qc_ref, kc_ref, vc_ref, kl_ref, vl_ref, lam_ref, subln_ref, o_ref, oc_ref):
    i = pl.program_id(1)
    n_lat_blocks = SEQ // DA_TQ

    @pl.when(i < n_lat_blocks)
    def _():
        _diff_attn_body(lam_init, q_ref, (kc_ref, vc_ref, kl_ref, vl_ref), lam_ref, subln_ref, o_ref)

    @pl.when(i == n_lat_blocks)
    def _():
        _diff_attn_body(lam_init, qc_ref, (kc_ref, vc_ref), lam_ref, subln_ref, oc_ref)


def _diff_attn_body(lam_init, q_ref, kv_refs, lam_ref, subln_ref, o_ref):
    n_src = len(kv_refs) // 2
    lp = lam_ref[...]
    lam = (jnp.exp(jnp.sum(lp[0:1] * lp[1:2], axis=-1, keepdims=True))
           - jnp.exp(jnp.sum(lp[2:3] * lp[3:4], axis=-1, keepdims=True)) + lam_init)
    q = q_ref[...]
    tq = q.shape[0]
    lower = lax.broadcasted_iota(jnp.int32, (tq, 2 * DA_HD), 1) < DA_HD
    zero = jnp.zeros((), bf16)
    hw = 2 * DA_HD
    outs = []
    for h in range(DA_HEADS):
        qh = q[:, h * hw:(h + 1) * hw]
        ks = [kv_refs[2 * s][:, h * hw:(h + 1) * hw] for s in range(n_src)]
        vs = [kv_refs[2 * s + 1][:, h * hw:(h + 1) * hw] for s in range(n_src)]
        qs = jnp.concatenate([jnp.where(lower, qh, zero), jnp.where(lower, zero, qh)], axis=0)
        ss = [_dot_nt(qs, k) for k in ks]
        mx = functools.reduce(jnp.maximum, [jnp.max(s, axis=-1, keepdims=True) for s in ss])
        es = [jnp.exp2(s - mx) for s in ss]
        den = functools.reduce(jnp.add, [jnp.sum(e, axis=-1, keepdims=True) for e in es])
        pv = functools.reduce(jnp.add, [_dot(es[s].astype(bf16), vs[s]) for s in range(n_src)])
        pv = pv * (1.0 / den)
        oh = pv[:tq] - lam * pv[tq:]
        outs.append(_rms(oh) * subln_ref[...] * (1.0 - lam_init))
    o_ref[...] = jnp.concatenate(outs, axis=1).astype(bf16)


def diff_attention(proj, lam_p, subln, lam_init):
    width = DA_HEADS * 2 * DA_HD
    qcol, kcol, vcol = 3, 4, 5
    ctx_blk0 = T_LAT // CTX_LEN
    nq = SEQ // DA_TQ

    def lat_rows(b, i):
        return b * nq + jnp.minimum(i, nq - 1)

    return pl.pallas_call(
        functools.partial(_diff_attn_kernel, lam_init),
        grid=(BATCH, nq + 1),
        in_specs=[
            pl.BlockSpec((DA_TQ, width), lambda b, i: (lat_rows(b, i), qcol)),
            pl.BlockSpec((CTX_LEN, width), lambda b, i: (ctx_blk0 + b, qcol)),
            pl.BlockSpec((CTX_LEN, width), lambda b, i: (ctx_blk0 + b, kcol)),
            pl.BlockSpec((CTX_LEN, width), lambda b, i: (ctx_blk0 + b, vcol)),
            pl.BlockSpec((SEQ, width), lambda b, i: (b, kcol)),
            pl.BlockSpec((SEQ, width), lambda b, i: (b, vcol)),
            _const_spec((4, DA_HD)),
            _const_spec((1, 2 * DA_HD)),
        ],
        out_specs=[
            pl.BlockSpec((DA_TQ, width), lambda b, i: (lat_rows(b, i), 0)),
            pl.BlockSpec((CTX_LEN, width), lambda b, i: (b, 0)),
        ],
        out_shape=[
            jax.ShapeDtypeStruct((T_LAT, width), bf16),
            jax.ShapeDtypeStruct((T_CTX, width), bf16),
        ],
        compiler_params=_cparams("parallel", "arbitrary"),
        name="diff_attention",
    )(proj, proj, proj, proj, proj, proj, lam_p, subln)


def _log_sigmoid(x):
    return jnp.minimum(x, 0.0) - jnp.log(1.0 + jnp.exp(-jnp.abs(x)))


def _retention_kernel(q_ref, k_ref, v_ref, g_ref, kc_ref, vc_ref, decay_ref, gn_ref, o_ref, st_ref):
    h = pl.program_id(1)
    ch = RET_CHUNK
    nchunk = SEQ // ch
    lgs = _log_sigmoid(decay_ref[...])
    sel = lax.broadcasted_iota(jnp.int32, lgs.shape, 1) == h
    lg = jnp.sum(jnp.where(sel, lgs, 0.0), axis=-1, keepdims=True)
    lgf = lg[0:1, :]
    lgb = lg[1:2, :]
    ri = lax.broadcasted_iota(jnp.int32, (ch, ch), 0).astype(f32)
    ci = lax.broadcasted_iota(jnp.int32, (ch, ch), 1).astype(f32)
    rel = ri - ci
    dsum = (jnp.where(rel >= 0, jnp.exp(jnp.maximum(rel, 0.0) * lgf), 0.0)
            + jnp.where(rel <= 0, jnp.exp(jnp.maximum(-rel, 0.0) * lgb), 0.0))
    zeta_f = jnp.exp((ch - 1 - ci) * lgf)
    zeta_b = jnp.exp(ci * lgb)
    xi_f = jnp.exp((ri + 1.0) * lgf)
    xi_b = jnp.exp((ch - ri) * lgb)
    gch_f = jnp.exp(ch * lgf)
    gch_b = jnp.exp(ch * lgb)
    dk = q_ref.shape[1]

    kct = kc_ref[...].astype(f32).T
    vc = vc_ref[...]
    cl = lax.broadcasted_iota(jnp.int32, kct.shape, 1).astype(f32)
    s_f = _dot((kct * jnp.exp((CTX_LEN - 1 - cl) * lgf)).astype(bf16), vc)
    s_b = _dot((kct * jnp.exp(cl * lgb)).astype(bf16), vc)

    def rows(n):
        return slice(n * ch, (n + 1) * ch)

    u_f, u_b = [], []
    for n in range(nchunk):
        kt = k_ref[rows(n), :].astype(f32).T
        vn = v_ref[rows(n), :]
        u_f.append(_dot((kt * zeta_f).astype(bf16), vn))
        u_b.append(_dot((kt * zeta_b).astype(bf16), vn))

    for n in range(nchunk):
        st_ref[n, 0:dk, :] = s_f.astype(bf16)
        s_f = gch_f * s_f + u_f[n]
    for n in reversed(range(nchunk)):
        st_ref[n, dk:2 * dk, :] = s_b.astype(bf16)
        s_b = gch_b * s_b + u_b[n]

    gn = gn_ref[...]
    for n in range(nchunk):
        qn = q_ref[rows(n), :]
        att = _dot_nt(qn, k_ref[rows(n), :]) * dsum
        qf = qn.astype(f32)
        lhs = jnp.concatenate([att.astype(bf16), (qf * xi_f).astype(bf16), (qf * xi_b).astype(bf16)], axis=1)
        rhs = jnp.concatenate([v_ref[rows(n), :], st_ref[n]], axis=0)
        o = _dot(lhs, rhs)
        mu = jnp.mean(o, axis=-1, keepdims=True)
        oc = o - mu
        var = jnp.mean(oc * oc, axis=-1, keepdims=True)
        y = oc * lax.rsqrt(var + EPS) * gn * _silu(g_ref[rows(n), :].astype(f32))
        o_ref[rows(n), :] = y.astype(bf16)


def retention(proj, decay, gn_w):
    dk = RET_DK
    ctx_blk0 = T_LAT // CTX_LEN
    return pl.pallas_call(
        _retention_kernel,
        grid=(BATCH, RET_HEADS),
        in_specs=[
            pl.BlockSpec((SEQ, dk), lambda b, h: (b, h)),
            pl.BlockSpec((SEQ, dk), lambda b, h: (b, RET_HEADS + h)),
            pl.BlockSpec((SEQ, dk), lambda b, h: (b, 2 * RET_HEADS + h)),
            pl.BlockSpec((SEQ, dk), lambda b, h: (b, 3 * RET_HEADS + h)),
            pl.BlockSpec((CTX_LEN, dk), lambda b, h: (ctx_blk0 + b, RET_HEADS + h)),
            pl.BlockSpec((CTX_LEN, dk), lambda b, h: (ctx_blk0 + b, 2 * RET_HEADS + h)),
            _const_spec((2, RET_HEADS)),
            pl.BlockSpec((1, dk), lambda b, h: (0, h)),
        ],
        out_specs=pl.BlockSpec((SEQ, dk), lambda b, h: (b, h)),
        out_shape=jax.ShapeDtypeStruct((T_LAT, RET_HEADS * dk), bf16),
        scratch_shapes=[pltpu.VMEM((SEQ // RET_CHUNK, 2 * dk, dk), bf16)],
        compiler_params=_cparams("parallel", "arbitrary"),
        name="retention",
    )(proj, proj, proj, proj, proj, proj, decay, gn_w)


GQ_TQ = 128
GQ_SPAN = 3 * GQ_TQ
GQ_NB = 8


def _gqa_kernel(q_ref, kv_ref, kvc_ref, sink_ref, o_ref):
    for b in range(GQ_NB):
        rows = slice(b * GQ_TQ, (b + 1) * GQ_TQ)
        o_ref[rows, :] = _gqa_block(pl.program_id(1) * GQ_NB + b, q_ref[rows, :], kv_ref, kvc_ref, sink_ref)


def _gqa_block(n, q, kv_ref, kvc_ref, sink_ref):
    start = pl.multiple_of(jnp.clip((n - 1) * GQ_TQ, 0, SEQ - GQ_SPAN), GQ_TQ)
    pw = 2 * GQ_HD
    n_heads = GQ_KV * GQ_GROUP
    kpos = start + lax.broadcasted_iota(jnp.int32, (GQ_TQ, GQ_SPAN), 1)
    qpos = n * GQ_TQ + lax.broadcasted_iota(jnp.int32, (GQ_TQ, GQ_SPAN), 0)
    mask = jnp.abs(kpos - qpos) <= WINDOW
    lower = lax.broadcasted_iota(jnp.int32, (GQ_TQ, pw), 1) < GQ_HD
    outs = [None] * n_heads
    for swapped in (0, 1):
        kcol = slice(2 * swapped * pw, (2 * swapped + 1) * pw)
        vcol = slice((2 * swapped + 1) * pw, (2 * swapped + 2) * pw)
        k = jnp.concatenate([kvc_ref[:, kcol], kv_ref[pl.ds(start, GQ_SPAN), kcol]], axis=0)
        v = jnp.concatenate([kvc_ref[:, vcol], kv_ref[pl.ds(start, GQ_SPAN), vcol]], axis=0)
        heads = [h for h in range(n_heads) if ((h // GQ_GROUP) == (h % 2)) == (swapped == 0)]
        qs = jnp.concatenate(
            [jnp.where(lower == (h % 2 == 0), q[:, (h // 2) * pw:(h // 2 + 1) * pw], jnp.zeros((), bf16))
             for h in heads], axis=0)
        s = _dot_nt(qs, k)
        es, invs = [], []
        for i, h in enumerate(heads):
            sh = s[i * GQ_TQ:(i + 1) * GQ_TQ]
            sh = jnp.concatenate([sh[:, :CTX_LEN], jnp.where(mask, sh[:, CTX_LEN:], NEG_INF)], axis=1)
            sink = sink_ref[h] * LOG2E
            mx = jnp.maximum(jnp.max(sh, axis=-1, keepdims=True), sink)
            e = jnp.exp2(sh - mx)
            invs.append(1.0 / (jnp.sum(e, axis=-1, keepdims=True) + jnp.exp2(sink - mx)))
            es.append(e.astype(bf16))
        o = _dot(jnp.concatenate(es, axis=0), v)
        for i, h in enumerate(heads):
            outs[h] = o[i * GQ_TQ:(i + 1) * GQ_TQ] * invs[i]
    return jnp.concatenate(
        [jnp.where(lower, outs[2 * j], outs[2 * j + 1]) for j in range(n_heads // 2)], axis=1).astype(bf16)


def window_gqa(proj, sink):
    width = GQ_KV * GQ_GROUP * GQ_HD
    nq = SEQ // (GQ_NB * GQ_TQ)
    kvw = 4 * GQ_KV * GQ_HD
    kv_col = (5 * SEG) // kvw
    ctx_blk0 = T_LAT // CTX_LEN
    return pl.pallas_call(
        _gqa_kernel,
        grid=(BATCH, nq),
        in_specs=[
            pl.BlockSpec((GQ_NB * GQ_TQ, width), lambda b, n: (b * nq + n, 4)),
            pl.BlockSpec((SEQ, kvw), lambda b, n: (b, kv_col)),
            pl.BlockSpec((CTX_LEN, kvw), lambda b, n: (ctx_blk0 + b, kv_col)),
            pl.BlockSpec(memory_space=pltpu.SMEM),
        ],
        out_specs=pl.BlockSpec((GQ_NB * GQ_TQ, width), lambda b, n: (b * nq + n, 0)),
        out_shape=jax.ShapeDtypeStruct((T_LAT, width), bf16),
        compiler_params=_cparams("parallel", "arbitrary"),
        name="window_gqa",
    )(proj, proj, proj, sink)


OUT_TM = 1024
ROUTE_W = LANES
MOE_TB = 256


def _route(logits):
    lane_i = lax.broadcasted_iota(jnp.int32, logits.shape, 1)
    lane = lane_i.astype(f32)
    big = float(1 << 20)
    valid = lane_i < N_EXPERTS
    le = logits
    lgx = pltpu.roll(logits, ROUTE_W - N_EXPERTS, 1)
    lgx = jnp.where(valid, lgx, NEG_INF)
    gmax = jnp.max(lgx, axis=-1, keepdims=True)
    grp = (lane_i // EXP_PER_GROUP).astype(f32)
    g_sel = jnp.min(jnp.where(lgx == gmax, grp, big), axis=-1, keepdims=True)
    p_grp = float(EXP_PER_GROUP) / jnp.sum(jnp.exp(lgx - gmax), axis=-1, keepdims=True)
    lm = jnp.where(valid, jnp.where(grp == g_sel, le, NEG_INF), NEG_INF)
    v1 = jnp.max(lm, axis=-1, keepdims=True)
    i1 = jnp.min(jnp.where(lm == v1, lane, big), axis=-1, keepdims=True)
    lm2 = jnp.where(lane == i1, NEG_INF, lm)
    v2 = jnp.max(lm2, axis=-1, keepdims=True)
    i2 = jnp.min(jnp.where(lm2 == v2, lane, big), axis=-1, keepdims=True)
    e2 = jnp.exp(v2 - v1)
    w1 = p_grp / (1.0 + e2)
    w2 = w1 * e2
    return jnp.where(lane == i1, w1, 0.0) + jnp.where(lane == i2, w2, 0.0)


def _outproj_kernel(stacked, *refs):
    i = pl.program_id(0)
    if stacked:
        (ya_ref, yac_ref, yb_ref, ybc_ref, w_ref, x_ref, c_ref, mod_ref, wr_ref, br_ref,
         hn_ref, v_ref, comb_ref, cnt_ref) = refs
        ya = _stacked_tile(i, OUT_TM, ya_ref, yac_ref)
        yb = _stacked_tile(i, OUT_TM, yb_ref, ybc_ref)
        h = _stacked_tile(i, OUT_TM, x_ref, c_ref)
    else:
        ya_ref, yb_ref, w_ref, h_ref, mod_ref, wr_ref, br_ref, hn_ref, v_ref, comb_ref, cnt_ref = refs
        ya = ya_ref[...]
        yb = yb_ref[...]
        h = h_ref[...]
    r = _mod_row(i, OUT_TM)
    g1 = mod_ref[pl.ds(r, 1), pl.ds(2 * D_MODEL, D_MODEL)]
    sh2 = mod_ref[pl.ds(r, 1), pl.ds(3 * D_MODEL, D_MODEL)]
    sc2 = mod_ref[pl.ds(r, 1), pl.ds(4 * D_MODEL, D_MODEL)]
    half = ya.shape[1]
    m = _dot(ya, w_ref[0:half, :]) + _dot(yb, w_ref[half:2 * half, :])
    hn = h + g1 * m
    hn_ref[...] = hn
    v = _rms(hn) * (1.0 + sc2) + sh2
    v_ref[...] = v.astype(bf16)
    vh, vl = _split(v)
    prod = _dot(jnp.concatenate([vh, vl], axis=0), wr_ref[...])
    tm = v.shape[0]
    comb = _route(prod[:tm, :ROUTE_W] + prod[:tm, ROUTE_W:] + prod[tm:, :ROUTE_W] + br_ref[...])
    comb_ref[...] = comb
    for s in range(OUT_TM // MOE_TB):
        cnt = jnp.sum((comb[s * MOE_TB:(s + 1) * MOE_TB] != 0.0).astype(f32), axis=0, keepdims=True)
        cnt_ref[s] = jnp.broadcast_to(cnt, (8, ROUTE_W)).astype(jnp.int32)


def out_projection(n_rows, yas, ybs, w, hs, mod, wr, br):
    stacked = len(yas) == 2
    assert stacked == (len(hs) == 2) == (len(ybs) == 2) and (not stacked or n_rows == T_ALL)
    half = ybs[0].shape[1]
    row_spec = lambda width: [pl.BlockSpec((OUT_TM, width), lambda i: (i, 0))]
    rows = lambda width: _stacked_specs(OUT_TM, width) if stacked else row_spec(width)
    wr_hi = wr.astype(bf16)
    wr_lo = (wr - wr_hi.astype(f32)).astype(bf16)
    return pl.pallas_call(
        functools.partial(_outproj_kernel, stacked),
        grid=(n_rows // OUT_TM,),
        in_specs=(
            rows(half) + rows(half)
            + [_const_spec((2 * half, D_MODEL))]
            + rows(D_MODEL)
            + [_const_spec((MOD_ROWS, 6 * D_MODEL)),
               _const_spec((D_MODEL, 2 * ROUTE_W)),
               _const_spec((1, ROUTE_W))]
        ),
        out_specs=[
            pl.BlockSpec((OUT_TM, D_MODEL), lambda i: (i, 0)),
            pl.BlockSpec((OUT_TM, D_MODEL), lambda i: (i, 0)),
            pl.BlockSpec((OUT_TM, ROUTE_W), lambda i: (i, 0)),
            pl.BlockSpec((OUT_TM // MOE_TB, 8, ROUTE_W), lambda i: (i, 0, 0)),
        ],
        out_shape=[
            jax.ShapeDtypeStruct((n_rows, D_MODEL), f32),
            jax.ShapeDtypeStruct((n_rows, D_MODEL), bf16),
            jax.ShapeDtypeStruct((n_rows, ROUTE_W), f32),
            jax.ShapeDtypeStruct((n_rows // MOE_TB, 8, ROUTE_W), jnp.int32),
        ],
        compiler_params=_cparams("parallel"),
        name="out_projection",
    )(*yas, *ybs, w, *hs, mod, jnp.concatenate([wr_hi, wr_lo], axis=1), br)


MOE_UNIT = 16
MOE_TG = 512
MOE_TOP = 2
MOE_RLOC = MOE_TOP * MOE_TB + N_EXPERTS * MOE_UNIT
MOE_NUNIT = MOE_RLOC // MOE_UNIT
MOE_NB = 4
MOE_SPARE = 2 * MOE_NB
MOE_META = 128


def _moe_rows(n_blk):
    return n_blk * MOE_RLOC + N_EXPERTS * MOE_TG


MOE_PLAN_ROWS = 128
MOE_PLAN_TILES = 256


def _lane_pick(x, lane, k):
    return jnp.sum(jnp.where(lane == k, x, 0.0), axis=1, keepdims=True)


def _moe_plan_kernel(n_blk, cnt_ref, ltri_ref, utri_ref, tabd_ref, tabc_ref, te_ref, meta_ref):
    shape = (MOE_PLAN_ROWS, ROUTE_W)
    lane = lax.broadcasted_iota(jnp.int32, shape, 1)
    cnt = cnt_ref[...].astype(f32)
    units = jnp.floor((cnt + (MOE_UNIT - 1.0)) * (1.0 / MOE_UNIT))
    ub = units.astype(bf16)
    utri = utri_ref[...]
    pre = _dot(ltri_ref[...], ub)
    lstart = _dot(ub, utri)
    n_e = jnp.sum(units, axis=0, keepdims=True)
    upt = MOE_TG // MOE_UNIT
    tiles_e = jnp.floor((n_e + (upt - 1.0)) * (1.0 / upt))
    goff = _dot(jnp.broadcast_to(tiles_e, (8, ROUTE_W)).astype(bf16), utri)[0:1, :]
    a = goff * upt + pre - lstart
    n_used = jnp.sum(units, axis=1, keepdims=True)
    j = lane.astype(f32)
    acc = jnp.zeros(shape, f32)
    for e in range(N_EXPERTS):
        ls = _lane_pick(lstart, lane, e)
        u = _lane_pick(units, lane, e)
        acc = acc + jnp.where(j >= ls, jnp.where(j < ls + u, _lane_pick(a, lane, e), 0.0), 0.0)
    rows = (acc + j) * MOE_UNIT
    used = j < n_used
    region = (lax.broadcasted_iota(jnp.int32, shape, 0) & (MOE_SPARE - 1)).astype(f32)
    spare = _moe_rows(n_blk) + region * MOE_RLOC + j * MOE_UNIT
    tabd_ref[...] = jnp.where(used, rows, spare).astype(jnp.int32)
    tabc_ref[...] = jnp.where(used, rows, _lane_pick(rows, lane, 0)).astype(jnp.int32)

    ends = goff + tiles_e
    lane1 = lax.broadcasted_iota(jnp.int32, (1, ROUTE_W), 1)
    ti = lax.broadcasted_iota(jnp.int32, (8, MOE_PLAN_TILES), 1).astype(f32)
    te = jnp.zeros((8, MOE_PLAN_TILES), f32)
    for e in range(N_EXPERTS):
        te = te + jnp.where(ti >= _lane_pick(ends, lane1, e), 1.0, 0.0)
    te_ref[...] = jnp.minimum(te, N_EXPERTS - 1.0).astype(jnp.int32)

    n_tiles = jnp.sum(tiles_e, axis=1, keepdims=True)
    gap_start = jnp.broadcast_to((goff * upt + n_e) * MOE_UNIT, (8, ROUTE_W))
    gap_units = jnp.broadcast_to(tiles_e * upt - n_e, (8, ROUTE_W))
    lane8 = lax.broadcasted_iota(jnp.int32, (8, ROUTE_W), 1)
    meta = jnp.where(lane8 == 0, n_tiles,
                     jnp.where(lane8 <= N_EXPERTS, pltpu.roll(gap_start, 1, 1),
                               jnp.where(lane8 <= 2 * N_EXPERTS, pltpu.roll(gap_units, 1 + N_EXPERTS, 1), 0.0)))
    meta_ref[...] = meta.astype(jnp.int32)


def moe_plan(n_blk, counts, ltri, utri):
    assert n_blk <= MOE_PLAN_ROWS and _moe_rows(n_blk) // MOE_TG <= MOE_PLAN_TILES
    shape = (MOE_PLAN_ROWS, ROUTE_W)
    tabd, tabc, te, meta = pl.pallas_call(
        functools.partial(_moe_plan_kernel, n_blk),
        out_shape=[
            jax.ShapeDtypeStruct(shape, jnp.int32),
            jax.ShapeDtypeStruct(shape, jnp.int32),
            jax.ShapeDtypeStruct((8, MOE_PLAN_TILES), jnp.int32),
            jax.ShapeDtypeStruct((8, ROUTE_W), jnp.int32),
        ],
        name="moe_plan",
    )(counts, ltri, utri)
    return tabd.reshape(-1), tabc.reshape(-1), te[0], meta[0]


def _block_routes(comb, ltri, utri):
    oh = comb != 0.0
    ohf = jnp.where(oh, 1.0, 0.0)
    rank = _dot(ltri, ohf.astype(bf16))
    cnt = jnp.sum(ohf, axis=0, keepdims=True)
    units = jnp.floor((cnt + (MOE_UNIT - 1.0)) * (1.0 / MOE_UNIT))
    seg = _dot(jnp.broadcast_to(units, (8, ROUTE_W)).astype(bf16), utri)[0:1, :] * MOE_UNIT
    dest = seg + rank
    big = float(1 << 20)
    d_a = jnp.min(jnp.where(oh, dest, big), axis=-1, keepdims=True)
    d_b = jnp.max(jnp.where(oh, dest, -1.0), axis=-1, keepdims=True)
    w_a = jnp.sum(jnp.where(oh, jnp.where(dest == d_a, comb, 0.0), 0.0), axis=-1, keepdims=True)
    w_b = jnp.sum(jnp.where(oh, jnp.where(dest == d_b, comb, 0.0), 0.0), axis=-1, keepdims=True)
    second = d_b != d_a
    return d_a, jnp.where(second, d_b, -1.0), w_a, jnp.where(second, w_b, 0.0)


def _one_hot_rows(d):
    r = lax.broadcasted_iota(jnp.int32, (d.shape[0], MOE_RLOC), 1).astype(f32)
    return jnp.where(r == d, 1.0, 0.0).astype(bf16)


def _block_gather_matrix(comb, utri_tok, ltri_exp):
    comb_t = comb.T
    oh = comb_t != 0.0
    ohf = jnp.where(oh, 1.0, 0.0)
    rank = _dot(ohf.astype(bf16), utri_tok)
    cnt = jnp.sum(ohf, axis=1, keepdims=True)
    units = jnp.floor((cnt + (MOE_UNIT - 1.0)) * (1.0 / MOE_UNIT))
    seg = _dot(ltri_exp, jnp.broadcast_to(units, (ROUTE_W, ROUTE_W)).astype(bf16))[:, 0:1] * MOE_UNIT
    dest = seg + rank
    d_a = jnp.min(jnp.where(oh, dest, float(1 << 20)), axis=0, keepdims=True)
    d_b = jnp.max(jnp.where(oh, dest, -1.0), axis=0, keepdims=True)
    d_b = jnp.where(d_b != d_a, d_b, -1.0)
    r = lax.broadcasted_iota(jnp.int32, (MOE_RLOC, comb.shape[0]), 0).astype(f32)
    return jnp.where(r == d_a, 1.0, jnp.where(r == d_b, 1.0, 0.0)).astype(bf16)


def _unit_rows(ref, tab_ref, t, j):
    return ref.at[pl.ds(pl.multiple_of(tab_ref[t * ROUTE_W + j], MOE_UNIT), MOE_UNIT)]


def _wait_all_units(local, remote, sem):
    pltpu.make_async_copy(local, remote.at[pl.ds(0, MOE_RLOC)], sem).wait()


def _gap_copies(meta_ref, zero_ref, remote, sem, wait):
    def per_expert(e, c):
        start = meta_ref[1 + e]

        def per_unit(u, c2):
            ro = remote.at[pl.ds(pl.multiple_of(start + u * MOE_UNIT, MOE_UNIT), MOE_UNIT)]
            cp = pltpu.make_async_copy(zero_ref, ro, sem)
            if wait:
                cp.wait()
            else:
                cp.start()
            return c2

        lax.fori_loop(0, meta_ref[1 + N_EXPERTS + e], per_unit, 0)
        return c

    lax.fori_loop(0, N_EXPERTS, per_expert, 0)


def _moe_dispatch_kernel(n_blk, tab_ref, meta_ref, x_ref, comb_ref, utri_tok_ref, ltri_exp_ref, xs_ref,
                         buf_ref, zero_ref, sem_ref):
    step = pl.program_id(0)
    slot = step % 2
    for bb in range(MOE_NB):
        tok = slice(bb * MOE_TB, (bb + 1) * MOE_TB)
        p = _block_gather_matrix(comb_ref[tok, :], utri_tok_ref[...], ltri_exp_ref[...])
        buf_ref[slot, bb] = _dot(p, x_ref[tok, :]).astype(bf16)
    for bb in range(MOE_NB):
        for j in range(MOE_NUNIT):
            pltpu.make_async_copy(buf_ref.at[slot, bb, pl.ds(j * MOE_UNIT, MOE_UNIT)],
                                  _unit_rows(xs_ref, tab_ref, step * MOE_NB + bb, j), sem_ref.at[slot]).start()

    def wait_slot(s):
        for bb in range(MOE_NB):
            _wait_all_units(buf_ref.at[s, bb], xs_ref, sem_ref.at[s])

    @pl.when(step > 0)
    def _():
        wait_slot(1 - slot)

    @pl.when(step == n_blk // MOE_NB - 1)
    def _():
        zero_ref[...] = jnp.zeros_like(zero_ref)
        _gap_copies(meta_ref, zero_ref, xs_ref, sem_ref.at[2], False)
        wait_slot(slot)
        _gap_copies(meta_ref, zero_ref, xs_ref, sem_ref.at[2], True)


def moe_dispatch(n_blk, tab, meta, x, comb, utri_tok, ltri_exp):
    return pl.pallas_call(
        functools.partial(_moe_dispatch_kernel, n_blk),
        grid_spec=pltpu.PrefetchScalarGridSpec(
            num_scalar_prefetch=2,
            grid=(n_blk // MOE_NB,),
            in_specs=[
                pl.BlockSpec((MOE_NB * MOE_TB, D_MODEL), lambda t, *_: (t, 0)),
                pl.BlockSpec((MOE_NB * MOE_TB, ROUTE_W), lambda t, *_: (t, 0)),
                pl.BlockSpec((MOE_TB, MOE_TB), lambda t, *_: (0, 0)),
                pl.BlockSpec((ROUTE_W, ROUTE_W), lambda t, *_: (0, 0)),
            ],
            out_specs=pl.BlockSpec(memory_space=pl.ANY),
            scratch_shapes=[
                pltpu.VMEM((2, MOE_NB, MOE_RLOC, D_MODEL), bf16),
                pltpu.VMEM((MOE_UNIT, D_MODEL), bf16),
                pltpu.SemaphoreType.DMA((3,)),
            ],
        ),
        out_shape=jax.ShapeDtypeStruct((_moe_rows(n_blk) + MOE_SPARE * MOE_RLOC, D_MODEL), bf16),
        compiler_params=_cparams("arbitrary"),
        name="moe_dispatch",
    )(tab, meta, x, comb, utri_tok, ltri_exp)


def _moe_expert_kernel(tile_exp_ref, meta_ref, xs_ref, wg_ref, wu_ref, wd_ref, ys_ref, wgub_ref, wdb_ref):
    i = pl.program_id(0)

    @pl.when(i < meta_ref[0])
    def _():
        prev = tile_exp_ref[jnp.maximum(i - 1, 0)]

        @pl.when(jnp.logical_or(i == 0, tile_exp_ref[i] != prev))
        def _():
            wgub_ref[:, :D_EXPERT] = wg_ref[0].astype(bf16)
            wgub_ref[:, D_EXPERT:] = wu_ref[0].astype(bf16)
            wdb_ref[...] = wd_ref[0].astype(bf16)

        gu = _dot(xs_ref[...], wgub_ref[...])
        a = _silu(gu[:, :D_EXPERT]) * gu[:, D_EXPERT:]
        ys_ref[...] = _dot(a.astype(bf16), wdb_ref[...]).astype(bf16)


def moe_experts(n_blk, layer, tile_exp, meta, xs, w_gate, w_up, w_down):
    n_tiles = _moe_rows(n_blk) // MOE_TG

    def row_map(i, te, meta):
        return (jnp.minimum(i, meta[0] - 1), 0)

    def w_map(i, te, meta):
        return (layer * N_EXPERTS + te[jnp.minimum(i, meta[0] - 1)], 0, 0)

    return pl.pallas_call(
        _moe_expert_kernel,
        grid_spec=pltpu.PrefetchScalarGridSpec(
            num_scalar_prefetch=2,
            grid=(n_tiles,),
            in_specs=[
                pl.BlockSpec((MOE_TG, D_MODEL), row_map),
                pl.BlockSpec((1, D_MODEL, D_EXPERT), w_map),
                pl.BlockSpec((1, D_MODEL, D_EXPERT), w_map),
                pl.BlockSpec((1, D_EXPERT, D_MODEL), w_map),
            ],
            out_specs=pl.BlockSpec((MOE_TG, D_MODEL), row_map),
            scratch_shapes=[
                pltpu.VMEM((D_MODEL, 2 * D_EXPERT), bf16),
                pltpu.VMEM((D_EXPERT, D_MODEL), bf16),
            ],
        ),
        out_shape=jax.ShapeDtypeStruct((_moe_rows(n_blk), D_MODEL), bf16),
        compiler_params=_cparams("arbitrary"),
        name="moe_experts",
    )(tile_exp, meta, xs, w_gate, w_up, w_down)


def _moe_combine_kernel(n_blk, tab_ref, ys_ref, comb_ref, h_ref, mod_ref, ltri_ref, utri_ref,
                        o_ref, buf_ref, sem_ref):
    step = pl.program_id(0)
    slot = step % 2

    def gather(st, s):
        for bb in range(MOE_NB):
            for j in range(MOE_NUNIT):
                pltpu.make_async_copy(_unit_rows(ys_ref, tab_ref, st * MOE_NB + bb, j),
                                      buf_ref.at[s, bb, pl.ds(j * MOE_UNIT, MOE_UNIT)], sem_ref.at[s]).start()

    @pl.when(step == 0)
    def _():
        gather(0, 0)

    @pl.when(step + 1 < n_blk // MOE_NB)
    def _():
        gather(step + 1, 1 - slot)

    for bb in range(MOE_NB):
        _wait_all_units(buf_ref.at[slot, bb], ys_ref, sem_ref.at[slot])
    for bb in range(MOE_NB):
        tok = slice(bb * MOE_TB, (bb + 1) * MOE_TB)
        d_a, d_b, w_a, w_b = _block_routes(comb_ref[tok, :], ltri_ref[...], utri_ref[...])
        p = jnp.concatenate([_one_hot_rows(d_a), _one_hot_rows(d_b)], axis=0)
        picked = _dot(p, buf_ref[slot, bb])
        m = w_a * picked[:MOE_TB] + w_b * picked[MOE_TB:]
        r = _mod_row(step * MOE_NB + bb, MOE_TB)
        g2 = mod_ref[pl.ds(r, 1), pl.ds(5 * D_MODEL, D_MODEL)]
        o_ref[tok, :] = h_ref[tok, :] + g2 * m


def moe_combine(n_blk, tab, ys, comb, h, mod, ltri, utri):
    return pl.pallas_call(
        functools.partial(_moe_combine_kernel, n_blk),
        grid_spec=pltpu.PrefetchScalarGridSpec(
            num_scalar_prefetch=1,
            grid=(n_blk // MOE_NB,),
            in_specs=[
                pl.BlockSpec(memory_space=pl.ANY),
                pl.BlockSpec((MOE_NB * MOE_TB, ROUTE_W), lambda t, *_: (t, 0)),
                pl.BlockSpec((MOE_NB * MOE_TB, D_MODEL), lambda t, *_: (t, 0)),
                pl.BlockSpec((MOD_ROWS, 6 * D_MODEL), lambda t, *_: (0, 0)),
                pl.BlockSpec((MOE_TB, MOE_TB), lambda t, *_: (0, 0)),
                pl.BlockSpec((ROUTE_W, ROUTE_W), lambda t, *_: (0, 0)),
            ],
            out_specs=pl.BlockSpec((MOE_NB * MOE_TB, D_MODEL), lambda t, *_: (t, 0)),
            scratch_shapes=[
                pltpu.VMEM((2, MOE_NB, MOE_RLOC, D_MODEL), bf16),
                pltpu.SemaphoreType.DMA((2,)),
            ],
        ),
        out_shape=jax.ShapeDtypeStruct((n_blk * MOE_TB, D_MODEL), f32),
        compiler_params=_cparams("arbitrary"),
        name="moe_combine",
    )(tab, ys, comb, h, mod, ltri, utri)


def sparse_moe(n_rows, layer, v, comb, counts, h, mod, w_gate, w_up, w_down, ltri, ltri_plan, utri):
    n_blk = n_rows // MOE_TB
    cnt = jnp.pad(counts[:, 0, :], ((0, MOE_PLAN_ROWS - n_blk), (0, 0)))
    tab_d, tab_c, tile_exp, meta = moe_plan(n_blk, cnt, ltri_plan, utri)
    assert MOE_PLAN_ROWS == ROUTE_W
    xs = moe_dispatch(n_blk, tab_d, meta, v, comb, ltri.T, ltri_plan)
    ys = moe_experts(n_blk, layer, tile_exp, meta, xs,
                     w_gate.reshape(DEPTH * N_EXPERTS, D_MODEL, D_EXPERT),
                     w_up.reshape(DEPTH * N_EXPERTS, D_MODEL, D_EXPERT),
                     w_down.reshape(DEPTH * N_EXPERTS, D_EXPERT, D_MODEL))
    return moe_combine(n_blk, tab_c, ys, comb, h, mod, ltri, utri)


def _dft_tables(L):
    k = np.arange(L, dtype=np.int64)
    ang = (2.0 * np.pi / (2 * L)) * ((k[:, None] * k[None, :]) % (2 * L)).astype(np.float64)
    return np.cos(ang).astype(np.float32), np.sin(ang).astype(np.float32)


def _filter_features(L):
    bands = (HY_EMB - 1) // 2
    t = np.linspace(0.0, 1.0, L, dtype=np.float32).astype(np.float64)[:, None]
    w = (2.0 * np.pi / L) * np.arange(L, dtype=np.float64)[:, None]
    fb = np.linspace(1e-4, bands - 1, bands, dtype=np.float32).astype(np.float64)[None, :]
    z = np.concatenate([t, np.cos(fb * w), -np.sin(fb * w)], axis=-1)
    zp = np.zeros((L, FEAT_PAD), np.float32)
    zp[:, :HY_EMB] = z
    deltas = np.abs(np.linspace(HY_MIN_DECAY, HY_MAX_DECAY, HY_CH, dtype=np.float32).astype(np.float64))
    decay = np.exp(-t * deltas[None, :]).astype(np.float32)
    return zp, decay


def _rope_table(cos, sin, half, tm):
    S, width = cos.shape
    low = (np.arange(width) % (2 * half)) < half
    tab = np.zeros((3, S + tm, width), np.float32)
    tab[0, :S] = cos
    tab[0, S:] = 1.0
    tab[1, :S] = np.where(low[None, :], 0.0, sin)
    tab[2, :S] = np.where(low[None, :], -sin, 0.0)
    return tab


def _axial_rope_table(head_dim, tm):
    rows = SEQ // GRID_W
    nf = head_dim // 4
    row = np.repeat(np.arange(rows), GRID_W).astype(np.float64)
    col = np.tile(np.arange(GRID_W), rows).astype(np.float64)
    inv = ROPE_BASE ** (-np.arange(nf, dtype=np.float64) / nf)
    ang = np.stack([row[:, None] * inv, col[:, None] * inv], axis=1)
    a = np.broadcast_to(ang[:, :, None, :], (SEQ, 2, 2, nf)).reshape(SEQ, head_dim)
    reps = LANES // head_dim
    a = np.tile(a, (1, reps))
    return _rope_table(np.cos(a), np.sin(a), nf, tm)


def _seq_rope_table(head_dim, tm):
    inv = 1.0 / (ROPE_BASE ** np.linspace(0.0, 1.0, head_dim // 2, dtype=np.float32).astype(np.float64))
    ang = np.arange(SEQ, dtype=np.float64)[:, None] * inv
    a = np.concatenate([ang, ang], axis=1)
    return _rope_table(np.cos(a), np.sin(a), head_dim // 2, tm)


def _group_mean_matrix():
    g = np.arange(SEG) // DA_HD
    return (g[:, None] == g[None, :]).astype(np.float32) / DA_HD


def _router_weights(w_grp, b_grp, w_rt, b_rt):
    pad = ROUTE_W - 2 * N_EXPERTS
    wr = jnp.concatenate([w_rt, jnp.repeat(w_grp, EXP_PER_GROUP, axis=1),
                          jnp.zeros((D_MODEL, pad), f32)], axis=1)
    br = jnp.concatenate([b_rt, jnp.repeat(b_grp, EXP_PER_GROUP), jnp.zeros((pad,), f32)])[None, :]
    return wr, br


def _strict_lower(n):
    i = np.arange(n)
    return (i[None, :] < i[:, None]).astype(np.float32)


def kernel(x, c, ctx, c_ctx, ada_w, ada_b, e_w_in, e_w_out, hy_conv_w, hy_conv_b, hy_f_w1, hy_f_b1, hy_f_w2, hy_f_b2, hy_f_w3, hy_f_freq, hy_bias, da_q_norm, da_k_norm, da_lam, da_subln, o_w_in, o_w_out, ret_decay, ret_gn, gq_q_norm, gq_k_norm, gq_sink, moe_w_grp, moe_b_grp, moe_w_rt, moe_b_rt, moe_w_gate, moe_w_up, moe_w_down):
    assert x.shape == (BATCH, SEQ, D_MODEL) and ctx.shape == (BATCH, CTX_LEN, D_MODEL)
    x_rows = x.reshape(T_LAT, D_MODEL)
    ctx_rows = ctx.reshape(T_CTX, D_MODEL)
    c_rows = jnp.concatenate([c, c_ctx[None, :], jnp.zeros((MOD_ROWS - BATCH - 1, D_MODEL), f32)], axis=0)
    mod = ada_modulation(c_rows, ada_w, ada_b)

    gmat = jnp.asarray(_group_mean_matrix()).astype(bf16)
    ax_tab = jnp.asarray(_axial_rope_table(DA_HD, PROJ_TM))
    r1_tab = jnp.asarray(_seq_rope_table(RET_DK, PROJ_TM))
    ones = jnp.ones((SEG,), f32)

    lam_init0 = 0.8 - 0.6 * math.exp(-0.3 * 0)
    reps = SEG // DA_HD
    gain0 = jnp.concatenate([ones, ones, ones, jnp.tile(da_q_norm[0], reps) * (DA_HD ** -0.5 * LOG2E),
                             jnp.tile(da_k_norm[0], reps), ones])[None, :]
    proj0 = in_projection("even", [x_rows, ctx_rows], mod[0], e_w_in[0].astype(bf16), gain0, gmat, [ax_tab])

    w3r = hy_f_w3[0].reshape(HY_FILT_HID, 4, HY_CH).transpose(1, 0, 2)
    w1p = jnp.concatenate([hy_f_w1[0], jnp.zeros((FEAT_PAD - HY_EMB, HY_FILT_HID), f32)], axis=0)
    y_hy = []
    for L, blk0 in ((SEQ, 0), (CTX_LEN, T_LAT // CTX_LEN)):
        zfeat, decay = _filter_features(L)
        cm, sm = _dft_tables(L)
        cm = jnp.asarray(cm).astype(bf16)
        sm = jnp.asarray(sm).astype(bf16)
        spec, nyq = hyena_filter_spectra(L, jnp.asarray(zfeat), w1p, hy_f_b1[0][None, :], hy_f_w2[0],
                                         hy_f_b2[0][None, :], w3r, hy_f_freq[0], jnp.asarray(decay), cm, sm)
        y_hy.append(hyena_mix(L, blk0, proj0, hy_conv_w[0], hy_conv_b[0][None, :], spec, nyq, hy_bias[0], cm, sm))

    y_da = diff_attention(proj0, da_lam[0], da_subln[0][None, :], lam_init0)

    wr0, br0 = _router_weights(moe_w_grp[0], moe_b_grp[0], moe_w_rt[0], moe_b_rt[0])
    ltri = jnp.asarray(_strict_lower(MOE_TB)).astype(bf16)
    ltri_plan = jnp.asarray(_strict_lower(MOE_PLAN_ROWS)).astype(bf16)
    utri = jnp.asarray(_strict_lower(ROUTE_W).T).astype(bf16)
    h, v, comb, counts = out_projection(T_ALL, y_hy, y_da, e_w_out[0].astype(bf16), [x_rows, ctx_rows], mod[0],
                                        wr0, br0)
    h = sparse_moe(T_ALL, 0, v, comb, counts, h, mod[0], moe_w_gate, moe_w_up, moe_w_down, ltri, ltri_plan, utri)

    w_in1 = jnp.concatenate([o_w_in[0], jnp.zeros((D_MODEL, PROJ_W - o_w_in.shape[2]), f32)], axis=1).astype(bf16)
    kq = GQ_KV * GQ_HD
    gain1 = jnp.concatenate([ones, ones * RET_DK ** -0.5, ones, ones,
                             jnp.tile(gq_q_norm[0], reps) * (GQ_HD ** -0.5 * LOG2E),
                             jnp.tile(gq_k_norm[0], kq // GQ_HD), jnp.ones((SEG - kq,), f32)])[None, :]
    proj1 = in_projection("odd", [h], mod[1], w_in1, gain1, gmat, [ax_tab, r1_tab])
    y_ret = retention(proj1, ret_decay[0], ret_gn[0][None, :])
    y_gq = window_gqa(proj1, gq_sink[0])
    wr1, br1 = _router_weights(moe_w_grp[1], moe_b_grp[1], moe_w_rt[1], moe_b_rt[1])
    h_lat, v, comb, counts = out_projection(T_LAT, [y_ret], [y_gq], o_w_out[0].astype(bf16), [h], mod[1], wr1, br1)
    out = sparse_moe(T_LAT, 1, v, comb, counts, h_lat, mod[1], moe_w_gate, moe_w_up, moe_w_down, ltri, ltri_plan,
                     utri)
    return out.reshape(BATCH, SEQ, D_MODEL)
```

```python
import functools
import math

import numpy as np
import jax
import jax.numpy as jnp
from jax import lax
from jax.experimental import pallas as pl
from jax.experimental.pallas import tpu as pltpu

f32 = jnp.float32
bf16 = jnp.bfloat16

D_MODEL = 1024
BATCH = 8
SEQ = 2048
DEPTH = 2
GRID_W = 64
CTX_LEN = 256
EPS = 1e-6
NEG_INF = -1e30
LOG2E = math.log2(math.e)
ROPE_BASE = 10000.0
HY_CH = D_MODEL // 2
HY_EMB = 33
HY_FILT_HID = 64
HY_MAX_DECAY = math.log(1e-2) / 0.3
HY_MIN_DECAY = math.log(1e-2) / 1.5
DA_HEADS = 4
DA_HD = D_MODEL // 16
RET_HEADS = 4
RET_DK = D_MODEL // 8
RET_CHUNK = 128
GQ_KV = 2
GQ_GROUP = 4
GQ_HD = D_MODEL // 16
WINDOW = 128
N_GROUPS = 4
EXP_PER_GROUP = 8
N_EXPERTS = N_GROUPS * EXP_PER_GROUP
D_EXPERT = D_MODEL // 4

T_LAT = BATCH * SEQ
T_CTX = BATCH * CTX_LEN
T_ALL = T_LAT + T_CTX
PROJ_W = 3072
SEG = 512
CTX_MOD_ROW = BATCH
MOD_ROWS = 16

LANES = 128
VMEM_LIMIT_BYTES = 56 * 1024 * 1024


def _cparams(*sem):
    return pltpu.CompilerParams(dimension_semantics=sem, vmem_limit_bytes=VMEM_LIMIT_BYTES)


def _dot(a, b):
    return jnp.dot(a, b, preferred_element_type=f32)


def _dot_nt(a, b):
    return lax.dot_general(a, b, (((1,), (1,)), ((), ())), preferred_element_type=f32)


def _split(x):
    hi = x.astype(bf16)
    lo = (x - hi.astype(f32)).astype(bf16)
    return hi, lo


def _dot3(a, b):
    ah, al = _split(a)
    bh, bl = _split(b)
    return _dot(ah, bh) + _dot(al, bh) + _dot(ah, bl)


def _silu(x):
    return x * jax.nn.sigmoid(x)


def _rms(x):
    return x * lax.rsqrt(jnp.mean(x * x, axis=-1, keepdims=True) + EPS)


def _const_spec(shape):
    nd = len(shape)
    return pl.BlockSpec(shape, lambda *_: (0,) * nd)


def _const_spec1(shape):
    nd = len(shape)
    return pl.BlockSpec(shape, lambda *_: (0,) * nd, pipeline_mode=pl.Buffered(1))


ADA_TN = 1536


def _ada_kernel(c_ref, w_ref, b_ref, o_ref):
    x = _silu(c_ref[...])
    o_ref[0] = _dot3(x, w_ref[0]) + b_ref[0]


def ada_modulation(c_rows, ada_w, ada_b):
    n = 6 * D_MODEL
    return pl.pallas_call(
        _ada_kernel,
        grid=(DEPTH, n // ADA_TN),
        in_specs=[
            pl.BlockSpec((MOD_ROWS, D_MODEL), lambda l, j: (0, 0)),
            pl.BlockSpec((1, D_MODEL, ADA_TN), lambda l, j: (l, 0, j)),
            pl.BlockSpec((1, 1, ADA_TN), lambda l, j: (l, 0, j)),
        ],
        out_specs=pl.BlockSpec((1, MOD_ROWS, ADA_TN), lambda l, j: (l, 0, j)),
        out_shape=jax.ShapeDtypeStruct((DEPTH, MOD_ROWS, n), f32),
        compiler_params=_cparams("arbitrary", "arbitrary"),
        name="ada_modulation",
    )(c_rows, ada_w, ada_b.reshape(DEPTH, 1, n))


PROJ_TM = 1024


def _mod_row(i, tm):
    return jnp.minimum((i * tm) // SEQ, CTX_MOD_ROW)


def _tile4(t):
    return jnp.concatenate([t, t, t, t], axis=1)


def _group_norm64(y, gmat):
    ms = _dot((y * y).astype(bf16), gmat)
    return y * lax.rsqrt(ms + EPS)


def _rope(y, tab, shift):
    w = y.shape[1]
    return y * tab[0] + pltpu.roll(y, shift, 1) * tab[1] + pltpu.roll(y, w - shift, 1) * tab[2]


def _stacked_specs(tm, width):
    n_lat = T_LAT // tm
    return [pl.BlockSpec((tm, width), lambda i: (jnp.minimum(i, n_lat - 1), 0)),
            pl.BlockSpec((tm, width), lambda i: (jnp.maximum(i - n_lat, 0), 0))]


def _stacked_tile(i, tm, lat_ref, ctx_ref):
    return jnp.where(i < T_LAT // tm, lat_ref[...], ctx_ref[...])


def _inproj_kernel(layer_kind, *refs):
    i = pl.program_id(0)
    if layer_kind == "even":
        x_ref, c_ref, mod_ref, w_ref, gain_ref, gmat_ref, ax_ref, o_ref = refs
        h = _stacked_tile(i, PROJ_TM, x_ref, c_ref)
    else:
        h_ref, mod_ref, w_ref, gain_ref, gmat_ref, ax_ref, r1_ref, o_ref = refs
        h = h_ref[...]
    r = _mod_row(i, PROJ_TM)
    sh = mod_ref[pl.ds(r, 1), pl.ds(0, D_MODEL)]
    sc = mod_ref[pl.ds(r, 1), pl.ds(D_MODEL, D_MODEL)]
    u = (_rms(h) * (1.0 + sc) + sh).astype(bf16)

    def seg(j):
        return _dot(u, w_ref[:, j * SEG:(j + 1) * SEG])

    def put(j, y):
        o_ref[:, j * SEG:(j + 1) * SEG] = y.astype(bf16)

    def gain(j):
        return gain_ref[:, j * SEG:(j + 1) * SEG]

    gmat = gmat_ref[...]
    ax = ax_ref[...]
    ax4 = (_tile4(ax[0]), _tile4(ax[1]), _tile4(ax[2]))
    if layer_kind == "even":
        for j in (3, 4):
            put(j, _rope(_group_norm64(seg(j), gmat) * gain(j), ax4, DA_HD // 4))
        for j in (0, 1, 2, 5):
            put(j, seg(j))
    else:
        r1 = r1_ref[...]
        r14 = (_tile4(r1[0]), _tile4(r1[1]), _tile4(r1[2]))
        put(4, _rope(_group_norm64(seg(4), gmat) * gain(4), ax4, GQ_HD // 4))
        y = seg(5)
        kw = GQ_KV * GQ_HD
        yk = _rope(_group_norm64(y[:, :kw], gmat[:kw, :kw]) * gain(5)[:, :kw], ax, GQ_HD // 4)
        yv = y[:, kw:2 * kw]
        pieces = (yk, yv, pltpu.roll(yk, GQ_HD, 1), pltpu.roll(yv, GQ_HD, 1))
        for p, piece in enumerate(pieces):
            o_ref[:, 5 * SEG + p * kw:5 * SEG + (p + 1) * kw] = piece.astype(bf16)
        for j in (0, 1):
            put(j, _rope(seg(j) * gain(j), r14, RET_DK // 2))
        for j in (2, 3):
            put(j, seg(j))


def in_projection(layer_kind, hs, mod, w, gain, gmat, tables):
    n_lat_tiles = T_LAT // PROJ_TM
    n_pos_tiles = SEQ // PROJ_TM

    def tab_map(i):
        return (0, jnp.where(i < n_lat_tiles, i % n_pos_tiles, n_pos_tiles), 0)

    tab_specs = [pl.BlockSpec((3, PROJ_TM, LANES), tab_map) for _ in tables]
    if layer_kind == "even":
        h_specs = _stacked_specs(PROJ_TM, D_MODEL)
    else:
        h_specs = [pl.BlockSpec((PROJ_TM, D_MODEL), lambda i: (i, 0))]
    return pl.pallas_call(
        functools.partial(_inproj_kernel, layer_kind),
        grid=(T_ALL // PROJ_TM,),
        in_specs=h_specs + [
            _const_spec((MOD_ROWS, 6 * D_MODEL)),
            _const_spec((D_MODEL, PROJ_W)),
            _const_spec((1, PROJ_W)),
            _const_spec((SEG, SEG)),
        ] + tab_specs,
        out_specs=pl.BlockSpec((PROJ_TM, PROJ_W), lambda i: (i, 0)),
        out_shape=jax.ShapeDtypeStruct((T_ALL, PROJ_W), bf16),
        compiler_params=_cparams("parallel"),
        name="in_projection_" + layer_kind,
    )(*hs, mod, w, gain, gmat, *tables)


HY_TC = 256
HY_FREQ_CHUNK = 512
FEAT_PAD = 64


def _alt_sign(shape, axis):
    idx = lax.broadcasted_iota(jnp.int32, shape, axis)
    return jnp.where((idx & 1) == 0, 1.0, -1.0).astype(f32)


def _filter_kernel(L, z_ref, w1_ref, b1_ref, w2_ref, b2_ref, wf_ref, wb_ref, freq_ref, dec_ref, c_ref, s_ref,
                   spec_ref, nyq_ref):
    hid = jnp.sin(freq_ref[0:1, :] * (_dot3(z_ref[...], w1_ref[...]) + b1_ref[...]))
    hid = jnp.sin(freq_ref[1:2, :] * (_dot3(hid, w2_ref[...]) + b2_ref[...]))
    dec = dec_ref[...]
    fwd = _dot3(hid, wf_ref[0]) * dec
    bwd = _dot3(hid, wb_ref[0]) * dec
    row = lax.broadcasted_iota(jnp.int32, fwd.shape, 0)
    bwd = jnp.where(row == 0, 0.0, bwd)
    even = fwd + bwd
    odd = bwd - fwd
    wk = jnp.where(row == 0, 0.5 / L, 1.0 / L).astype(f32)
    spec_ref[0, 0] = _dot(c_ref[...], even.astype(bf16)) * wk
    spec_ref[0, 1] = _dot(s_ref[...], odd.astype(bf16)) * wk
    nyq = jnp.sum(even * _alt_sign(even.shape, 0), axis=0, keepdims=True) * (0.5 / L)
    nyq_ref[0] = jnp.broadcast_to(nyq, (8, nyq.shape[1]))


def hyena_filter_spectra(L, zfeat, w1, b1, w2, b2, w3r, freq, decay, cmat, smat):
    nct = HY_CH // HY_TC
    return pl.pallas_call(
        functools.partial(_filter_kernel, L),
        grid=(2, nct),
        in_specs=[
            _const_spec((L, FEAT_PAD)),
            _const_spec((FEAT_PAD, HY_FILT_HID)),
            _const_spec((1, HY_FILT_HID)),
            _const_spec((HY_FILT_HID, HY_FILT_HID)),
            _const_spec((1, HY_FILT_HID)),
            pl.BlockSpec((1, HY_FILT_HID, HY_TC), lambda n, c: (2 * n, 0, c)),
            pl.BlockSpec((1, HY_FILT_HID, HY_TC), lambda n, c: (2 * n + 1, 0, c)),
            _const_spec((2, HY_FILT_HID)),
            pl.BlockSpec((L, HY_TC), lambda n, c: (0, c)),
            _const_spec1((L, L)),
            _const_spec1((L, L)),
        ],
        out_specs=[
            pl.BlockSpec((1, 2, L, HY_TC), lambda n, c: (n, 0, 0, c)),
            pl.BlockSpec((1, 8, HY_TC), lambda n, c: (n, 0, c)),
        ],
        out_shape=[
            jax.ShapeDtypeStruct((2, 2, L, HY_CH), f32),
            jax.ShapeDtypeStruct((2, 8, HY_CH), f32),
        ],
        compiler_params=_cparams("arbitrary", "arbitrary"),
        name="hyena_filter_L%d" % L,
    )(zfeat, w1, b1, w2, b2, w3r, w3r, freq, decay, cmat, smat)


def _conv3(u, w, b):
    L = u.shape[0]
    row = lax.broadcasted_iota(jnp.int32, u.shape, 0)
    prev = jnp.where(row == 0, 0.0, pltpu.roll(u, 1, 0))
    nxt = jnp.where(row == L - 1, 0.0, pltpu.roll(u, L - 1, 0))
    return prev * w[0:1, :] + u * w[1:2, :] + nxt * w[2:3, :] + b


def _hyena_kernel(v_ref, x1_ref, x2_ref, wv_ref, w1_ref, w2_ref, bv_ref, b1_ref, b2_ref, spec_ref, nyq_ref,
                  bias_ref, c_ref, s_ref, o_ref, yr_ref, yi_ref):
    L = v_ref.shape[0]
    fch = min(L, HY_FREQ_CHUNK)
    z = _conv3(v_ref[...].astype(f32), wv_ref[...], bv_ref[...])
    gate_refs = ((x1_ref, w1_ref, b1_ref), (x2_ref, w2_ref, b2_ref))
    alt = _alt_sign(z.shape, 0)
    for n in range(2):
        zb = z.astype(bf16)
        for k in range(L // fch):
            rows = slice(k * fch, (k + 1) * fch)
            a = _dot(c_ref[rows, :], zb)
            b = _dot(s_ref[rows, :], zb)
            hr = spec_ref[n, 0, rows, :]
            hi = spec_ref[n, 1, rows, :]
            yr_ref[rows, :] = (a * hr + b * hi).astype(bf16)
            yi_ref[rows, :] = (a * hi - b * hr).astype(bf16)
        x_nyq = jnp.sum(z * alt, axis=0, keepdims=True)
        y = (_dot(c_ref[...], yr_ref[...]) - _dot(s_ref[...], yi_ref[...])
             + alt * (x_nyq * nyq_ref[n, 0:1, :]))
        x_ref, w_ref, b_ref = gate_refs[n]
        gate = _conv3(x_ref[...].astype(f32), w_ref[...], b_ref[...])
        z = gate * (y + z * bias_ref[n:n + 1, :])
    o_ref[...] = z.astype(bf16)


def hyena_mix(L, row_block0, proj, conv_w, conv_b, spec, nyq, bias, cmat, smat):
    nct = HY_CH // HY_TC
    nseg = HY_CH // HY_TC

    def col(k):
        return lambda c, b: (row_block0 + b, k * nseg + c)

    def par(k):
        return lambda c, b: (0, k * nseg + c)

    in_specs = (
        [pl.BlockSpec((L, HY_TC), col(k)) for k in range(3)]
        + [pl.BlockSpec((3, HY_TC), par(k)) for k in range(3)]
        + [pl.BlockSpec((1, HY_TC), par(k)) for k in range(3)]
        + [
            pl.BlockSpec((2, 2, L, HY_TC), lambda c, b: (0, 0, 0, c), pipeline_mode=pl.Buffered(1)),
            pl.BlockSpec((2, 8, HY_TC), lambda c, b: (0, 0, c)),
            pl.BlockSpec((2, HY_TC), lambda c, b: (0, c)),
            _const_spec1((L, L)),
            _const_spec1((L, L)),
        ]
    )
    args = [proj, proj, proj, conv_w, conv_w, conv_w, conv_b, conv_b, conv_b, spec, nyq, bias, cmat, smat]
    return pl.pallas_call(
        _hyena_kernel,
        grid=(nct, BATCH),
        in_specs=in_specs,
        out_specs=pl.BlockSpec((L, HY_TC), lambda c, b: (b, c)),
        out_shape=jax.ShapeDtypeStruct((BATCH * L, HY_CH), bf16),
        scratch_shapes=[pltpu.VMEM((L, HY_TC), bf16), pltpu.VMEM((L, HY_TC), bf16)],
        compiler_params=_cparams("arbitrary", "arbitrary"),
        name="hyena_mix_L%d" % L,
    )(*args)


DA_TQ = 512


def _diff_attn_kernel(lam_init, q_ref, qc_ref, kc_ref, vc_ref, kl_ref, vl_ref, lam_ref, subln_ref, o_ref, oc_ref):
    i = pl.program_id(1)
    n_lat_blocks = SEQ // DA_TQ

    @pl.when(i < n_lat_blocks)
    def _():
        _diff_attn_body(lam_init, q_ref, (kc_ref, vc_ref, kl_ref, vl_ref), lam_ref, subln_ref, o_ref)

    @pl.when(i == n_lat_blocks)
    def _():
        _diff_attn_body(lam_init, qc_ref, (kc_ref, vc_ref), lam_ref, subln_ref, oc_ref)


def _diff_attn_body(lam_init, q_ref, kv_refs, lam_ref, subln_ref, o_ref):
    n_src = len(kv_refs) // 2
    lp = lam_ref[...]
    lam = (jnp.exp(jnp.sum(lp[0:1] * lp[1:2], axis=-1, keepdims=True))
           - jnp.exp(jnp.sum(lp[2:3] * lp[3:4], axis=-1, keepdims=True)) + lam_init)
    q = q_ref[...]
    tq = q.shape[0]
    lower = lax.broadcasted_iota(jnp.int32, (tq, 2 * DA_HD), 1) < DA_HD
    zero = jnp.zeros((), bf16)
    hw = 2 * DA_HD
    outs = []
    for h in range(DA_HEADS):
        qh = q[:, h * hw:(h + 1) * hw]
        ks = [kv_refs[2 * s][:, h * hw:(h + 1) * hw] for s in range(n_src)]
        vs = [kv_refs[2 * s + 1][:, h * hw:(h + 1) * hw] for s in range(n_src)]
        qs = jnp.concatenate([jnp.where(lower, qh, zero), jnp.where(lower, zero, qh)], axis=0)
        ss = [_dot_nt(qs, k) for k in ks]
        mx = functools.reduce(jnp.maximum, [jnp.max(s, axis=-1, keepdims=True) for s in ss])
        es = [jnp.exp2(s - mx) for s in ss]
        pv = functools.reduce(jnp.add, [
            _dot(es[s].astype(bf16), jnp.concatenate([vs[s], jnp.ones_like(vs[s])], axis=1)) for s in range(n_src)])
        pv = pv[:, :hw] * (1.0 / pv[:, hw:])
        oh = pv[:tq] - lam * pv[tq:]
        outs.append(_rms(oh) * subln_ref[...] * (1.0 - lam_init))
    o_ref[...] = jnp.concatenate(outs, axis=1).astype(bf16)


def diff_attention(proj, lam_p, subln, lam_init):
    width = DA_HEADS * 2 * DA_HD
    qcol, kcol, vcol = 3, 4, 5
    ctx_blk0 = T_LAT // CTX_LEN
    nq = SEQ // DA_TQ

    def lat_rows(b, i):
        return b * nq + jnp.minimum(i, nq - 1)

    return pl.pallas_call(
        functools.partial(_diff_attn_kernel, lam_init),
        grid=(BATCH, nq + 1),
        in_specs=[
            pl.BlockSpec((DA_TQ, width), lambda b, i: (lat_rows(b, i), qcol)),
            pl.BlockSpec((CTX_LEN, width), lambda b, i: (ctx_blk0 + b, qcol)),
            pl.BlockSpec((CTX_LEN, width), lambda b, i: (ctx_blk0 + b, kcol)),
            pl.BlockSpec((CTX_LEN, width), lambda b, i: (ctx_blk0 + b, vcol)),
            pl.BlockSpec((SEQ, width), lambda b, i: (b, kcol)),
            pl.BlockSpec((SEQ, width), lambda b, i: (b, vcol)),
            _const_spec((4, DA_HD)),
            _const_spec((1, 2 * DA_HD)),
        ],
        out_specs=[
            pl.BlockSpec((DA_TQ, width), lambda b, i: (lat_rows(b, i), 0)),
            pl.BlockSpec((CTX_LEN, width), lambda b, i: (b, 0)),
        ],
        out_shape=[
            jax.ShapeDtypeStruct((T_LAT, width), bf16),
            jax.ShapeDtypeStruct((T_CTX, width), bf16),
        ],
        compiler_params=_cparams("parallel", "arbitrary"),
        name="diff_attention",
    )(proj, proj, proj, proj, proj, proj, lam_p, subln)


def _log_sigmoid(x):
    return jnp.minimum(x, 0.0) - jnp.log(1.0 + jnp.exp(-jnp.abs(x)))


def _retention_kernel(q_ref, k_ref, v_ref, g_ref, kc_ref, vc_ref, decay_ref, gn_ref, o_ref, st_ref):
    h = pl.program_id(1)
    ch = RET_CHUNK
    nchunk = SEQ // ch
    lgs = _log_sigmoid(decay_ref[...])
    sel = lax.broadcasted_iota(jnp.int32, lgs.shape, 1) == h
    lg = jnp.sum(jnp.where(sel, lgs, 0.0), axis=-1, keepdims=True)
    lgf = lg[0:1, :]
    lgb = lg[1:2, :]
    ri = lax.broadcasted_iota(jnp.int32, (ch, ch), 0).astype(f32)
    ci = lax.broadcasted_iota(jnp.int32, (ch, ch), 1).astype(f32)
    rel = ri - ci
    dsum = (jnp.where(rel >= 0, jnp.exp(jnp.maximum(rel, 0.0) * lgf), 0.0)
            + jnp.where(rel <= 0, jnp.exp(jnp.maximum(-rel, 0.0) * lgb), 0.0))
    zeta_f = jnp.exp((ch - 1 - ci) * lgf)
    zeta_b = jnp.exp(ci * lgb)
    xi_f = jnp.exp((ri + 1.0) * lgf)
    xi_b = jnp.exp((ch - ri) * lgb)
    gch_f = jnp.exp(ch * lgf)
    gch_b = jnp.exp(ch * lgb)
    dk = q_ref.shape[1]

    kct = kc_ref[...].astype(f32).T
    vc = vc_ref[...]
    cl = lax.broadcasted_iota(jnp.int32, kct.shape, 1).astype(f32)
    s_f = _dot((kct * jnp.exp((CTX_LEN - 1 - cl) * lgf)).astype(bf16), vc)
    s_b = _dot((kct * jnp.exp(cl * lgb)).astype(bf16), vc)

    def rows(n):
        return slice(n * ch, (n + 1) * ch)

    u_f, u_b = [], []
    for n in range(nchunk):
        kt = k_ref[rows(n), :].astype(f32).T
        vn = v_ref[rows(n), :]
        u_f.append(_dot((kt * zeta_f).astype(bf16), vn))
        u_b.append(_dot((kt * zeta_b).astype(bf16), vn))

    for n in range(nchunk):
        st_ref[n, 0:dk, :] = s_f.astype(bf16)
        s_f = gch_f * s_f + u_f[n]
    for n in reversed(range(nchunk)):
        st_ref[n, dk:2 * dk, :] = s_b.astype(bf16)
        s_b = gch_b * s_b + u_b[n]

    gn = gn_ref[...]
    for n in range(nchunk):
        qn = q_ref[rows(n), :]
        att = _dot_nt(qn, k_ref[rows(n), :]) * dsum
        qf = qn.astype(f32)
        lhs = jnp.concatenate([att.astype(bf16), (qf * xi_f).astype(bf16), (qf * xi_b).astype(bf16)], axis=1)
        rhs = jnp.concatenate([v_ref[rows(n), :], st_ref[n]], axis=0)
        o = _dot(lhs, rhs)
        mu = jnp.mean(o, axis=-1, keepdims=True)
        oc = o - mu
        var = jnp.mean(oc * oc, axis=-1, keepdims=True)
        y = oc * lax.rsqrt(var + EPS) * gn * _silu(g_ref[rows(n), :].astype(f32))
        o_ref[rows(n), :] = y.astype(bf16)


def retention(proj, decay, gn_w):
    dk = RET_DK
    ctx_blk0 = T_LAT // CTX_LEN
    return pl.pallas_call(
        _retention_kernel,
        grid=(BATCH, RET_HEADS),
        in_specs=[
            pl.BlockSpec((SEQ, dk), lambda b, h: (b, h)),
            pl.BlockSpec((SEQ, dk), lambda b, h: (b, RET_HEADS + h)),
            pl.BlockSpec((SEQ, dk), lambda b, h: (b, 2 * RET_HEADS + h)),
            pl.BlockSpec((SEQ, dk), lambda b, h: (b, 3 * RET_HEADS + h)),
            pl.BlockSpec((CTX_LEN, dk), lambda b, h: (ctx_blk0 + b, RET_HEADS + h)),
            pl.BlockSpec((CTX_LEN, dk), lambda b, h: (ctx_blk0 + b, 2 * RET_HEADS + h)),
            _const_spec((2, RET_HEADS)),
            pl.BlockSpec((1, dk), lambda b, h: (0, h)),
        ],
        out_specs=pl.BlockSpec((SEQ, dk), lambda b, h: (b, h)),
        out_shape=jax.ShapeDtypeStruct((T_LAT, RET_HEADS * dk), bf16),
        scratch_shapes=[pltpu.VMEM((SEQ // RET_CHUNK, 2 * dk, dk), bf16)],
        compiler_params=_cparams("parallel", "arbitrary"),
        name="retention",
    )(proj, proj, proj, proj, proj, proj, decay, gn_w)


GQ_TQ = 128
GQ_SPAN = 3 * GQ_TQ
GQ_NB = 8


def _gqa_kernel(q_ref, kv_ref, kvc_ref, sink_ref, o_ref):
    for b in range(GQ_NB):
        rows = slice(b * GQ_TQ, (b + 1) * GQ_TQ)
        o_ref[rows, :] = _gqa_block(pl.program_id(1) * GQ_NB + b, q_ref[rows, :], kv_ref, kvc_ref, sink_ref)


def _gqa_block(n, q, kv_ref, kvc_ref, sink_ref):
    start = pl.multiple_of(jnp.clip((n - 1) * GQ_TQ, 0, SEQ - GQ_SPAN), GQ_TQ)
    pw = 2 * GQ_HD
    n_heads = GQ_KV * GQ_GROUP
    kpos = start + lax.broadcasted_iota(jnp.int32, (GQ_TQ, GQ_SPAN), 1)
    qpos = n * GQ_TQ + lax.broadcasted_iota(jnp.int32, (GQ_TQ, GQ_SPAN), 0)
    mask = jnp.abs(kpos - qpos) <= WINDOW
    lower = lax.broadcasted_iota(jnp.int32, (GQ_TQ, pw), 1) < GQ_HD
    outs = [None] * n_heads
    for swapped in (0, 1):
        kcol = slice(2 * swapped * pw, (2 * swapped + 1) * pw)
        vcol = slice((2 * swapped + 1) * pw, (2 * swapped + 2) * pw)
        k = jnp.concatenate([kvc_ref[:, kcol], kv_ref[pl.ds(start, GQ_SPAN), kcol]], axis=0)
        v = jnp.concatenate([kvc_ref[:, vcol], kv_ref[pl.ds(start, GQ_SPAN), vcol]], axis=0)
        heads = [h for h in range(n_heads) if ((h // GQ_GROUP) == (h % 2)) == (swapped == 0)]
        qs = jnp.concatenate(
            [jnp.where(lower == (h % 2 == 0), q[:, (h // 2) * pw:(h // 2 + 1) * pw], jnp.zeros((), bf16))
             for h in heads], axis=0)
        s = _dot_nt(qs, k)
        es, sink_terms = [], []
        for i, h in enumerate(heads):
            sh = s[i * GQ_TQ:(i + 1) * GQ_TQ]
            sh = jnp.concatenate([sh[:, :CTX_LEN], jnp.where(mask, sh[:, CTX_LEN:], NEG_INF)], axis=1)
            sink = sink_ref[h] * LOG2E
            mx = jnp.maximum(jnp.max(sh, axis=-1, keepdims=True), sink)
            es.append(jnp.exp2(sh - mx).astype(bf16))
            sink_terms.append(jnp.exp2(sink - mx))
        o = _dot(jnp.concatenate(es, axis=0), jnp.concatenate([v, jnp.ones_like(v)], axis=1))
        for i, h in enumerate(heads):
            oh = o[i * GQ_TQ:(i + 1) * GQ_TQ]
            outs[h] = oh[:, :pw] * (1.0 / (oh[:, pw:] + sink_terms[i]))
    return jnp.concatenate(
        [jnp.where(lower, outs[2 * j], outs[2 * j + 1]) for j in range(n_heads // 2)], axis=1).astype(bf16)


def window_gqa(proj, sink):
    width = GQ_KV * GQ_GROUP * GQ_HD
    nq = SEQ // (GQ_NB * GQ_TQ)
    kvw = 4 * GQ_KV * GQ_HD
    kv_col = (5 * SEG) // kvw
    ctx_blk0 = T_LAT // CTX_LEN
    return pl.pallas_call(
        _gqa_kernel,
        grid=(BATCH, nq),
        in_specs=[
            pl.BlockSpec((GQ_NB * GQ_TQ, width), lambda b, n: (b * nq + n, 4)),
            pl.BlockSpec((SEQ, kvw), lambda b, n: (b, kv_col)),
            pl.BlockSpec((CTX_LEN, kvw), lambda b, n: (ctx_blk0 + b, kv_col)),
            pl.BlockSpec(memory_space=pltpu.SMEM),
        ],
        out_specs=pl.BlockSpec((GQ_NB * GQ_TQ, width), lambda b, n: (b * nq + n, 0)),
        out_shape=jax.ShapeDtypeStruct((T_LAT, width), bf16),
        compiler_params=_cparams("parallel", "arbitrary"),
        name="window_gqa",
    )(proj, proj, proj, sink)


OUT_TM = 1024
ROUTE_W = LANES
MOE_TB = 256


def _route(logits):
    lane_i = lax.broadcasted_iota(jnp.int32, logits.shape, 1)
    lane = lane_i.astype(f32)
    big = float(1 << 20)
    valid = lane_i < N_EXPERTS
    le = logits
    lgx = pltpu.roll(logits, ROUTE_W - N_EXPERTS, 1)
    lgx = jnp.where(valid, lgx, NEG_INF)
    gmax = jnp.max(lgx, axis=-1, keepdims=True)
    grp = (lane_i // EXP_PER_GROUP).astype(f32)
    g_sel = jnp.min(jnp.where(lgx == gmax, grp, big), axis=-1, keepdims=True)
    p_grp = float(EXP_PER_GROUP) / jnp.sum(jnp.exp(lgx - gmax), axis=-1, keepdims=True)
    lm = jnp.where(valid, jnp.where(grp == g_sel, le, NEG_INF), NEG_INF)
    v1 = jnp.max(lm, axis=-1, keepdims=True)
    i1 = jnp.min(jnp.where(lm == v1, lane, big), axis=-1, keepdims=True)
    lm2 = jnp.where(lane == i1, NEG_INF, lm)
    v2 = jnp.max(lm2, axis=-1, keepdims=True)
    i2 = jnp.min(jnp.where(lm2 == v2, lane, big), axis=-1, keepdims=True)
    e2 = jnp.exp(v2 - v1)
    w1 = p_grp / (1.0 + e2)
    w2 = w1 * e2
    return jnp.where(lane == i1, w1, 0.0) + jnp.where(lane == i2, w2, 0.0)


def _outproj_kernel(stacked, *refs):
    i = pl.program_id(0)
    if stacked:
        (ya_ref, yac_ref, yb_ref, ybc_ref, w_ref, x_ref, c_ref, mod_ref, wr_ref, br_ref,
         hn_ref, v_ref, comb_ref, cnt_ref) = refs
        ya = _stacked_tile(i, OUT_TM, ya_ref, yac_ref)
        yb = _stacked_tile(i, OUT_TM, yb_ref, ybc_ref)
        h = _stacked_tile(i, OUT_TM, x_ref, c_ref)
    else:
        ya_ref, yb_ref, w_ref, h_ref, mod_ref, wr_ref, br_ref, hn_ref, v_ref, comb_ref, cnt_ref = refs
        ya = ya_ref[...]
        yb = yb_ref[...]
        h = h_ref[...]
    r = _mod_row(i, OUT_TM)
    g1 = mod_ref[pl.ds(r, 1), pl.ds(2 * D_MODEL, D_MODEL)]
    sh2 = mod_ref[pl.ds(r, 1), pl.ds(3 * D_MODEL, D_MODEL)]
    sc2 = mod_ref[pl.ds(r, 1), pl.ds(4 * D_MODEL, D_MODEL)]
    half = ya.shape[1]
    m = _dot(ya, w_ref[0:half, :]) + _dot(yb, w_ref[half:2 * half, :])
    hn = h + g1 * m
    hn_ref[...] = hn
    v = _rms(hn) * (1.0 + sc2) + sh2
    v_ref[...] = v.astype(bf16)
    vh, vl = _split(v)
    prod = _dot(jnp.concatenate([vh, vl], axis=0), wr_ref[...])
    tm = v.shape[0]
    comb = _route(prod[:tm, :ROUTE_W] + prod[:tm, ROUTE_W:] + prod[tm:, :ROUTE_W] + br_ref[...])
    comb_ref[...] = comb
    for s in range(OUT_TM // MOE_TB):
        cnt = jnp.sum((comb[s * MOE_TB:(s + 1) * MOE_TB] != 0.0).astype(f32), axis=0, keepdims=True)
        cnt_ref[s] = jnp.broadcast_to(cnt, (8, ROUTE_W)).astype(jnp.int32)


def out_projection(n_rows, yas, ybs, w, hs, mod, wr, br):
    stacked = len(yas) == 2
    assert stacked == (len(hs) == 2) == (len(ybs) == 2) and (not stacked or n_rows == T_ALL)
    half = ybs[0].shape[1]
    row_spec = lambda width: [pl.BlockSpec((OUT_TM, width), lambda i: (i, 0))]
    rows = lambda width: _stacked_specs(OUT_TM, width) if stacked else row_spec(width)
    wr_hi = wr.astype(bf16)
    wr_lo = (wr - wr_hi.astype(f32)).astype(bf16)
    return pl.pallas_call(
        functools.partial(_outproj_kernel, stacked),
        grid=(n_rows // OUT_TM,),
        in_specs=(
            rows(half) + rows(half)
            + [_const_spec((2 * half, D_MODEL))]
            + rows(D_MODEL)
            + [_const_spec((MOD_ROWS, 6 * D_MODEL)),
               _const_spec((D_MODEL, 2 * ROUTE_W)),
               _const_spec((1, ROUTE_W))]
        ),
        out_specs=[
            pl.BlockSpec((OUT_TM, D_MODEL), lambda i: (i, 0)),
            pl.BlockSpec((OUT_TM, D_MODEL), lambda i: (i, 0)),
            pl.BlockSpec((OUT_TM, ROUTE_W), lambda i: (i, 0)),
            pl.BlockSpec((OUT_TM // MOE_TB, 8, ROUTE_W), lambda i: (i, 0, 0)),
        ],
        out_shape=[
            jax.ShapeDtypeStruct((n_rows, D_MODEL), f32),
            jax.ShapeDtypeStruct((n_rows, D_MODEL), bf16),
            jax.ShapeDtypeStruct((n_rows, ROUTE_W), f32),
            jax.ShapeDtypeStruct((n_rows // MOE_TB, 8, ROUTE_W), jnp.int32),
        ],
        compiler_params=_cparams("parallel"),
        name="out_projection",
    )(*yas, *ybs, w, *hs, mod, jnp.concatenate([wr_hi, wr_lo], axis=1), br)


MOE_UNIT = 16
MOE_TG = 1024
MOE_TOP = 2
MOE_RLOC = MOE_TOP * MOE_TB + N_EXPERTS * MOE_UNIT
MOE_NUNIT = MOE_RLOC // MOE_UNIT
MOE_NB = 4
MOE_SPARE = 2 * MOE_NB
MOE_META = 128


def _moe_rows(n_blk):
    return n_blk * MOE_RLOC + N_EXPERTS * MOE_TG


MOE_PLAN_ROWS = 128
MOE_PLAN_TILES = 256


def _lane_pick(x, lane, k):
    return jnp.sum(jnp.where(lane == k, x, 0.0), axis=1, keepdims=True)


def _moe_plan_kernel(n_blk, cnt_ref, ltri_ref, utri_ref, tabd_ref, tabc_ref, te_ref, meta_ref):
    shape = (MOE_PLAN_ROWS, ROUTE_W)
    lane = lax.broadcasted_iota(jnp.int32, shape, 1)
    cnt = cnt_ref[...].astype(f32)
    units = jnp.floor((cnt + (MOE_UNIT - 1.0)) * (1.0 / MOE_UNIT))
    ub = units.astype(bf16)
    utri = utri_ref[...]
    pre = _dot(ltri_ref[...], ub)
    lstart = _dot(ub, utri)
    n_e = jnp.sum(units, axis=0, keepdims=True)
    upt = MOE_TG // MOE_UNIT
    tiles_e = jnp.floor((n_e + (upt - 1.0)) * (1.0 / upt))
    goff = _dot(jnp.broadcast_to(tiles_e, (8, ROUTE_W)).astype(bf16), utri)[0:1, :]
    a = goff * upt + pre - lstart
    n_used = jnp.sum(units, axis=1, keepdims=True)
    j = lane.astype(f32)
    acc = jnp.zeros(shape, f32)
    for e in range(N_EXPERTS):
        ls = _lane_pick(lstart, lane, e)
        u = _lane_pick(units, lane, e)
        acc = acc + jnp.where(j >= ls, jnp.where(j < ls + u, _lane_pick(a, lane, e), 0.0), 0.0)
    rows = (acc + j) * MOE_UNIT
    used = j < n_used
    region = (lax.broadcasted_iota(jnp.int32, shape, 0) & (MOE_SPARE - 1)).astype(f32)
    spare = _moe_rows(n_blk) + region * MOE_RLOC + j * MOE_UNIT
    tabd_ref[...] = jnp.where(used, rows, spare).astype(jnp.int32)
    tabc_ref[...] = jnp.where(used, rows, _lane_pick(rows, lane, 0)).astype(jnp.int32)

    ends = goff + tiles_e
    lane1 = lax.broadcasted_iota(jnp.int32, (1, ROUTE_W), 1)
    ti = lax.broadcasted_iota(jnp.int32, (8, MOE_PLAN_TILES), 1).astype(f32)
    te = jnp.zeros((8, MOE_PLAN_TILES), f32)
    for e in range(N_EXPERTS):
        te = te + jnp.where(ti >= _lane_pick(ends, lane1, e), 1.0, 0.0)
    te_ref[...] = jnp.minimum(te, N_EXPERTS - 1.0).astype(jnp.int32)

    n_tiles = jnp.sum(tiles_e, axis=1, keepdims=True)
    gap_start = jnp.broadcast_to((goff * upt + n_e) * MOE_UNIT, (8, ROUTE_W))
    gap_units = jnp.broadcast_to(tiles_e * upt - n_e, (8, ROUTE_W))
    lane8 = lax.broadcasted_iota(jnp.int32, (8, ROUTE_W), 1)
    meta = jnp.where(lane8 == 0, n_tiles,
                     jnp.where(lane8 <= N_EXPERTS, pltpu.roll(gap_start, 1, 1),
                               jnp.where(lane8 <= 2 * N_EXPERTS, pltpu.roll(gap_units, 1 + N_EXPERTS, 1), 0.0)))
    meta_ref[...] = meta.astype(jnp.int32)


def moe_plan(n_blk, counts, ltri, utri):
    assert n_blk <= MOE_PLAN_ROWS and _moe_rows(n_blk) // MOE_TG <= MOE_PLAN_TILES
    shape = (MOE_PLAN_ROWS, ROUTE_W)
    tabd, tabc, te, meta = pl.pallas_call(
        functools.partial(_moe_plan_kernel, n_blk),
        out_shape=[
            jax.ShapeDtypeStruct(shape, jnp.int32),
            jax.ShapeDtypeStruct(shape, jnp.int32),
            jax.ShapeDtypeStruct((8, MOE_PLAN_TILES), jnp.int32),
            jax.ShapeDtypeStruct((8, ROUTE_W), jnp.int32),
        ],
        name="moe_plan",
    )(counts, ltri, utri)
    return tabd.reshape(-1), tabc.reshape(-1), te[0], meta[0]


def _block_routes(comb, ltri, utri):
    oh = comb != 0.0
    ohf = jnp.where(oh, 1.0, 0.0)
    rank = _dot(ltri, ohf.astype(bf16))
    cnt = jnp.sum(ohf, axis=0, keepdims=True)
    units = jnp.floor((cnt + (MOE_UNIT - 1.0)) * (1.0 / MOE_UNIT))
    seg = _dot(jnp.broadcast_to(units, (8, ROUTE_W)).astype(bf16), utri)[0:1, :] * MOE_UNIT
    dest = seg + rank
    big = float(1 << 20)
    d_a = jnp.min(jnp.where(oh, dest, big), axis=-1, keepdims=True)
    d_b = jnp.max(jnp.where(oh, dest, -1.0), axis=-1, keepdims=True)
    w_a = jnp.sum(jnp.where(oh, jnp.where(dest == d_a, comb, 0.0), 0.0), axis=-1, keepdims=True)
    w_b = jnp.sum(jnp.where(oh, jnp.where(dest == d_b, comb, 0.0), 0.0), axis=-1, keepdims=True)
    second = d_b != d_a
    return d_a, jnp.where(second, d_b, -1.0), w_a, jnp.where(second, w_b, 0.0)


def _one_hot_rows(d):
    r = lax.broadcasted_iota(jnp.int32, (d.shape[0], MOE_RLOC), 1).astype(f32)
    return jnp.where(r == d, 1.0, 0.0).astype(bf16)


def _block_gather_matrix(comb, utri_tok, ltri_exp):
    comb_t = comb.T
    oh = comb_t != 0.0
    ohf = jnp.where(oh, 1.0, 0.0)
    rank = _dot(ohf.astype(bf16), utri_tok)
    cnt = jnp.sum(ohf, axis=1, keepdims=True)
    units = jnp.floor((cnt + (MOE_UNIT - 1.0)) * (1.0 / MOE_UNIT))
    seg = _dot(ltri_exp, jnp.broadcast_to(units, (ROUTE_W, ROUTE_W)).astype(bf16))[:, 0:1] * MOE_UNIT
    dest = seg + rank
    d_a = jnp.min(jnp.where(oh, dest, float(1 << 20)), axis=0, keepdims=True)
    d_b = jnp.max(jnp.where(oh, dest, -1.0), axis=0, keepdims=True)
    d_b = jnp.where(d_b != d_a, d_b, -1.0)
    r = lax.broadcasted_iota(jnp.int32, (MOE_RLOC, comb.shape[0]), 0).astype(f32)
    return jnp.where(r == d_a, 1.0, jnp.where(r == d_b, 1.0, 0.0)).astype(bf16)


def _unit_rows(ref, tab_ref, t, j):
    return ref.at[pl.ds(pl.multiple_of(tab_ref[t * ROUTE_W + j], MOE_UNIT), MOE_UNIT)]


def _wait_all_units(local, remote, sem):
    pltpu.make_async_copy(local, remote.at[pl.ds(0, MOE_RLOC)], sem).wait()


def _gap_copies(meta_ref, zero_ref, remote, sem, wait):
    def per_expert(e, c):
        start = meta_ref[1 + e]

        def per_unit(u, c2):
            ro = remote.at[pl.ds(pl.multiple_of(start + u * MOE_UNIT, MOE_UNIT), MOE_UNIT)]
            cp = pltpu.make_async_copy(zero_ref, ro, sem)
            if wait:
                cp.wait()
            else:
                cp.start()
            return c2

        lax.fori_loop(0, meta_ref[1 + N_EXPERTS + e], per_unit, 0)
        return c

    lax.fori_loop(0, N_EXPERTS, per_expert, 0)


def _moe_dispatch_kernel(n_blk, tab_ref, meta_ref, x_ref, comb_ref, utri_tok_ref, ltri_exp_ref, xs_ref,
                         buf_ref, zero_ref, sem_ref):
    step = pl.program_id(0)
    slot = step % 2
    for bb in range(MOE_NB):
        tok = slice(bb * MOE_TB, (bb + 1) * MOE_TB)
        p = _block_gather_matrix(comb_ref[tok, :], utri_tok_ref[...], ltri_exp_ref[...])
        buf_ref[slot, bb] = _dot(p, x_ref[tok, :]).astype(bf16)
    for bb in range(MOE_NB):
        for j in range(MOE_NUNIT):
            pltpu.make_async_copy(buf_ref.at[slot, bb, pl.ds(j * MOE_UNIT, MOE_UNIT)],
                                  _unit_rows(xs_ref, tab_ref, step * MOE_NB + bb, j), sem_ref.at[slot]).start()

    def wait_slot(s):
        for bb in range(MOE_NB):
            _wait_all_units(buf_ref.at[s, bb], xs_ref, sem_ref.at[s])

    @pl.when(step > 0)
    def _():
        wait_slot(1 - slot)

    @pl.when(step == n_blk // MOE_NB - 1)
    def _():
        zero_ref[...] = jnp.zeros_like(zero_ref)
        _gap_copies(meta_ref, zero_ref, xs_ref, sem_ref.at[2], False)
        wait_slot(slot)
        _gap_copies(meta_ref, zero_ref, xs_ref, sem_ref.at[2], True)


def moe_dispatch(n_blk, tab, meta, x, comb, utri_tok, ltri_exp):
    return pl.pallas_call(
        functools.partial(_moe_dispatch_kernel, n_blk),
        grid_spec=pltpu.PrefetchScalarGridSpec(
            num_scalar_prefetch=2,
            grid=(n_blk // MOE_NB,),
            in_specs=[
                pl.BlockSpec((MOE_NB * MOE_TB, D_MODEL), lambda t, *_: (t, 0)),
                pl.BlockSpec((MOE_NB * MOE_TB, ROUTE_W), lambda t, *_: (t, 0)),
                pl.BlockSpec((MOE_TB, MOE_TB), lambda t, *_: (0, 0)),
                pl.BlockSpec((ROUTE_W, ROUTE_W), lambda t, *_: (0, 0)),
            ],
            out_specs=pl.BlockSpec(memory_space=pl.ANY),
            scratch_shapes=[
                pltpu.VMEM((2, MOE_NB, MOE_RLOC, D_MODEL), bf16),
                pltpu.VMEM((MOE_UNIT, D_MODEL), bf16),
                pltpu.SemaphoreType.DMA((3,)),
            ],
        ),
        out_shape=jax.ShapeDtypeStruct((_moe_rows(n_blk) + MOE_SPARE * MOE_RLOC, D_MODEL), bf16),
        compiler_params=_cparams("arbitrary"),
        name="moe_dispatch",
    )(tab, meta, x, comb, utri_tok, ltri_exp)


def _moe_expert_kernel(tile_exp_ref, meta_ref, xs_ref, wg_ref, wu_ref, wd_ref, ys_ref, wgub_ref, wdb_ref):
    i = pl.program_id(0)

    @pl.when(i < meta_ref[0])
    def _():
        prev = tile_exp_ref[jnp.maximum(i - 1, 0)]

        @pl.when(jnp.logical_or(i == 0, tile_exp_ref[i] != prev))
        def _():
            wgub_ref[:, :D_EXPERT] = wg_ref[0].astype(bf16)
            wgub_ref[:, D_EXPERT:] = wu_ref[0].astype(bf16)
            wdb_ref[...] = wd_ref[0].astype(bf16)

        gu = _dot(xs_ref[...], wgub_ref[...])
        a = _silu(gu[:, :D_EXPERT]) * gu[:, D_EXPERT:]
        ys_ref[...] = _dot(a.astype(bf16), wdb_ref[...]).astype(bf16)


def moe_experts(n_blk, layer, tile_exp, meta, xs, w_gate, w_up, w_down):
    n_tiles = _moe_rows(n_blk) // MOE_TG

    def row_map(i, te, meta):
        return (jnp.minimum(i, meta[0] - 1), 0)

    def w_map(i, te, meta):
        return (layer * N_EXPERTS + te[jnp.minimum(i, meta[0] - 1)], 0, 0)

    return pl.pallas_call(
        _moe_expert_kernel,
        grid_spec=pltpu.PrefetchScalarGridSpec(
            num_scalar_prefetch=2,
            grid=(n_tiles,),
            in_specs=[
                pl.BlockSpec((MOE_TG, D_MODEL), row_map),
                pl.BlockSpec((1, D_MODEL, D_EXPERT), w_map),
                pl.BlockSpec((1, D_MODEL, D_EXPERT), w_map),
                pl.BlockSpec((1, D_EXPERT, D_MODEL), w_map),
            ],
            out_specs=pl.BlockSpec((MOE_TG, D_MODEL), row_map),
            scratch_shapes=[
                pltpu.VMEM((D_MODEL, 2 * D_EXPERT), bf16),
                pltpu.VMEM((D_EXPERT, D_MODEL), bf16),
            ],
        ),
        out_shape=jax.ShapeDtypeStruct((_moe_rows(n_blk), D_MODEL), bf16),
        compiler_params=_cparams("arbitrary"),
        name="moe_experts",
    )(tile_exp, meta, xs, w_gate, w_up, w_down)


def _moe_combine_kernel(n_blk, tab_ref, ys_ref, comb_ref, h_ref, mod_ref, ltri_ref, utri_ref,
                        o_ref, buf_ref, sem_ref):
    step = pl.program_id(0)
    slot = step % 2

    def gather(st, s):
        for bb in range(MOE_NB):
            for j in range(MOE_NUNIT):
                pltpu.make_async_copy(_unit_rows(ys_ref, tab_ref, st * MOE_NB + bb, j),
                                      buf_ref.at[s, bb, pl.ds(j * MOE_UNIT, MOE_UNIT)], sem_ref.at[s]).start()

    @pl.when(step == 0)
    def _():
        gather(0, 0)

    @pl.when(step + 1 < n_blk // MOE_NB)
    def _():
        gather(step + 1, 1 - slot)

    for bb in range(MOE_NB):
        _wait_all_units(buf_ref.at[slot, bb], ys_ref, sem_ref.at[slot])
    for bb in range(MOE_NB):
        tok = slice(bb * MOE_TB, (bb + 1) * MOE_TB)
        d_a, d_b, w_a, w_b = _block_routes(comb_ref[tok, :], ltri_ref[...], utri_ref[...])
        p = jnp.concatenate([_one_hot_rows(d_a), _one_hot_rows(d_b)], axis=0)
        picked = _dot(p, buf_ref[slot, bb])
        m = w_a * picked[:MOE_TB] + w_b * picked[MOE_TB:]
        r = _mod_row(step * MOE_NB + bb, MOE_TB)
        g2 = mod_ref[pl.ds(r, 1), pl.ds(5 * D_MODEL, D_MODEL)]
        o_ref[tok, :] = h_ref[tok, :] + g2 * m


def moe_combine(n_blk, tab, ys, comb, h, mod, ltri, utri):
    return pl.pallas_call(
        functools.partial(_moe_combine_kernel, n_blk),
        grid_spec=pltpu.PrefetchScalarGridSpec(
            num_scalar_prefetch=1,
            grid=(n_blk // MOE_NB,),
            in_specs=[
                pl.BlockSpec(memory_space=pl.ANY),
                pl.BlockSpec((MOE_NB * MOE_TB, ROUTE_W), lambda t, *_: (t, 0)),
                pl.BlockSpec((MOE_NB * MOE_TB, D_MODEL), lambda t, *_: (t, 0)),
                pl.BlockSpec((MOD_ROWS, 6 * D_MODEL), lambda t, *_: (0, 0)),
                pl.BlockSpec((MOE_TB, MOE_TB), lambda t, *_: (0, 0)),
                pl.BlockSpec((ROUTE_W, ROUTE_W), lambda t, *_: (0, 0)),
            ],
            out_specs=pl.BlockSpec((MOE_NB * MOE_TB, D_MODEL), lambda t, *_: (t, 0)),
            scratch_shapes=[
                pltpu.VMEM((2, MOE_NB, MOE_RLOC, D_MODEL), bf16),
                pltpu.SemaphoreType.DMA((2,)),
            ],
        ),
        out_shape=jax.ShapeDtypeStruct((n_blk * MOE_TB, D_MODEL), f32),
        compiler_params=_cparams("arbitrary"),
        name="moe_combine",
    )(tab, ys, comb, h, mod, ltri, utri)


def sparse_moe(n_rows, layer, v, comb, counts, h, mod, w_gate, w_up, w_down, ltri, ltri_plan, utri):
    n_blk = n_rows // MOE_TB
    cnt = jnp.pad(counts[:, 0, :], ((0, MOE_PLAN_ROWS - n_blk), (0, 0)))
    tab_d, tab_c, tile_exp, meta = moe_plan(n_blk, cnt, ltri_plan, utri)
    assert MOE_PLAN_ROWS == ROUTE_W
    xs = moe_dispatch(n_blk, tab_d, meta, v, comb, ltri.T, ltri_plan)
    ys = moe_experts(n_blk, layer, tile_exp, meta, xs,
                     w_gate.reshape(DEPTH * N_EXPERTS, D_MODEL, D_EXPERT),
                     w_up.reshape(DEPTH * N_EXPERTS, D_MODEL, D_EXPERT),
                     w_down.reshape(DEPTH * N_EXPERTS, D_EXPERT, D_MODEL))
    return moe_combine(n_blk, tab_c, ys, comb, h, mod, ltri, utri)


def _dft_tables(L):
    k = np.arange(L, dtype=np.int64)
    ang = (2.0 * np.pi / (2 * L)) * ((k[:, None] * k[None, :]) % (2 * L)).astype(np.float64)
    return np.cos(ang).astype(np.float32), np.sin(ang).astype(np.float32)


def _filter_features(L):
    bands = (HY_EMB - 1) // 2
    t = np.linspace(0.0, 1.0, L, dtype=np.float32).astype(np.float64)[:, None]
    w = (2.0 * np.pi / L) * np.arange(L, dtype=np.float64)[:, None]
    fb = np.linspace(1e-4, bands - 1, bands, dtype=np.float32).astype(np.float64)[None, :]
    z = np.concatenate([t, np.cos(fb * w), -np.sin(fb * w)], axis=-1)
    zp = np.zeros((L, FEAT_PAD), np.float32)
    zp[:, :HY_EMB] = z
    deltas = np.abs(np.linspace(HY_MIN_DECAY, HY_MAX_DECAY, HY_CH, dtype=np.float32).astype(np.float64))
    decay = np.exp(-t * deltas[None, :]).astype(np.float32)
    return zp, decay


def _rope_table(cos, sin, half, tm):
    S, width = cos.shape
    low = (np.arange(width) % (2 * half)) < half
    tab = np.zeros((3, S + tm, width), np.float32)
    tab[0, :S] = cos
    tab[0, S:] = 1.0
    tab[1, :S] = np.where(low[None, :], 0.0, sin)
    tab[2, :S] = np.where(low[None, :], -sin, 0.0)
    return tab


def _axial_rope_table(head_dim, tm):
    rows = SEQ // GRID_W
    nf = head_dim // 4
    row = np.repeat(np.arange(rows), GRID_W).astype(np.float64)
    col = np.tile(np.arange(GRID_W), rows).astype(np.float64)
    inv = ROPE_BASE ** (-np.arange(nf, dtype=np.float64) / nf)
    ang = np.stack([row[:, None] * inv, col[:, None] * inv], axis=1)
    a = np.broadcast_to(ang[:, :, None, :], (SEQ, 2, 2, nf)).reshape(SEQ, head_dim)
    reps = LANES // head_dim
    a = np.tile(a, (1, reps))
    return _rope_table(np.cos(a), np.sin(a), nf, tm)


def _seq_rope_table(head_dim, tm):
    inv = 1.0 / (ROPE_BASE ** np.linspace(0.0, 1.0, head_dim // 2, dtype=np.float32).astype(np.float64))
    ang = np.arange(SEQ, dtype=np.float64)[:, None] * inv
    a = np.concatenate([ang, ang], axis=1)
    return _rope_table(np.cos(a), np.sin(a), head_dim // 2, tm)


def _group_mean_matrix():
    g = np.arange(SEG) // DA_HD
    return (g[:, None] == g[None, :]).astype(np.float32) / DA_HD


def _router_weights(w_grp, b_grp, w_rt, b_rt):
    pad = ROUTE_W - 2 * N_EXPERTS
    wr = jnp.concatenate([w_rt, jnp.repeat(w_grp, EXP_PER_GROUP, axis=1),
                          jnp.zeros((D_MODEL, pad), f32)], axis=1)
    br = jnp.concatenate([b_rt, jnp.repeat(b_grp, EXP_PER_GROUP), jnp.zeros((pad,), f32)])[None, :]
    return wr, br


def _strict_lower(n):
    i = np.arange(n)
    return (i[None, :] < i[:, None]).astype(np.float32)


def kernel(x, c, ctx, c_ctx, ada_w, ada_b, e_w_in, e_w_out, hy_conv_w, hy_conv_b, hy_f_w1, hy_f_b1, hy_f_w2, hy_f_b2, hy_f_w3, hy_f_freq, hy_bias, da_q_norm, da_k_norm, da_lam, da_subln, o_w_in, o_w_out, ret_decay, ret_gn, gq_q_norm, gq_k_norm, gq_sink, moe_w_grp, moe_b_grp, moe_w_rt, moe_b_rt, moe_w_gate, moe_w_up, moe_w_down):
    assert x.shape == (BATCH, SEQ, D_MODEL) and ctx.shape == (BATCH, CTX_LEN, D_MODEL)
    x_rows = x.reshape(T_LAT, D_MODEL)
    ctx_rows = ctx.reshape(T_CTX, D_MODEL)
    c_rows = jnp.concatenate([c, c_ctx[None, :], jnp.zeros((MOD_ROWS - BATCH - 1, D_MODEL), f32)], axis=0)
    mod = ada_modulation(c_rows, ada_w, ada_b)

    gmat = jnp.asarray(_group_mean_matrix()).astype(bf16)
    ax_tab = jnp.asarray(_axial_rope_table(DA_HD, PROJ_TM))
    r1_tab = jnp.asarray(_seq_rope_table(RET_DK, PROJ_TM))
    ones = jnp.ones((SEG,), f32)

    lam_init0 = 0.8 - 0.6 * math.exp(-0.3 * 0)
    reps = SEG // DA_HD
    gain0 = jnp.concatenate([ones, ones, ones, jnp.tile(da_q_norm[0], reps) * (DA_HD ** -0.5 * LOG2E),
                             jnp.tile(da_k_norm[0], reps), ones])[None, :]
    proj0 = in_projection("even", [x_rows, ctx_rows], mod[0], e_w_in[0].astype(bf16), gain0, gmat, [ax_tab])

    w3r = hy_f_w3[0].reshape(HY_FILT_HID, 4, HY_CH).transpose(1, 0, 2)
    w1p = jnp.concatenate([hy_f_w1[0], jnp.zeros((FEAT_PAD - HY_EMB, HY_FILT_HID), f32)], axis=0)
    y_hy = []
    for L, blk0 in ((SEQ, 0), (CTX_LEN, T_LAT // CTX_LEN)):
        zfeat, decay = _filter_features(L)
        cm, sm = _dft_tables(L)
        cm = jnp.asarray(cm).astype(bf16)
        sm = jnp.asarray(sm).astype(bf16)
        spec, nyq = hyena_filter_spectra(L, jnp.asarray(zfeat), w1p, hy_f_b1[0][None, :], hy_f_w2[0],
                                         hy_f_b2[0][None, :], w3r, hy_f_freq[0], jnp.asarray(decay), cm, sm)
        y_hy.append(hyena_mix(L, blk0, proj0, hy_conv_w[0], hy_conv_b[0][None, :], spec, nyq, hy_bias[0], cm, sm))

    y_da = diff_attention(proj0, da_lam[0], da_subln[0][None, :], lam_init0)

    wr0, br0 = _router_weights(moe_w_grp[0], moe_b_grp[0], moe_w_rt[0], moe_b_rt[0])
    ltri = jnp.asarray(_strict_lower(MOE_TB)).astype(bf16)
    ltri_plan = jnp.asarray(_strict_lower(MOE_PLAN_ROWS)).astype(bf16)
    utri = jnp.asarray(_strict_lower(ROUTE_W).T).astype(bf16)
    h, v, comb, counts = out_projection(T_ALL, y_hy, y_da, e_w_out[0].astype(bf16), [x_rows, ctx_rows], mod[0],
                                        wr0, br0)
    h = sparse_moe(T_ALL, 0, v, comb, counts, h, mod[0], moe_w_gate, moe_w_up, moe_w_down, ltri, ltri_plan, utri)

    w_in1 = jnp.concatenate([o_w_in[0], jnp.zeros((D_MODEL, PROJ_W - o_w_in.shape[2]), f32)], axis=1).astype(bf16)
    kq = GQ_KV * GQ_HD
    gain1 = jnp.concatenate([ones, ones * RET_DK ** -0.5, ones, ones,
                             jnp.tile(gq_q_norm[0], reps) * (GQ_HD ** -0.5 * LOG2E),
                             jnp.tile(gq_k_norm[0], kq // GQ_HD), jnp.ones((SEG - kq,), f32)])[None, :]
    proj1 = in_projection("odd", [h], mod[1], w_in1, gain1, gmat, [ax_tab, r1_tab])
    y_ret = retention(proj1, ret_decay[0], ret_gn[0][None, :])
    y_gq = window_gqa(proj1, gq_sink[0])
    wr1, br1 = _router_weights(moe_w_grp[1], moe_b_grp[1], moe_w_rt[1], moe_b_rt[1])
    h_lat, v, comb, counts = out_projection(T_LAT, [y_ret], [y_gq], o_w_out[0].astype(bf16), [h], mod[1], wr1, br1)
    out = sparse_moe(T_LAT, 1, v, comb, counts, h_lat, mod[1], moe_w_gate, moe_w_up, moe_w_down, ltri, ltri_plan,
                     utri)
    return out.reshape(BATCH, SEQ, D_MODEL)
```

```python
import functools
import math

import numpy as np
import jax
import jax.numpy as jnp
from jax import lax
from jax.experimental import pallas as pl
from jax.experimental.pallas import tpu as pltpu

f32 = jnp.float32
bf16 = jnp.bfloat16

D_MODEL = 1024
BATCH = 8
SEQ = 2048
DEPTH = 2
GRID_W = 64
CTX_LEN = 256
EPS = 1e-6
NEG_INF = -1e30
LOG2E = math.log2(math.e)
ROPE_BASE = 10000.0
HY_CH = D_MODEL // 2
HY_EMB = 33
HY_FILT_HID = 64
HY_MAX_DECAY = math.log(1e-2) / 0.3
HY_MIN_DECAY = math.log(1e-2) / 1.5
DA_HEADS = 4
DA_HD = D_MODEL // 16
RET_HEADS = 4
RET_DK = D_MODEL // 8
RET_CHUNK = 128
GQ_KV = 2
GQ_GROUP = 4
GQ_HD = D_MODEL // 16
WINDOW = 128
N_GROUPS = 4
EXP_PER_GROUP = 8
N_EXPERTS = N_GROUPS * EXP_PER_GROUP
D_EXPERT = D_MODEL // 4

T_LAT = BATCH * SEQ
T_CTX = BATCH * CTX_LEN
T_ALL = T_LAT + T_CTX
PROJ_W = 3072
SEG = 512
CTX_MOD_ROW = BATCH
MOD_ROWS = 16

LANES = 128
VMEM_LIMIT_BYTES = 56 * 1024 * 1024


def _cparams(*sem):
    return pltpu.CompilerParams(dimension_semantics=sem, vmem_limit_bytes=VMEM_LIMIT_BYTES)


def _dot(a, b):
    return jnp.dot(a, b, preferred_element_type=f32)


def _dot_nt(a, b):
    return lax.dot_general(a, b, (((1,), (1,)), ((), ())), preferred_element_type=f32)


def _split(x):
    hi = x.astype(bf16)
    lo = (x - hi.astype(f32)).astype(bf16)
    return hi, lo


def _dot3(a, b):
    ah, al = _split(a)
    bh, bl = _split(b)
    return _dot(ah, bh) + _dot(al, bh) + _dot(ah, bl)


def _silu(x):
    return x * jax.nn.sigmoid(x)


def _rms(x):
    return x * lax.rsqrt(jnp.mean(x * x, axis=-1, keepdims=True) + EPS)


def _const_spec(shape):
    nd = len(shape)
    return pl.BlockSpec(shape, lambda *_: (0,) * nd)


def _const_spec1(shape):
    nd = len(shape)
    return pl.BlockSpec(shape, lambda *_: (0,) * nd, pipeline_mode=pl.Buffered(1))


ADA_TN = 1536


def _ada_kernel(c_ref, w_ref, b_ref, o_ref):
    x = _silu(c_ref[...])
    o_ref[0] = _dot3(x, w_ref[0]) + b_ref[0]


def ada_modulation(c_rows, ada_w, ada_b):
    n = 6 * D_MODEL
    return pl.pallas_call(
        _ada_kernel,
        grid=(DEPTH, n // ADA_TN),
        in_specs=[
            pl.BlockSpec((MOD_ROWS, D_MODEL), lambda l, j: (0, 0)),
            pl.BlockSpec((1, D_MODEL, ADA_TN), lambda l, j: (l, 0, j)),
            pl.BlockSpec((1, 1, ADA_TN), lambda l, j: (l, 0, j)),
        ],
        out_specs=pl.BlockSpec((1, MOD_ROWS, ADA_TN), lambda l, j: (l, 0, j)),
        out_shape=jax.ShapeDtypeStruct((DEPTH, MOD_ROWS, n), f32),
        compiler_params=_cparams("arbitrary", "arbitrary"),
        name="ada_modulation",
    )(c_rows, ada_w, ada_b.reshape(DEPTH, 1, n))


PROJ_TM = 1024


def _mod_row(i, tm):
    return jnp.minimum((i * tm) // SEQ, CTX_MOD_ROW)


def _tile4(t):
    return jnp.concatenate([t, t, t, t], axis=1)


def _group_norm64(y, gmat):
    ms = _dot((y * y).astype(bf16), gmat)
    return y * lax.rsqrt(ms + EPS)


def _rope(y, tab, shift):
    w = y.shape[1]
    return y * tab[0] + pltpu.roll(y, shift, 1) * tab[1] + pltpu.roll(y, w - shift, 1) * tab[2]


def _stacked_specs(tm, width):
    n_lat = T_LAT // tm
    return [pl.BlockSpec((tm, width), lambda i: (jnp.minimum(i, n_lat - 1), 0)),
            pl.BlockSpec((tm, width), lambda i: (jnp.maximum(i - n_lat, 0), 0))]


def _stacked_tile(i, tm, lat_ref, ctx_ref):
    return jnp.where(i < T_LAT // tm, lat_ref[...], ctx_ref[...])


def _inproj_kernel(layer_kind, *refs):
    i = pl.program_id(0)
    if layer_kind == "even":
        x_ref, c_ref, mod_ref, w_ref, gain_ref, gmat_ref, ax_ref, o_ref = refs
        h = _stacked_tile(i, PROJ_TM, x_ref, c_ref)
    else:
        h_ref, mod_ref, w_ref, gain_ref, gmat_ref, ax_ref, r1_ref, o_ref = refs
        h = h_ref[...]
    r = _mod_row(i, PROJ_TM)
    sh = mod_ref[pl.ds(r, 1), pl.ds(0, D_MODEL)]
    sc = mod_ref[pl.ds(r, 1), pl.ds(D_MODEL, D_MODEL)]
    u = (_rms(h) * (1.0 + sc) + sh).astype(bf16)

    def seg(j):
        return _dot(u, w_ref[:, j * SEG:(j + 1) * SEG])

    def put(j, y):
        o_ref[:, j * SEG:(j + 1) * SEG] = y.astype(bf16)

    def gain(j):
        return gain_ref[:, j * SEG:(j + 1) * SEG]

    gmat = gmat_ref[...]
    ax = ax_ref[...]
    ax4 = (_tile4(ax[0]), _tile4(ax[1]), _tile4(ax[2]))
    if layer_kind == "even":
        for j in (3, 4):
            put(j, _rope(_group_norm64(seg(j), gmat) * gain(j), ax4, DA_HD // 4))
        for j in (0, 1, 2, 5):
            put(j, seg(j))
    else:
        r1 = r1_ref[...]
        r14 = (_tile4(r1[0]), _tile4(r1[1]), _tile4(r1[2]))
        put(4, _rope(_group_norm64(seg(4), gmat) * gain(4), ax4, GQ_HD // 4))
        y = seg(5)
        kw = GQ_KV * GQ_HD
        yk = _rope(_group_norm64(y[:, :kw], gmat[:kw, :kw]) * gain(5)[:, :kw], ax, GQ_HD // 4)
        yv = y[:, kw:2 * kw]
        pieces = (yk, yv, pltpu.roll(yk, GQ_HD, 1), pltpu.roll(yv, GQ_HD, 1))
        for p, piece in enumerate(pieces):
            o_ref[:, 5 * SEG + p * kw:5 * SEG + (p + 1) * kw] = piece.astype(bf16)
        for j in (0, 1):
            put(j, _rope(seg(j) * gain(j), r14, RET_DK // 2))
        for j in (2, 3):
            put(j, seg(j))


def in_projection(layer_kind, hs, mod, w, gain, gmat, tables):
    n_lat_tiles = T_LAT // PROJ_TM
    n_pos_tiles = SEQ // PROJ_TM

    def tab_map(i):
        return (0, jnp.where(i < n_lat_tiles, i % n_pos_tiles, n_pos_tiles), 0)

    tab_specs = [pl.BlockSpec((3, PROJ_TM, LANES), tab_map) for _ in tables]
    if layer_kind == "even":
        h_specs = _stacked_specs(PROJ_TM, D_MODEL)
    else:
        h_specs = [pl.BlockSpec((PROJ_TM, D_MODEL), lambda i: (i, 0))]
    return pl.pallas_call(
        functools.partial(_inproj_kernel, layer_kind),
        grid=(T_ALL // PROJ_TM,),
        in_specs=h_specs + [
            _const_spec((MOD_ROWS, 6 * D_MODEL)),
            _const_spec((D_MODEL, PROJ_W)),
            _const_spec((1, PROJ_W)),
            _const_spec((SEG, SEG)),
        ] + tab_specs,
        out_specs=pl.BlockSpec((PROJ_TM, PROJ_W), lambda i: (i, 0)),
        out_shape=jax.ShapeDtypeStruct((T_ALL, PROJ_W), bf16),
        compiler_params=_cparams("parallel"),
        name="in_projection_" + layer_kind,
    )(*hs, mod, w, gain, gmat, *tables)


HY_TC = 256
HY_FREQ_CHUNK = 512
FEAT_PAD = 64


def _alt_sign(shape, axis):
    idx = lax.broadcasted_iota(jnp.int32, shape, axis)
    return jnp.where((idx & 1) == 0, 1.0, -1.0).astype(f32)


def _filter_kernel(L, z_ref, w1_ref, b1_ref, w2_ref, b2_ref, wf_ref, wb_ref, freq_ref, dec_ref, c_ref, s_ref,
                   spec_ref, nyq_ref):
    hid = jnp.sin(freq_ref[0:1, :] * (_dot3(z_ref[...], w1_ref[...]) + b1_ref[...]))
    hid = jnp.sin(freq_ref[1:2, :] * (_dot3(hid, w2_ref[...]) + b2_ref[...]))
    dec = dec_ref[...]
    fwd = _dot3(hid, wf_ref[0]) * dec
    bwd = _dot3(hid, wb_ref[0]) * dec
    row = lax.broadcasted_iota(jnp.int32, fwd.shape, 0)
    bwd = jnp.where(row == 0, 0.0, bwd)
    even = fwd + bwd
    odd = bwd - fwd
    wk = jnp.where(row == 0, 0.5 / L, 1.0 / L).astype(f32)
    spec_ref[0, 0] = _dot(c_ref[...], even.astype(bf16)) * wk
    spec_ref[0, 1] = _dot(s_ref[...], odd.astype(bf16)) * wk
    nyq = jnp.sum(even * _alt_sign(even.shape, 0), axis=0, keepdims=True) * (0.5 / L)
    nyq_ref[0] = jnp.broadcast_to(nyq, (8, nyq.shape[1]))


def hyena_filter_spectra(L, zfeat, w1, b1, w2, b2, w3r, freq, decay, cmat, smat):
    nct = HY_CH // HY_TC
    return pl.pallas_call(
        functools.partial(_filter_kernel, L),
        grid=(2, nct),
        in_specs=[
            _const_spec((L, FEAT_PAD)),
            _const_spec((FEAT_PAD, HY_FILT_HID)),
            _const_spec((1, HY_FILT_HID)),
            _const_spec((HY_FILT_HID, HY_FILT_HID)),
            _const_spec((1, HY_FILT_HID)),
            pl.BlockSpec((1, HY_FILT_HID, HY_TC), lambda n, c: (2 * n, 0, c)),
            pl.BlockSpec((1, HY_FILT_HID, HY_TC), lambda n, c: (2 * n + 1, 0, c)),
            _const_spec((2, HY_FILT_HID)),
            pl.BlockSpec((L, HY_TC), lambda n, c: (0, c)),
            _const_spec1((L, L)),
            _const_spec1((L, L)),
        ],
        out_specs=[
            pl.BlockSpec((1, 2, L, HY_TC), lambda n, c: (n, 0, 0, c)),
            pl.BlockSpec((1, 8, HY_TC), lambda n, c: (n, 0, c)),
        ],
        out_shape=[
            jax.ShapeDtypeStruct((2, 2, L, HY_CH), f32),
            jax.ShapeDtypeStruct((2, 8, HY_CH), f32),
        ],
        compiler_params=_cparams("arbitrary", "arbitrary"),
        name="hyena_filter_L%d" % L,
    )(zfeat, w1, b1, w2, b2, w3r, w3r, freq, decay, cmat, smat)


def _conv3(u, w, b):
    L = u.shape[0]
    row = lax.broadcasted_iota(jnp.int32, u.shape, 0)
    prev = jnp.where(row == 0, 0.0, pltpu.roll(u, 1, 0))
    nxt = jnp.where(row == L - 1, 0.0, pltpu.roll(u, L - 1, 0))
    return prev * w[0:1, :] + u * w[1:2, :] + nxt * w[2:3, :] + b


def _hyena_kernel(v_ref, x1_ref, x2_ref, wv_ref, w1_ref, w2_ref, bv_ref, b1_ref, b2_ref, spec_ref, nyq_ref,
                  bias_ref, c_ref, s_ref, o_ref, yr_ref, yi_ref):
    L = v_ref.shape[0]
    fch = min(L, HY_FREQ_CHUNK)
    z = _conv3(v_ref[...].astype(f32), wv_ref[...], bv_ref[...])
    gate_refs = ((x1_ref, w1_ref, b1_ref), (x2_ref, w2_ref, b2_ref))
    alt = _alt_sign(z.shape, 0)
    for n in range(2):
        zb = z.astype(bf16)
        for k in range(L // fch):
            rows = slice(k * fch, (k + 1) * fch)
            a = _dot(c_ref[rows, :], zb)
            b = _dot(s_ref[rows, :], zb)
            hr = spec_ref[n, 0, rows, :]
            hi = spec_ref[n, 1, rows, :]
            yr_ref[rows, :] = (a * hr + b * hi).astype(bf16)
            yi_ref[rows, :] = (a * hi - b * hr).astype(bf16)
        x_nyq = jnp.sum(z * alt, axis=0, keepdims=True)
        y = (_dot(c_ref[...], yr_ref[...]) - _dot(s_ref[...], yi_ref[...])
             + alt * (x_nyq * nyq_ref[n, 0:1, :]))
        x_ref, w_ref, b_ref = gate_refs[n]
        gate = _conv3(x_ref[...].astype(f32), w_ref[...], b_ref[...])
        z = gate * (y + z * bias_ref[n:n + 1, :])
    o_ref[...] = z.astype(bf16)


def hyena_mix(L, row_block0, proj, conv_w, conv_b, spec, nyq, bias, cmat, smat):
    nct = HY_CH // HY_TC
    nseg = HY_CH // HY_TC

    def col(k):
        return lambda c, b: (row_block0 + b, k * nseg + c)

    def par(k):
        return lambda c, b: (0, k * nseg + c)

    in_specs = (
        [pl.BlockSpec((L, HY_TC), col(k)) for k in range(3)]
        + [pl.BlockSpec((3, HY_TC), par(k)) for k in range(3)]
        + [pl.BlockSpec((1, HY_TC), par(k)) for k in range(3)]
        + [
            pl.BlockSpec((2, 2, L, HY_TC), lambda c, b: (0, 0, 0, c), pipeline_mode=pl.Buffered(1)),
            pl.BlockSpec((2, 8, HY_TC), lambda c, b: (0, 0, c)),
            pl.BlockSpec((2, HY_TC), lambda c, b: (0, c)),
            _const_spec1((L, L)),
            _const_spec1((L, L)),
        ]
    )
    args = [proj, proj, proj, conv_w, conv_w, conv_w, conv_b, conv_b, conv_b, spec, nyq, bias, cmat, smat]
    return pl.pallas_call(
        _hyena_kernel,
        grid=(nct, BATCH),
        in_specs=in_specs,
        out_specs=pl.BlockSpec((L, HY_TC), lambda c, b: (b, c)),
        out_shape=jax.ShapeDtypeStruct((BATCH * L, HY_CH), bf16),
        scratch_shapes=[pltpu.VMEM((L, HY_TC), bf16), pltpu.VMEM((L, HY_TC), bf16)],
        compiler_params=_cparams("arbitrary", "arbitrary"),
        name="hyena_mix_L%d" % L,
    )(*args)


DA_TQ = 512


def _diff_attn_kernel(lam_init, q_ref, qc_ref, kc_ref, vc_ref, kl_ref, vl_ref, lam_ref, subln_ref, o_ref, oc_ref):
    i = pl.program_id(1)
    n_lat_blocks = SEQ // DA_TQ

    @pl.when(i < n_lat_blocks)
    def _():
        _diff_attn_body(lam_init, q_ref, (kc_ref, vc_ref, kl_ref, vl_ref), lam_ref, subln_ref, o_ref)

    @pl.when(i == n_lat_blocks)
    def _():
        _diff_attn_body(lam_init, qc_ref, (kc_ref, vc_ref), lam_ref, subln_ref, oc_ref)


def _diff_attn_body(lam_init, q_ref, kv_refs, lam_ref, subln_ref, o_ref):
    n_src = len(kv_refs) // 2
    lp = lam_ref[...]
    lam = (jnp.exp(jnp.sum(lp[0:1] * lp[1:2], axis=-1, keepdims=True))
           - jnp.exp(jnp.sum(lp[2:3] * lp[3:4], axis=-1, keepdims=True)) + lam_init)
    q = q_ref[...]
    tq = q.shape[0]
    lower = lax.broadcasted_iota(jnp.int32, (tq, 2 * DA_HD), 1) < DA_HD
    zero = jnp.zeros((), bf16)
    hw = 2 * DA_HD
    outs = []
    for h in range(DA_HEADS):
        qh = q[:, h * hw:(h + 1) * hw]
        ks = [kv_refs[2 * s][:, h * hw:(h + 1) * hw] for s in range(n_src)]
        vs = [kv_refs[2 * s + 1][:, h * hw:(h + 1) * hw] for s in range(n_src)]
        qs = jnp.concatenate([jnp.where(lower, qh, zero), jnp.where(lower, zero, qh)], axis=0)
        ss = [_dot_nt(qs, k) for k in ks]
        mx = functools.reduce(jnp.maximum, [jnp.max(s, axis=-1, keepdims=True) for s in ss])
        es = [jnp.exp2(s - mx) for s in ss]
        pv = functools.reduce(jnp.add, [
            _dot(es[s].astype(bf16), jnp.concatenate([vs[s], jnp.ones_like(vs[s])], axis=1)) for s in range(n_src)])
        pv = pv[:, :hw] * (1.0 / pv[:, hw:])
        oh = pv[:tq] - lam * pv[tq:]
        outs.append(_rms(oh) * subln_ref[...] * (1.0 - lam_init))
    o_ref[...] = jnp.concatenate(outs, axis=1).astype(bf16)


def diff_attention(proj, lam_p, subln, lam_init):
    width = DA_HEADS * 2 * DA_HD
    qcol, kcol, vcol = 3, 4, 5
    ctx_blk0 = T_LAT // CTX_LEN
    nq = SEQ // DA_TQ

    def lat_rows(b, i):
        return b * nq + jnp.minimum(i, nq - 1)

    return pl.pallas_call(
        functools.partial(_diff_attn_kernel, lam_init),
        grid=(BATCH, nq + 1),
        in_specs=[
            pl.BlockSpec((DA_TQ, width), lambda b, i: (lat_rows(b, i), qcol)),
            pl.BlockSpec((CTX_LEN, width), lambda b, i: (ctx_blk0 + b, qcol)),
            pl.BlockSpec((CTX_LEN, width), lambda b, i: (ctx_blk0 + b, kcol)),
            pl.BlockSpec((CTX_LEN, width), lambda b, i: (ctx_blk0 + b, vcol)),
            pl.BlockSpec((SEQ, width), lambda b, i: (b, kcol)),
            pl.BlockSpec((SEQ, width), lambda b, i: (b, vcol)),
            _const_spec((4, DA_HD)),
            _const_spec((1, 2 * DA_HD)),
        ],
        out_specs=[
            pl.BlockSpec((DA_TQ, width), lambda b, i: (lat_rows(b, i), 0)),
            pl.BlockSpec((CTX_LEN, width), lambda b, i: (b, 0)),
        ],
        out_shape=[
            jax.ShapeDtypeStruct((T_LAT, width), bf16),
            jax.ShapeDtypeStruct((T_CTX, width), bf16),
        ],
        compiler_params=_cparams("parallel", "arbitrary"),
        name="diff_attention",
    )(proj, proj, proj, proj, proj, proj, lam_p, subln)


def _log_sigmoid(x):
    return jnp.minimum(x, 0.0) - jnp.log(1.0 + jnp.exp(-jnp.abs(x)))


def _retention_kernel(q_ref, k_ref, v_ref, g_ref, kc_ref, vc_ref, decay_ref, gn_ref, o_ref, st_ref):
    h = pl.program_id(1)
    ch = RET_CHUNK
    nchunk = SEQ // ch
    lgs = _log_sigmoid(decay_ref[...])
    sel = lax.broadcasted_iota(jnp.int32, lgs.shape, 1) == h
    lg = jnp.sum(jnp.where(sel, lgs, 0.0), axis=-1, keepdims=True)
    lgf = lg[0:1, :]
    lgb = lg[1:2, :]
    ri = lax.broadcasted_iota(jnp.int32, (ch, ch), 0).astype(f32)
    ci = lax.broadcasted_iota(jnp.int32, (ch, ch), 1).astype(f32)
    rel = ri - ci
    dsum = (jnp.where(rel >= 0, jnp.exp(jnp.maximum(rel, 0.0) * lgf), 0.0)
            + jnp.where(rel <= 0, jnp.exp(jnp.maximum(-rel, 0.0) * lgb), 0.0))
    zeta_f = jnp.exp((ch - 1 - ci) * lgf)
    zeta_b = jnp.exp(ci * lgb)
    xi_f = jnp.exp((ri + 1.0) * lgf)
    xi_b = jnp.exp((ch - ri) * lgb)
    gch_f = jnp.exp(ch * lgf)
    gch_b = jnp.exp(ch * lgb)
    dk = q_ref.shape[1]

    kct = kc_ref[...].astype(f32).T
    vc = vc_ref[...]
    cl = lax.broadcasted_iota(jnp.int32, kct.shape, 1).astype(f32)
    s_f = _dot((kct * jnp.exp((CTX_LEN - 1 - cl) * lgf)).astype(bf16), vc)
    s_b = _dot((kct * jnp.exp(cl * lgb)).astype(bf16), vc)

    def rows(n):
        return slice(n * ch, (n + 1) * ch)

    u_f, u_b = [], []
    for n in range(nchunk):
        kt = k_ref[rows(n), :].astype(f32).T
        vn = v_ref[rows(n), :]
        u_f.append(_dot((kt * zeta_f).astype(bf16), vn))
        u_b.append(_dot((kt * zeta_b).astype(bf16), vn))

    for n in range(nchunk):
        st_ref[n, 0:dk, :] = s_f.astype(bf16)
        s_f = gch_f * s_f + u_f[n]
    for n in reversed(range(nchunk)):
        st_ref[n, dk:2 * dk, :] = s_b.astype(bf16)
        s_b = gch_b * s_b + u_b[n]

    gn = gn_ref[...]
    for n in range(nchunk):
        qn = q_ref[rows(n), :]
        att = _dot_nt(qn, k_ref[rows(n), :]) * dsum
        qf = qn.astype(f32)
        lhs = jnp.concatenate([att.astype(bf16), (qf * xi_f).astype(bf16), (qf * xi_b).astype(bf16)], axis=1)
        rhs = jnp.concatenate([v_ref[rows(n), :], st_ref[n]], axis=0)
        o = _dot(lhs, rhs)
        mu = jnp.mean(o, axis=-1, keepdims=True)
        oc = o - mu
        var = jnp.mean(oc * oc, axis=-1, keepdims=True)
        y = oc * lax.rsqrt(var + EPS) * gn * _silu(g_ref[rows(n), :].astype(f32))
        o_ref[rows(n), :] = y.astype(bf16)


def retention(proj, decay, gn_w):
    dk = RET_DK
    ctx_blk0 = T_LAT // CTX_LEN
    return pl.pallas_call(
        _retention_kernel,
        grid=(BATCH, RET_HEADS),
        in_specs=[
            pl.BlockSpec((SEQ, dk), lambda b, h: (b, h)),
            pl.BlockSpec((SEQ, dk), lambda b, h: (b, RET_HEADS + h)),
            pl.BlockSpec((SEQ, dk), lambda b, h: (b, 2 * RET_HEADS + h)),
            pl.BlockSpec((SEQ, dk), lambda b, h: (b, 3 * RET_HEADS + h)),
            pl.BlockSpec((CTX_LEN, dk), lambda b, h: (ctx_blk0 + b, RET_HEADS + h)),
            pl.BlockSpec((CTX_LEN, dk), lambda b, h: (ctx_blk0 + b, 2 * RET_HEADS + h)),
            _const_spec((2, RET_HEADS)),
            pl.BlockSpec((1, dk), lambda b, h: (0, h)),
        ],
        out_specs=pl.BlockSpec((SEQ, dk), lambda b, h: (b, h)),
        out_shape=jax.ShapeDtypeStruct((T_LAT, RET_HEADS * dk), bf16),
        scratch_shapes=[pltpu.VMEM((SEQ // RET_CHUNK, 2 * dk, dk), bf16)],
        compiler_params=_cparams("parallel", "arbitrary"),
        name="retention",
    )(proj, proj, proj, proj, proj, proj, decay, gn_w)


GQ_TQ = 128
GQ_SPAN = 3 * GQ_TQ
GQ_NB = 8


def _gqa_kernel(q_ref, kv_ref, kvc_ref, sink_ref, o_ref):
    for b in range(GQ_NB):
        rows = slice(b * GQ_TQ, (b + 1) * GQ_TQ)
        o_ref[rows, :] = _gqa_block(pl.program_id(1) * GQ_NB + b, q_ref[rows, :], kv_ref, kvc_ref, sink_ref)


def _gqa_block(n, q, kv_ref, kvc_ref, sink_ref):
    start = pl.multiple_of(jnp.clip((n - 1) * GQ_TQ, 0, SEQ - GQ_SPAN), GQ_TQ)
    pw = 2 * GQ_HD
    n_heads = GQ_KV * GQ_GROUP
    kpos = start + lax.broadcasted_iota(jnp.int32, (GQ_TQ, GQ_SPAN), 1)
    qpos = n * GQ_TQ + lax.broadcasted_iota(jnp.int32, (GQ_TQ, GQ_SPAN), 0)
    mask = jnp.abs(kpos - qpos) <= WINDOW
    lower = lax.broadcasted_iota(jnp.int32, (GQ_TQ, pw), 1) < GQ_HD
    outs = [None] * n_heads
    for swapped in (0, 1):
        kcol = slice(2 * swapped * pw, (2 * swapped + 1) * pw)
        vcol = slice((2 * swapped + 1) * pw, (2 * swapped + 2) * pw)
        k = jnp.concatenate([kvc_ref[:, kcol], kv_ref[pl.ds(start, GQ_SPAN), kcol]], axis=0)
        v = jnp.concatenate([kvc_ref[:, vcol], kv_ref[pl.ds(start, GQ_SPAN), vcol]], axis=0)
        heads = [h for h in range(n_heads) if ((h // GQ_GROUP) == (h % 2)) == (swapped == 0)]
        qs = jnp.concatenate(
            [jnp.where(lower == (h % 2 == 0), q[:, (h // 2) * pw:(h // 2 + 1) * pw], jnp.zeros((), bf16))
             for h in heads], axis=0)
        s = _dot_nt(qs, k)
        es, sink_terms = [], []
        for i, h in enumerate(heads):
            sh = s[i * GQ_TQ:(i + 1) * GQ_TQ]
            sh = jnp.concatenate([sh[:, :CTX_LEN], jnp.where(mask, sh[:, CTX_LEN:], NEG_INF)], axis=1)
            sink = sink_ref[h] * LOG2E
            mx = jnp.maximum(jnp.max(sh, axis=-1, keepdims=True), sink)
            es.append(jnp.exp2(sh - mx).astype(bf16))
            sink_terms.append(jnp.exp2(sink - mx))
        o = _dot(jnp.concatenate(es, axis=0), jnp.concatenate([v, jnp.ones_like(v)], axis=1))
        for i, h in enumerate(heads):
            oh = o[i * GQ_TQ:(i + 1) * GQ_TQ]
            outs[h] = oh[:, :pw] * (1.0 / (oh[:, pw:] + sink_terms[i]))
    return jnp.concatenate(
        [jnp.where(lower, outs[2 * j], outs[2 * j + 1]) for j in range(n_heads // 2)], axis=1).astype(bf16)


def window_gqa(proj, sink):
    width = GQ_KV * GQ_GROUP * GQ_HD
    nq = SEQ // (GQ_NB * GQ_TQ)
    kvw = 4 * GQ_KV * GQ_HD
    kv_col = (5 * SEG) // kvw
    ctx_blk0 = T_LAT // CTX_LEN
    return pl.pallas_call(
        _gqa_kernel,
        grid=(BATCH, nq),
        in_specs=[
            pl.BlockSpec((GQ_NB * GQ_TQ, width), lambda b, n: (b * nq + n, 4)),
            pl.BlockSpec((SEQ, kvw), lambda b, n: (b, kv_col)),
            pl.BlockSpec((CTX_LEN, kvw), lambda b, n: (ctx_blk0 + b, kv_col)),
            pl.BlockSpec(memory_space=pltpu.SMEM),
        ],
        out_specs=pl.BlockSpec((GQ_NB * GQ_TQ, width), lambda b, n: (b * nq + n, 0)),
        out_shape=jax.ShapeDtypeStruct((T_LAT, width), bf16),
        compiler_params=_cparams("parallel", "arbitrary"),
        name="window_gqa",
    )(proj, proj, proj, sink)


OUT_TM = 1024
ROUTE_W = LANES
MOE_TB = 256


def _route(logits):
    lane_i = lax.broadcasted_iota(jnp.int32, logits.shape, 1)
    lane = lane_i.astype(f32)
    big = float(1 << 20)
    valid = lane_i < N_EXPERTS
    le = logits
    lgx = pltpu.roll(logits, ROUTE_W - N_EXPERTS, 1)
    lgx = jnp.where(valid, lgx, NEG_INF)
    gmax = jnp.max(lgx, axis=-1, keepdims=True)
    grp = (lane_i // EXP_PER_GROUP).astype(f32)
    g_sel = jnp.min(jnp.where(lgx == gmax, grp, big), axis=-1, keepdims=True)
    p_grp = float(EXP_PER_GROUP) / jnp.sum(jnp.exp(lgx - gmax), axis=-1, keepdims=True)
    lm = jnp.where(valid, jnp.where(grp == g_sel, le, NEG_INF), NEG_INF)
    v1 = jnp.max(lm, axis=-1, keepdims=True)
    i1 = jnp.min(jnp.where(lm == v1, lane, big), axis=-1, keepdims=True)
    lm2 = jnp.where(lane == i1, NEG_INF, lm)
    v2 = jnp.max(lm2, axis=-1, keepdims=True)
    i2 = jnp.min(jnp.where(lm2 == v2, lane, big), axis=-1, keepdims=True)
    e2 = jnp.exp(v2 - v1)
    w1 = p_grp / (1.0 + e2)
    w2 = w1 * e2
    return jnp.where(lane == i1, w1, 0.0) + jnp.where(lane == i2, w2, 0.0)


def _outproj_kernel(stacked, *refs):
    i = pl.program_id(0)
    if stacked:
        (ya_ref, yac_ref, yb_ref, ybc_ref, w_ref, x_ref, c_ref, mod_ref, wr_ref, br_ref,
         hn_ref, v_ref, comb_ref, cnt_ref) = refs
        ya = _stacked_tile(i, OUT_TM, ya_ref, yac_ref)
        yb = _stacked_tile(i, OUT_TM, yb_ref, ybc_ref)
        h = _stacked_tile(i, OUT_TM, x_ref, c_ref)
    else:
        ya_ref, yb_ref, w_ref, h_ref, mod_ref, wr_ref, br_ref, hn_ref, v_ref, comb_ref, cnt_ref = refs
        ya = ya_ref[...]
        yb = yb_ref[...]
        h = h_ref[...]
    r = _mod_row(i, OUT_TM)
    g1 = mod_ref[pl.ds(r, 1), pl.ds(2 * D_MODEL, D_MODEL)]
    sh2 = mod_ref[pl.ds(r, 1), pl.ds(3 * D_MODEL, D_MODEL)]
    sc2 = mod_ref[pl.ds(r, 1), pl.ds(4 * D_MODEL, D_MODEL)]
    half = ya.shape[1]
    m = _dot(ya, w_ref[0:half, :]) + _dot(yb, w_ref[half:2 * half, :])
    hn = h + g1 * m
    hn_ref[...] = hn
    v = _rms(hn) * (1.0 + sc2) + sh2
    v_ref[...] = v.astype(bf16)
    vh, vl = _split(v)
    prod = _dot(jnp.concatenate([vh, vl], axis=0), wr_ref[...])
    tm = v.shape[0]
    comb = _route(prod[:tm, :ROUTE_W] + prod[:tm, ROUTE_W:] + prod[tm:, :ROUTE_W] + br_ref[...])
    comb_ref[...] = comb
    for s in range(OUT_TM // MOE_TB):
        cnt = jnp.sum((comb[s * MOE_TB:(s + 1) * MOE_TB] != 0.0).astype(f32), axis=0, keepdims=True)
        cnt_ref[s] = jnp.broadcast_to(cnt, (8, ROUTE_W)).astype(jnp.int32)


def out_projection(n_rows, yas, ybs, w, hs, mod, wr, br):
    stacked = len(yas) == 2
    assert stacked == (len(hs) == 2) == (len(ybs) == 2) and (not stacked or n_rows == T_ALL)
    half = ybs[0].shape[1]
    row_spec = lambda width: [pl.BlockSpec((OUT_TM, width), lambda i: (i, 0))]
    rows = lambda width: _stacked_specs(OUT_TM, width) if stacked else row_spec(width)
    wr_hi = wr.astype(bf16)
    wr_lo = (wr - wr_hi.astype(f32)).astype(bf16)
    return pl.pallas_call(
        functools.partial(_outproj_kernel, stacked),
        grid=(n_rows // OUT_TM,),
        in_specs=(
            rows(half) + rows(half)
            + [_const_spec((2 * half, D_MODEL))]
            + rows(D_MODEL)
            + [_const_spec((MOD_ROWS, 6 * D_MODEL)),
               _const_spec((D_MODEL, 2 * ROUTE_W)),
               _const_spec((1, ROUTE_W))]
        ),
        out_specs=[
            pl.BlockSpec((OUT_TM, D_MODEL), lambda i: (i, 0)),
            pl.BlockSpec((OUT_TM, D_MODEL), lambda i: (i, 0)),
            pl.BlockSpec((OUT_TM, ROUTE_W), lambda i: (i, 0)),
            pl.BlockSpec((OUT_TM // MOE_TB, 8, ROUTE_W), lambda i: (i, 0, 0)),
        ],
        out_shape=[
            jax.ShapeDtypeStruct((n_rows, D_MODEL), f32),
            jax.ShapeDtypeStruct((n_rows, D_MODEL), bf16),
            jax.ShapeDtypeStruct((n_rows, ROUTE_W), f32),
            jax.ShapeDtypeStruct((n_rows // MOE_TB, 8, ROUTE_W), jnp.int32),
        ],
        compiler_params=_cparams("parallel"),
        name="out_projection",
    )(*yas, *ybs, w, *hs, mod, jnp.concatenate([wr_hi, wr_lo], axis=1), br)


MOE_UNIT = 16
MOE_TG = 1024
MOE_TOP = 2
MOE_RLOC = MOE_TOP * MOE_TB + N_EXPERTS * MOE_UNIT
MOE_NUNIT = MOE_RLOC // MOE_UNIT
MOE_NB = 4
MOE_SPARE = 2 * MOE_NB
MOE_GAP_UNITS = 8
MOE_META = 128


def _moe_rows(n_blk):
    return n_blk * MOE_RLOC + N_EXPERTS * MOE_TG


MOE_PLAN_ROWS = 128
MOE_PLAN_TILES = 256


def _lane_pick(x, lane, k):
    return jnp.sum(jnp.where(lane == k, x, 0.0), axis=1, keepdims=True)


def _moe_plan_kernel(n_blk, cnt_ref, ltri_ref, utri_ref, tabd_ref, tabc_ref, te_ref, meta_ref):
    shape = (MOE_PLAN_ROWS, ROUTE_W)
    lane = lax.broadcasted_iota(jnp.int32, shape, 1)
    cnt = cnt_ref[...].astype(f32)
    units = jnp.floor((cnt + (MOE_UNIT - 1.0)) * (1.0 / MOE_UNIT))
    ub = units.astype(bf16)
    utri = utri_ref[...]
    pre = _dot(ltri_ref[...], ub)
    lstart = _dot(ub, utri)
    n_e = jnp.sum(units, axis=0, keepdims=True)
    upt = MOE_TG // MOE_UNIT
    tiles_e = jnp.floor((n_e + (upt - 1.0)) * (1.0 / upt))
    goff = _dot(jnp.broadcast_to(tiles_e, (8, ROUTE_W)).astype(bf16), utri)[0:1, :]
    a = goff * upt + pre - lstart
    n_used = jnp.sum(units, axis=1, keepdims=True)
    j = lane.astype(f32)
    acc = jnp.zeros(shape, f32)
    for e in range(N_EXPERTS):
        ls = _lane_pick(lstart, lane, e)
        u = _lane_pick(units, lane, e)
        acc = acc + jnp.where(j >= ls, jnp.where(j < ls + u, _lane_pick(a, lane, e), 0.0), 0.0)
    rows = (acc + j) * MOE_UNIT
    used = j < n_used
    region = (lax.broadcasted_iota(jnp.int32, shape, 0) & (MOE_SPARE - 1)).astype(f32)
    spare = _moe_rows(n_blk) + region * MOE_RLOC + j * MOE_UNIT
    tabd_ref[...] = jnp.where(used, rows, spare).astype(jnp.int32)
    tabc_ref[...] = jnp.where(used, rows, _lane_pick(rows, lane, 0)).astype(jnp.int32)

    ends = goff + tiles_e
    lane1 = lax.broadcasted_iota(jnp.int32, (1, ROUTE_W), 1)
    ti = lax.broadcasted_iota(jnp.int32, (8, MOE_PLAN_TILES), 1).astype(f32)
    te = jnp.zeros((8, MOE_PLAN_TILES), f32)
    for e in range(N_EXPERTS):
        te = te + jnp.where(ti >= _lane_pick(ends, lane1, e), 1.0, 0.0)
    te_ref[...] = jnp.minimum(te, N_EXPERTS - 1.0).astype(jnp.int32)

    n_tiles = jnp.sum(tiles_e, axis=1, keepdims=True)
    gap_start = jnp.broadcast_to((goff * upt + n_e) * MOE_UNIT, (8, ROUTE_W))
    gap_units = jnp.broadcast_to(tiles_e * upt - n_e, (8, ROUTE_W))
    lane8 = lax.broadcasted_iota(jnp.int32, (8, ROUTE_W), 1)
    meta = jnp.where(lane8 == 0, n_tiles,
                     jnp.where(lane8 <= N_EXPERTS, pltpu.roll(gap_start, 1, 1),
                               jnp.where(lane8 <= 2 * N_EXPERTS, pltpu.roll(gap_units, 1 + N_EXPERTS, 1), 0.0)))
    meta_ref[...] = meta.astype(jnp.int32)


def moe_plan(n_blk, counts, ltri, utri):
    assert n_blk <= MOE_PLAN_ROWS and _moe_rows(n_blk) // MOE_TG <= MOE_PLAN_TILES
    shape = (MOE_PLAN_ROWS, ROUTE_W)
    tabd, tabc, te, meta = pl.pallas_call(
        functools.partial(_moe_plan_kernel, n_blk),
        out_shape=[
            jax.ShapeDtypeStruct(shape, jnp.int32),
            jax.ShapeDtypeStruct(shape, jnp.int32),
            jax.ShapeDtypeStruct((8, MOE_PLAN_TILES), jnp.int32),
            jax.ShapeDtypeStruct((8, ROUTE_W), jnp.int32),
        ],
        name="moe_plan",
    )(counts, ltri, utri)
    return tabd.reshape(-1), tabc.reshape(-1), te[0], meta[0]


def _block_routes(comb, ltri, utri):
    oh = comb != 0.0
    ohf = jnp.where(oh, 1.0, 0.0)
    rank = _dot(ltri, ohf.astype(bf16))
    cnt = jnp.sum(ohf, axis=0, keepdims=True)
    units = jnp.floor((cnt + (MOE_UNIT - 1.0)) * (1.0 / MOE_UNIT))
    seg = _dot(jnp.broadcast_to(units, (8, ROUTE_W)).astype(bf16), utri)[0:1, :] * MOE_UNIT
    dest = seg + rank
    big = float(1 << 20)
    d_a = jnp.min(jnp.where(oh, dest, big), axis=-1, keepdims=True)
    d_b = jnp.max(jnp.where(oh, dest, -1.0), axis=-1, keepdims=True)
    w_a = jnp.sum(jnp.where(oh, jnp.where(dest == d_a, comb, 0.0), 0.0), axis=-1, keepdims=True)
    w_b = jnp.sum(jnp.where(oh, jnp.where(dest == d_b, comb, 0.0), 0.0), axis=-1, keepdims=True)
    second = d_b != d_a
    return d_a, jnp.where(second, d_b, -1.0), w_a, jnp.where(second, w_b, 0.0)


def _one_hot_rows(d):
    r = lax.broadcasted_iota(jnp.int32, (d.shape[0], MOE_RLOC), 1).astype(f32)
    return jnp.where(r == d, 1.0, 0.0).astype(bf16)


def _block_gather_matrix(comb, utri_tok, ltri_exp):
    comb_t = comb.T
    oh = comb_t != 0.0
    ohf = jnp.where(oh, 1.0, 0.0)
    rank = _dot(ohf.astype(bf16), utri_tok)
    cnt = jnp.sum(ohf, axis=1, keepdims=True)
    units = jnp.floor((cnt + (MOE_UNIT - 1.0)) * (1.0 / MOE_UNIT))
    seg = _dot(ltri_exp, jnp.broadcast_to(units, (ROUTE_W, ROUTE_W)).astype(bf16))[:, 0:1] * MOE_UNIT
    dest = seg + rank
    d_a = jnp.min(jnp.where(oh, dest, float(1 << 20)), axis=0, keepdims=True)
    d_b = jnp.max(jnp.where(oh, dest, -1.0), axis=0, keepdims=True)
    d_b = jnp.where(d_b != d_a, d_b, -1.0)
    r = lax.broadcasted_iota(jnp.int32, (MOE_RLOC, comb.shape[0]), 0).astype(f32)
    return jnp.where(r == d_a, 1.0, jnp.where(r == d_b, 1.0, 0.0)).astype(bf16)


def _unit_rows(ref, tab_ref, t, j):
    return ref.at[pl.ds(pl.multiple_of(tab_ref[t * ROUTE_W + j], MOE_UNIT), MOE_UNIT)]


def _wait_all_units(local, remote, sem):
    pltpu.make_async_copy(local, remote.at[pl.ds(0, MOE_RLOC)], sem).wait()


def _gap_copies(meta_ref, zero_ref, remote, sem, wait):
    def copy(src, row, n_rows):
        cp = pltpu.make_async_copy(src, remote.at[pl.ds(pl.multiple_of(row, MOE_UNIT), n_rows)], sem)
        if wait:
            cp.wait()
        else:
            cp.start()

    def per_expert(e, c):
        start = meta_ref[1 + e]
        units = meta_ref[1 + N_EXPERTS + e]
        n_big = units // MOE_GAP_UNITS
        big_rows = MOE_GAP_UNITS * MOE_UNIT

        def per_chunk(k, c2):
            copy(zero_ref, start + k * big_rows, big_rows)
            return c2

        def per_unit(u, c2):
            copy(zero_ref.at[pl.ds(0, MOE_UNIT)], start + n_big * big_rows + u * MOE_UNIT, MOE_UNIT)
            return c2

        lax.fori_loop(0, n_big, per_chunk, 0)
        lax.fori_loop(0, units - n_big * MOE_GAP_UNITS, per_unit, 0)
        return c

    lax.fori_loop(0, N_EXPERTS, per_expert, 0)


def _moe_dispatch_kernel(n_blk, tab_ref, meta_ref, x_ref, comb_ref, utri_tok_ref, ltri_exp_ref, xs_ref,
                         buf_ref, zero_ref, sem_ref):
    step = pl.program_id(0)
    slot = step % 2
    for bb in range(MOE_NB):
        tok = slice(bb * MOE_TB, (bb + 1) * MOE_TB)
        p = _block_gather_matrix(comb_ref[tok, :], utri_tok_ref[...], ltri_exp_ref[...])
        buf_ref[slot, bb] = _dot(p, x_ref[tok, :]).astype(bf16)
    for bb in range(MOE_NB):
        for j in range(MOE_NUNIT):
            pltpu.make_async_copy(buf_ref.at[slot, bb, pl.ds(j * MOE_UNIT, MOE_UNIT)],
                                  _unit_rows(xs_ref, tab_ref, step * MOE_NB + bb, j), sem_ref.at[slot]).start()

    def wait_slot(s):
        for bb in range(MOE_NB):
            _wait_all_units(buf_ref.at[s, bb], xs_ref, sem_ref.at[s])

    @pl.when(step > 0)
    def _():
        wait_slot(1 - slot)

    @pl.when(step == n_blk // MOE_NB - 1)
    def _():
        zero_ref[...] = jnp.zeros_like(zero_ref)
        _gap_copies(meta_ref, zero_ref, xs_ref, sem_ref.at[2], False)
        wait_slot(slot)
        _gap_copies(meta_ref, zero_ref, xs_ref, sem_ref.at[2], True)


def moe_dispatch(n_blk, tab, meta, x, comb, utri_tok, ltri_exp):
    return pl.pallas_call(
        functools.partial(_moe_dispatch_kernel, n_blk),
        grid_spec=pltpu.PrefetchScalarGridSpec(
            num_scalar_prefetch=2,
            grid=(n_blk // MOE_NB,),
            in_specs=[
                pl.BlockSpec((MOE_NB * MOE_TB, D_MODEL), lambda t, *_: (t, 0)),
                pl.BlockSpec((MOE_NB * MOE_TB, ROUTE_W), lambda t, *_: (t, 0)),
                pl.BlockSpec((MOE_TB, MOE_TB), lambda t, *_: (0, 0)),
                pl.BlockSpec((ROUTE_W, ROUTE_W), lambda t, *_: (0, 0)),
            ],
            out_specs=pl.BlockSpec(memory_space=pl.ANY),
            scratch_shapes=[
                pltpu.VMEM((2, MOE_NB, MOE_RLOC, D_MODEL), bf16),
                pltpu.VMEM((MOE_GAP_UNITS * MOE_UNIT, D_MODEL), bf16),
                pltpu.SemaphoreType.DMA((3,)),
            ],
        ),
        out_shape=jax.ShapeDtypeStruct((_moe_rows(n_blk) + MOE_SPARE * MOE_RLOC, D_MODEL), bf16),
        compiler_params=_cparams("arbitrary"),
        name="moe_dispatch",
    )(tab, meta, x, comb, utri_tok, ltri_exp)


def _moe_expert_kernel(tile_exp_ref, meta_ref, xs_ref, wg_ref, wu_ref, wd_ref, ys_ref, wgub_ref, wdb_ref):
    i = pl.program_id(0)

    @pl.when(i < meta_ref[0])
    def _():
        prev = tile_exp_ref[jnp.maximum(i - 1, 0)]

        @pl.when(jnp.logical_or(i == 0, tile_exp_ref[i] != prev))
        def _():
            wgub_ref[:, :D_EXPERT] = wg_ref[0].astype(bf16)
            wgub_ref[:, D_EXPERT:] = wu_ref[0].astype(bf16)
            wdb_ref[...] = wd_ref[0].astype(bf16)

        gu = _dot(xs_ref[...], wgub_ref[...])
        a = _silu(gu[:, :D_EXPERT]) * gu[:, D_EXPERT:]
        ys_ref[...] = _dot(a.astype(bf16), wdb_ref[...]).astype(bf16)


def moe_experts(n_blk, layer, tile_exp, meta, xs, w_gate, w_up, w_down):
    n_tiles = _moe_rows(n_blk) // MOE_TG

    def row_map(i, te, meta):
        return (jnp.minimum(i, meta[0] - 1), 0)

    def w_map(i, te, meta):
        return (layer * N_EXPERTS + te[jnp.minimum(i, meta[0] - 1)], 0, 0)

    return pl.pallas_call(
        _moe_expert_kernel,
        grid_spec=pltpu.PrefetchScalarGridSpec(
            num_scalar_prefetch=2,
            grid=(n_tiles,),
            in_specs=[
                pl.BlockSpec((MOE_TG, D_MODEL), row_map),
                pl.BlockSpec((1, D_MODEL, D_EXPERT), w_map),
                pl.BlockSpec((1, D_MODEL, D_EXPERT), w_map),
                pl.BlockSpec((1, D_EXPERT, D_MODEL), w_map),
            ],
            out_specs=pl.BlockSpec((MOE_TG, D_MODEL), row_map),
            scratch_shapes=[
                pltpu.VMEM((D_MODEL, 2 * D_EXPERT), bf16),
                pltpu.VMEM((D_EXPERT, D_MODEL), bf16),
            ],
        ),
        out_shape=jax.ShapeDtypeStruct((_moe_rows(n_blk), D_MODEL), bf16),
        compiler_params=_cparams("arbitrary"),
        name="moe_experts",
    )(tile_exp, meta, xs, w_gate, w_up, w_down)


def _moe_combine_kernel(n_blk, tab_ref, ys_ref, comb_ref, h_ref, mod_ref, ltri_ref, utri_ref,
                        o_ref, buf_ref, sem_ref):
    step = pl.program_id(0)
    slot = step % 2

    def gather(st, s):
        for bb in range(MOE_NB):
            for j in range(MOE_NUNIT):
                pltpu.make_async_copy(_unit_rows(ys_ref, tab_ref, st * MOE_NB + bb, j),
                                      buf_ref.at[s, bb, pl.ds(j * MOE_UNIT, MOE_UNIT)], sem_ref.at[s]).start()

    @pl.when(step == 0)
    def _():
        gather(0, 0)

    @pl.when(step + 1 < n_blk // MOE_NB)
    def _():
        gather(step + 1, 1 - slot)

    for bb in range(MOE_NB):
        _wait_all_units(buf_ref.at[slot, bb], ys_ref, sem_ref.at[slot])
    for bb in range(MOE_NB):
        tok = slice(bb * MOE_TB, (bb + 1) * MOE_TB)
        d_a, d_b, w_a, w_b = _block_routes(comb_ref[tok, :], ltri_ref[...], utri_ref[...])
        p = jnp.concatenate([_one_hot_rows(d_a), _one_hot_rows(d_b)], axis=0)
        picked = _dot(p, buf_ref[slot, bb])
        m = w_a * picked[:MOE_TB] + w_b * picked[MOE_TB:]
        r = _mod_row(step * MOE_NB + bb, MOE_TB)
        g2 = mod_ref[pl.ds(r, 1), pl.ds(5 * D_MODEL, D_MODEL)]
        o_ref[tok, :] = h_ref[tok, :] + g2 * m


def moe_combine(n_blk, tab, ys, comb, h, mod, ltri, utri):
    return pl.pallas_call(
        functools.partial(_moe_combine_kernel, n_blk),
        grid_spec=pltpu.PrefetchScalarGridSpec(
            num_scalar_prefetch=1,
            grid=(n_blk // MOE_NB,),
            in_specs=[
                pl.BlockSpec(memory_space=pl.ANY),
                pl.BlockSpec((MOE_NB * MOE_TB, ROUTE_W), lambda t, *_: (t, 0)),
                pl.BlockSpec((MOE_NB * MOE_TB, D_MODEL), lambda t, *_: (t, 0)),
                pl.BlockSpec((MOD_ROWS, 6 * D_MODEL), lambda t, *_: (0, 0)),
                pl.BlockSpec((MOE_TB, MOE_TB), lambda t, *_: (0, 0)),
                pl.BlockSpec((ROUTE_W, ROUTE_W), lambda t, *_: (0, 0)),
            ],
            out_specs=pl.BlockSpec((MOE_NB * MOE_TB, D_MODEL), lambda t, *_: (t, 0)),
            scratch_shapes=[
                pltpu.VMEM((2, MOE_NB, MOE_RLOC, D_MODEL), bf16),
                pltpu.SemaphoreType.DMA((2,)),
            ],
        ),
        out_shape=jax.ShapeDtypeStruct((n_blk * MOE_TB, D_MODEL), f32),
        compiler_params=_cparams("arbitrary"),
        name="moe_combine",
    )(tab, ys, comb, h, mod, ltri, utri)


def sparse_moe(n_rows, layer, v, comb, counts, h, mod, w_gate, w_up, w_down, ltri, ltri_plan, utri):
    n_blk = n_rows // MOE_TB
    cnt = jnp.pad(counts[:, 0, :], ((0, MOE_PLAN_ROWS - n_blk), (0, 0)))
    tab_d, tab_c, tile_exp, meta = moe_plan(n_blk, cnt, ltri_plan, utri)
    assert MOE_PLAN_ROWS == ROUTE_W
    xs = moe_dispatch(n_blk, tab_d, meta, v, comb, ltri.T, ltri_plan)
    ys = moe_experts(n_blk, layer, tile_exp, meta, xs,
                     w_gate.reshape(DEPTH * N_EXPERTS, D_MODEL, D_EXPERT),
                     w_up.reshape(DEPTH * N_EXPERTS, D_MODEL, D_EXPERT),
                     w_down.reshape(DEPTH * N_EXPERTS, D_EXPERT, D_MODEL))
    return moe_combine(n_blk, tab_c, ys, comb, h, mod, ltri, utri)


def _dft_tables(L):
    k = np.arange(L, dtype=np.int64)
    ang = (2.0 * np.pi / (2 * L)) * ((k[:, None] * k[None, :]) % (2 * L)).astype(np.float64)
    return np.cos(ang).astype(np.float32), np.sin(ang).astype(np.float32)


def _filter_features(L):
    bands = (HY_EMB - 1) // 2
    t = np.linspace(0.0, 1.0, L, dtype=np.float32).astype(np.float64)[:, None]
    w = (2.0 * np.pi / L) * np.arange(L, dtype=np.float64)[:, None]
    fb = np.linspace(1e-4, bands - 1, bands, dtype=np.float32).astype(np.float64)[None, :]
    z = np.concatenate([t, np.cos(fb * w), -np.sin(fb * w)], axis=-1)
    zp = np.zeros((L, FEAT_PAD), np.float32)
    zp[:, :HY_EMB] = z
    deltas = np.abs(np.linspace(HY_MIN_DECAY, HY_MAX_DECAY, HY_CH, dtype=np.float32).astype(np.float64))
    decay = np.exp(-t * deltas[None, :]).astype(np.float32)
    return zp, decay


def _rope_table(cos, sin, half, tm):
    S, width = cos.shape
    low = (np.arange(width) % (2 * half)) < half
    tab = np.zeros((3, S + tm, width), np.float32)
    tab[0, :S] = cos
    tab[0, S:] = 1.0
    tab[1, :S] = np.where(low[None, :], 0.0, sin)
    tab[2, :S] = np.where(low[None, :], -sin, 0.0)
    return tab


def _axial_rope_table(head_dim, tm):
    rows = SEQ // GRID_W
    nf = head_dim // 4
    row = np.repeat(np.arange(rows), GRID_W).astype(np.float64)
    col = np.tile(np.arange(GRID_W), rows).astype(np.float64)
    inv = ROPE_BASE ** (-np.arange(nf, dtype=np.float64) / nf)
    ang = np.stack([row[:, None] * inv, col[:, None] * inv], axis=1)
    a = np.broadcast_to(ang[:, :, None, :], (SEQ, 2, 2, nf)).reshape(SEQ, head_dim)
    reps = LANES // head_dim
    a = np.tile(a, (1, reps))
    return _rope_table(np.cos(a), np.sin(a), nf, tm)


def _seq_rope_table(head_dim, tm):
    inv = 1.0 / (ROPE_BASE ** np.linspace(0.0, 1.0, head_dim // 2, dtype=np.float32).astype(np.float64))
    ang = np.arange(SEQ, dtype=np.float64)[:, None] * inv
    a = np.concatenate([ang, ang], axis=1)
    return _rope_table(np.cos(a), np.sin(a), head_dim // 2, tm)


def _group_mean_matrix():
    g = np.arange(SEG) // DA_HD
    return (g[:, None] == g[None, :]).astype(np.float32) / DA_HD


def _router_weights(w_grp, b_grp, w_rt, b_rt):
    pad = ROUTE_W - 2 * N_EXPERTS
    wr = jnp.concatenate([w_rt, jnp.repeat(w_grp, EXP_PER_GROUP, axis=1),
                          jnp.zeros((D_MODEL, pad), f32)], axis=1)
    br = jnp.concatenate([b_rt, jnp.repeat(b_grp, EXP_PER_GROUP), jnp.zeros((pad,), f32)])[None, :]
    return wr, br


def _strict_lower(n):
    i = np.arange(n)
    return (i[None, :] < i[:, None]).astype(np.float32)


def kernel(x, c, ctx, c_ctx, ada_w, ada_b, e_w_in, e_w_out, hy_conv_w, hy_conv_b, hy_f_w1, hy_f_b1, hy_f_w2, hy_f_b2, hy_f_w3, hy_f_freq, hy_bias, da_q_norm, da_k_norm, da_lam, da_subln, o_w_in, o_w_out, ret_decay, ret_gn, gq_q_norm, gq_k_norm, gq_sink, moe_w_grp, moe_b_grp, moe_w_rt, moe_b_rt, moe_w_gate, moe_w_up, moe_w_down):
    assert x.shape == (BATCH, SEQ, D_MODEL) and ctx.shape == (BATCH, CTX_LEN, D_MODEL)
    x_rows = x.reshape(T_LAT, D_MODEL)
    ctx_rows = ctx.reshape(T_CTX, D_MODEL)
    c_rows = jnp.concatenate([c, c_ctx[None, :], jnp.zeros((MOD_ROWS - BATCH - 1, D_MODEL), f32)], axis=0)
    mod = ada_modulation(c_rows, ada_w, ada_b)

    gmat = jnp.asarray(_group_mean_matrix()).astype(bf16)
    ax_tab = jnp.asarray(_axial_rope_table(DA_HD, PROJ_TM))
    r1_tab = jnp.asarray(_seq_rope_table(RET_DK, PROJ_TM))
    ones = jnp.ones((SEG,), f32)

    lam_init0 = 0.8 - 0.6 * math.exp(-0.3 * 0)
    reps = SEG // DA_HD
    gain0 = jnp.concatenate([ones, ones, ones, jnp.tile(da_q_norm[0], reps) * (DA_HD ** -0.5 * LOG2E),
                             jnp.tile(da_k_norm[0], reps), ones])[None, :]
    proj0 = in_projection("even", [x_rows, ctx_rows], mod[0], e_w_in[0].astype(bf16), gain0, gmat, [ax_tab])

    w3r = hy_f_w3[0].reshape(HY_FILT_HID, 4, HY_CH).transpose(1, 0, 2)
    w1p = jnp.concatenate([hy_f_w1[0], jnp.zeros((FEAT_PAD - HY_EMB, HY_FILT_HID), f32)], axis=0)
    y_hy = []
    for L, blk0 in ((SEQ, 0), (CTX_LEN, T_LAT // CTX_LEN)):
        zfeat, decay = _filter_features(L)
        cm, sm = _dft_tables(L)
        cm = jnp.asarray(cm).astype(bf16)
        sm = jnp.asarray(sm).astype(bf16)
        spec, nyq = hyena_filter_spectra(L, jnp.asarray(zfeat), w1p, hy_f_b1[0][None, :], hy_f_w2[0],
                                         hy_f_b2[0][None, :], w3r, hy_f_freq[0], jnp.asarray(decay), cm, sm)
        y_hy.append(hyena_mix(L, blk0, proj0, hy_conv_w[0], hy_conv_b[0][None, :], spec, nyq, hy_bias[0], cm, sm))

    y_da = diff_attention(proj0, da_lam[0], da_subln[0][None, :], lam_init0)

    wr0, br0 = _router_weights(moe_w_grp[0], moe_b_grp[0], moe_w_rt[0], moe_b_rt[0])
    ltri = jnp.asarray(_strict_lower(MOE_TB)).astype(bf16)
    ltri_plan = jnp.asarray(_strict_lower(MOE_PLAN_ROWS)).astype(bf16)
    utri = jnp.asarray(_strict_lower(ROUTE_W).T).astype(bf16)
    h, v, comb, counts = out_projection(T_ALL, y_hy, y_da, e_w_out[0].astype(bf16), [x_rows, ctx_rows], mod[0],
                                        wr0, br0)
    h = sparse_moe(T_ALL, 0, v, comb, counts, h, mod[0], moe_w_gate, moe_w_up, moe_w_down, ltri, ltri_plan, utri)

    w_in1 = jnp.concatenate([o_w_in[0], jnp.zeros((D_MODEL, PROJ_W - o_w_in.shape[2]), f32)], axis=1).astype(bf16)
    kq = GQ_KV * GQ_HD
    gain1 = jnp.concatenate([ones, ones * RET_DK ** -0.5, ones, ones,
                             jnp.tile(gq_q_norm[0], reps) * (GQ_HD ** -0.5 * LOG2E),
                             jnp.tile(gq_k_norm[0], kq // GQ_HD), jnp.ones((SEG - kq,), f32)])[None, :]
    proj1 = in_projection("odd", [h], mod[1], w_in1, gain1, gmat, [ax_tab, r1_tab])
    y_ret = retention(proj1, ret_decay[0], ret_gn[0][None, :])
    y_gq = window_gqa(proj1, gq_sink[0])
    wr1, br1 = _router_weights(moe_w_grp[1], moe_b_grp[1], moe_w_rt[1], moe_b_rt[1])
    h_lat, v, comb, counts = out_projection(T_LAT, [y_ret], [y_gq], o_w_out[0].astype(bf16), [h], mod[1], wr1, br1)
    out = sparse_moe(T_LAT, 1, v, comb, counts, h_lat, mod[1], moe_w_gate, moe_w_up, moe_w_down, ltri, ltri_plan,
                     utri)
    return out.reshape(BATCH, SEQ, D_MODEL)
```

```python
import functools
import math

import numpy as np
import jax
import jax.numpy as jnp
from jax import lax
from jax.experimental import pallas as pl
from jax.experimental.pallas import tpu as pltpu

f32 = jnp.float32
bf16 = jnp.bfloat16

D_MODEL = 1024
BATCH = 8
SEQ = 2048
DEPTH = 2
GRID_W = 64
CTX_LEN = 256
EPS = 1e-6
NEG_INF = -1e30
LOG2E = math.log2(math.e)
ROPE_BASE = 10000.0
HY_CH = D_MODEL // 2
HY_EMB = 33
HY_FILT_HID = 64
HY_MAX_DECAY = math.log(1e-2) / 0.3
HY_MIN_DECAY = math.log(1e-2) / 1.5
DA_HEADS = 4
DA_HD = D_MODEL // 16
RET_HEADS = 4
RET_DK = D_MODEL // 8
RET_CHUNK = 128
GQ_KV = 2
GQ_GROUP = 4
GQ_HD = D_MODEL // 16
WINDOW = 128
N_GROUPS = 4
EXP_PER_GROUP = 8
N_EXPERTS = N_GROUPS * EXP_PER_GROUP
D_EXPERT = D_MODEL // 4

T_LAT = BATCH * SEQ
T_CTX = BATCH * CTX_LEN
T_ALL = T_LAT + T_CTX
PROJ_W = 3072
SEG = 512
CTX_MOD_ROW = BATCH
MOD_ROWS = 16

LANES = 128
VMEM_LIMIT_BYTES = 56 * 1024 * 1024


def _cparams(*sem):
    return pltpu.CompilerParams(dimension_semantics=sem, vmem_limit_bytes=VMEM_LIMIT_BYTES)


def _dot(a, b):
    return jnp.dot(a, b, preferred_element_type=f32)


def _dot_nt(a, b):
    return lax.dot_general(a, b, (((1,), (1,)), ((), ())), preferred_element_type=f32)


def _split(x):
    hi = x.astype(bf16)
    lo = (x - hi.astype(f32)).astype(bf16)
    return hi, lo


def _dot3(a, b):
    ah, al = _split(a)
    bh, bl = _split(b)
    return _dot(ah, bh) + _dot(al, bh) + _dot(ah, bl)


def _silu(x):
    return x * jax.nn.sigmoid(x)


def _rms(x):
    return x * lax.rsqrt(jnp.mean(x * x, axis=-1, keepdims=True) + EPS)


def _const_spec(shape):
    nd = len(shape)
    return pl.BlockSpec(shape, lambda *_: (0,) * nd)


def _const_spec1(shape):
    nd = len(shape)
    return pl.BlockSpec(shape, lambda *_: (0,) * nd, pipeline_mode=pl.Buffered(1))


ADA_TN = 1536


def _ada_kernel(c_ref, w_ref, b_ref, o_ref):
    x = _silu(c_ref[...])
    o_ref[0] = _dot3(x, w_ref[0]) + b_ref[0]


def ada_modulation(c_rows, ada_w, ada_b):
    n = 6 * D_MODEL
    return pl.pallas_call(
        _ada_kernel,
        grid=(DEPTH, n // ADA_TN),
        in_specs=[
            pl.BlockSpec((MOD_ROWS, D_MODEL), lambda l, j: (0, 0)),
            pl.BlockSpec((1, D_MODEL, ADA_TN), lambda l, j: (l, 0, j)),
            pl.BlockSpec((1, 1, ADA_TN), lambda l, j: (l, 0, j)),
        ],
        out_specs=pl.BlockSpec((1, MOD_ROWS, ADA_TN), lambda l, j: (l, 0, j)),
        out_shape=jax.ShapeDtypeStruct((DEPTH, MOD_ROWS, n), f32),
        compiler_params=_cparams("arbitrary", "arbitrary"),
        name="ada_modulation",
    )(c_rows, ada_w, ada_b.reshape(DEPTH, 1, n))


PROJ_TM = 1024


def _mod_row(i, tm):
    return jnp.minimum((i * tm) // SEQ, CTX_MOD_ROW)


def _tile4(t):
    return jnp.concatenate([t, t, t, t], axis=1)


def _group_norm64(y, gmat):
    ms = _dot((y * y).astype(bf16), gmat)
    return y * lax.rsqrt(ms + EPS)


def _rope(y, tab, shift):
    w = y.shape[1]
    return y * tab[0] + pltpu.roll(y, shift, 1) * tab[1] + pltpu.roll(y, w - shift, 1) * tab[2]


def _stacked_specs(tm, width):
    n_lat = T_LAT // tm
    return [pl.BlockSpec((tm, width), lambda i: (jnp.minimum(i, n_lat - 1), 0)),
            pl.BlockSpec((tm, width), lambda i: (jnp.maximum(i - n_lat, 0), 0))]


def _stacked_tile(i, tm, lat_ref, ctx_ref):
    return jnp.where(i < T_LAT // tm, lat_ref[...], ctx_ref[...])


def _inproj_kernel(layer_kind, *refs):
    i = pl.program_id(0)
    if layer_kind == "even":
        x_ref, c_ref, mod_ref, w_ref, gain_ref, gmat_ref, ax_ref, o_ref = refs
        h = _stacked_tile(i, PROJ_TM, x_ref, c_ref)
    else:
        h_ref, mod_ref, w_ref, gain_ref, gmat_ref, ax_ref, r1_ref, o_ref = refs
        h = h_ref[...]
    r = _mod_row(i, PROJ_TM)
    sh = mod_ref[pl.ds(r, 1), pl.ds(0, D_MODEL)]
    sc = mod_ref[pl.ds(r, 1), pl.ds(D_MODEL, D_MODEL)]
    u = (_rms(h) * (1.0 + sc) + sh).astype(bf16)

    def seg(j):
        return _dot(u, w_ref[:, j * SEG:(j + 1) * SEG])

    def put(j, y):
        o_ref[:, j * SEG:(j + 1) * SEG] = y.astype(bf16)

    def gain(j):
        return gain_ref[:, j * SEG:(j + 1) * SEG]

    gmat = gmat_ref[...]
    ax = ax_ref[...]
    ax4 = (_tile4(ax[0]), _tile4(ax[1]), _tile4(ax[2]))
    if layer_kind == "even":
        for j in (3, 4):
            put(j, _rope(_group_norm64(seg(j), gmat) * gain(j), ax4, DA_HD // 4))
        for j in (0, 1, 2, 5):
            put(j, seg(j))
    else:
        r1 = r1_ref[...]
        r14 = (_tile4(r1[0]), _tile4(r1[1]), _tile4(r1[2]))
        put(4, _rope(_group_norm64(seg(4), gmat) * gain(4), ax4, GQ_HD // 4))
        y = seg(5)
        kw = GQ_KV * GQ_HD
        yk = _rope(_group_norm64(y[:, :kw], gmat[:kw, :kw]) * gain(5)[:, :kw], ax, GQ_HD // 4)
        yv = y[:, kw:2 * kw]
        pieces = (yk, yv, pltpu.roll(yk, GQ_HD, 1), pltpu.roll(yv, GQ_HD, 1))
        for p, piece in enumerate(pieces):
            o_ref[:, 5 * SEG + p * kw:5 * SEG + (p + 1) * kw] = piece.astype(bf16)
        for j in (0, 1):
            put(j, _rope(seg(j) * gain(j), r14, RET_DK // 2))
        for j in (2, 3):
            put(j, seg(j))


def in_projection(layer_kind, hs, mod, w, gain, gmat, tables):
    n_lat_tiles = T_LAT // PROJ_TM
    n_pos_tiles = SEQ // PROJ_TM

    def tab_map(i):
        return (0, jnp.where(i < n_lat_tiles, i % n_pos_tiles, n_pos_tiles), 0)

    tab_specs = [pl.BlockSpec((3, PROJ_TM, LANES), tab_map) for _ in tables]
    if layer_kind == "even":
        h_specs = _stacked_specs(PROJ_TM, D_MODEL)
    else:
        h_specs = [pl.BlockSpec((PROJ_TM, D_MODEL), lambda i: (i, 0))]
    return pl.pallas_call(
        functools.partial(_inproj_kernel, layer_kind),
        grid=(T_ALL // PROJ_TM,),
        in_specs=h_specs + [
            _const_spec((MOD_ROWS, 6 * D_MODEL)),
            _const_spec((D_MODEL, PROJ_W)),
            _const_spec((1, PROJ_W)),
            _const_spec((SEG, SEG)),
        ] + tab_specs,
        out_specs=pl.BlockSpec((PROJ_TM, PROJ_W), lambda i: (i, 0)),
        out_shape=jax.ShapeDtypeStruct((T_ALL, PROJ_W), bf16),
        compiler_params=_cparams("parallel"),
        name="in_projection_" + layer_kind,
    )(*hs, mod, w, gain, gmat, *tables)


HY_TC = 256
HY_FREQ_CHUNK = 512
FEAT_PAD = 64


def _alt_sign(shape, axis):
    idx = lax.broadcasted_iota(jnp.int32, shape, axis)
    return jnp.where((idx & 1) == 0, 1.0, -1.0).astype(f32)


def _filter_kernel(L, z_ref, w1_ref, b1_ref, w2_ref, b2_ref, wf_ref, wb_ref, freq_ref, dec_ref, c_ref, s_ref,
                   spec_ref, nyq_ref):
    hid = jnp.sin(freq_ref[0:1, :] * (_dot3(z_ref[...], w1_ref[...]) + b1_ref[...]))
    hid = jnp.sin(freq_ref[1:2, :] * (_dot3(hid, w2_ref[...]) + b2_ref[...]))
    dec = dec_ref[...]
    fwd = _dot3(hid, wf_ref[0]) * dec
    bwd = _dot3(hid, wb_ref[0]) * dec
    row = lax.broadcasted_iota(jnp.int32, fwd.shape, 0)
    bwd = jnp.where(row == 0, 0.0, bwd)
    even = fwd + bwd
    odd = bwd - fwd
    wk = jnp.where(row == 0, 0.5 / L, 1.0 / L).astype(f32)
    spec_ref[0, 0] = _dot(c_ref[...], even.astype(bf16)) * wk
    spec_ref[0, 1] = _dot(s_ref[...], odd.astype(bf16)) * wk
    nyq = jnp.sum(even * _alt_sign(even.shape, 0), axis=0, keepdims=True) * (0.5 / L)
    nyq_ref[0] = jnp.broadcast_to(nyq, (8, nyq.shape[1]))


def hyena_filter_spectra(L, zfeat, w1, b1, w2, b2, w3r, freq, decay, cmat, smat):
    nct = HY_CH // HY_TC
    return pl.pallas_call(
        functools.partial(_filter_kernel, L),
        grid=(2, nct),
        in_specs=[
            _const_spec((L, FEAT_PAD)),
            _const_spec((FEAT_PAD, HY_FILT_HID)),
            _const_spec((1, HY_FILT_HID)),
            _const_spec((HY_FILT_HID, HY_FILT_HID)),
            _const_spec((1, HY_FILT_HID)),
            pl.BlockSpec((1, HY_FILT_HID, HY_TC), lambda n, c: (2 * n, 0, c)),
            pl.BlockSpec((1, HY_FILT_HID, HY_TC), lambda n, c: (2 * n + 1, 0, c)),
            _const_spec((2, HY_FILT_HID)),
            pl.BlockSpec((L, HY_TC), lambda n, c: (0, c)),
            _const_spec1((L, L)),
            _const_spec1((L, L)),
        ],
        out_specs=[
            pl.BlockSpec((1, 2, L, HY_TC), lambda n, c: (n, 0, 0, c)),
            pl.BlockSpec((1, 8, HY_TC), lambda n, c: (n, 0, c)),
        ],
        out_shape=[
            jax.ShapeDtypeStruct((2, 2, L, HY_CH), f32),
            jax.ShapeDtypeStruct((2, 8, HY_CH), f32),
        ],
        compiler_params=_cparams("arbitrary", "arbitrary"),
        name="hyena_filter_L%d" % L,
    )(zfeat, w1, b1, w2, b2, w3r, w3r, freq, decay, cmat, smat)


def _conv3(u, w, b):
    L = u.shape[0]
    row = lax.broadcasted_iota(jnp.int32, u.shape, 0)
    prev = jnp.where(row == 0, 0.0, pltpu.roll(u, 1, 0))
    nxt = jnp.where(row == L - 1, 0.0, pltpu.roll(u, L - 1, 0))
    return prev * w[0:1, :] + u * w[1:2, :] + nxt * w[2:3, :] + b


def _hyena_kernel(v_ref, x1_ref, x2_ref, wv_ref, w1_ref, w2_ref, bv_ref, b1_ref, b2_ref, spec_ref, nyq_ref,
                  bias_ref, c_ref, s_ref, o_ref, yr_ref, yi_ref):
    L = v_ref.shape[0]
    fch = min(L, HY_FREQ_CHUNK)
    z = _conv3(v_ref[...].astype(f32), wv_ref[...], bv_ref[...])
    gate_refs = ((x1_ref, w1_ref, b1_ref), (x2_ref, w2_ref, b2_ref))
    alt = _alt_sign(z.shape, 0)
    for n in range(2):
        zb = z.astype(bf16)
        for k in range(L // fch):
            rows = slice(k * fch, (k + 1) * fch)
            a = _dot(c_ref[rows, :], zb)
            b = _dot(s_ref[rows, :], zb)
            hr = spec_ref[n, 0, rows, :]
            hi = spec_ref[n, 1, rows, :]
            yr_ref[rows, :] = (a * hr + b * hi).astype(bf16)
            yi_ref[rows, :] = (a * hi - b * hr).astype(bf16)
        x_nyq = jnp.sum(z * alt, axis=0, keepdims=True)
        y = (_dot(c_ref[...], yr_ref[...]) - _dot(s_ref[...], yi_ref[...])
             + alt * (x_nyq * nyq_ref[n, 0:1, :]))
        x_ref, w_ref, b_ref = gate_refs[n]
        gate = _conv3(x_ref[...].astype(f32), w_ref[...], b_ref[...])
        z = gate * (y + z * bias_ref[n:n + 1, :])
    o_ref[...] = z.astype(bf16)


def hyena_mix(L, row_block0, proj, conv_w, conv_b, spec, nyq, bias, cmat, smat):
    nct = HY_CH // HY_TC
    nseg = HY_CH // HY_TC

    def col(k):
        return lambda c, b: (row_block0 + b, k * nseg + c)

    def par(k):
        return lambda c, b: (0, k * nseg + c)

    in_specs = (
        [pl.BlockSpec((L, HY_TC), col(k)) for k in range(3)]
        + [pl.BlockSpec((3, HY_TC), par(k)) for k in range(3)]
        + [pl.BlockSpec((1, HY_TC), par(k)) for k in range(3)]
        + [
            pl.BlockSpec((2, 2, L, HY_TC), lambda c, b: (0, 0, 0, c), pipeline_mode=pl.Buffered(1)),
            pl.BlockSpec((2, 8, HY_TC), lambda c, b: (0, 0, c)),
            pl.BlockSpec((2, HY_TC), lambda c, b: (0, c)),
            _const_spec1((L, L)),
            _const_spec1((L, L)),
        ]
    )
    args = [proj, proj, proj, conv_w, conv_w, conv_w, conv_b, conv_b, conv_b, spec, nyq, bias, cmat, smat]
    return pl.pallas_call(
        _hyena_kernel,
        grid=(nct, BATCH),
        in_specs=in_specs,
        out_specs=pl.BlockSpec((L, HY_TC), lambda c, b: (b, c)),
        out_shape=jax.ShapeDtypeStruct((BATCH * L, HY_CH), bf16),
        scratch_shapes=[pltpu.VMEM((L, HY_TC), bf16), pltpu.VMEM((L, HY_TC), bf16)],
        compiler_params=_cparams("arbitrary", "arbitrary"),
        name="hyena_mix_L%d" % L,
    )(*args)


DA_TQ = 512


def _diff_attn_kernel(lam_init, q_ref, qc_ref, kc_ref, vc_ref, kl_ref, vl_ref, lam_ref, subln_ref, o_ref, oc_ref):
    i = pl.program_id(1)
    n_lat_blocks = SEQ // DA_TQ

    @pl.when(i < n_lat_blocks)
    def _():
        _diff_attn_body(lam_init, q_ref, (kc_ref, vc_ref, kl_ref, vl_ref), lam_ref, subln_ref, o_ref)

    @pl.when(i == n_lat_blocks)
    def _():
        _diff_attn_body(lam_init, qc_ref, (kc_ref, vc_ref), lam_ref, subln_ref, oc_ref)


def _diff_attn_body(lam_init, q_ref, kv_refs, lam_ref, subln_ref, o_ref):
    n_src = len(kv_refs) // 2
    lp = lam_ref[...]
    lam = (jnp.exp(jnp.sum(lp[0:1] * lp[1:2], axis=-1, keepdims=True))
           - jnp.exp(jnp.sum(lp[2:3] * lp[3:4], axis=-1, keepdims=True)) + lam_init)
    q = q_ref[...]
    tq = q.shape[0]
    lower = lax.broadcasted_iota(jnp.int32, (tq, 2 * DA_HD), 1) < DA_HD
    zero = jnp.zeros((), bf16)
    hw = 2 * DA_HD
    outs = []
    for h in range(DA_HEADS):
        qh = q[:, h * hw:(h + 1) * hw]
        ks = [kv_refs[2 * s][:, h * hw:(h + 1) * hw] for s in range(n_src)]
        vs = [kv_refs[2 * s + 1][:, h * hw:(h + 1) * hw] for s in range(n_src)]
        qs = jnp.concatenate([jnp.where(lower, qh, zero), jnp.where(lower, zero, qh)], axis=0)
        ss = [_dot_nt(qs, k) for k in ks]
        mx = functools.reduce(jnp.maximum, [jnp.max(s, axis=-1, keepdims=True) for s in ss])
        es = [jnp.exp2(s - mx) for s in ss]
        pv = functools.reduce(jnp.add, [
            _dot(es[s].astype(bf16), jnp.concatenate([vs[s], jnp.ones_like(vs[s])], axis=1)) for s in range(n_src)])
        pv = pv[:, :hw] * (1.0 / pv[:, hw:])
        oh = pv[:tq] - lam * pv[tq:]
        outs.append(_rms(oh) * subln_ref[...] * (1.0 - lam_init))
    o_ref[...] = jnp.concatenate(outs, axis=1).astype(bf16)


def diff_attention(proj, lam_p, subln, lam_init):
    width = DA_HEADS * 2 * DA_HD
    qcol, kcol, vcol = 3, 4, 5
    ctx_blk0 = T_LAT // CTX_LEN
    nq = SEQ // DA_TQ

    def lat_rows(b, i):
        return b * nq + jnp.minimum(i, nq - 1)

    return pl.pallas_call(
        functools.partial(_diff_attn_kernel, lam_init),
        grid=(BATCH, nq + 1),
        in_specs=[
            pl.BlockSpec((DA_TQ, width), lambda b, i: (lat_rows(b, i), qcol)),
            pl.BlockSpec((CTX_LEN, width), lambda b, i: (ctx_blk0 + b, qcol)),
            pl.BlockSpec((CTX_LEN, width), lambda b, i: (ctx_blk0 + b, kcol)),
            pl.BlockSpec((CTX_LEN, width), lambda b, i: (ctx_blk0 + b, vcol)),
            pl.BlockSpec((SEQ, width), lambda b, i: (b, kcol)),
            pl.BlockSpec((SEQ, width), lambda b, i: (b, vcol)),
            _const_spec((4, DA_HD)),
            _const_spec((1, 2 * DA_HD)),
        ],
        out_specs=[
            pl.BlockSpec((DA_TQ, width), lambda b, i: (lat_rows(b, i), 0)),
            pl.BlockSpec((CTX_LEN, width), lambda b, i: (b, 0)),
        ],
        out_shape=[
            jax.ShapeDtypeStruct((T_LAT, width), bf16),
            jax.ShapeDtypeStruct((T_CTX, width), bf16),
        ],
        compiler_params=_cparams("parallel", "arbitrary"),
        name="diff_attention",
    )(proj, proj, proj, proj, proj, proj, lam_p, subln)


def _log_sigmoid(x):
    return jnp.minimum(x, 0.0) - jnp.log(1.0 + jnp.exp(-jnp.abs(x)))


def _retention_kernel(q_ref, k_ref, v_ref, g_ref, kc_ref, vc_ref, decay_ref, gn_ref, o_ref, st_ref):
    h = pl.program_id(1)
    ch = RET_CHUNK
    nchunk = SEQ // ch
    lgs = _log_sigmoid(decay_ref[...])
    sel = lax.broadcasted_iota(jnp.int32, lgs.shape, 1) == h
    lg = jnp.sum(jnp.where(sel, lgs, 0.0), axis=-1, keepdims=True)
    lgf = lg[0:1, :]
    lgb = lg[1:2, :]
    ri = lax.broadcasted_iota(jnp.int32, (ch, ch), 0).astype(f32)
    ci = lax.broadcasted_iota(jnp.int32, (ch, ch), 1).astype(f32)
    rel = ri - ci
    dsum = (jnp.where(rel >= 0, jnp.exp(jnp.maximum(rel, 0.0) * lgf), 0.0)
            + jnp.where(rel <= 0, jnp.exp(jnp.maximum(-rel, 0.0) * lgb), 0.0))
    zeta_f = jnp.exp((ch - 1 - ci) * lgf)
    zeta_b = jnp.exp(ci * lgb)
    xi_f = jnp.exp((ri + 1.0) * lgf)
    xi_b = jnp.exp((ch - ri) * lgb)
    gch_f = jnp.exp(ch * lgf)
    gch_b = jnp.exp(ch * lgb)
    dk = q_ref.shape[1]

    kct = kc_ref[...].astype(f32).T
    vc = vc_ref[...]
    cl = lax.broadcasted_iota(jnp.int32, kct.shape, 1).astype(f32)
    s_f = _dot((kct * jnp.exp((CTX_LEN - 1 - cl) * lgf)).astype(bf16), vc)
    s_b = _dot((kct * jnp.exp(cl * lgb)).astype(bf16), vc)

    def rows(n):
        return slice(n * ch, (n + 1) * ch)

    u_f, u_b = [], []
    for n in range(nchunk):
        kt = k_ref[rows(n), :].astype(f32).T
        vn = v_ref[rows(n), :]
        u_f.append(_dot((kt * zeta_f).astype(bf16), vn))
        u_b.append(_dot((kt * zeta_b).astype(bf16), vn))

    for n in range(nchunk):
        st_ref[n, 0:dk, :] = s_f.astype(bf16)
        s_f = gch_f * s_f + u_f[n]
    for n in reversed(range(nchunk)):
        st_ref[n, dk:2 * dk, :] = s_b.astype(bf16)
        s_b = gch_b * s_b + u_b[n]

    gn = gn_ref[...]
    for n in range(nchunk):
        qn = q_ref[rows(n), :]
        att = _dot_nt(qn, k_ref[rows(n), :]) * dsum
        qf = qn.astype(f32)
        lhs = jnp.concatenate([att.astype(bf16), (qf * xi_f).astype(bf16), (qf * xi_b).astype(bf16)], axis=1)
        rhs = jnp.concatenate([v_ref[rows(n), :], st_ref[n]], axis=0)
        o = _dot(lhs, rhs)
        mu = jnp.mean(o, axis=-1, keepdims=True)
        oc = o - mu
        var = jnp.mean(oc * oc, axis=-1, keepdims=True)
        y = oc * lax.rsqrt(var + EPS) * gn * _silu(g_ref[rows(n), :].astype(f32))
        o_ref[rows(n), :] = y.astype(bf16)


def retention(proj, decay, gn_w):
    dk = RET_DK
    ctx_blk0 = T_LAT // CTX_LEN
    return pl.pallas_call(
        _retention_kernel,
        grid=(BATCH, RET_HEADS),
        in_specs=[
            pl.BlockSpec((SEQ, dk), lambda b, h: (b, h)),
            pl.BlockSpec((SEQ, dk), lambda b, h: (b, RET_HEADS + h)),
            pl.BlockSpec((SEQ, dk), lambda b, h: (b, 2 * RET_HEADS + h)),
            pl.BlockSpec((SEQ, dk), lambda b, h: (b, 3 * RET_HEADS + h)),
            pl.BlockSpec((CTX_LEN, dk), lambda b, h: (ctx_blk0 + b, RET_HEADS + h)),
            pl.BlockSpec((CTX_LEN, dk), lambda b, h: (ctx_blk0 + b, 2 * RET_HEADS + h)),
            _const_spec((2, RET_HEADS)),
            pl.BlockSpec((1, dk), lambda b, h: (0, h)),
        ],
        out_specs=pl.BlockSpec((SEQ, dk), lambda b, h: (b, h)),
        out_shape=jax.ShapeDtypeStruct((T_LAT, RET_HEADS * dk), bf16),
        scratch_shapes=[pltpu.VMEM((SEQ // RET_CHUNK, 2 * dk, dk), bf16)],
        compiler_params=_cparams("parallel", "arbitrary"),
        name="retention",
    )(proj, proj, proj, proj, proj, proj, decay, gn_w)


GQ_TQ = 128
GQ_SPAN = 3 * GQ_TQ
GQ_NB = 8


def _gqa_kernel(q_ref, kv_ref, kvc_ref, sink_ref, o_ref):
    for b in range(GQ_NB):
        rows = slice(b * GQ_TQ, (b + 1) * GQ_TQ)
        o_ref[rows, :] = _gqa_block(pl.program_id(1) * GQ_NB + b, q_ref[rows, :], kv_ref, kvc_ref, sink_ref)


def _gqa_block(n, q, kv_ref, kvc_ref, sink_ref):
    start = pl.multiple_of(jnp.clip((n - 1) * GQ_TQ, 0, SEQ - GQ_SPAN), GQ_TQ)
    pw = 2 * GQ_HD
    n_heads = GQ_KV * GQ_GROUP
    kpos = start + lax.broadcasted_iota(jnp.int32, (GQ_TQ, GQ_SPAN), 1)
    qpos = n * GQ_TQ + lax.broadcasted_iota(jnp.int32, (GQ_TQ, GQ_SPAN), 0)
    mask = jnp.abs(kpos - qpos) <= WINDOW
    lower = lax.broadcasted_iota(jnp.int32, (GQ_TQ, pw), 1) < GQ_HD
    outs = [None] * n_heads
    for swapped in (0, 1):
        kcol = slice(2 * swapped * pw, (2 * swapped + 1) * pw)
        vcol = slice((2 * swapped + 1) * pw, (2 * swapped + 2) * pw)
        k = jnp.concatenate([kvc_ref[:, kcol], kv_ref[pl.ds(start, GQ_SPAN), kcol]], axis=0)
        v = jnp.concatenate([kvc_ref[:, vcol], kv_ref[pl.ds(start, GQ_SPAN), vcol]], axis=0)
        heads = [h for h in range(n_heads) if ((h // GQ_GROUP) == (h % 2)) == (swapped == 0)]
        qs = jnp.concatenate(
            [jnp.where(lower == (h % 2 == 0), q[:, (h // 2) * pw:(h // 2 + 1) * pw], jnp.zeros((), bf16))
             for h in heads], axis=0)
        s = _dot_nt(qs, k)
        es, sink_terms = [], []
        for i, h in enumerate(heads):
            sh = s[i * GQ_TQ:(i + 1) * GQ_TQ]
            sh = jnp.concatenate([sh[:, :CTX_LEN], jnp.where(mask, sh[:, CTX_LEN:], NEG_INF)], axis=1)
            sink = sink_ref[h] * LOG2E
            mx = jnp.maximum(jnp.max(sh, axis=-1, keepdims=True), sink)
            es.append(jnp.exp2(sh - mx).astype(bf16))
            sink_terms.append(jnp.exp2(sink - mx))
        o = _dot(jnp.concatenate(es, axis=0), jnp.concatenate([v, jnp.ones_like(v)], axis=1))
        for i, h in enumerate(heads):
            oh = o[i * GQ_TQ:(i + 1) * GQ_TQ]
            outs[h] = oh[:, :pw] * (1.0 / (oh[:, pw:] + sink_terms[i]))
    return jnp.concatenate(
        [jnp.where(lower, outs[2 * j], outs[2 * j + 1]) for j in range(n_heads // 2)], axis=1).astype(bf16)


def window_gqa(proj, sink):
    width = GQ_KV * GQ_GROUP * GQ_HD
    nq = SEQ // (GQ_NB * GQ_TQ)
    kvw = 4 * GQ_KV * GQ_HD
    kv_col = (5 * SEG) // kvw
    ctx_blk0 = T_LAT // CTX_LEN
    return pl.pallas_call(
        _gqa_kernel,
        grid=(BATCH, nq),
        in_specs=[
            pl.BlockSpec((GQ_NB * GQ_TQ, width), lambda b, n: (b * nq + n, 4)),
            pl.BlockSpec((SEQ, kvw), lambda b, n: (b, kv_col)),
            pl.BlockSpec((CTX_LEN, kvw), lambda b, n: (ctx_blk0 + b, kv_col)),
            pl.BlockSpec(memory_space=pltpu.SMEM),
        ],
        out_specs=pl.BlockSpec((GQ_NB * GQ_TQ, width), lambda b, n: (b * nq + n, 0)),
        out_shape=jax.ShapeDtypeStruct((T_LAT, width), bf16),
        compiler_params=_cparams("parallel", "arbitrary"),
        name="window_gqa",
    )(proj, proj, proj, sink)


OUT_TM = 1024
ROUTE_W = LANES
MOE_TB = 256


def _route(logits):
    lane_i = lax.broadcasted_iota(jnp.int32, logits.shape, 1)
    lane = lane_i.astype(f32)
    big = float(1 << 20)
    valid = lane_i < N_EXPERTS
    le = logits
    lgx = pltpu.roll(logits, ROUTE_W - N_EXPERTS, 1)
    lgx = jnp.where(valid, lgx, NEG_INF)
    gmax = jnp.max(lgx, axis=-1, keepdims=True)
    grp = (lane_i // EXP_PER_GROUP).astype(f32)
    g_sel = jnp.min(jnp.where(lgx == gmax, grp, big), axis=-1, keepdims=True)
    p_grp = float(EXP_PER_GROUP) / jnp.sum(jnp.exp(lgx - gmax), axis=-1, keepdims=True)
    lm = jnp.where(valid, jnp.where(grp == g_sel, le, NEG_INF), NEG_INF)
    v1 = jnp.max(lm, axis=-1, keepdims=True)
    i1 = jnp.min(jnp.where(lm == v1, lane, big), axis=-1, keepdims=True)
    lm2 = jnp.where(lane == i1, NEG_INF, lm)
    v2 = jnp.max(lm2, axis=-1, keepdims=True)
    i2 = jnp.min(jnp.where(lm2 == v2, lane, big), axis=-1, keepdims=True)
    e2 = jnp.exp(v2 - v1)
    w1 = p_grp / (1.0 + e2)
    w2 = w1 * e2
    return jnp.where(lane == i1, w1, 0.0) + jnp.where(lane == i2, w2, 0.0)


def _outproj_kernel(stacked, *refs):
    i = pl.program_id(0)
    if stacked:
        (ya_ref, yac_ref, yb_ref, ybc_ref, w_ref, x_ref, c_ref, mod_ref, wr_ref, br_ref,
         hn_ref, v_ref, comb_ref, cnt_ref) = refs
        ya = _stacked_tile(i, OUT_TM, ya_ref, yac_ref)
        yb = _stacked_tile(i, OUT_TM, yb_ref, ybc_ref)
        h = _stacked_tile(i, OUT_TM, x_ref, c_ref)
    else:
        ya_ref, yb_ref, w_ref, h_ref, mod_ref, wr_ref, br_ref, hn_ref, v_ref, comb_ref, cnt_ref = refs
        ya = ya_ref[...]
        yb = yb_ref[...]
        h = h_ref[...]
    r = _mod_row(i, OUT_TM)
    g1 = mod_ref[pl.ds(r, 1), pl.ds(2 * D_MODEL, D_MODEL)]
    sh2 = mod_ref[pl.ds(r, 1), pl.ds(3 * D_MODEL, D_MODEL)]
    sc2 = mod_ref[pl.ds(r, 1), pl.ds(4 * D_MODEL, D_MODEL)]
    half = ya.shape[1]
    m = _dot(ya, w_ref[0:half, :]) + _dot(yb, w_ref[half:2 * half, :])
    hn = h + g1 * m
    hn_ref[...] = hn
    v = _rms(hn) * (1.0 + sc2) + sh2
    v_ref[...] = v.astype(bf16)
    vh, vl = _split(v)
    prod = _dot(jnp.concatenate([vh, vl], axis=0), wr_ref[...])
    tm = v.shape[0]
    comb = _route(prod[:tm, :ROUTE_W] + prod[:tm, ROUTE_W:] + prod[tm:, :ROUTE_W] + br_ref[...])
    comb_ref[...] = comb
    for s in range(OUT_TM // MOE_TB):
        cnt = jnp.sum((comb[s * MOE_TB:(s + 1) * MOE_TB] != 0.0).astype(f32), axis=0, keepdims=True)
        cnt_ref[s] = jnp.broadcast_to(cnt, (8, ROUTE_W)).astype(jnp.int32)


def out_projection(n_rows, yas, ybs, w, hs, mod, wr, br):
    stacked = len(yas) == 2
    assert stacked == (len(hs) == 2) == (len(ybs) == 2) and (not stacked or n_rows == T_ALL)
    half = ybs[0].shape[1]
    row_spec = lambda width: [pl.BlockSpec((OUT_TM, width), lambda i: (i, 0))]
    rows = lambda width: _stacked_specs(OUT_TM, width) if stacked else row_spec(width)
    wr_hi = wr.astype(bf16)
    wr_lo = (wr - wr_hi.astype(f32)).astype(bf16)
    return pl.pallas_call(
        functools.partial(_outproj_kernel, stacked),
        grid=(n_rows // OUT_TM,),
        in_specs=(
            rows(half) + rows(half)
            + [_const_spec((2 * half, D_MODEL))]
            + rows(D_MODEL)
            + [_const_spec((MOD_ROWS, 6 * D_MODEL)),
               _const_spec((D_MODEL, 2 * ROUTE_W)),
               _const_spec((1, ROUTE_W))]
        ),
        out_specs=[
            pl.BlockSpec((OUT_TM, D_MODEL), lambda i: (i, 0)),
            pl.BlockSpec((OUT_TM, D_MODEL), lambda i: (i, 0)),
            pl.BlockSpec((OUT_TM, ROUTE_W), lambda i: (i, 0)),
            pl.BlockSpec((OUT_TM // MOE_TB, 8, ROUTE_W), lambda i: (i, 0, 0)),
        ],
        out_shape=[
            jax.ShapeDtypeStruct((n_rows, D_MODEL), f32),
            jax.ShapeDtypeStruct((n_rows, D_MODEL), bf16),
            jax.ShapeDtypeStruct((n_rows, ROUTE_W), f32),
            jax.ShapeDtypeStruct((n_rows // MOE_TB, 8, ROUTE_W), jnp.int32),
        ],
        compiler_params=_cparams("parallel"),
        name="out_projection",
    )(*yas, *ybs, w, *hs, mod, jnp.concatenate([wr_hi, wr_lo], axis=1), br)


MOE_UNIT = 16
MOE_TG = 1024
MOE_TOP = 2
MOE_RLOC = MOE_TOP * MOE_TB + N_EXPERTS * MOE_UNIT
MOE_NUNIT = MOE_RLOC // MOE_UNIT
MOE_NB = 4
MOE_SPARE = 2 * MOE_NB
MOE_GAP_UNITS = 8
MOE_META = 128


def _moe_rows(n_blk):
    return n_blk * MOE_RLOC + N_EXPERTS * MOE_TG


MOE_PLAN_ROWS = 128
MOE_PLAN_TILES = 256


def _lane_pick(x, lane, k):
    return jnp.sum(jnp.where(lane == k, x, 0.0), axis=1, keepdims=True)


def _moe_plan_kernel(n_blk, cnt_ref, ltri_ref, utri_ref, tabd_ref, tabc_ref, te_ref, meta_ref):
    shape = (MOE_PLAN_ROWS, ROUTE_W)
    lane = lax.broadcasted_iota(jnp.int32, shape, 1)
    cnt = cnt_ref[...].astype(f32)
    units = jnp.floor((cnt + (MOE_UNIT - 1.0)) * (1.0 / MOE_UNIT))
    ub = units.astype(bf16)
    utri = utri_ref[...]
    pre = _dot(ltri_ref[...], ub)
    lstart = _dot(ub, utri)
    n_e = jnp.sum(units, axis=0, keepdims=True)
    upt = MOE_TG // MOE_UNIT
    tiles_e = jnp.floor((n_e + (upt - 1.0)) * (1.0 / upt))
    goff = _dot(jnp.broadcast_to(tiles_e, (8, ROUTE_W)).astype(bf16), utri)[0:1, :]
    a = goff * upt + pre - lstart
    n_used = jnp.sum(units, axis=1, keepdims=True)
    j = lane.astype(f32)
    acc = jnp.zeros(shape, f32)
    for e in range(N_EXPERTS):
        ls = _lane_pick(lstart, lane, e)
        u = _lane_pick(units, lane, e)
        acc = acc + jnp.where(j >= ls, jnp.where(j < ls + u, _lane_pick(a, lane, e), 0.0), 0.0)
    rows = (acc + j) * MOE_UNIT
    used = j < n_used
    region = (lax.broadcasted_iota(jnp.int32, shape, 0) & (MOE_SPARE - 1)).astype(f32)
    spare = _moe_rows(n_blk) + region * MOE_RLOC + j * MOE_UNIT
    tabd_ref[...] = jnp.where(used, rows, spare).astype(jnp.int32)
    tabc_ref[...] = jnp.where(used, rows, _lane_pick(rows, lane, 0)).astype(jnp.int32)

    ends = goff + tiles_e
    lane1 = lax.broadcasted_iota(jnp.int32, (1, ROUTE_W), 1)
    ti = lax.broadcasted_iota(jnp.int32, (8, MOE_PLAN_TILES), 1).astype(f32)
    te = jnp.zeros((8, MOE_PLAN_TILES), f32)
    for e in range(N_EXPERTS):
        te = te + jnp.where(ti >= _lane_pick(ends, lane1, e), 1.0, 0.0)
    te_ref[...] = jnp.minimum(te, N_EXPERTS - 1.0).astype(jnp.int32)

    n_tiles = jnp.sum(tiles_e, axis=1, keepdims=True)
    gap_start = jnp.broadcast_to((goff * upt + n_e) * MOE_UNIT, (8, ROUTE_W))
    gap_units = jnp.broadcast_to(tiles_e * upt - n_e, (8, ROUTE_W))
    lane8 = lax.broadcasted_iota(jnp.int32, (8, ROUTE_W), 1)
    meta = jnp.where(lane8 == 0, n_tiles,
                     jnp.where(lane8 <= N_EXPERTS, pltpu.roll(gap_start, 1, 1),
                               jnp.where(lane8 <= 2 * N_EXPERTS, pltpu.roll(gap_units, 1 + N_EXPERTS, 1), 0.0)))
    meta_ref[...] = meta.astype(jnp.int32)


def moe_plan(n_blk, counts, ltri, utri):
    assert n_blk <= MOE_PLAN_ROWS and _moe_rows(n_blk) // MOE_TG <= MOE_PLAN_TILES
    shape = (MOE_PLAN_ROWS, ROUTE_W)
    tabd, tabc, te, meta = pl.pallas_call(
        functools.partial(_moe_plan_kernel, n_blk),
        out_shape=[
            jax.ShapeDtypeStruct(shape, jnp.int32),
            jax.ShapeDtypeStruct(shape, jnp.int32),
            jax.ShapeDtypeStruct((8, MOE_PLAN_TILES), jnp.int32),
            jax.ShapeDtypeStruct((8, ROUTE_W), jnp.int32),
        ],
        name="moe_plan",
    )(counts, ltri, utri)
    return tabd.reshape(-1), tabc.reshape(-1), te[0], meta[0]


def _block_routes(comb, ltri, utri):
    oh = comb != 0.0
    ohf = jnp.where(oh, 1.0, 0.0)
    rank = _dot(ltri, ohf.astype(bf16))
    cnt = jnp.sum(ohf, axis=0, keepdims=True)
    units = jnp.floor((cnt + (MOE_UNIT - 1.0)) * (1.0 / MOE_UNIT))
    seg = _dot(jnp.broadcast_to(units, (8, ROUTE_W)).astype(bf16), utri)[0:1, :] * MOE_UNIT
    dest = seg + rank
    big = float(1 << 20)
    d_a = jnp.min(jnp.where(oh, dest, big), axis=-1, keepdims=True)
    d_b = jnp.max(jnp.where(oh, dest, -1.0), axis=-1, keepdims=True)
    w_a = jnp.sum(jnp.where(oh, jnp.where(dest == d_a, comb, 0.0), 0.0), axis=-1, keepdims=True)
    w_b = jnp.sum(jnp.where(oh, jnp.where(dest == d_b, comb, 0.0), 0.0), axis=-1, keepdims=True)
    second = d_b != d_a
    return d_a, jnp.where(second, d_b, -1.0), w_a, jnp.where(second, w_b, 0.0)


def _one_hot_rows(d):
    r = lax.broadcasted_iota(jnp.int32, (d.shape[0], MOE_RLOC), 1).astype(f32)
    return jnp.where(r == d, 1.0, 0.0).astype(bf16)


def _block_gather_matrix(comb, utri_tok, ltri_exp):
    comb_t = comb.T
    oh = comb_t != 0.0
    ohf = jnp.where(oh, 1.0, 0.0)
    rank = _dot(ohf.astype(bf16), utri_tok)
    cnt = jnp.sum(ohf, axis=1, keepdims=True)
    units = jnp.floor((cnt + (MOE_UNIT - 1.0)) * (1.0 / MOE_UNIT))
    seg = _dot(ltri_exp, jnp.broadcast_to(units, (ROUTE_W, ROUTE_W)).astype(bf16))[:, 0:1] * MOE_UNIT
    dest = seg + rank
    d_a = jnp.min(jnp.where(oh, dest, float(1 << 20)), axis=0, keepdims=True)
    d_b = jnp.max(jnp.where(oh, dest, -1.0), axis=0, keepdims=True)
    d_b = jnp.where(d_b != d_a, d_b, -1.0)
    r = lax.broadcasted_iota(jnp.int32, (MOE_RLOC, comb.shape[0]), 0).astype(f32)
    return jnp.where(r == d_a, 1.0, jnp.where(r == d_b, 1.0, 0.0)).astype(bf16)


def _unit_rows(ref, tab_ref, t, j):
    return ref.at[pl.ds(pl.multiple_of(tab_ref[t * ROUTE_W + j], MOE_UNIT), MOE_UNIT)]


def _wait_all_units(local, remote, sem):
    pltpu.make_async_copy(local, remote.at[pl.ds(0, MOE_RLOC)], sem).wait()


def _gap_copies(meta_ref, zero_ref, remote, sem, wait):
    def copy(src, row, n_rows):
        cp = pltpu.make_async_copy(src, remote.at[pl.ds(pl.multiple_of(row, MOE_UNIT), n_rows)], sem)
        if wait:
            cp.wait()
        else:
            cp.start()

    def per_expert(e, c):
        start = meta_ref[1 + e]
        units = meta_ref[1 + N_EXPERTS + e]
        n_big = units // MOE_GAP_UNITS
        big_rows = MOE_GAP_UNITS * MOE_UNIT

        def per_chunk(k, c2):
            copy(zero_ref, start + k * big_rows, big_rows)
            return c2

        def per_unit(u, c2):
            copy(zero_ref.at[pl.ds(0, MOE_UNIT)], start + n_big * big_rows + u * MOE_UNIT, MOE_UNIT)
            return c2

        lax.fori_loop(0, n_big, per_chunk, 0)
        lax.fori_loop(0, units - n_big * MOE_GAP_UNITS, per_unit, 0)
        return c

    lax.fori_loop(0, N_EXPERTS, per_expert, 0)


def _moe_dispatch_kernel(n_blk, tab_ref, meta_ref, x_ref, comb_ref, utri_tok_ref, ltri_exp_ref, xs_ref,
                         buf_ref, zero_ref, sem_ref):
    step = pl.program_id(0)
    slot = step % 2
    for bb in range(MOE_NB):
        tok = slice(bb * MOE_TB, (bb + 1) * MOE_TB)
        p = _block_gather_matrix(comb_ref[tok, :], utri_tok_ref[...], ltri_exp_ref[...])
        buf_ref[slot, bb] = _dot(p, x_ref[tok, :]).astype(bf16)
    for bb in range(MOE_NB):
        for j in range(MOE_NUNIT):
            pltpu.make_async_copy(buf_ref.at[slot, bb, pl.ds(j * MOE_UNIT, MOE_UNIT)],
                                  _unit_rows(xs_ref, tab_ref, step * MOE_NB + bb, j), sem_ref.at[slot]
                                  ).start(priority=j % 2)

    def wait_slot(s):
        for bb in range(MOE_NB):
            _wait_all_units(buf_ref.at[s, bb], xs_ref, sem_ref.at[s])

    @pl.when(step > 0)
    def _():
        wait_slot(1 - slot)

    @pl.when(step == n_blk // MOE_NB - 1)
    def _():
        zero_ref[...] = jnp.zeros_like(zero_ref)
        _gap_copies(meta_ref, zero_ref, xs_ref, sem_ref.at[2], False)
        wait_slot(slot)
        _gap_copies(meta_ref, zero_ref, xs_ref, sem_ref.at[2], True)


def moe_dispatch(n_blk, tab, meta, x, comb, utri_tok, ltri_exp):
    return pl.pallas_call(
        functools.partial(_moe_dispatch_kernel, n_blk),
        grid_spec=pltpu.PrefetchScalarGridSpec(
            num_scalar_prefetch=2,
            grid=(n_blk // MOE_NB,),
            in_specs=[
                pl.BlockSpec((MOE_NB * MOE_TB, D_MODEL), lambda t, *_: (t, 0)),
                pl.BlockSpec((MOE_NB * MOE_TB, ROUTE_W), lambda t, *_: (t, 0)),
                pl.BlockSpec((MOE_TB, MOE_TB), lambda t, *_: (0, 0)),
                pl.BlockSpec((ROUTE_W, ROUTE_W), lambda t, *_: (0, 0)),
            ],
            out_specs=pl.BlockSpec(memory_space=pl.ANY),
            scratch_shapes=[
                pltpu.VMEM((2, MOE_NB, MOE_RLOC, D_MODEL), bf16),
                pltpu.VMEM((MOE_GAP_UNITS * MOE_UNIT, D_MODEL), bf16),
                pltpu.SemaphoreType.DMA((3,)),
            ],
        ),
        out_shape=jax.ShapeDtypeStruct((_moe_rows(n_blk) + MOE_SPARE * MOE_RLOC, D_MODEL), bf16),
        compiler_params=_cparams("arbitrary"),
        name="moe_dispatch",
    )(tab, meta, x, comb, utri_tok, ltri_exp)


def _moe_expert_kernel(tile_exp_ref, meta_ref, xs_ref, wg_ref, wu_ref, wd_ref, ys_ref, wgub_ref, wdb_ref):
    i = pl.program_id(0)

    @pl.when(i < meta_ref[0])
    def _():
        prev = tile_exp_ref[jnp.maximum(i - 1, 0)]

        @pl.when(jnp.logical_or(i == 0, tile_exp_ref[i] != prev))
        def _():
            wgub_ref[:, :D_EXPERT] = wg_ref[0].astype(bf16)
            wgub_ref[:, D_EXPERT:] = wu_ref[0].astype(bf16)
            wdb_ref[...] = wd_ref[0].astype(bf16)

        gu = _dot(xs_ref[...], wgub_ref[...])
        a = _silu(gu[:, :D_EXPERT]) * gu[:, D_EXPERT:]
        ys_ref[...] = _dot(a.astype(bf16), wdb_ref[...]).astype(bf16)


def moe_experts(n_blk, layer, tile_exp, meta, xs, w_gate, w_up, w_down):
    n_tiles = _moe_rows(n_blk) // MOE_TG

    def row_map(i, te, meta):
        return (jnp.minimum(i, meta[0] - 1), 0)

    def w_map(i, te, meta):
        return (layer * N_EXPERTS + te[jnp.minimum(i, meta[0] - 1)], 0, 0)

    return pl.pallas_call(
        _moe_expert_kernel,
        grid_spec=pltpu.PrefetchScalarGridSpec(
            num_scalar_prefetch=2,
            grid=(n_tiles,),
            in_specs=[
                pl.BlockSpec((MOE_TG, D_MODEL), row_map),
                pl.BlockSpec((1, D_MODEL, D_EXPERT), w_map),
                pl.BlockSpec((1, D_MODEL, D_EXPERT), w_map),
                pl.BlockSpec((1, D_EXPERT, D_MODEL), w_map),
            ],
            out_specs=pl.BlockSpec((MOE_TG, D_MODEL), row_map),
            scratch_shapes=[
                pltpu.VMEM((D_MODEL, 2 * D_EXPERT), bf16),
                pltpu.VMEM((D_EXPERT, D_MODEL), bf16),
            ],
        ),
        out_shape=jax.ShapeDtypeStruct((_moe_rows(n_blk), D_MODEL), bf16),
        compiler_params=_cparams("arbitrary"),
        name="moe_experts",
    )(tile_exp, meta, xs, w_gate, w_up, w_down)


def _moe_combine_kernel(n_blk, tab_ref, ys_ref, comb_ref, h_ref, mod_ref, ltri_ref, utri_ref,
                        o_ref, buf_ref, sem_ref):
    step = pl.program_id(0)
    slot = step % 2

    def gather(st, s):
        for bb in range(MOE_NB):
            for j in range(MOE_NUNIT):
                pltpu.make_async_copy(_unit_rows(ys_ref, tab_ref, st * MOE_NB + bb, j),
                                      buf_ref.at[s, bb, pl.ds(j * MOE_UNIT, MOE_UNIT)], sem_ref.at[s]
                                      ).start(priority=j % 2)

    @pl.when(step == 0)
    def _():
        gather(0, 0)

    @pl.when(step + 1 < n_blk // MOE_NB)
    def _():
        gather(step + 1, 1 - slot)

    for bb in range(MOE_NB):
        _wait_all_units(buf_ref.at[slot, bb], ys_ref, sem_ref.at[slot])
    for bb in range(MOE_NB):
        tok = slice(bb * MOE_TB, (bb + 1) * MOE_TB)
        d_a, d_b, w_a, w_b = _block_routes(comb_ref[tok, :], ltri_ref[...], utri_ref[...])
        p = jnp.concatenate([_one_hot_rows(d_a), _one_hot_rows(d_b)], axis=0)
        picked = _dot(p, buf_ref[slot, bb])
        m = w_a * picked[:MOE_TB] + w_b * picked[MOE_TB:]
        r = _mod_row(step * MOE_NB + bb, MOE_TB)
        g2 = mod_ref[pl.ds(r, 1), pl.ds(5 * D_MODEL, D_MODEL)]
        o_ref[tok, :] = h_ref[tok, :] + g2 * m


def moe_combine(n_blk, tab, ys, comb, h, mod, ltri, utri):
    return pl.pallas_call(
        functools.partial(_moe_combine_kernel, n_blk),
        grid_spec=pltpu.PrefetchScalarGridSpec(
            num_scalar_prefetch=1,
            grid=(n_blk // MOE_NB,),
            in_specs=[
                pl.BlockSpec(memory_space=pl.ANY),
                pl.BlockSpec((MOE_NB * MOE_TB, ROUTE_W), lambda t, *_: (t, 0)),
                pl.BlockSpec((MOE_NB * MOE_TB, D_MODEL), lambda t, *_: (t, 0)),
                pl.BlockSpec((MOD_ROWS, 6 * D_MODEL), lambda t, *_: (0, 0)),
                pl.BlockSpec((MOE_TB, MOE_TB), lambda t, *_: (0, 0)),
                pl.BlockSpec((ROUTE_W, ROUTE_W), lambda t, *_: (0, 0)),
            ],
            out_specs=pl.BlockSpec((MOE_NB * MOE_TB, D_MODEL), lambda t, *_: (t, 0)),
            scratch_shapes=[
                pltpu.VMEM((2, MOE_NB, MOE_RLOC, D_MODEL), bf16),
                pltpu.SemaphoreType.DMA((2,)),
            ],
        ),
        out_shape=jax.ShapeDtypeStruct((n_blk * MOE_TB, D_MODEL), f32),
        compiler_params=_cparams("arbitrary"),
        name="moe_combine",
    )(tab, ys, comb, h, mod, ltri, utri)


def sparse_moe(n_rows, layer, v, comb, counts, h, mod, w_gate, w_up, w_down, ltri, ltri_plan, utri):
    n_blk = n_rows // MOE_TB
    cnt = jnp.pad(counts[:, 0, :], ((0, MOE_PLAN_ROWS - n_blk), (0, 0)))
    tab_d, tab_c, tile_exp, meta = moe_plan(n_blk, cnt, ltri_plan, utri)
    assert MOE_PLAN_ROWS == ROUTE_W
    xs = moe_dispatch(n_blk, tab_d, meta, v, comb, ltri.T, ltri_plan)
    ys = moe_experts(n_blk, layer, tile_exp, meta, xs,
                     w_gate.reshape(DEPTH * N_EXPERTS, D_MODEL, D_EXPERT),
                     w_up.reshape(DEPTH * N_EXPERTS, D_MODEL, D_EXPERT),
                     w_down.reshape(DEPTH * N_EXPERTS, D_EXPERT, D_MODEL))
    return moe_combine(n_blk, tab_c, ys, comb, h, mod, ltri, utri)


def _dft_tables(L):
    k = np.arange(L, dtype=np.int64)
    ang = (2.0 * np.pi / (2 * L)) * ((k[:, None] * k[None, :]) % (2 * L)).astype(np.float64)
    return np.cos(ang).astype(np.float32), np.sin(ang).astype(np.float32)


def _filter_features(L):
    bands = (HY_EMB - 1) // 2
    t = np.linspace(0.0, 1.0, L, dtype=np.float32).astype(np.float64)[:, None]
    w = (2.0 * np.pi / L) * np.arange(L, dtype=np.float64)[:, None]
    fb = np.linspace(1e-4, bands - 1, bands, dtype=np.float32).astype(np.float64)[None, :]
    z = np.concatenate([t, np.cos(fb * w), -np.sin(fb * w)], axis=-1)
    zp = np.zeros((L, FEAT_PAD), np.float32)
    zp[:, :HY_EMB] = z
    deltas = np.abs(np.linspace(HY_MIN_DECAY, HY_MAX_DECAY, HY_CH, dtype=np.float32).astype(np.float64))
    decay = np.exp(-t * deltas[None, :]).astype(np.float32)
    return zp, decay


def _rope_table(cos, sin, half, tm):
    S, width = cos.shape
    low = (np.arange(width) % (2 * half)) < half
    tab = np.zeros((3, S + tm, width), np.float32)
    tab[0, :S] = cos
    tab[0, S:] = 1.0
    tab[1, :S] = np.where(low[None, :], 0.0, sin)
    tab[2, :S] = np.where(low[None, :], -sin, 0.0)
    return tab


def _axial_rope_table(head_dim, tm):
    rows = SEQ // GRID_W
    nf = head_dim // 4
    row = np.repeat(np.arange(rows), GRID_W).astype(np.float64)
    col = np.tile(np.arange(GRID_W), rows).astype(np.float64)
    inv = ROPE_BASE ** (-np.arange(nf, dtype=np.float64) / nf)
    ang = np.stack([row[:, None] * inv, col[:, None] * inv], axis=1)
    a = np.broadcast_to(ang[:, :, None, :], (SEQ, 2, 2, nf)).reshape(SEQ, head_dim)
    reps = LANES // head_dim
    a = np.tile(a, (1, reps))
    return _rope_table(np.cos(a), np.sin(a), nf, tm)


def _seq_rope_table(head_dim, tm):
    inv = 1.0 / (ROPE_BASE ** np.linspace(0.0, 1.0, head_dim // 2, dtype=np.float32).astype(np.float64))
    ang = np.arange(SEQ, dtype=np.float64)[:, None] * inv
    a = np.concatenate([ang, ang], axis=1)
    return _rope_table(np.cos(a), np.sin(a), head_dim // 2, tm)


def _group_mean_matrix():
    g = np.arange(SEG) // DA_HD
    return (g[:, None] == g[None, :]).astype(np.float32) / DA_HD


def _router_weights(w_grp, b_grp, w_rt, b_rt):
    pad = ROUTE_W - 2 * N_EXPERTS
    wr = jnp.concatenate([w_rt, jnp.repeat(w_grp, EXP_PER_GROUP, axis=1),
                          jnp.zeros((D_MODEL, pad), f32)], axis=1)
    br = jnp.concatenate([b_rt, jnp.repeat(b_grp, EXP_PER_GROUP), jnp.zeros((pad,), f32)])[None, :]
    return wr, br


def _strict_lower(n):
    i = np.arange(n)
    return (i[None, :] < i[:, None]).astype(np.float32)


def kernel(x, c, ctx, c_ctx, ada_w, ada_b, e_w_in, e_w_out, hy_conv_w, hy_conv_b, hy_f_w1, hy_f_b1, hy_f_w2, hy_f_b2, hy_f_w3, hy_f_freq, hy_bias, da_q_norm, da_k_norm, da_lam, da_subln, o_w_in, o_w_out, ret_decay, ret_gn, gq_q_norm, gq_k_norm, gq_sink, moe_w_grp, moe_b_grp, moe_w_rt, moe_b_rt, moe_w_gate, moe_w_up, moe_w_down):
    assert x.shape == (BATCH, SEQ, D_MODEL) and ctx.shape == (BATCH, CTX_LEN, D_MODEL)
    x_rows = x.reshape(T_LAT, D_MODEL)
    ctx_rows = ctx.reshape(T_CTX, D_MODEL)
    c_rows = jnp.concatenate([c, c_ctx[None, :], jnp.zeros((MOD_ROWS - BATCH - 1, D_MODEL), f32)], axis=0)
    mod = ada_modulation(c_rows, ada_w, ada_b)

    gmat = jnp.asarray(_group_mean_matrix()).astype(bf16)
    ax_tab = jnp.asarray(_axial_rope_table(DA_HD, PROJ_TM))
    r1_tab = jnp.asarray(_seq_rope_table(RET_DK, PROJ_TM))
    ones = jnp.ones((SEG,), f32)

    lam_init0 = 0.8 - 0.6 * math.exp(-0.3 * 0)
    reps = SEG // DA_HD
    gain0 = jnp.concatenate([ones, ones, ones, jnp.tile(da_q_norm[0], reps) * (DA_HD ** -0.5 * LOG2E),
                             jnp.tile(da_k_norm[0], reps), ones])[None, :]
    proj0 = in_projection("even", [x_rows, ctx_rows], mod[0], e_w_in[0].astype(bf16), gain0, gmat, [ax_tab])

    w3r = hy_f_w3[0].reshape(HY_FILT_HID, 4, HY_CH).transpose(1, 0, 2)
    w1p = jnp.concatenate([hy_f_w1[0], jnp.zeros((FEAT_PAD - HY_EMB, HY_FILT_HID), f32)], axis=0)
    y_hy = []
    for L, blk0 in ((SEQ, 0), (CTX_LEN, T_LAT // CTX_LEN)):
        zfeat, decay = _filter_features(L)
        cm, sm = _dft_tables(L)
        cm = jnp.asarray(cm).astype(bf16)
        sm = jnp.asarray(sm).astype(bf16)
        spec, nyq = hyena_filter_spectra(L, jnp.asarray(zfeat), w1p, hy_f_b1[0][None, :], hy_f_w2[0],
                                         hy_f_b2[0][None, :], w3r, hy_f_freq[0], jnp.asarray(decay), cm, sm)
        y_hy.append(hyena_mix(L, blk0, proj0, hy_conv_w[0], hy_conv_b[0][None, :], spec, nyq, hy_bias[0], cm, sm))

    y_da = diff_attention(proj0, da_lam[0], da_subln[0][None, :], lam_init0)

    wr0, br0 = _router_weights(moe_w_grp[0], moe_b_grp[0], moe_w_rt[0], moe_b_rt[0])
    ltri = jnp.asarray(_strict_lower(MOE_TB)).astype(bf16)
    ltri_plan = jnp.asarray(_strict_lower(MOE_PLAN_ROWS)).astype(bf16)
    utri = jnp.asarray(_strict_lower(ROUTE_W).T).astype(bf16)
    h, v, comb, counts = out_projection(T_ALL, y_hy, y_da, e_w_out[0].astype(bf16), [x_rows, ctx_rows], mod[0],
                                        wr0, br0)
    h = sparse_moe(T_ALL, 0, v, comb, counts, h, mod[0], moe_w_gate, moe_w_up, moe_w_down, ltri, ltri_plan, utri)

    w_in1 = jnp.concatenate([o_w_in[0], jnp.zeros((D_MODEL, PROJ_W - o_w_in.shape[2]), f32)], axis=1).astype(bf16)
    kq = GQ_KV * GQ_HD
    gain1 = jnp.concatenate([ones, ones * RET_DK ** -0.5, ones, ones,
                             jnp.tile(gq_q_norm[0], reps) * (GQ_HD ** -0.5 * LOG2E),
                             jnp.tile(gq_k_norm[0], kq // GQ_HD), jnp.ones((SEG - kq,), f32)])[None, :]
    proj1 = in_projection("odd", [h], mod[1], w_in1, gain1, gmat, [ax_tab, r1_tab])
    y_ret = retention(proj1, ret_decay[0], ret_gn[0][None, :])
    y_gq = window_gqa(proj1, gq_sink[0])
    wr1, br1 = _router_weights(moe_w_grp[1], moe_b_grp[1], moe_w_rt[1], moe_b_rt[1])
    h_lat, v, comb, counts = out_projection(T_LAT, [y_ret], [y_gq], o_w_out[0].astype(bf16), [h], mod[1], wr1, br1)
    out = sparse_moe(T_LAT, 1, v, comb, counts, h_lat, mod[1], moe_w_gate, moe_w_up, moe_w_down, ltri, ltri_plan,
                     utri)
    return out.reshape(BATCH, SEQ, D_MODEL)
```
